```python
import math
import jax, jax.numpy as jnp
from jax import lax
import numpy as np

D_MODEL = 1024
BATCH = 4
SEQ = 4096
DEPTH = 1
DEC_BATCH = 128
DEC_SEQ = 1
PAST_LEN = 2048
PAGE_SIZE = 128

N_MEM = 256
SSM_WIDTH = D_MODEL // 2
SSM_GROUP = 16
SSM_GROUPS = SSM_WIDTH // SSM_GROUP
SSM_STATE = 64
ATT_HEAD_DIM = 64
ATT_HEADS_PER_GROUP = 4
DIL_PAIRS = ((128, 1), (512, 4), (2048, 16))
N_DIL = len(DIL_PAIRS)
ATT_HEADS = N_DIL * ATT_HEADS_PER_GROUP
ATT_WIDTH = ATT_HEADS * ATT_HEAD_DIM
ATT_OUT_WIDTH = ATT_HEADS_PER_GROUP * ATT_HEAD_DIM
MEM_HEADS = 4
MEM_HEAD_DIM = 128
MEM_WIDTH = MEM_HEADS * MEM_HEAD_DIM
N_BRANCH = 3
IN_WIDTH = SSM_WIDTH + 3 * ATT_WIDTH + MEM_WIDTH + N_BRANCH * D_MODEL
D_FF = 2816
CONV_W = 3
EPS = 1e-6

kernel_name = 'hybrid_s5_dilated_alibi_memxattn_convffn_step'


def alibi_slopes():
    return jnp.asarray(2.0 ** (-8.0 * np.arange(1, ATT_HEADS + 1) / ATT_HEADS), dtype=jnp.float32)


def rmsnorm(x, g):
    xf = x.astype(jnp.float32)
    y = xf * lax.rsqrt(jnp.mean(xf * xf, axis=-1, keepdims=True) + EPS)
    return (y * g.astype(jnp.float32)).astype(x.dtype)


def split_in(h, w_in):
    n, l, _ = h.shape
    z = h @ w_in
    offs = [SSM_WIDTH, SSM_WIDTH + ATT_WIDTH, SSM_WIDTH + 2 * ATT_WIDTH,
            SSM_WIDTH + 3 * ATT_WIDTH, SSM_WIDTH + 3 * ATT_WIDTH + MEM_WIDTH]
    u, q, k, v, qm, gl = jnp.split(z, offs, axis=-1)
    q = q.reshape(n, l, ATT_HEADS, ATT_HEAD_DIM)
    k = k.reshape(n, l, ATT_HEADS, ATT_HEAD_DIM)
    v = v.reshape(n, l, ATT_HEADS, ATT_HEAD_DIM)
    qm = qm.reshape(n, l, MEM_HEADS, MEM_HEAD_DIM)
    gates = jax.nn.sigmoid(gl.reshape(n, l, N_BRANCH, D_MODEL))
    return u, q, k, v, qm, gates


def _cplx_affine_combine(e1, e2):
    a1r, a1i, b1r, b1i = e1
    a2r, a2i, b2r, b2i = e2
    return (a2r * a1r - a2i * a1i, a2r * a1i + a2i * a1r,
            a2r * b1r - a2i * b1i + b2r, a2r * b1i + a2i * b1r + b2i)


def s5_scan(u, s0_re, s0_im, a_re, a_im, log_dt, b_re, b_im, c_re, c_im, d_skip):
    f32 = jnp.float32
    n, l, _ = u.shape
    a_re, a_im = a_re.astype(f32), a_im.astype(f32)
    dt = jnp.exp(log_dt.astype(f32))[:, None]
    mag = jnp.exp(a_re * dt)
    ab_re, ab_im = mag * jnp.cos(a_im * dt), mag * jnp.sin(a_im * dt)
    den = a_re * a_re + a_im * a_im
    q_re = ((ab_re - 1.0) * a_re + ab_im * a_im) / den
    q_im = (ab_im * a_re - (ab_re - 1.0) * a_im) / den
    b_re, b_im = b_re.astype(f32), b_im.astype(f32)
    bb_re = q_re[..., None] * b_re - q_im[..., None] * b_im
    bb_im = q_re[..., None] * b_im + q_im[..., None] * b_re
    uf = u.astype(f32)
    ug = uf.reshape(n, l, SSM_GROUPS, SSM_GROUP)
    bu_re = jnp.einsum('gpc,nlgc->nlgp', bb_re, ug)
    bu_im = jnp.einsum('gpc,nlgc->nlgp', bb_im, ug)
    ar = jnp.broadcast_to(ab_re, bu_re.shape)
    ai = jnp.broadcast_to(ab_im, bu_im.shape)
    pr, pi, hr, hi = lax.associative_scan(_cplx_affine_combine, (ar, ai, bu_re, bu_im), axis=1)
    s0r = s0_re.astype(f32)[:, None]
    s0i = s0_im.astype(f32)[:, None]
    s_re = hr + pr * s0r - pi * s0i
    s_im = hi + pr * s0i + pi * s0r
    y = (jnp.einsum('gcp,nlgp->nlgc', c_re.astype(f32), s_re)
         - jnp.einsum('gcp,nlgp->nlgc', c_im.astype(f32), s_im))
    y = y.reshape(n, l, SSM_WIDTH) + d_skip.astype(f32) * uf
    return y.astype(u.dtype), s_re[:, -1].astype(u.dtype), s_im[:, -1].astype(u.dtype)


def ssm_branch(u, s0_re, s0_im, a_re, a_im, log_dt, b_re, b_im, c_re, c_im, d_skip, w_glu):
    y, sr, si = s5_scan(u, s0_re, s0_im, a_re, a_im, log_dt, b_re, b_im, c_re, c_im, d_skip)
    z = jax.nn.gelu(y)
    ga, gb = jnp.split(z @ w_glu, 2, axis=-1)
    return ga * jax.nn.sigmoid(gb), sr, si


def dilated_prompt(q, k, v, dil, steps, slopes):
    f32 = jnp.float32
    n, l, h, e = q.shape
    sub = l // dil
    nb = -(-sub // steps)
    pad = nb * steps - sub

    def to_blocks(t):
        t = t.reshape(n, sub, dil, h, e).transpose(0, 2, 1, 3, 4).reshape(n * dil, sub, h, e)
        t = jnp.pad(t, ((0, 0), (0, pad), (0, 0), (0, 0)))
        return t.reshape(n * dil, nb, steps, h, e)

    def with_prev(t):
        prev = jnp.pad(t, ((0, 0), (1, 0), (0, 0), (0, 0), (0, 0)))[:, :-1]
        return jnp.concatenate([prev, t], axis=2)

    qb = to_blocks(q)
    kk = with_prev(to_blocks(k))
    vv = with_prev(to_blocks(v))
    s = jnp.einsum('xbqhe,xbkhe->xbhqk', qb, kk).astype(f32) * (e ** -0.5)
    qi = jnp.arange(steps)
    kj = jnp.arange(2 * steps)
    dist = qi[:, None] + steps - kj[None, :]
    valid = (dist >= 0) & (dist <= steps)
    first_ok = (jnp.arange(nb)[:, None, None] > 0) | (kj[None, None, :] >= steps)
    mask = valid[None] & first_ok
    bias = -slopes[:, None, None] * (dist * dil).astype(f32)[None]
    s = jnp.where(mask[None, :, None], s + bias[None, None], -jnp.inf)
    m = jnp.max(s, axis=-1, keepdims=True)
    p = jnp.exp(s - m)
    den = jnp.sum(p, axis=-1)
    o = jnp.einsum('xbhqk,xbkhe->xbqhe', p, vv.astype(f32)) / den.transpose(0, 1, 3, 2)[..., None]
    lse = (m[..., 0] + jnp.log(den)).transpose(0, 1, 3, 2)
    o = o.reshape(n * dil, nb * steps, h, e)[:, :sub]
    o = o.reshape(n, dil, sub, h, e).transpose(0, 2, 1, 3, 4).reshape(n, l, h, e)
    lse = lse.reshape(n * dil, nb * steps, h)[:, :sub]
    lse = lse.reshape(n, dil, sub, h).transpose(0, 2, 1, 3).reshape(n, l, h)
    return o, lse


def dilated_sample(q, k_all, v_all, n_past, dil, steps, slopes):
    f32 = jnp.float32
    t, e = q.shape[1], q.shape[3]
    step = jnp.arange(steps + 1)
    idx = n_past + jnp.arange(t)[:, None] - dil * step[None, :]
    valid = idx >= 0
    idx = jnp.maximum(idx, 0)
    kg = k_all[:, idx]
    vg = v_all[:, idx]
    s = jnp.einsum('nthe,ntshe->nths', q, kg).astype(f32) * (e ** -0.5)
    s = s - slopes[:, None] * (step * dil).astype(f32)[None, :]
    s = jnp.where(valid[None, :, None, :], s, -jnp.inf)
    m = jnp.max(s, axis=-1, keepdims=True)
    p = jnp.exp(s - m)
    den = jnp.sum(p, axis=-1)
    o = jnp.einsum('nths,ntshe->nthe', p, vg.astype(f32)) / den[..., None]
    lse = m[..., 0] + jnp.log(den)
    return o, lse


def merge_dilated(outs, lses, w_att_o, dtype):
    w = jax.nn.softmax(jnp.stack(lses, axis=0), axis=0)
    o = jnp.einsum('gnlh,gnlhe->nlhe', w, jnp.stack(outs, axis=0))
    n, l = o.shape[0], o.shape[1]
    return o.reshape(n, l, ATT_OUT_WIDTH).astype(dtype) @ w_att_o


def mem_kv(mem, g_mem, w_mem_kv):
    n = mem.shape[0]
    mk, mv = jnp.split(rmsnorm(mem, g_mem) @ w_mem_kv, 2, axis=-1)
    return (mk.reshape(n, N_MEM, MEM_HEADS, MEM_HEAD_DIM),
            mv.reshape(n, N_MEM, MEM_HEADS, MEM_HEAD_DIM))


def mem_attend(qm, mk, mv, w_mem_o):
    n, l = qm.shape[0], qm.shape[1]
    s = jnp.einsum('nlhe,nmhe->nhlm', qm, mk.astype(qm.dtype)).astype(jnp.float32) * (MEM_HEAD_DIM ** -0.5)
    p = jax.nn.softmax(s, axis=-1)
    o = jnp.einsum('nhlm,nmhe->nlhe', p, mv.astype(jnp.float32))
    return o.reshape(n, l, MEM_WIDTH).astype(qm.dtype) @ w_mem_o


def merge_branches(gates, b_ssm, b_att, b_mem, w_out):
    m = gates[:, :, 0] * b_ssm + gates[:, :, 1] * b_att + gates[:, :, 2] * b_mem
    return m @ w_out


def conv_ffn(h, conv_buf, w_up, conv_w, conv_b, w_down):
    l = h.shape[1]
    a, v = jnp.split(h @ w_up, 2, axis=-1)
    full = jnp.concatenate([conv_buf.astype(a.dtype), a], axis=1)
    c = full[:, 0:l] * conv_w[0]
    for j in range(1, CONV_W):
        c = c + full[:, j:j + l] * conv_w[j]
    c = c + conv_b
    y = (jax.nn.gelu(c) * v) @ w_down
    return y, full[:, -(CONV_W - 1):]


def setup_inputs(seed: int = 0) -> dict:
    key = jax.random.key(seed)
    ks = iter(jax.random.split(key, 48))
    f32 = jnp.float32

    def nrm(shape, scale):
        return jax.random.normal(next(ks), shape, f32) * scale

    L = DEPTH
    wl = [min(w, PAST_LEN) for w, _ in DIL_PAIRS]
    hpg, hd = ATT_HEADS_PER_GROUP, ATT_HEAD_DIM
    n_idx = jnp.arange(SSM_STATE, dtype=f32)
    return {
        'x_prompt': nrm((BATCH, SEQ, D_MODEL), 1.0),
        'x_sample': nrm((DEC_BATCH, DEC_SEQ, D_MODEL), 1.0),
        'state_ssm_re': nrm((L, DEC_BATCH, SSM_GROUPS, SSM_STATE), 0.5),
        'state_ssm_im': nrm((L, DEC_BATCH, SSM_GROUPS, SSM_STATE), 0.5),
        'cache_w1_k': nrm((L, DEC_BATCH, wl[0], hpg, hd), 1.0),
        'cache_w1_v': nrm((L, DEC_BATCH, wl[0], hpg, hd), 1.0),
        'cache_w2_k': nrm((L, DEC_BATCH, wl[1], hpg, hd), 1.0),
        'cache_w2_v': nrm((L, DEC_BATCH, wl[1], hpg, hd), 1.0),
        'cache_w3_k': nrm((L, DEC_BATCH, wl[2], hpg, hd), 1.0),
        'cache_w3_v': nrm((L, DEC_BATCH, wl[2], hpg, hd), 1.0),
        'cache_mem_k': nrm((L, DEC_BATCH, N_MEM, MEM_HEADS, MEM_HEAD_DIM), 1.0),
        'cache_mem_v': nrm((L, DEC_BATCH, N_MEM, MEM_HEADS, MEM_HEAD_DIM), 1.0),
        'state_ffn_conv': nrm((L, DEC_BATCH, CONV_W - 1, D_FF), 1.0),
        'mem_prompt': nrm((BATCH, N_MEM, D_MODEL), 1.0),
        'norm1_g': 1.0 + nrm((L, D_MODEL), 0.01),
        'w_in': nrm((L, D_MODEL, IN_WIDTH), D_MODEL ** -0.5),
        'ssm_a_re': -0.5 + nrm((L, SSM_GROUPS, SSM_STATE), 0.01),
        'ssm_a_im': math.pi * n_idx + nrm((L, SSM_GROUPS, SSM_STATE), 0.01),
        'ssm_log_dt': jax.random.uniform(next(ks), (L, SSM_GROUPS), f32, math.log(1e-3), math.log(1e-1)),
        'ssm_b_re': nrm((L, SSM_GROUPS, SSM_STATE, SSM_GROUP), (2.0 * SSM_GROUP) ** -0.5),
        'ssm_b_im': nrm((L, SSM_GROUPS, SSM_STATE, SSM_GROUP), (2.0 * SSM_GROUP) ** -0.5),
        'ssm_c_re': nrm((L, SSM_GROUPS, SSM_GROUP, SSM_STATE), SSM_STATE ** -0.5),
        'ssm_c_im': nrm((L, SSM_GROUPS, SSM_GROUP, SSM_STATE), SSM_STATE ** -0.5),
        'ssm_d': 1.0 + nrm((L, SSM_WIDTH), 0.1),
        'w_ssm_glu': nrm((L, SSM_WIDTH, 2 * D_MODEL), SSM_WIDTH ** -0.5),
        'w_att_o': nrm((L, ATT_OUT_WIDTH, D_MODEL), ATT_OUT_WIDTH ** -0.5),
        'mem_norm_g': 1.0 + nrm((L, D_MODEL), 0.01),
        'w_mem_kv': nrm((L, D_MODEL, 2 * MEM_WIDTH), D_MODEL ** -0.5),
        'w_mem_o': nrm((L, MEM_WIDTH, D_MODEL), MEM_WIDTH ** -0.5),
        'w_out': nrm((L, D_MODEL, D_MODEL), D_MODEL ** -0.5),
        'norm2_g': 1.0 + nrm((L, D_MODEL), 0.01),
        'w_up': nrm((L, D_MODEL, 2 * D_FF), D_MODEL ** -0.5),
        'ffn_conv_w': nrm((L, CONV_W, D_FF), CONV_W ** -0.5),
        'ffn_conv_b': nrm((L, D_FF), 0.01),
        'w_down': nrm((L, D_FF, D_MODEL), D_FF ** -0.5),
        'final_norm_g': 1.0 + nrm((D_MODEL,), 0.01),
    }


def reference(x_prompt, x_sample, state_ssm_re, state_ssm_im, cache_w1_k, cache_w1_v, cache_w2_k,
              cache_w2_v, cache_w3_k, cache_w3_v, cache_mem_k, cache_mem_v, state_ffn_conv, mem_prompt,
              norm1_g, w_in, ssm_a_re, ssm_a_im, ssm_log_dt, ssm_b_re, ssm_b_im, ssm_c_re, ssm_c_im, ssm_d,
              w_ssm_glu, w_att_o, mem_norm_g, w_mem_kv, w_mem_o, w_out, norm2_g, w_up, ffn_conv_w,
              ffn_conv_b, w_down, final_norm_g):
    slopes = alibi_slopes()
    hpg = ATT_HEADS_PER_GROUP
    xp, xs = x_prompt, x_sample
    p_states, s_states = [], []
    for i in range(DEPTH):
        ssm_w = (ssm_a_re[i], ssm_a_im[i], ssm_log_dt[i], ssm_b_re[i], ssm_b_im[i],
                 ssm_c_re[i], ssm_c_im[i], ssm_d[i], w_ssm_glu[i])

        n_p, l_p = xp.shape[0], xp.shape[1]
        u, q, k, v, qm, gates = split_in(rmsnorm(xp, norm1_g[i]), w_in[i])
        z0 = jnp.zeros((n_p, SSM_GROUPS, SSM_STATE), jnp.float32)
        b_ssm, sr_p, si_p = ssm_branch(u, z0, z0, *ssm_w)
        outs, lses, win_p = [], [], []
        for g, (win, dil) in enumerate(DIL_PAIRS):
            sl = slice(g * hpg, (g + 1) * hpg)
            o, lse = dilated_prompt(q[:, :, sl], k[:, :, sl], v[:, :, sl], dil, win // dil, slopes[sl])
            outs.append(o)
            lses.append(lse)
            keep = min(win, l_p)
            win_p += [k[:, l_p - keep:, sl], v[:, l_p - keep:, sl]]
        b_att = merge_dilated(outs, lses, w_att_o[i], xp.dtype)
        mk_p, mv_p = mem_kv(mem_prompt, mem_norm_g[i], w_mem_kv[i])
        b_mem = mem_attend(qm, mk_p, mv_p, w_mem_o[i])
        xp = xp + merge_branches(gates, b_ssm, b_att, b_mem, w_out[i])
        zbuf = jnp.zeros((n_p, CONV_W - 1, D_FF), xp.dtype)
        f, conv_p = conv_ffn(rmsnorm(xp, norm2_g[i]), zbuf, w_up[i], ffn_conv_w[i], ffn_conv_b[i], w_down[i])
        xp = xp + f
        p_states.append((sr_p, si_p, win_p[0], win_p[1], win_p[2], win_p[3], win_p[4], win_p[5],
                         mk_p, mv_p, conv_p))

        u, q, k, v, qm, gates = split_in(rmsnorm(xs, norm1_g[i]), w_in[i])
        b_ssm, sr_s, si_s = ssm_branch(u, state_ssm_re[i], state_ssm_im[i], *ssm_w)
        k_bufs = (cache_w1_k[i], cache_w2_k[i], cache_w3_k[i])
        v_bufs = (cache_w1_v[i], cache_w2_v[i], cache_w3_v[i])
        outs, lses, win_s = [], [], []
        for g, (win, dil) in enumerate(DIL_PAIRS):
            sl = slice(g * hpg, (g + 1) * hpg)
            k_new, v_new = k[:, :, sl], v[:, :, sl]
            k_all = jnp.concatenate([k_bufs[g].astype(k_new.dtype), k_new], axis=1)
            v_all = jnp.concatenate([v_bufs[g].astype(v_new.dtype), v_new], axis=1)
            o, lse = dilated_sample(q[:, :, sl], k_all, v_all, k_bufs[g].shape[1], dil, win // dil, slopes[sl])
            outs.append(o)
            lses.append(lse)
            win_s += [k_new, v_new]
        b_att = merge_dilated(outs, lses, w_att_o[i], xs.dtype)
        b_mem = mem_attend(qm, cache_mem_k[i], cache_mem_v[i], w_mem_o[i])
        xs = xs + merge_branches(gates, b_ssm, b_att, b_mem, w_out[i])
        f, conv_s = conv_ffn(rmsnorm(xs, norm2_g[i]), state_ffn_conv[i], w_up[i], ffn_conv_w[i],
                             ffn_conv_b[i], w_down[i])
        xs = xs + f
        s_states.append((sr_s, si_s, win_s[0], win_s[1], win_s[2], win_s[3], win_s[4], win_s[5], conv_s))

    y_prompt = rmsnorm(xp, final_norm_g)
    y_sample = rmsnorm(xs, final_norm_g)
    (p_ssm_re, p_ssm_im, p_w1_k, p_w1_v, p_w2_k, p_w2_v, p_w3_k, p_w3_v,
     p_mem_k, p_mem_v, p_ffn_conv) = [jnp.stack(a, axis=0) for a in zip(*p_states)]
    (s_ssm_re, s_ssm_im, s_w1_k, s_w1_v, s_w2_k, s_w2_v, s_w3_k, s_w3_v,
     s_ffn_conv) = [jnp.stack(a, axis=0) for a in zip(*s_states)]
    return (y_prompt, y_sample,
            p_ssm_re, p_ssm_im, p_w1_k, p_w1_v, p_w2_k, p_w2_v, p_w3_k, p_w3_v, p_mem_k, p_mem_v, p_ffn_conv,
            s_ssm_re, s_ssm_im, s_w1_k, s_w1_v, s_w2_k, s_w2_v, s_w3_k, s_w3_v, s_ffn_conv)
```

```python
import functools

import jax
import jax.numpy as jnp
import numpy as np
from jax import lax
from jax.experimental import pallas as pl
from jax.experimental.pallas import tpu as pltpu

F32 = jnp.float32
BF16 = jnp.bfloat16

D_MODEL = 1024
SSM_WIDTH = 512
SSM_GROUP = 16
SSM_GROUPS = 32
SSM_STATE = 64
ATT_HEAD_DIM = 64
HEADS_PER_GROUP = 4
DIL_PAIRS = ((128, 1), (512, 4), (2048, 16))
ATT_HEADS = len(DIL_PAIRS) * HEADS_PER_GROUP
ATT_WIDTH = ATT_HEADS * ATT_HEAD_DIM
ATT_GROUP_WIDTH = HEADS_PER_GROUP * ATT_HEAD_DIM
ATT_STEPS = 128
N_MEM = 256
MEM_HEADS = 4
MEM_HEAD_DIM = 128
MEM_WIDTH = MEM_HEADS * MEM_HEAD_DIM
N_BRANCH = 3
D_FF = 2816
CONV_W = 3
EPS = 1e-6
IN_SPLITS = (SSM_WIDTH, ATT_WIDTH, ATT_WIDTH, ATT_WIDTH, MEM_WIDTH, N_BRANCH * D_MODEL)

LANES = 128
SUBLANES = 8
VMEM_LIMIT_BYTES = 56 * 1024 * 1024

SSM_T = 8
SSM_LANE_BLOCKS = SSM_WIDTH // LANES
SSM_GROUPS_PER_BLOCK = LANES // SSM_GROUP
SSM_STATES_PER_BLOCK = SSM_GROUPS_PER_BLOCK * SSM_STATE
SSM_STATE_LANES = 2 * SSM_STATES_PER_BLOCK

ROW_TILE = 256
FFN_CHUNK = 1408
NEG_BIG = -1e30


def _alibi_slopes():
    return [float(2.0 ** (-8.0 * h / ATT_HEADS)) for h in range(1, ATT_HEADS + 1)]


def _params(*sem):
    return pltpu.CompilerParams(dimension_semantics=sem, vmem_limit_bytes=VMEM_LIMIT_BYTES)


def _rmsnorm(x, g):
    ms = jnp.mean(x * x, axis=-1, keepdims=True)
    return x * lax.rsqrt(ms + EPS) * g


def _full(shape):
    nd = len(shape)
    return pl.BlockSpec(shape, lambda *_: (0,) * nd)


def _norm_proj_kernel(x_ref, g_ref, w_ref, *out_refs, splits):
    h = _rmsnorm(x_ref[...], g_ref[...]).astype(BF16)
    off = 0
    for o_ref, width in zip(out_refs, splits):
        for c0 in range(0, width, 512):
            cw = min(512, width - c0)
            o_ref[:, c0:c0 + cw] = jnp.dot(h, w_ref[:, off + c0:off + c0 + cw], preferred_element_type=F32)
        off += width


def _norm_proj(x, g, w_bf16, splits, tm):
    rows, d = x.shape
    assert rows % tm == 0 and sum(splits) == w_bf16.shape[1]
    return pl.pallas_call(
        functools.partial(_norm_proj_kernel, splits=splits),
        grid=(rows // tm,),
        in_specs=[pl.BlockSpec((tm, d), lambda i: (i, 0)), _full((1, d)), _full(w_bf16.shape)],
        out_specs=[pl.BlockSpec((tm, s), lambda i: (i, 0)) for s in splits],
        out_shape=[jax.ShapeDtypeStruct((rows, s), F32) for s in splits],
        compiler_params=_params("parallel"),
        name="norm_proj",
    )(x, g.reshape(1, d), w_bf16)


def _ssm_layout(a_re, a_im, log_dt, b_re, b_im, c_re, c_im):
    nb, gpb, p, c = SSM_LANE_BLOCKS, SSM_GROUPS_PER_BLOCK, SSM_STATE, SSM_GROUP
    rows = jnp.stack([a_re.reshape(nb, gpb * p), a_im.reshape(nb, gpb * p),
                      jnp.repeat(log_dt, p).reshape(nb, gpb * p)], axis=1)
    cols = rows.transpose(0, 2, 1)
    eye = jnp.eye(gpb, dtype=F32)

    def place_b(b):
        return jnp.einsum('bgpc,gh->bgchp', b.reshape(nb, gpb, p, c), eye).reshape(nb, gpb * c, gpb * p)

    def place_c(m):
        return jnp.einsum('bgcp,gh->bhpgc', m.reshape(nb, gpb, c, p), eye).reshape(nb, gpb * p, gpb * c)

    return rows, cols, place_b(b_re), place_b(b_im), place_c(c_re), place_c(c_im)


def _ssm_prep_kernel(rows_ref, cols_ref, bre_ref, bim_ref, cre_ref, cim_ref,
                     m_ref, w_ref, v_ref, v0_ref, arow_ref, *, t_chunk):
    sp = SSM_STATES_PER_BLOCK

    def apow(k, a_re, a_im, dt):
        mag = jnp.exp(a_re * dt * k)
        ang = a_im * dt * k
        return mag * jnp.cos(ang), mag * jnp.sin(ang)

    a_re, a_im, dt = rows_ref[0:1, :], rows_ref[1:2, :], jnp.exp(rows_ref[2:3, :])
    ab_re, ab_im = apow(1.0, a_re, a_im, dt)
    den = a_re * a_re + a_im * a_im
    q_re = ((ab_re - 1.0) * a_re + ab_im * a_im) / den
    q_im = (ab_im * a_re - (ab_re - 1.0) * a_im) / den
    bre, bim = bre_ref[...], bim_ref[...]
    bb_re = q_re * bre - q_im * bim
    bb_im = q_re * bim + q_im * bre
    cre, cim = cre_ref[...], cim_ref[...]

    m_ref[...] = jnp.zeros(m_ref.shape, m_ref.dtype)
    for k in range(t_chunk):
        pk_re, pk_im = apow(float(k), a_re, a_im, dt)
        bk_re = bb_re * pk_re - bb_im * pk_im
        bk_im = bb_re * pk_im + bb_im * pk_re
        t = t_chunk - 1 - k
        w_ref[t * LANES:(t + 1) * LANES, 0:sp] = bk_re.astype(w_ref.dtype)
        w_ref[t * LANES:(t + 1) * LANES, sp:2 * sp] = bk_im.astype(w_ref.dtype)
        kk = (jnp.dot(bk_re, cre, precision=lax.Precision.HIGHEST, preferred_element_type=F32)
              - jnp.dot(bk_im, cim, precision=lax.Precision.HIGHEST, preferred_element_type=F32))
        kk = kk.astype(m_ref.dtype)
        for t0 in range(t_chunk - k):
            m_ref[t0 * LANES:(t0 + 1) * LANES, (t0 + k) * LANES:(t0 + k + 1) * LANES] = kk

    ca_re, ca_im, cdt = cols_ref[:, 0:1], cols_ref[:, 1:2], jnp.exp(cols_ref[:, 2:3])
    for t in range(t_chunk):
        pc_re, pc_im = apow(float(t + 1), ca_re, ca_im, cdt)
        v_ref[0:sp, t * LANES:(t + 1) * LANES] = (cre * pc_re - cim * pc_im).astype(v_ref.dtype)
        v_ref[sp:2 * sp, t * LANES:(t + 1) * LANES] = (-(cre * pc_im + cim * pc_re)).astype(v_ref.dtype)
    v0_ref[0:sp, :] = cre.astype(v0_ref.dtype)
    v0_ref[sp:2 * sp, :] = (-cim).astype(v0_ref.dtype)

    pt_re, pt_im = apow(float(t_chunk), a_re, a_im, dt)
    arow_ref[0:1, 0:sp] = pt_re
    arow_ref[0:1, sp:2 * sp] = pt_im
    arow_ref[1:2, 0:sp] = ab_re
    arow_ref[1:2, sp:2 * sp] = ab_im


def _ssm_prep(a_re, a_im, log_dt, b_re, b_im, c_re, c_im):
    rows, cols, pbre, pbim, pcre, pcim = _ssm_layout(a_re, a_im, log_dt, b_re, b_im, c_re, c_im)
    nb, sp, sl, tl = SSM_LANE_BLOCKS, SSM_STATES_PER_BLOCK, SSM_STATE_LANES, SSM_T * LANES

    def blk(shape):
        return pl.BlockSpec((None,) + shape, lambda b: (b, 0, 0))

    return pl.pallas_call(
        functools.partial(_ssm_prep_kernel, t_chunk=SSM_T),
        grid=(nb,),
        in_specs=[blk((3, sp)), blk((sp, 3)), blk((LANES, sp)), blk((LANES, sp)), blk((sp, LANES)), blk((sp, LANES))],
        out_specs=[blk((tl, tl)), blk((tl, sl)), blk((sl, tl)), blk((sl, LANES)), blk((2, sl))],
        out_shape=[jax.ShapeDtypeStruct((nb, tl, tl), BF16), jax.ShapeDtypeStruct((nb, tl, sl), BF16),
                   jax.ShapeDtypeStruct((nb, sl, tl), BF16), jax.ShapeDtypeStruct((nb, sl, LANES), BF16),
                   jax.ShapeDtypeStruct((nb, 2, sl), F32)],
        compiler_params=_params("parallel"),
        name="ssm_prep",
    )(rows, cols, pbre, pbim, pcre, pcim)


def _chunk_lanes(uc_ref, b, t_chunk):
    return jnp.concatenate(
        [uc_ref[:, t * SSM_WIDTH + b * LANES:t * SSM_WIDTH + (b + 1) * LANES] for t in range(t_chunk)], axis=1)


def _ssm_end_state_kernel(uc_ref, w_ref, e_ref, *, t_chunk):
    for b in range(SSM_LANE_BLOCKS):
        ub = _chunk_lanes(uc_ref, b, t_chunk).astype(BF16)
        e_ref[b] = jnp.dot(ub, w_ref[b], preferred_element_type=F32)


def _ssm_scan_kernel(ere_ref, eim_ref, are_ref, aim_ref, spre_ref, spim_ref, fre_ref, fim_ref, *, n_seq, n_chunks):
    a_re = are_ref[0:1, :]
    a_im = aim_ref[0:1, :]

    def body(c, carry):
        s_re, s_im = carry
        rows = pl.ds(c, n_seq, stride=n_chunks)
        spre_ref[rows, :] = s_re
        spim_ref[rows, :] = s_im
        return (a_re * s_re - a_im * s_im + ere_ref[rows, :], a_re * s_im + a_im * s_re + eim_ref[rows, :])

    zero = jnp.zeros((n_seq, LANES), F32)
    s_re, s_im = lax.fori_loop(0, n_chunks, body, (zero, zero))
    fre_ref[...] = s_re
    fim_ref[...] = s_im


def _ssm_output_kernel(uc_ref, spre_ref, spim_ref, m_ref, v_ref, d_ref, y_ref, *, t_chunk):
    for b in range(SSM_LANE_BLOCKS):
        ub = _chunk_lanes(uc_ref, b, t_chunk).astype(BF16)
        yb = jnp.dot(ub, m_ref[b], preferred_element_type=F32)
        sprev = jnp.concatenate([spre_ref[b], spim_ref[b]], axis=1).astype(BF16)
        yb = yb + jnp.dot(sprev, v_ref[b], preferred_element_type=F32)
        d = d_ref[:, b * LANES:(b + 1) * LANES]
        for t in range(t_chunk):
            col = slice(t * SSM_WIDTH + b * LANES, t * SSM_WIDTH + (b + 1) * LANES)
            y_ref[:, col] = yb[:, t * LANES:(t + 1) * LANES] + d * uc_ref[:, col]


def _ssm_prompt(u, n_seq, m_mat, w_mat, v_mat, arow, d_skip):
    rows, t_chunk, nb, sl = u.shape[0], SSM_T, SSM_LANE_BLOCKS, SSM_STATE_LANES
    assert rows % (t_chunk * n_seq) == 0
    n_rows = rows // t_chunk
    n_chunks = n_rows // n_seq
    tr = min(ROW_TILE, n_rows)
    assert n_rows % tr == 0
    uc = u.reshape(n_rows, t_chunk * SSM_WIDTH)
    tl = t_chunk * LANES
    e = pl.pallas_call(
        functools.partial(_ssm_end_state_kernel, t_chunk=t_chunk),
        grid=(n_rows // tr,),
        in_specs=[pl.BlockSpec((tr, t_chunk * SSM_WIDTH), lambda i: (i, 0)), _full(w_mat.shape)],
        out_specs=pl.BlockSpec((nb, tr, sl), lambda i: (0, i, 0)),
        out_shape=jax.ShapeDtypeStruct((nb, n_rows, sl), F32),
        compiler_params=_params("parallel"),
        name="ssm_end_state",
    )(uc, w_mat)
    sp = SSM_STATES_PER_BLOCK
    lane_chunks = sp // LANES

    def re_part(shape_rows):
        return pl.BlockSpec((None, shape_rows, LANES), lambda b, j: (b, 0, j))

    def im_part(shape_rows):
        return pl.BlockSpec((None, shape_rows, LANES), lambda b, j: (b, 0, lane_chunks + j))

    sp_re, sp_im, fin_re, fin_im = pl.pallas_call(
        functools.partial(_ssm_scan_kernel, n_seq=n_seq, n_chunks=n_chunks),
        grid=(nb, lane_chunks),
        in_specs=[re_part(n_rows), im_part(n_rows), re_part(2), im_part(2)],
        out_specs=[re_part(n_rows), re_part(n_rows), re_part(n_seq), re_part(n_seq)],
        out_shape=[jax.ShapeDtypeStruct((nb, n_rows, sp), F32), jax.ShapeDtypeStruct((nb, n_rows, sp), F32),
                   jax.ShapeDtypeStruct((nb, n_seq, sp), F32), jax.ShapeDtypeStruct((nb, n_seq, sp), F32)],
        compiler_params=_params("parallel", "parallel"),
        name="ssm_scan",
    )(e, e, arow, arow)
    sp_spec = pl.BlockSpec((nb, tr, sp), lambda i: (0, i, 0))
    y = pl.pallas_call(
        functools.partial(_ssm_output_kernel, t_chunk=t_chunk),
        grid=(n_rows // tr,),
        in_specs=[pl.BlockSpec((tr, t_chunk * SSM_WIDTH), lambda i: (i, 0)), sp_spec, sp_spec,
                  _full(m_mat.shape), _full(v_mat.shape), _full((1, SSM_WIDTH))],
        out_specs=pl.BlockSpec((tr, t_chunk * SSM_WIDTH), lambda i: (i, 0)),
        out_shape=jax.ShapeDtypeStruct((n_rows, t_chunk * SSM_WIDTH), F32),
        compiler_params=_params("parallel"),
        name="ssm_output",
    )(uc, sp_re, sp_im, m_mat, v_mat, d_skip.reshape(1, SSM_WIDTH))
    return y.reshape(rows, SSM_WIDTH), fin_re, fin_im


def _ssm_step_kernel(u_ref, sre_ref, sim_ref, w0_ref, v0_ref, arow_ref, d_ref, y_ref, nre_ref, nim_ref):
    sp = SSM_STATES_PER_BLOCK
    for b in range(SSM_LANE_BLOCKS):
        lanes = slice(b * LANES, (b + 1) * LANES)
        states = slice(b * sp, (b + 1) * sp)
        u = u_ref[:, lanes]
        e = jnp.dot(u.astype(BF16), w0_ref[b], preferred_element_type=F32)
        a_re, a_im = arow_ref[b, 1:2, 0:sp], arow_ref[b, 1:2, sp:2 * sp]
        s_re, s_im = sre_ref[:, states], sim_ref[:, states]
        n_re = a_re * s_re - a_im * s_im + e[:, 0:sp]
        n_im = a_re * s_im + a_im * s_re + e[:, sp:2 * sp]
        nre_ref[:, states] = n_re
        nim_ref[:, states] = n_im
        sn = jnp.concatenate([n_re, n_im], axis=1).astype(BF16)
        y_ref[:, lanes] = jnp.dot(sn, v0_ref[b], preferred_element_type=F32) + d_ref[:, lanes] * u


def _ssm_step(u, s_re, s_im, w0, v0, arow, d_skip):
    rows = u.shape[0]
    ns = SSM_GROUPS * SSM_STATE
    args = (u, s_re.reshape(rows, ns), s_im.reshape(rows, ns), w0, v0, arow, d_skip.reshape(1, SSM_WIDTH))
    return pl.pallas_call(
        _ssm_step_kernel,
        grid=(1,),
        in_specs=[_full(a.shape) for a in args],
        out_specs=[_full((rows, SSM_WIDTH)), _full((rows, ns)), _full((rows, ns))],
        out_shape=[jax.ShapeDtypeStruct((rows, SSM_WIDTH), F32), jax.ShapeDtypeStruct((rows, ns), F32),
                   jax.ShapeDtypeStruct((rows, ns), F32)],
        compiler_params=_params("arbitrary"),
        name="ssm_step",
    )(*args)


def _dil_attn_kernel(q_ref, kc_ref, kp_ref, vc_ref, vp_ref, o_ref, *, dil, slopes):
    steps, gw, hd = ATT_STEPS, ATT_GROUP_WIDTH, ATT_HEAD_DIM
    j = pl.program_id(2)
    q = q_ref[...]
    kk = jnp.concatenate([kp_ref[...], kc_ref[...]], axis=0).astype(BF16)
    vv = jnp.concatenate([vp_ref[...], vc_ref[...]], axis=0).astype(BF16)
    qi = lax.broadcasted_iota(jnp.int32, (steps, 2 * steps), 0)
    kj = lax.broadcasted_iota(jnp.int32, (steps, 2 * steps), 1)
    dist = qi + steps - kj
    valid = (dist >= 0) & (dist <= steps) & ((j > 0) | (kj >= steps))
    distf = (dist * dil).astype(F32)
    lane = lax.broadcasted_iota(jnp.int32, (steps, gw), 1)
    out = jnp.zeros((steps, gw), F32)
    lse = jnp.zeros((steps, gw), F32)
    for h in range(HEADS_PER_GROUP):
        head = (lane >= h * hd) & (lane < (h + 1) * hd)
        qh = jnp.where(head, q, 0.0).astype(BF16)
        s = lax.dot_general(qh, kk, (((1,), (1,)), ((), ())), preferred_element_type=F32) * (hd ** -0.5)
        s = jnp.where(valid, s - slopes[h] * distf, NEG_BIG)
        m = jnp.max(s, axis=-1, keepdims=True)
        p = jnp.exp(s - m)
        den = jnp.sum(p, axis=-1, keepdims=True)
        oh = jnp.dot(p.astype(BF16), vv, preferred_element_type=F32) / den
        out = jnp.where(head, oh, out)
        lse = jnp.where(head, m + jnp.log(den), lse)
    o_ref[:, 0:gw] = out
    o_ref[:, gw:2 * gw] = lse


def _dil_attn(q, k, v, n_seq, seq_len, group):
    win, dil = DIL_PAIRS[group]
    steps, gw = ATT_STEPS, ATT_GROUP_WIDTH
    assert win // dil == steps and seq_len % (dil * steps) == 0
    sub = seq_len // dil
    blocks_per_token = ATT_WIDTH // gw

    def view(x):
        return x.reshape(n_seq, sub, dil * ATT_WIDTH)

    cur = pl.BlockSpec((None, steps, gw), lambda n, r, j: (n, j, blocks_per_token * r + group))
    prev = pl.BlockSpec((None, steps, gw), lambda n, r, j: (n, jnp.maximum(j - 1, 0), blocks_per_token * r + group))
    slopes = tuple(_alibi_slopes()[group * HEADS_PER_GROUP:(group + 1) * HEADS_PER_GROUP])
    out = pl.pallas_call(
        functools.partial(_dil_attn_kernel, dil=dil, slopes=slopes),
        grid=(n_seq, dil, sub // steps),
        in_specs=[cur, cur, prev, cur, prev],
        out_specs=pl.BlockSpec((None, steps, 2 * gw), lambda n, r, j: (n, j, r)),
        out_shape=jax.ShapeDtypeStruct((n_seq, sub, dil * 2 * gw), F32),
        compiler_params=_params("parallel", "parallel", "parallel"),
        name=f"dil_attn_w{win}",
    )(view(q), view(k), view(k), view(v), view(v))
    return out.reshape(n_seq * seq_len, 2 * gw)


def _mem_attn_kernel(q_ref, mk_ref, mv_ref, o_ref):
    hd = MEM_HEAD_DIM
    for h in range(MEM_HEADS):
        lanes = slice(h * hd, (h + 1) * hd)
        s = lax.dot_general(q_ref[:, lanes].astype(BF16), mk_ref[:, lanes].astype(BF16),
                            (((1,), (1,)), ((), ())), preferred_element_type=F32) * (hd ** -0.5)
        m = jnp.max(s, axis=-1, keepdims=True)
        p = jnp.exp(s - m)
        den = jnp.sum(p, axis=-1, keepdims=True)
        o_ref[:, lanes] = jnp.dot(p.astype(BF16), mv_ref[:, lanes].astype(BF16), preferred_element_type=F32) / den


def _mem_attn(qm, mk, mv, n_seq, seq_len, tm):
    assert seq_len % tm == 0
    tiles = seq_len // tm
    kv = pl.BlockSpec((None, N_MEM, MEM_WIDTH), lambda n, i: (n, 0, 0))
    return pl.pallas_call(
        _mem_attn_kernel,
        grid=(n_seq, tiles),
        in_specs=[pl.BlockSpec((tm, MEM_WIDTH), lambda n, i: (n * tiles + i, 0)), kv, kv],
        out_specs=pl.BlockSpec((tm, MEM_WIDTH), lambda n, i: (n * tiles + i, 0)),
        out_shape=jax.ShapeDtypeStruct(qm.shape, F32),
        compiler_params=_params("parallel", "parallel"),
        name="mem_attn",
    )(qm, mk, mv)


def _head_sum_matrix(width, head_dim):
    head = np.arange(width) // head_dim
    return jnp.asarray(head[:, None] == head[None, :], dtype=BF16)


def _row_attention(q3, k_blocks, v_blocks, biases, head_sum, scale):
    nb, _, width = q3.shape
    scores = []
    for k, bias in zip(k_blocks, biases):
        rows = k.shape[1]
        p = (k * q3).astype(BF16).reshape(nb * rows, width)
        s = jnp.dot(p, head_sum, preferred_element_type=F32).reshape(nb, rows, width) * scale
        scores.append(s if bias is None else s + bias)
    m = functools.reduce(jnp.maximum, [jnp.max(s, axis=1, keepdims=True) for s in scores])
    den = jnp.zeros((nb, 1, width), F32)
    acc = jnp.zeros((nb, 1, width), F32)
    for s, v in zip(scores, v_blocks):
        p = jnp.exp(s - m)
        den = den + jnp.sum(p, axis=1, keepdims=True)
        acc = acc + jnp.sum(p * v, axis=1, keepdims=True)
    return acc / den, m + jnp.log(den)


def _cache_attn_kernel(q_ref, kn_ref, vn_ref, k1_ref, v1_ref, k2_ref, v2_ref, k3_ref, v3_ref, hs_ref,
                       o_ref, *, slopes):
    gw, hd, steps = ATT_GROUP_WIDTH, ATT_HEAD_DIM, ATT_STEPS
    nb = q_ref.shape[0]
    head_sum = hs_ref[...]
    lane_head = lax.broadcasted_iota(jnp.int32, (steps, gw), 1) // hd
    row = lax.broadcasted_iota(jnp.int32, (steps, gw), 0)
    outs, lses = [], []
    for g, (k_ref, v_ref) in enumerate(((k1_ref, v1_ref), (k2_ref, v2_ref), (k3_ref, v3_ref))):
        dil = DIL_PAIRS[g][1]
        lanes = slice(g * gw, (g + 1) * gw)
        slope = jnp.zeros((steps, gw), F32)
        for h in range(HEADS_PER_GROUP):
            slope = jnp.where(lane_head == h, slopes[g * HEADS_PER_GROUP + h], slope)
        bias = -slope * ((steps - row) * dil).astype(F32)
        kn = jnp.broadcast_to(kn_ref[:, :, lanes], (nb, SUBLANES, gw))
        vn = jnp.broadcast_to(vn_ref[:, :, lanes], (nb, SUBLANES, gw))
        new_bias = jnp.where(lax.broadcasted_iota(jnp.int32, (SUBLANES, gw), 0) == 0, 0.0, NEG_BIG)
        o, lse = _row_attention(q_ref[:, :, lanes], [k_ref[...], kn], [v_ref[...], vn], [bias, new_bias],
                                head_sum, hd ** -0.5)
        outs.append(o)
        lses.append(lse)
    top = functools.reduce(jnp.maximum, lses)
    ws = [jnp.exp(l - top) for l in lses]
    o_ref[...] = sum(w * o for w, o in zip(ws, outs)) / sum(ws)


def _cache_attn(q, k_new, v_new, caches_k, caches_v, nb):
    rows = q.shape[0]
    gw, steps = ATT_GROUP_WIDTH, ATT_STEPS
    assert rows % nb == 0
    new = pl.BlockSpec((nb, 1, ATT_WIDTH), lambda i: (i, 0, 0))
    cache_specs, cache_args = [], []
    for g, (win, dil) in enumerate(DIL_PAIRS):
        for c in (caches_k[g], caches_v[g]):
            assert c.shape == (rows, win, HEADS_PER_GROUP, ATT_HEAD_DIM), c.shape
            cache_args.append(c.reshape(rows, steps, dil * gw))
            cache_specs.append(pl.BlockSpec((nb, steps, gw), lambda i: (i, 0, 0)))
    head_sum = _head_sum_matrix(gw, ATT_HEAD_DIM)
    out = pl.pallas_call(
        functools.partial(_cache_attn_kernel, slopes=tuple(_alibi_slopes())),
        grid=(rows // nb,),
        in_specs=[new, new, new] + cache_specs + [_full(head_sum.shape)],
        out_specs=pl.BlockSpec((nb, 1, gw), lambda i: (i, 0, 0)),
        out_shape=jax.ShapeDtypeStruct((rows, 1, gw), F32),
        compiler_params=_params("parallel"),
        name="cache_attn",
    )(q.reshape(rows, 1, ATT_WIDTH), k_new.reshape(rows, 1, ATT_WIDTH), v_new.reshape(rows, 1, ATT_WIDTH),
      *cache_args, head_sum)
    return out.reshape(rows, gw)


def _cache_mem_attn_kernel(q_ref, k_ref, v_ref, hs_ref, o_ref):
    half = hs_ref.shape[0]
    for c in range(MEM_WIDTH // half):
        lanes = slice(c * half, (c + 1) * half)
        o, _ = _row_attention(q_ref[:, :, lanes], [k_ref[:, :, lanes]], [v_ref[:, :, lanes]], [None],
                              hs_ref[...], MEM_HEAD_DIM ** -0.5)
        o_ref[:, :, lanes] = o


def _cache_mem_attn(qm, mem_k, mem_v, nb):
    rows = qm.shape[0]
    assert rows % nb == 0
    head_sum = _head_sum_matrix(2 * MEM_HEAD_DIM, MEM_HEAD_DIM)
    kv = pl.BlockSpec((nb, N_MEM, MEM_WIDTH), lambda i: (i, 0, 0))
    q = pl.BlockSpec((nb, 1, MEM_WIDTH), lambda i: (i, 0, 0))
    out = pl.pallas_call(
        _cache_mem_attn_kernel,
        grid=(rows // nb,),
        in_specs=[q, kv, kv, _full(head_sum.shape)],
        out_specs=q,
        out_shape=jax.ShapeDtypeStruct((rows, 1, MEM_WIDTH), F32),
        compiler_params=_params("parallel"),
        name="cache_mem_attn",
    )(qm.reshape(rows, 1, MEM_WIDTH), mem_k.reshape(rows, N_MEM, MEM_WIDTH),
      mem_v.reshape(rows, N_MEM, MEM_WIDTH), head_sum)
    return out.reshape(rows, MEM_WIDTH)


def _merge_kernel(x_ref, y_ref, *refs, n_att):
    att_refs = refs[:n_att]
    mem_ref, gl_ref, wglu_ref, watt_ref, wmem_ref, wout_ref, o_ref = refs[n_att:]
    d, gw = D_MODEL, ATT_GROUP_WIDTH
    z = jax.nn.gelu(y_ref[...]).astype(BF16)
    glu = jnp.dot(z, wglu_ref[...], preferred_element_type=F32)
    merged = jax.nn.sigmoid(gl_ref[:, 0:d]) * (glu[:, 0:d] * jax.nn.sigmoid(glu[:, d:2 * d]))
    if n_att == 1:
        att = att_refs[0][...]
    else:
        lses = [r[:, gw:2 * gw] for r in att_refs]
        top = functools.reduce(jnp.maximum, lses)
        ws = [jnp.exp(l - top) for l in lses]
        att = sum(w * r[:, 0:gw] for w, r in zip(ws, att_refs)) / sum(ws)
    b_att = jnp.dot(att.astype(BF16), watt_ref[...], preferred_element_type=F32)
    merged = merged + jax.nn.sigmoid(gl_ref[:, d:2 * d]) * b_att
    b_mem = jnp.dot(mem_ref[...].astype(BF16), wmem_ref[...], preferred_element_type=F32)
    merged = merged + jax.nn.sigmoid(gl_ref[:, 2 * d:3 * d]) * b_mem
    o_ref[...] = x_ref[...] + jnp.dot(merged.astype(BF16), wout_ref[...], preferred_element_type=F32)


def _merge(x, y_ssm, atts, o_mem, gl, w_glu, w_att_o, w_mem_o, w_out, tm):
    rows = x.shape[0]
    assert rows % tm == 0

    def tile(a):
        return pl.BlockSpec((tm, a.shape[1]), lambda i: (i, 0))

    acts = (x, y_ssm, *atts, o_mem, gl)
    weights = (w_glu, w_att_o, w_mem_o, w_out)
    return pl.pallas_call(
        functools.partial(_merge_kernel, n_att=len(atts)),
        grid=(rows // tm,),
        in_specs=[tile(a) for a in acts] + [_full(w.shape) for w in weights],
        out_specs=tile(x),
        out_shape=jax.ShapeDtypeStruct(x.shape, F32),
        compiler_params=_params("parallel"),
        name="merge",
    )(*acts, *weights)


def _ffn_kernel(*refs, tm, tiles_per_seq, stepwise):
    if stepwise:
        x_ref, g2_ref, wup_ref, cw_ref, cb_ref, wdn_ref, gf_ref, prev_ref, y_ref, conv_ref = refs
    else:
        x_ref, g2_ref, wup_ref, cw_ref, cb_ref, wdn_ref, gf_ref, y_ref, conv_ref, a_scr = refs
        i = pl.program_id(0)
        first = i % tiles_per_seq == 0

        @pl.when(first)
        def _():
            a_scr[0:SUBLANES, :] = jnp.zeros((SUBLANES, D_FF), F32)

        @pl.when(jnp.logical_not(first))
        def _():
            a_scr[0:SUBLANES, :] = a_scr[tm:tm + SUBLANES, :]

    x = x_ref[...]
    h = _rmsnorm(x, g2_ref[...]).astype(BF16)
    acc = jnp.zeros((tm, D_MODEL), F32)
    for c0 in range(0, D_FF, FFN_CHUNK):
        cols = slice(c0, c0 + FFN_CHUNK)
        a = jnp.dot(h, wup_ref[:, cols], preferred_element_type=F32)
        if stepwise:
            a2, a1 = prev_ref[:, cols], prev_ref[:, D_FF + c0:D_FF + c0 + FFN_CHUNK]
            conv_ref[:, cols] = a1
            conv_ref[:, D_FF + c0:D_FF + c0 + FFN_CHUNK] = a
        else:
            a_scr[SUBLANES:SUBLANES + tm, cols] = a
            a1 = a_scr[SUBLANES - 1:SUBLANES - 1 + tm, cols]
            a2 = a_scr[SUBLANES - 2:SUBLANES - 2 + tm, cols]
        c = a2 * cw_ref[0:1, cols] + a1 * cw_ref[1:2, cols] + a * cw_ref[2:3, cols] + cb_ref[:, cols]
        v = jnp.dot(h, wup_ref[:, D_FF + c0:D_FF + c0 + FFN_CHUNK], preferred_element_type=F32)
        acc = acc + jnp.dot((jax.nn.gelu(c) * v).astype(BF16), wdn_ref[cols, :], preferred_element_type=F32)
    y_ref[...] = _rmsnorm(x + acc, gf_ref[...])
    if not stepwise:
        @pl.when(i % tiles_per_seq == tiles_per_seq - 1)
        def _():
            conv_ref[...] = a_scr[SUBLANES + tm - (CONV_W - 1):SUBLANES + tm, :]


def _ffn(x, g2, w_up, conv_w, conv_b, w_down, gf, tm, n_seq=None, prev=None):
    rows, d = x.shape
    assert rows % tm == 0 and D_FF % FFN_CHUNK == 0 and FFN_CHUNK % LANES == 0
    stepwise = prev is not None
    weights = (g2.reshape(1, d), w_up, conv_w, conv_b.reshape(1, D_FF), w_down, gf.reshape(1, d))
    in_specs = [pl.BlockSpec((tm, d), lambda i: (i, 0))] + [_full(w.shape) for w in weights]
    args = (x,) + weights
    if stepwise:
        tiles_per_seq = 1
        in_specs.append(pl.BlockSpec((tm, 2 * D_FF), lambda i: (i, 0)))
        args += (prev,)
        conv_spec = pl.BlockSpec((tm, 2 * D_FF), lambda i: (i, 0))
        conv_shape = jax.ShapeDtypeStruct((rows, 2 * D_FF), F32)
        scratch = []
    else:
        tiles_per_seq = rows // n_seq // tm
        assert tiles_per_seq * tm * n_seq == rows
        conv_spec = pl.BlockSpec((None, CONV_W - 1, D_FF), lambda i: (i // tiles_per_seq, 0, 0))
        conv_shape = jax.ShapeDtypeStruct((n_seq, CONV_W - 1, D_FF), F32)
        scratch = [pltpu.VMEM((tm + 2 * SUBLANES, D_FF), F32)]
    return pl.pallas_call(
        functools.partial(_ffn_kernel, tm=tm, tiles_per_seq=tiles_per_seq, stepwise=stepwise),
        grid=(rows // tm,),
        in_specs=in_specs,
        out_specs=[pl.BlockSpec((tm, d), lambda i: (i, 0)), conv_spec],
        out_shape=[jax.ShapeDtypeStruct((rows, d), F32), conv_shape],
        scratch_shapes=scratch,
        compiler_params=_params("arbitrary"),
        name="ffn_step" if stepwise else "ffn",
    )(*args)


def kernel(x_prompt, x_sample, state_ssm_re, state_ssm_im, cache_w1_k, cache_w1_v, cache_w2_k, cache_w2_v, cache_w3_k, cache_w3_v, cache_mem_k, cache_mem_v, state_ffn_conv, mem_prompt, norm1_g, w_in, ssm_a_re, ssm_a_im, ssm_log_dt, ssm_b_re, ssm_b_im, ssm_c_re, ssm_c_im, ssm_d, w_ssm_glu, w_att_o, mem_norm_g, w_mem_kv, w_mem_o, w_out, norm2_g, w_up, ffn_conv_w, ffn_conv_b, w_down, final_norm_g):
    n_seq, seq_len, d = x_prompt.shape
    n_dec, dec_len, _ = x_sample.shape
    depth = norm1_g.shape[0]
    assert d == D_MODEL and depth == 1 and dec_len == 1
    assert w_in.shape == (depth, D_MODEL, sum(IN_SPLITS)) and w_up.shape == (depth, D_MODEL, 2 * D_FF)
    assert mem_prompt.shape == (n_seq, N_MEM, D_MODEL)
    assert ssm_a_re.shape == (depth, SSM_GROUPS, SSM_STATE)
    assert seq_len % ROW_TILE == 0 and n_dec % SUBLANES == 0
    hpg, hd = HEADS_PER_GROUP, ATT_HEAD_DIM
    rows_p = n_seq * seq_len
    dec_tile = n_dec if n_dec <= ROW_TILE else ROW_TILE

    wb = lambda w: w[0].astype(BF16)
    w_in_b, w_glu_b, w_att_b, w_memkv_b = wb(w_in), wb(w_ssm_glu), wb(w_att_o), wb(w_mem_kv)
    w_memo_b, w_out_b, w_up_b, w_down_b = wb(w_mem_o), wb(w_out), wb(w_up), wb(w_down)

    m_mat, w_mat, v_mat, v0_mat, arow = _ssm_prep(ssm_a_re[0], ssm_a_im[0], ssm_log_dt[0], ssm_b_re[0],
                                                  ssm_b_im[0], ssm_c_re[0], ssm_c_im[0])
    w0_mat = w_mat[:, (SSM_T - 1) * LANES:, :]

    xp = x_prompt.reshape(rows_p, d)
    u, q, k, v, qm, gl = _norm_proj(xp, norm1_g[0], w_in_b, IN_SPLITS, ROW_TILE)
    y_ssm, fin_re, fin_im = _ssm_prompt(u, n_seq, m_mat, w_mat, v_mat, arow, ssm_d[0])
    atts = [_dil_attn(q, k, v, n_seq, seq_len, g) for g in range(len(DIL_PAIRS))]
    mk, mv = _norm_proj(mem_prompt.reshape(n_seq * N_MEM, d), mem_norm_g[0], w_memkv_b, (MEM_WIDTH, MEM_WIDTH),
                        ROW_TILE)
    o_mem = _mem_attn(qm, mk.reshape(n_seq, N_MEM, MEM_WIDTH), mv.reshape(n_seq, N_MEM, MEM_WIDTH),
                      n_seq, seq_len, 2 * ROW_TILE)
    x1 = _merge(xp, y_ssm, atts, o_mem, gl, w_glu_b, w_att_b, w_memo_b, w_out_b, ROW_TILE)
    y_p, conv_p = _ffn(x1, norm2_g[0], w_up_b, ffn_conv_w[0], ffn_conv_b[0], w_down_b, final_norm_g,
                       ROW_TILE, n_seq=n_seq)

    def final_state(s):
        s = s.reshape(SSM_LANE_BLOCKS, n_seq, SSM_GROUPS_PER_BLOCK, SSM_STATE)
        return s.transpose(1, 0, 2, 3).reshape(1, n_seq, SSM_GROUPS, SSM_STATE)
    k4 = k.reshape(n_seq, seq_len, ATT_HEADS, hd)
    v4 = v.reshape(n_seq, seq_len, ATT_HEADS, hd)
    win_p = []
    for g, (win, _) in enumerate(DIL_PAIRS):
        keep = min(win, seq_len)
        win_p += [k4[None, :, seq_len - keep:, g * hpg:(g + 1) * hpg], v4[None, :, seq_len - keep:, g * hpg:(g + 1) * hpg]]
    mem_kv = [mk.reshape(1, n_seq, N_MEM, MEM_HEADS, MEM_HEAD_DIM), mv.reshape(1, n_seq, N_MEM, MEM_HEADS, MEM_HEAD_DIM)]

    xs = x_sample.reshape(n_dec, d)
    us, qs, ks, vs, qms, gls = _norm_proj(xs, norm1_g[0], w_in_b, IN_SPLITS, dec_tile)
    ys_ssm, sn_re, sn_im = _ssm_step(us, state_ssm_re[0], state_ssm_im[0], w0_mat, v0_mat, arow, ssm_d[0])
    att_s = _cache_attn(qs, ks, vs, (cache_w1_k[0], cache_w2_k[0], cache_w3_k[0]),
                        (cache_w1_v[0], cache_w2_v[0], cache_w3_v[0]), SUBLANES)
    mem_s = _cache_mem_attn(qms, cache_mem_k[0], cache_mem_v[0], SUBLANES)
    xs1 = _merge(xs, ys_ssm, [att_s], mem_s, gls, w_glu_b, w_att_b, w_memo_b, w_out_b, dec_tile)
    y_s, conv_s = _ffn(xs1, norm2_g[0], w_up_b, ffn_conv_w[0], ffn_conv_b[0], w_down_b, final_norm_g,
                       dec_tile, prev=state_ffn_conv[0].reshape(n_dec, (CONV_W - 1) * D_FF))

    ks4 = ks.reshape(1, n_dec, 1, ATT_HEADS, hd)
    vs4 = vs.reshape(1, n_dec, 1, ATT_HEADS, hd)
    win_s = []
    for g in range(len(DIL_PAIRS)):
        win_s += [ks4[:, :, :, g * hpg:(g + 1) * hpg], vs4[:, :, :, g * hpg:(g + 1) * hpg]]

    return (y_p.reshape(n_seq, seq_len, d), y_s.reshape(n_dec, 1, d),
            final_state(fin_re), final_state(fin_im), *win_p, *mem_kv, conv_p[None],
            sn_re.reshape(1, n_dec, SSM_GROUPS, SSM_STATE), sn_im.reshape(1, n_dec, SSM_GROUPS, SSM_STATE),
            *win_s, conv_s.reshape(1, n_dec, CONV_W - 1, D_FF))
```

```python
import functools

import jax
import jax.numpy as jnp
from jax import lax
from jax.experimental import pallas as pl
from jax.experimental.pallas import tpu as pltpu

F32 = jnp.float32
BF16 = jnp.bfloat16

D_MODEL = 1024
SSM_WIDTH = 512
SSM_GROUP = 16
SSM_GROUPS = 32
SSM_STATE = 64
ATT_HEAD_DIM = 64
HEADS_PER_GROUP = 4
DIL_PAIRS = ((128, 1), (512, 4), (2048, 16))
ATT_HEADS = len(DIL_PAIRS) * HEADS_PER_GROUP
ATT_WIDTH = ATT_HEADS * ATT_HEAD_DIM
ATT_GROUP_WIDTH = HEADS_PER_GROUP * ATT_HEAD_DIM
ATT_STEPS = 128
N_MEM = 256
MEM_HEADS = 4
MEM_HEAD_DIM = 128
MEM_WIDTH = MEM_HEADS * MEM_HEAD_DIM
N_BRANCH = 3
D_FF = 2816
CONV_W = 3
EPS = 1e-6
IN_SPLITS = (SSM_WIDTH, ATT_WIDTH, ATT_WIDTH, ATT_WIDTH, MEM_WIDTH, N_BRANCH * D_MODEL)

LANES = 128
SUBLANES = 8
VMEM_LIMIT_BYTES = 56 * 1024 * 1024

SSM_T = 8
SSM_LANE_BLOCKS = SSM_WIDTH // LANES
SSM_GROUPS_PER_BLOCK = LANES // SSM_GROUP
SSM_STATES_PER_BLOCK = SSM_GROUPS_PER_BLOCK * SSM_STATE
SSM_STATE_LANES = 2 * SSM_STATES_PER_BLOCK

ROW_TILE = 256
FFN_CHUNK = 1408
CACHE_ROWS_PER_STEP = 4
NEG_BIG = -1e30


def _alibi_slopes():
    return [float(2.0 ** (-8.0 * h / ATT_HEADS)) for h in range(1, ATT_HEADS + 1)]


def _params(*sem):
    return pltpu.CompilerParams(dimension_semantics=sem, vmem_limit_bytes=VMEM_LIMIT_BYTES)


def _rmsnorm(x, g):
    ms = jnp.mean(x * x, axis=-1, keepdims=True)
    return x * lax.rsqrt(ms + EPS) * g


def _full(shape):
    nd = len(shape)
    return pl.BlockSpec(shape, lambda *_: (0,) * nd)


def _norm_proj_kernel(x_ref, g_ref, w_ref, *out_refs, splits):
    h = _rmsnorm(x_ref[...], g_ref[...]).astype(BF16)
    off = 0
    for o_ref, width in zip(out_refs, splits):
        for c0 in range(0, width, 512):
            cw = min(512, width - c0)
            o_ref[:, c0:c0 + cw] = jnp.dot(h, w_ref[:, off + c0:off + c0 + cw], preferred_element_type=F32)
        off += width


def _norm_proj(x, g, w_bf16, splits, tm):
    rows, d = x.shape
    assert rows % tm == 0 and sum(splits) == w_bf16.shape[1]
    return pl.pallas_call(
        functools.partial(_norm_proj_kernel, splits=splits),
        grid=(rows // tm,),
        in_specs=[pl.BlockSpec((tm, d), lambda i: (i, 0)), _full((1, d)), _full(w_bf16.shape)],
        out_specs=[pl.BlockSpec((tm, s), lambda i: (i, 0)) for s in splits],
        out_shape=[jax.ShapeDtypeStruct((rows, s), F32) for s in splits],
        compiler_params=_params("parallel"),
        name="norm_proj",
    )(x, g.reshape(1, d), w_bf16)


def _ssm_layout(a_re, a_im, log_dt, b_re, b_im, c_re, c_im):
    nb, gpb, p, c = SSM_LANE_BLOCKS, SSM_GROUPS_PER_BLOCK, SSM_STATE, SSM_GROUP
    rows = jnp.stack([a_re.reshape(nb, gpb * p), a_im.reshape(nb, gpb * p),
                      jnp.repeat(log_dt, p).reshape(nb, gpb * p)], axis=1)
    cols = rows.transpose(0, 2, 1)
    eye = jnp.eye(gpb, dtype=F32)

    def place_b(b):
        return jnp.einsum('bgpc,gh->bgchp', b.reshape(nb, gpb, p, c), eye).reshape(nb, gpb * c, gpb * p)

    def place_c(m):
        return jnp.einsum('bgcp,gh->bhpgc', m.reshape(nb, gpb, c, p), eye).reshape(nb, gpb * p, gpb * c)

    return rows, cols, place_b(b_re), place_b(b_im), place_c(c_re), place_c(c_im)


def _ssm_prep_kernel(rows_ref, cols_ref, bre_ref, bim_ref, cre_ref, cim_ref,
                     m_ref, w_ref, v_ref, v0_ref, arow_ref, *, t_chunk):
    sp = SSM_STATES_PER_BLOCK

    def apow(k, a_re, a_im, dt):
        mag = jnp.exp(a_re * dt * k)
        ang = a_im * dt * k
        return mag * jnp.cos(ang), mag * jnp.sin(ang)

    a_re, a_im, dt = rows_ref[0:1, :], rows_ref[1:2, :], jnp.exp(rows_ref[2:3, :])
    ab_re, ab_im = apow(1.0, a_re, a_im, dt)
    den = a_re * a_re + a_im * a_im
    q_re = ((ab_re - 1.0) * a_re + ab_im * a_im) / den
    q_im = (ab_im * a_re - (ab_re - 1.0) * a_im) / den
    bre, bim = bre_ref[...], bim_ref[...]
    bb_re = q_re * bre - q_im * bim
    bb_im = q_re * bim + q_im * bre
    cre, cim = cre_ref[...], cim_ref[...]

    m_ref[...] = jnp.zeros(m_ref.shape, m_ref.dtype)
    for k in range(t_chunk):
        pk_re, pk_im = apow(float(k), a_re, a_im, dt)
        bk_re = bb_re * pk_re - bb_im * pk_im
        bk_im = bb_re * pk_im + bb_im * pk_re
        t = t_chunk - 1 - k
        w_ref[t * LANES:(t + 1) * LANES, 0:sp] = bk_re.astype(w_ref.dtype)
        w_ref[t * LANES:(t + 1) * LANES, sp:2 * sp] = bk_im.astype(w_ref.dtype)
        kk = (jnp.dot(bk_re, cre, precision=lax.Precision.HIGHEST, preferred_element_type=F32)
              - jnp.dot(bk_im, cim, precision=lax.Precision.HIGHEST, preferred_element_type=F32))
        kk = kk.astype(m_ref.dtype)
        for t0 in range(t_chunk - k):
            m_ref[t0 * LANES:(t0 + 1) * LANES, (t0 + k) * LANES:(t0 + k + 1) * LANES] = kk

    ca_re, ca_im, cdt = cols_ref[:, 0:1], cols_ref[:, 1:2], jnp.exp(cols_ref[:, 2:3])
    for t in range(t_chunk):
        pc_re, pc_im = apow(float(t + 1), ca_re, ca_im, cdt)
        v_ref[0:sp, t * LANES:(t + 1) * LANES] = (cre * pc_re - cim * pc_im).astype(v_ref.dtype)
        v_ref[sp:2 * sp, t * LANES:(t + 1) * LANES] = (-(cre * pc_im + cim * pc_re)).astype(v_ref.dtype)
    v0_ref[0:sp, :] = cre.astype(v0_ref.dtype)
    v0_ref[sp:2 * sp, :] = (-cim).astype(v0_ref.dtype)

    pt_re, pt_im = apow(float(t_chunk), a_re, a_im, dt)
    arow_ref[0:1, 0:sp] = pt_re
    arow_ref[0:1, sp:2 * sp] = pt_im
    arow_ref[1:2, 0:sp] = ab_re
    arow_ref[1:2, sp:2 * sp] = ab_im


def _ssm_prep(a_re, a_im, log_dt, b_re, b_im, c_re, c_im):
    rows, cols, pbre, pbim, pcre, pcim = _ssm_layout(a_re, a_im, log_dt, b_re, b_im, c_re, c_im)
    nb, sp, sl, tl = SSM_LANE_BLOCKS, SSM_STATES_PER_BLOCK, SSM_STATE_LANES, SSM_T * LANES

    def blk(shape):
        return pl.BlockSpec((None,) + shape, lambda b: (b, 0, 0))

    return pl.pallas_call(
        functools.partial(_ssm_prep_kernel, t_chunk=SSM_T),
        grid=(nb,),
        in_specs=[blk((3, sp)), blk((sp, 3)), blk((LANES, sp)), blk((LANES, sp)), blk((sp, LANES)), blk((sp, LANES))],
        out_specs=[blk((tl, tl)), blk((tl, sl)), blk((sl, tl)), blk((sl, LANES)), blk((2, sl))],
        out_shape=[jax.ShapeDtypeStruct((nb, tl, tl), BF16), jax.ShapeDtypeStruct((nb, tl, sl), BF16),
                   jax.ShapeDtypeStruct((nb, sl, tl), BF16), jax.ShapeDtypeStruct((nb, sl, LANES), BF16),
                   jax.ShapeDtypeStruct((nb, 2, sl), F32)],
        compiler_params=_params("parallel"),
        name="ssm_prep",
    )(rows, cols, pbre, pbim, pcre, pcim)


def _chunk_lanes(uc_ref, b, t_chunk):
    return jnp.concatenate(
        [uc_ref[:, t * SSM_WIDTH + b * LANES:t * SSM_WIDTH + (b + 1) * LANES] for t in range(t_chunk)], axis=1)


def _ssm_end_state_kernel(uc_ref, w_ref, e_ref, *, t_chunk):
    for b in range(SSM_LANE_BLOCKS):
        ub = _chunk_lanes(uc_ref, b, t_chunk).astype(BF16)
        e_ref[b] = jnp.dot(ub, w_ref[b], preferred_element_type=F32)


def _ssm_scan_kernel(ere_ref, eim_ref, are_ref, aim_ref, spre_ref, spim_ref, fre_ref, fim_ref, *, n_seq, n_chunks):
    a_re = are_ref[0:1, :]
    a_im = aim_ref[0:1, :]

    def body(c, carry):
        s_re, s_im = carry
        rows = pl.ds(c, n_seq, stride=n_chunks)
        spre_ref[rows, :] = s_re
        spim_ref[rows, :] = s_im
        return (a_re * s_re - a_im * s_im + ere_ref[rows, :], a_re * s_im + a_im * s_re + eim_ref[rows, :])

    zero = jnp.zeros((n_seq, LANES), F32)
    s_re, s_im = lax.fori_loop(0, n_chunks, body, (zero, zero))
    fre_ref[...] = s_re
    fim_ref[...] = s_im


def _ssm_output_kernel(uc_ref, spre_ref, spim_ref, m_ref, v_ref, d_ref, y_ref, *, t_chunk):
    for b in range(SSM_LANE_BLOCKS):
        ub = _chunk_lanes(uc_ref, b, t_chunk).astype(BF16)
        yb = jnp.dot(ub, m_ref[b], preferred_element_type=F32)
        sprev = jnp.concatenate([spre_ref[b], spim_ref[b]], axis=1).astype(BF16)
        yb = yb + jnp.dot(sprev, v_ref[b], preferred_element_type=F32)
        d = d_ref[:, b * LANES:(b + 1) * LANES]
        for t in range(t_chunk):
            col = slice(t * SSM_WIDTH + b * LANES, t * SSM_WIDTH + (b + 1) * LANES)
            y_ref[:, col] = yb[:, t * LANES:(t + 1) * LANES] + d * uc_ref[:, col]


def _ssm_prompt(u, n_seq, m_mat, w_mat, v_mat, arow, d_skip):
    rows, t_chunk, nb, sl = u.shape[0], SSM_T, SSM_LANE_BLOCKS, SSM_STATE_LANES
    assert rows % (t_chunk * n_seq) == 0
    n_rows = rows // t_chunk
    n_chunks = n_rows // n_seq
    tr = min(ROW_TILE, n_rows)
    assert n_rows % tr == 0
    uc = u.reshape(n_rows, t_chunk * SSM_WIDTH)
    tl = t_chunk * LANES
    e = pl.pallas_call(
        functools.partial(_ssm_end_state_kernel, t_chunk=t_chunk),
        grid=(n_rows // tr,),
        in_specs=[pl.BlockSpec((tr, t_chunk * SSM_WIDTH), lambda i: (i, 0)), _full(w_mat.shape)],
        out_specs=pl.BlockSpec((nb, tr, sl), lambda i: (0, i, 0)),
        out_shape=jax.ShapeDtypeStruct((nb, n_rows, sl), F32),
        compiler_params=_params("parallel"),
        name="ssm_end_state",
    )(uc, w_mat)
    sp = SSM_STATES_PER_BLOCK
    lane_chunks = sp // LANES

    def re_part(shape_rows):
        return pl.BlockSpec((None, shape_rows, LANES), lambda b, j: (b, 0, j))

    def im_part(shape_rows):
        return pl.BlockSpec((None, shape_rows, LANES), lambda b, j: (b, 0, lane_chunks + j))

    sp_re, sp_im, fin_re, fin_im = pl.pallas_call(
        functools.partial(_ssm_scan_kernel, n_seq=n_seq, n_chunks=n_chunks),
        grid=(nb, lane_chunks),
        in_specs=[re_part(n_rows), im_part(n_rows), re_part(2), im_part(2)],
        out_specs=[re_part(n_rows), re_part(n_rows), re_part(n_seq), re_part(n_seq)],
        out_shape=[jax.ShapeDtypeStruct((nb, n_rows, sp), F32), jax.ShapeDtypeStruct((nb, n_rows, sp), F32),
                   jax.ShapeDtypeStruct((nb, n_seq, sp), F32), jax.ShapeDtypeStruct((nb, n_seq, sp), F32)],
        compiler_params=_params("parallel", "parallel"),
        name="ssm_scan",
    )(e, e, arow, arow)
    sp_spec = pl.BlockSpec((nb, tr, sp), lambda i: (0, i, 0))
    y = pl.pallas_call(
        functools.partial(_ssm_output_kernel, t_chunk=t_chunk),
        grid=(n_rows // tr,),
        in_specs=[pl.BlockSpec((tr, t_chunk * SSM_WIDTH), lambda i: (i, 0)), sp_spec, sp_spec,
                  _full(m_mat.shape), _full(v_mat.shape), _full((1, SSM_WIDTH))],
        out_specs=pl.BlockSpec((tr, t_chunk * SSM_WIDTH), lambda i: (i, 0)),
        out_shape=jax.ShapeDtypeStruct((n_rows, t_chunk * SSM_WIDTH), F32),
        compiler_params=_params("parallel"),
        name="ssm_output",
    )(uc, sp_re, sp_im, m_mat, v_mat, d_skip.reshape(1, SSM_WIDTH))
    return y.reshape(rows, SSM_WIDTH), fin_re, fin_im


def _ssm_step_kernel(u_ref, sre_ref, sim_ref, w0_ref, v0_ref, arow_ref, d_ref, y_ref, nre_ref, nim_ref):
    sp = SSM_STATES_PER_BLOCK
    for b in range(SSM_LANE_BLOCKS):
        lanes = slice(b * LANES, (b + 1) * LANES)
        states = slice(b * sp, (b + 1) * sp)
        u = u_ref[:, lanes]
        e = jnp.dot(u.astype(BF16), w0_ref[b], preferred_element_type=F32)
        a_re, a_im = arow_ref[b, 1:2, 0:sp], arow_ref[b, 1:2, sp:2 * sp]
        s_re, s_im = sre_ref[:, states], sim_ref[:, states]
        n_re = a_re * s_re - a_im * s_im + e[:, 0:sp]
        n_im = a_re * s_im + a_im * s_re + e[:, sp:2 * sp]
        nre_ref[:, states] = n_re
        nim_ref[:, states] = n_im
        sn = jnp.concatenate([n_re, n_im], axis=1).astype(BF16)
        y_ref[:, lanes] = jnp.dot(sn, v0_ref[b], preferred_element_type=F32) + d_ref[:, lanes] * u


def _ssm_step(u, s_re, s_im, w0, v0, arow, d_skip):
    rows = u.shape[0]
    ns = SSM_GROUPS * SSM_STATE
    args = (u, s_re.reshape(rows, ns), s_im.reshape(rows, ns), w0, v0, arow, d_skip.reshape(1, SSM_WIDTH))
    return pl.pallas_call(
        _ssm_step_kernel,
        grid=(1,),
        in_specs=[_full(a.shape) for a in args],
        out_specs=[_full((rows, SSM_WIDTH)), _full((rows, ns)), _full((rows, ns))],
        out_shape=[jax.ShapeDtypeStruct((rows, SSM_WIDTH), F32), jax.ShapeDtypeStruct((rows, ns), F32),
                   jax.ShapeDtypeStruct((rows, ns), F32)],
        compiler_params=_params("arbitrary"),
        name="ssm_step",
    )(*args)


def _dil_attn_kernel(q_ref, kc_ref, kp_ref, vc_ref, vp_ref, o_ref, *, dil, slopes):
    steps, gw, hd = ATT_STEPS, ATT_GROUP_WIDTH, ATT_HEAD_DIM
    j = pl.program_id(2)
    q = q_ref[...]
    kk = jnp.concatenate([kp_ref[...], kc_ref[...]], axis=0).astype(BF16)
    vv = jnp.concatenate([vp_ref[...], vc_ref[...]], axis=0).astype(BF16)
    qi = lax.broadcasted_iota(jnp.int32, (steps, 2 * steps), 0)
    kj = lax.broadcasted_iota(jnp.int32, (steps, 2 * steps), 1)
    dist = qi + steps - kj
    valid = (dist >= 0) & (dist <= steps) & ((j > 0) | (kj >= steps))
    distf = (dist * dil).astype(F32)
    lane = lax.broadcasted_iota(jnp.int32, (steps, gw), 1)
    out = jnp.zeros((steps, gw), F32)
    lse = jnp.zeros((steps, gw), F32)
    for h in range(HEADS_PER_GROUP):
        head = (lane >= h * hd) & (lane < (h + 1) * hd)
        qh = jnp.where(head, q, 0.0).astype(BF16)
        s = lax.dot_general(qh, kk, (((1,), (1,)), ((), ())), preferred_element_type=F32) * (hd ** -0.5)
        s = jnp.where(valid, s - slopes[h] * distf, NEG_BIG)
        m = jnp.max(s, axis=-1, keepdims=True)
        p = jnp.exp(s - m)
        den = jnp.sum(p, axis=-1, keepdims=True)
        oh = jnp.dot(p.astype(BF16), vv, preferred_element_type=F32) / den
        out = jnp.where(head, oh, out)
        lse = jnp.where(head, m + jnp.log(den), lse)
    o_ref[:, 0:gw] = out
    o_ref[:, gw:2 * gw] = lse


def _dil_attn(q, k, v, n_seq, seq_len, group):
    win, dil = DIL_PAIRS[group]
    steps, gw = ATT_STEPS, ATT_GROUP_WIDTH
    assert win // dil == steps and seq_len % (dil * steps) == 0
    sub = seq_len // dil
    blocks_per_token = ATT_WIDTH // gw

    def view(x):
        return x.reshape(n_seq, sub, dil * ATT_WIDTH)

    cur = pl.BlockSpec((None, steps, gw), lambda n, r, j: (n, j, blocks_per_token * r + group))
    prev = pl.BlockSpec((None, steps, gw), lambda n, r, j: (n, jnp.maximum(j - 1, 0), blocks_per_token * r + group))
    slopes = tuple(_alibi_slopes()[group * HEADS_PER_GROUP:(group + 1) * HEADS_PER_GROUP])
    out = pl.pallas_call(
        functools.partial(_dil_attn_kernel, dil=dil, slopes=slopes),
        grid=(n_seq, dil, sub // steps),
        in_specs=[cur, cur, prev, cur, prev],
        out_specs=pl.BlockSpec((None, steps, 2 * gw), lambda n, r, j: (n, j, r)),
        out_shape=jax.ShapeDtypeStruct((n_seq, sub, dil * 2 * gw), F32),
        compiler_params=_params("parallel", "parallel", "parallel"),
        name=f"dil_attn_w{win}",
    )(view(q), view(k), view(k), view(v), view(v))
    return out.reshape(n_seq * seq_len, 2 * gw)


def _mem_attn_kernel(q_ref, mk_ref, mv_ref, o_ref):
    hd = MEM_HEAD_DIM
    for h in range(MEM_HEADS):
        lanes = slice(h * hd, (h + 1) * hd)
        s = lax.dot_general(q_ref[:, lanes].astype(BF16), mk_ref[:, lanes].astype(BF16),
                            (((1,), (1,)), ((), ())), preferred_element_type=F32) * (hd ** -0.5)
        m = jnp.max(s, axis=-1, keepdims=True)
        p = jnp.exp(s - m)
        den = jnp.sum(p, axis=-1, keepdims=True)
        o_ref[:, lanes] = jnp.dot(p.astype(BF16), mv_ref[:, lanes].astype(BF16), preferred_element_type=F32) / den


def _mem_attn(qm, mk, mv, n_seq, seq_len, tm):
    assert seq_len % tm == 0
    tiles = seq_len // tm
    kv = pl.BlockSpec((None, N_MEM, MEM_WIDTH), lambda n, i: (n, 0, 0))
    return pl.pallas_call(
        _mem_attn_kernel,
        grid=(n_seq, tiles),
        in_specs=[pl.BlockSpec((tm, MEM_WIDTH), lambda n, i: (n * tiles + i, 0)), kv, kv],
        out_specs=pl.BlockSpec((tm, MEM_WIDTH), lambda n, i: (n * tiles + i, 0)),
        out_shape=jax.ShapeDtypeStruct(qm.shape, F32),
        compiler_params=_params("parallel", "parallel"),
        name="mem_attn",
    )(qm, mk, mv)


def _row_attention(q, k_blocks, v_blocks, biases, scale):
    qb = q[:, None]
    scores = []
    for k, bias in zip(k_blocks, biases):
        s = jnp.sum(k * qb, axis=-1, keepdims=True) * scale
        scores.append(s if bias is None else s + bias)
    m = functools.reduce(jnp.maximum, [jnp.max(s, axis=1, keepdims=True) for s in scores])
    den, acc = 0.0, 0.0
    for s, v in zip(scores, v_blocks):
        p = jnp.exp(s - m)
        den = den + jnp.sum(p, axis=1, keepdims=True)
        acc = acc + jnp.sum(p * v, axis=1, keepdims=True)
    return acc / den, m + jnp.log(den)


def _cache_attn_kernel(q_ref, kn_ref, vn_ref, k1_ref, v1_ref, k2_ref, v2_ref, k3_ref, v3_ref, o_ref, *, slopes):
    hpg, steps = HEADS_PER_GROUP, ATT_STEPS
    head = lax.broadcasted_iota(jnp.int32, (steps, hpg, 1), 1)
    back = steps - lax.broadcasted_iota(jnp.int32, (steps, hpg, 1), 0)
    outs, lses = [], []
    for g, (k_ref, v_ref) in enumerate(((k1_ref, v1_ref), (k2_ref, v2_ref), (k3_ref, v3_ref))):
        dil = DIL_PAIRS[g][1]
        slope = jnp.zeros((steps, hpg, 1), F32)
        for h in range(hpg):
            slope = jnp.where(head == h, slopes[g * hpg + h], slope)
        bias = -slope * (back * dil).astype(F32)
        o, lse = _row_attention(q_ref[:, g], [k_ref[...], kn_ref[:, g:g + 1]], [v_ref[...], vn_ref[:, g:g + 1]],
                                [bias, None], ATT_HEAD_DIM ** -0.5)
        outs.append(o)
        lses.append(lse)
    top = functools.reduce(jnp.maximum, lses)
    ws = [jnp.exp(l - top) for l in lses]
    o_ref[...] = (sum(w * o for w, o in zip(ws, outs)) / sum(ws))[:, 0]


def _cache_attn(q, k_new, v_new, caches_k, caches_v, nb):
    rows = q.shape[0]
    n_g, hpg, hd, steps = len(DIL_PAIRS), HEADS_PER_GROUP, ATT_HEAD_DIM, ATT_STEPS
    assert rows % nb == 0
    new = pl.BlockSpec((nb, n_g, hpg, hd), lambda i: (i, 0, 0, 0))
    cache_specs, cache_args = [], []
    for g, (win, dil) in enumerate(DIL_PAIRS):
        for c in (caches_k[g], caches_v[g]):
            assert c.shape == (rows, win, hpg, hd), c.shape
            cache_args.append(c.reshape(rows, steps, dil, hpg, hd))
            cache_specs.append(pl.BlockSpec((nb, steps, None, hpg, hd), lambda i: (i, 0, 0, 0, 0)))
    return pl.pallas_call(
        functools.partial(_cache_attn_kernel, slopes=tuple(_alibi_slopes())),
        grid=(rows // nb,),
        in_specs=[new, new, new] + cache_specs,
        out_specs=pl.BlockSpec((nb, hpg, hd), lambda i: (i, 0, 0)),
        out_shape=jax.ShapeDtypeStruct((rows, hpg, hd), F32),
        compiler_params=_params("parallel"),
        name="cache_attn",
    )(q, k_new, v_new, *cache_args)


def _cache_mem_attn_kernel(q_ref, k_ref, v_ref, o_ref):
    o, _ = _row_attention(q_ref[...], [k_ref[...]], [v_ref[...]], [None], MEM_HEAD_DIM ** -0.5)
    o_ref[...] = o[:, 0]


def _cache_mem_attn(qm, mem_k, mem_v, nb):
    rows = qm.shape[0]
    assert rows % nb == 0
    kv = pl.BlockSpec((nb, N_MEM, MEM_HEADS, MEM_HEAD_DIM), lambda i: (i, 0, 0, 0))
    q = pl.BlockSpec((nb, MEM_HEADS, MEM_HEAD_DIM), lambda i: (i, 0, 0))
    return pl.pallas_call(
        _cache_mem_attn_kernel,
        grid=(rows // nb,),
        in_specs=[q, kv, kv],
        out_specs=q,
        out_shape=jax.ShapeDtypeStruct(qm.shape, F32),
        compiler_params=_params("parallel"),
        name="cache_mem_attn",
    )(qm, mem_k, mem_v)


def _merge_kernel(x_ref, y_ref, *refs, n_att):
    att_refs = refs[:n_att]
    mem_ref, gl_ref, wglu_ref, watt_ref, wmem_ref, wout_ref, o_ref = refs[n_att:]
    d, gw = D_MODEL, ATT_GROUP_WIDTH
    z = jax.nn.gelu(y_ref[...]).astype(BF16)
    glu = jnp.dot(z, wglu_ref[...], preferred_element_type=F32)
    merged = jax.nn.sigmoid(gl_ref[:, 0:d]) * (glu[:, 0:d] * jax.nn.sigmoid(glu[:, d:2 * d]))
    if n_att == 1:
        att = att_refs[0][...]
    else:
        lses = [r[:, gw:2 * gw] for r in att_refs]
        top = functools.reduce(jnp.maximum, lses)
        ws = [jnp.exp(l - top) for l in lses]
        att = sum(w * r[:, 0:gw] for w, r in zip(ws, att_refs)) / sum(ws)
    b_att = jnp.dot(att.astype(BF16), watt_ref[...], preferred_element_type=F32)
    merged = merged + jax.nn.sigmoid(gl_ref[:, d:2 * d]) * b_att
    b_mem = jnp.dot(mem_ref[...].astype(BF16), wmem_ref[...], preferred_element_type=F32)
    merged = merged + jax.nn.sigmoid(gl_ref[:, 2 * d:3 * d]) * b_mem
    o_ref[...] = x_ref[...] + jnp.dot(merged.astype(BF16), wout_ref[...], preferred_element_type=F32)


def _merge(x, y_ssm, atts, o_mem, gl, w_glu, w_att_o, w_mem_o, w_out, tm):
    rows = x.shape[0]
    assert rows % tm == 0

    def tile(a):
        return pl.BlockSpec((tm, a.shape[1]), lambda i: (i, 0))

    acts = (x, y_ssm, *atts, o_mem, gl)
    weights = (w_glu, w_att_o, w_mem_o, w_out)
    return pl.pallas_call(
        functools.partial(_merge_kernel, n_att=len(atts)),
        grid=(rows // tm,),
        in_specs=[tile(a) for a in acts] + [_full(w.shape) for w in weights],
        out_specs=tile(x),
        out_shape=jax.ShapeDtypeStruct(x.shape, F32),
        compiler_params=_params("parallel"),
        name="merge",
    )(*acts, *weights)


def _ffn_kernel(*refs, tm, tiles_per_seq, stepwise):
    if stepwise:
        x_ref, g2_ref, wup_ref, cw_ref, cb_ref, wdn_ref, gf_ref, prev_ref, y_ref, conv_ref = refs
    else:
        x_ref, g2_ref, wup_ref, cw_ref, cb_ref, wdn_ref, gf_ref, y_ref, conv_ref, a_scr = refs
        i = pl.program_id(0)
        first = i % tiles_per_seq == 0

        @pl.when(first)
        def _():
            a_scr[0:SUBLANES, :] = jnp.zeros((SUBLANES, D_FF), F32)

        @pl.when(jnp.logical_not(first))
        def _():
            a_scr[0:SUBLANES, :] = a_scr[tm:tm + SUBLANES, :]

    x = x_ref[...]
    h = _rmsnorm(x, g2_ref[...]).astype(BF16)
    acc = jnp.zeros((tm, D_MODEL), F32)
    for c0 in range(0, D_FF, FFN_CHUNK):
        cols = slice(c0, c0 + FFN_CHUNK)
        a = jnp.dot(h, wup_ref[:, cols], preferred_element_type=F32)
        if stepwise:
            a2, a1 = prev_ref[:, cols], prev_ref[:, D_FF + c0:D_FF + c0 + FFN_CHUNK]
            conv_ref[:, cols] = a1
            conv_ref[:, D_FF + c0:D_FF + c0 + FFN_CHUNK] = a
        else:
            a_scr[SUBLANES:SUBLANES + tm, cols] = a
            a1 = a_scr[SUBLANES - 1:SUBLANES - 1 + tm, cols]
            a2 = a_scr[SUBLANES - 2:SUBLANES - 2 + tm, cols]
        c = a2 * cw_ref[0:1, cols] + a1 * cw_ref[1:2, cols] + a * cw_ref[2:3, cols] + cb_ref[:, cols]
        v = jnp.dot(h, wup_ref[:, D_FF + c0:D_FF + c0 + FFN_CHUNK], preferred_element_type=F32)
        acc = acc + jnp.dot((jax.nn.gelu(c) * v).astype(BF16), wdn_ref[cols, :], preferred_element_type=F32)
    y_ref[...] = _rmsnorm(x + acc, gf_ref[...])
    if not stepwise:
        @pl.when(i % tiles_per_seq == tiles_per_seq - 1)
        def _():
            conv_ref[...] = a_scr[SUBLANES + tm - (CONV_W - 1):SUBLANES + tm, :]


def _ffn(x, g2, w_up, conv_w, conv_b, w_down, gf, tm, n_seq=None, prev=None):
    rows, d = x.shape
    assert rows % tm == 0 and D_FF % FFN_CHUNK == 0 and FFN_CHUNK % LANES == 0
    stepwise = prev is not None
    weights = (g2.reshape(1, d), w_up, conv_w, conv_b.reshape(1, D_FF), w_down, gf.reshape(1, d))
    in_specs = [pl.BlockSpec((tm, d), lambda i: (i, 0))] + [_full(w.shape) for w in weights]
    args = (x,) + weights
    if stepwise:
        tiles_per_seq = 1
        in_specs.append(pl.BlockSpec((tm, 2 * D_FF), lambda i: (i, 0)))
        args += (prev,)
        conv_spec = pl.BlockSpec((tm, 2 * D_FF), lambda i: (i, 0))
        conv_shape = jax.ShapeDtypeStruct((rows, 2 * D_FF), F32)
        scratch = []
    else:
        tiles_per_seq = rows // n_seq // tm
        assert tiles_per_seq * tm * n_seq == rows
        conv_spec = pl.BlockSpec((None, CONV_W - 1, D_FF), lambda i: (i // tiles_per_seq, 0, 0))
        conv_shape = jax.ShapeDtypeStruct((n_seq, CONV_W - 1, D_FF), F32)
        scratch = [pltpu.VMEM((tm + 2 * SUBLANES, D_FF), F32)]
    return pl.pallas_call(
        functools.partial(_ffn_kernel, tm=tm, tiles_per_seq=tiles_per_seq, stepwise=stepwise),
        grid=(rows // tm,),
        in_specs=in_specs,
        out_specs=[pl.BlockSpec((tm, d), lambda i: (i, 0)), conv_spec],
        out_shape=[jax.ShapeDtypeStruct((rows, d), F32), conv_shape],
        scratch_shapes=scratch,
        compiler_params=_params("arbitrary"),
        name="ffn_step" if stepwise else "ffn",
    )(*args)


def kernel(x_prompt, x_sample, state_ssm_re, state_ssm_im, cache_w1_k, cache_w1_v, cache_w2_k, cache_w2_v, cache_w3_k, cache_w3_v, cache_mem_k, cache_mem_v, state_ffn_conv, mem_prompt, norm1_g, w_in, ssm_a_re, ssm_a_im, ssm_log_dt, ssm_b_re, ssm_b_im, ssm_c_re, ssm_c_im, ssm_d, w_ssm_glu, w_att_o, mem_norm_g, w_mem_kv, w_mem_o, w_out, norm2_g, w_up, ffn_conv_w, ffn_conv_b, w_down, final_norm_g):
    n_seq, seq_len, d = x_prompt.shape
    n_dec, dec_len, _ = x_sample.shape
    depth = norm1_g.shape[0]
    assert d == D_MODEL and depth == 1 and dec_len == 1
    assert w_in.shape == (depth, D_MODEL, sum(IN_SPLITS)) and w_up.shape == (depth, D_MODEL, 2 * D_FF)
    assert mem_prompt.shape == (n_seq, N_MEM, D_MODEL)
    assert ssm_a_re.shape == (depth, SSM_GROUPS, SSM_STATE)
    assert seq_len % ROW_TILE == 0 and n_dec % SUBLANES == 0
    hpg, hd = HEADS_PER_GROUP, ATT_HEAD_DIM
    rows_p = n_seq * seq_len
    dec_tile = n_dec if n_dec <= ROW_TILE else ROW_TILE

    wb = lambda w: w[0].astype(BF16)
    w_in_b, w_glu_b, w_att_b, w_memkv_b = wb(w_in), wb(w_ssm_glu), wb(w_att_o), wb(w_mem_kv)
    w_memo_b, w_out_b, w_up_b, w_down_b = wb(w_mem_o), wb(w_out), wb(w_up), wb(w_down)

    m_mat, w_mat, v_mat, v0_mat, arow = _ssm_prep(ssm_a_re[0], ssm_a_im[0], ssm_log_dt[0], ssm_b_re[0],
                                                  ssm_b_im[0], ssm_c_re[0], ssm_c_im[0])
    w0_mat = w_mat[:, (SSM_T - 1) * LANES:, :]

    xp = x_prompt.reshape(rows_p, d)
    u, q, k, v, qm, gl = _norm_proj(xp, norm1_g[0], w_in_b, IN_SPLITS, ROW_TILE)
    y_ssm, fin_re, fin_im = _ssm_prompt(u, n_seq, m_mat, w_mat, v_mat, arow, ssm_d[0])
    atts = [_dil_attn(q, k, v, n_seq, seq_len, g) for g in range(len(DIL_PAIRS))]
    mk, mv = _norm_proj(mem_prompt.reshape(n_seq * N_MEM, d), mem_norm_g[0], w_memkv_b, (MEM_WIDTH, MEM_WIDTH),
                        ROW_TILE)
    o_mem = _mem_attn(qm, mk.reshape(n_seq, N_MEM, MEM_WIDTH), mv.reshape(n_seq, N_MEM, MEM_WIDTH),
                      n_seq, seq_len, 2 * ROW_TILE)
    x1 = _merge(xp, y_ssm, atts, o_mem, gl, w_glu_b, w_att_b, w_memo_b, w_out_b, ROW_TILE)
    y_p, conv_p = _ffn(x1, norm2_g[0], w_up_b, ffn_conv_w[0], ffn_conv_b[0], w_down_b, final_norm_g,
                       ROW_TILE, n_seq=n_seq)

    def final_state(s):
        s = s.reshape(SSM_LANE_BLOCKS, n_seq, SSM_GROUPS_PER_BLOCK, SSM_STATE)
        return s.transpose(1, 0, 2, 3).reshape(1, n_seq, SSM_GROUPS, SSM_STATE)
    k3 = k.reshape(n_seq, seq_len, ATT_WIDTH)
    v3 = v.reshape(n_seq, seq_len, ATT_WIDTH)
    gw = ATT_GROUP_WIDTH
    win_p = []
    for g, (win, _) in enumerate(DIL_PAIRS):
        keep = min(win, seq_len)
        for t in (k3, v3):
            win_p.append(t[:, seq_len - keep:, g * gw:(g + 1) * gw].reshape(1, n_seq, keep, hpg, hd))
    mem_kv = [mk.reshape(1, n_seq, N_MEM, MEM_HEADS, MEM_HEAD_DIM), mv.reshape(1, n_seq, N_MEM, MEM_HEADS, MEM_HEAD_DIM)]

    xs = x_sample.reshape(n_dec, d)
    us, qs, ks, vs, qms, gls = _norm_proj(xs, norm1_g[0], w_in_b, IN_SPLITS, dec_tile)
    ys_ssm, sn_re, sn_im = _ssm_step(us, state_ssm_re[0], state_ssm_im[0], w0_mat, v0_mat, arow, ssm_d[0])
    n_g = len(DIL_PAIRS)
    qs4, ks4, vs4 = (t.reshape(n_dec, n_g, hpg, hd) for t in (qs, ks, vs))
    att_s = _cache_attn(qs4, ks4, vs4, (cache_w1_k[0], cache_w2_k[0], cache_w3_k[0]),
                        (cache_w1_v[0], cache_w2_v[0], cache_w3_v[0]), CACHE_ROWS_PER_STEP)
    mem_s = _cache_mem_attn(qms.reshape(n_dec, MEM_HEADS, MEM_HEAD_DIM), cache_mem_k[0], cache_mem_v[0],
                            CACHE_ROWS_PER_STEP)
    xs1 = _merge(xs, ys_ssm, [att_s.reshape(n_dec, ATT_GROUP_WIDTH)], mem_s.reshape(n_dec, MEM_WIDTH), gls,
                 w_glu_b, w_att_b, w_memo_b, w_out_b, dec_tile)
    y_s, conv_s = _ffn(xs1, norm2_g[0], w_up_b, ffn_conv_w[0], ffn_conv_b[0], w_down_b, final_norm_g,
                       dec_tile, prev=state_ffn_conv[0].reshape(n_dec, (CONV_W - 1) * D_FF))

    win_s = []
    for g in range(len(DIL_PAIRS)):
        win_s += [ks4[None, :, g:g + 1], vs4[None, :, g:g + 1]]

    return (y_p.reshape(n_seq, seq_len, d), y_s.reshape(n_dec, 1, d),
            final_state(fin_re), final_state(fin_im), *win_p, *mem_kv, conv_p[None],
            sn_re.reshape(1, n_dec, SSM_GROUPS, SSM_STATE), sn_im.reshape(1, n_dec, SSM_GROUPS, SSM_STATE),
            *win_s, conv_s.reshape(1, n_dec, CONV_W - 1, D_FF))
```

```python
import functools

import jax
import jax.numpy as jnp
from jax import lax
from jax.experimental import pallas as pl
from jax.experimental.pallas import tpu as pltpu

F32 = jnp.float32
BF16 = jnp.bfloat16

D_MODEL = 1024
SSM_WIDTH = 512
SSM_GROUP = 16
SSM_GROUPS = 32
SSM_STATE = 64
ATT_HEAD_DIM = 64
HEADS_PER_GROUP = 4
DIL_PAIRS = ((128, 1), (512, 4), (2048, 16))
ATT_HEADS = len(DIL_PAIRS) * HEADS_PER_GROUP
ATT_WIDTH = ATT_HEADS * ATT_HEAD_DIM
ATT_GROUP_WIDTH = HEADS_PER_GROUP * ATT_HEAD_DIM
ATT_STEPS = 128
N_MEM = 256
MEM_HEADS = 4
MEM_HEAD_DIM = 128
MEM_WIDTH = MEM_HEADS * MEM_HEAD_DIM
N_BRANCH = 3
D_FF = 2816
CONV_W = 3
EPS = 1e-6
IN_SPLITS = (SSM_WIDTH, ATT_WIDTH, ATT_WIDTH, ATT_WIDTH, MEM_WIDTH, N_BRANCH * D_MODEL)

LANES = 128
SUBLANES = 8
VMEM_LIMIT_BYTES = 56 * 1024 * 1024

SSM_T = 8
SSM_LANE_BLOCKS = SSM_WIDTH // LANES
SSM_GROUPS_PER_BLOCK = LANES // SSM_GROUP
SSM_STATES_PER_BLOCK = SSM_GROUPS_PER_BLOCK * SSM_STATE
SSM_STATE_LANES = 2 * SSM_STATES_PER_BLOCK

ROW_TILE = 256
FFN_CHUNK = 1408
CACHE_ROWS_PER_STEP = 4
NEG_BIG = -1e30


def _alibi_slopes():
    return [float(2.0 ** (-8.0 * h / ATT_HEADS)) for h in range(1, ATT_HEADS + 1)]


def _params(*sem):
    return pltpu.CompilerParams(dimension_semantics=sem, vmem_limit_bytes=VMEM_LIMIT_BYTES)


def _rmsnorm(x, g):
    ms = jnp.mean(x * x, axis=-1, keepdims=True)
    return x * lax.rsqrt(ms + EPS) * g


def _full(shape):
    nd = len(shape)
    return pl.BlockSpec(shape, lambda *_: (0,) * nd)


def _norm_proj_kernel(x_ref, g_ref, w_ref, *out_refs, splits):
    h = _rmsnorm(x_ref[...], g_ref[...]).astype(BF16)
    off = 0
    for o_ref, width in zip(out_refs, splits):
        for c0 in range(0, width, 512):
            cw = min(512, width - c0)
            o_ref[:, c0:c0 + cw] = jnp.dot(h, w_ref[:, off + c0:off + c0 + cw], preferred_element_type=F32)
        off += width


def _norm_proj(x, g, w_bf16, splits, tm):
    rows, d = x.shape
    assert rows % tm == 0 and sum(splits) == w_bf16.shape[1]
    return pl.pallas_call(
        functools.partial(_norm_proj_kernel, splits=splits),
        grid=(rows // tm,),
        in_specs=[pl.BlockSpec((tm, d), lambda i: (i, 0)), _full((1, d)), _full(w_bf16.shape)],
        out_specs=[pl.BlockSpec((tm, s), lambda i: (i, 0)) for s in splits],
        out_shape=[jax.ShapeDtypeStruct((rows, s), F32) for s in splits],
        compiler_params=_params("parallel"),
        name="norm_proj",
    )(x, g.reshape(1, d), w_bf16)


def _ssm_layout(a_re, a_im, log_dt, b_re, b_im, c_re, c_im):
    nb, gpb, p, c = SSM_LANE_BLOCKS, SSM_GROUPS_PER_BLOCK, SSM_STATE, SSM_GROUP
    rows = jnp.stack([a_re.reshape(nb, gpb * p), a_im.reshape(nb, gpb * p),
                      jnp.repeat(log_dt, p).reshape(nb, gpb * p)], axis=1)
    cols = rows.transpose(0, 2, 1)
    eye = jnp.eye(gpb, dtype=F32)

    def place_b(b):
        return jnp.einsum('bgpc,gh->bgchp', b.reshape(nb, gpb, p, c), eye).reshape(nb, gpb * c, gpb * p)

    def place_c(m):
        return jnp.einsum('bgcp,gh->bhpgc', m.reshape(nb, gpb, c, p), eye).reshape(nb, gpb * p, gpb * c)

    return rows, cols, place_b(b_re), place_b(b_im), place_c(c_re), place_c(c_im)


def _ssm_prep_kernel(rows_ref, cols_ref, bre_ref, bim_ref, cre_ref, cim_ref,
                     m_ref, w_ref, v_ref, v0_ref, arow_ref, *, t_chunk):
    sp = SSM_STATES_PER_BLOCK

    def apow(k, a_re, a_im, dt):
        mag = jnp.exp(a_re * dt * k)
        ang = a_im * dt * k
        return mag * jnp.cos(ang), mag * jnp.sin(ang)

    a_re, a_im, dt = rows_ref[0:1, :], rows_ref[1:2, :], jnp.exp(rows_ref[2:3, :])
    ab_re, ab_im = apow(1.0, a_re, a_im, dt)
    den = a_re * a_re + a_im * a_im
    q_re = ((ab_re - 1.0) * a_re + ab_im * a_im) / den
    q_im = (ab_im * a_re - (ab_re - 1.0) * a_im) / den
    bre, bim = bre_ref[...], bim_ref[...]
    bb_re = q_re * bre - q_im * bim
    bb_im = q_re * bim + q_im * bre
    cre, cim = cre_ref[...], cim_ref[...]

    m_ref[...] = jnp.zeros(m_ref.shape, m_ref.dtype)
    for k in range(t_chunk):
        pk_re, pk_im = apow(float(k), a_re, a_im, dt)
        bk_re = bb_re * pk_re - bb_im * pk_im
        bk_im = bb_re * pk_im + bb_im * pk_re
        t = t_chunk - 1 - k
        w_ref[t * LANES:(t + 1) * LANES, 0:sp] = bk_re.astype(w_ref.dtype)
        w_ref[t * LANES:(t + 1) * LANES, sp:2 * sp] = bk_im.astype(w_ref.dtype)
        kk = (jnp.dot(bk_re, cre, precision=lax.Precision.HIGHEST, preferred_element_type=F32)
              - jnp.dot(bk_im, cim, precision=lax.Precision.HIGHEST, preferred_element_type=F32))
        kk = kk.astype(m_ref.dtype)
        for t0 in range(t_chunk - k):
            m_ref[t0 * LANES:(t0 + 1) * LANES, (t0 + k) * LANES:(t0 + k + 1) * LANES] = kk

    ca_re, ca_im, cdt = cols_ref[:, 0:1], cols_ref[:, 1:2], jnp.exp(cols_ref[:, 2:3])
    for t in range(t_chunk):
        pc_re, pc_im = apow(float(t + 1), ca_re, ca_im, cdt)
        v_ref[0:sp, t * LANES:(t + 1) * LANES] = (cre * pc_re - cim * pc_im).astype(v_ref.dtype)
        v_ref[sp:2 * sp, t * LANES:(t + 1) * LANES] = (-(cre * pc_im + cim * pc_re)).astype(v_ref.dtype)
    v0_ref[0:sp, :] = cre.astype(v0_ref.dtype)
    v0_ref[sp:2 * sp, :] = (-cim).astype(v0_ref.dtype)

    pt_re, pt_im = apow(float(t_chunk), a_re, a_im, dt)
    arow_ref[0:1, 0:sp] = pt_re
    arow_ref[0:1, sp:2 * sp] = pt_im
    arow_ref[1:2, 0:sp] = ab_re
    arow_ref[1:2, sp:2 * sp] = ab_im


def _ssm_prep(a_re, a_im, log_dt, b_re, b_im, c_re, c_im):
    rows, cols, pbre, pbim, pcre, pcim = _ssm_layout(a_re, a_im, log_dt, b_re, b_im, c_re, c_im)
    nb, sp, sl, tl = SSM_LANE_BLOCKS, SSM_STATES_PER_BLOCK, SSM_STATE_LANES, SSM_T * LANES

    def blk(shape):
        return pl.BlockSpec((None,) + shape, lambda b: (b, 0, 0))

    return pl.pallas_call(
        functools.partial(_ssm_prep_kernel, t_chunk=SSM_T),
        grid=(nb,),
        in_specs=[blk((3, sp)), blk((sp, 3)), blk((LANES, sp)), blk((LANES, sp)), blk((sp, LANES)), blk((sp, LANES))],
        out_specs=[blk((tl, tl)), blk((tl, sl)), blk((sl, tl)), blk((sl, LANES)), blk((2, sl))],
        out_shape=[jax.ShapeDtypeStruct((nb, tl, tl), BF16), jax.ShapeDtypeStruct((nb, tl, sl), BF16),
                   jax.ShapeDtypeStruct((nb, sl, tl), BF16), jax.ShapeDtypeStruct((nb, sl, LANES), BF16),
                   jax.ShapeDtypeStruct((nb, 2, sl), F32)],
        compiler_params=_params("parallel"),
        name="ssm_prep",
    )(rows, cols, pbre, pbim, pcre, pcim)


def _chunk_lanes(uc_ref, b, t_chunk):
    return jnp.concatenate(
        [uc_ref[:, t * SSM_WIDTH + b * LANES:t * SSM_WIDTH + (b + 1) * LANES] for t in range(t_chunk)], axis=1)


def _ssm_end_state_kernel(uc_ref, w_ref, e_ref, *, t_chunk):
    for b in range(SSM_LANE_BLOCKS):
        ub = _chunk_lanes(uc_ref, b, t_chunk).astype(BF16)
        e_ref[b] = jnp.dot(ub, w_ref[b], preferred_element_type=F32)


def _ssm_scan_kernel(ere_ref, eim_ref, are_ref, aim_ref, spre_ref, spim_ref, fre_ref, fim_ref, *, n_seq, n_chunks):
    a_re = are_ref[0:1, :]
    a_im = aim_ref[0:1, :]

    def body(c, carry):
        s_re, s_im = carry
        rows = pl.ds(c, n_seq, stride=n_chunks)
        spre_ref[rows, :] = s_re
        spim_ref[rows, :] = s_im
        return (a_re * s_re - a_im * s_im + ere_ref[rows, :], a_re * s_im + a_im * s_re + eim_ref[rows, :])

    zero = jnp.zeros((n_seq, LANES), F32)
    s_re, s_im = lax.fori_loop(0, n_chunks, body, (zero, zero))
    fre_ref[...] = s_re
    fim_ref[...] = s_im


def _ssm_output_kernel(uc_ref, spre_ref, spim_ref, m_ref, v_ref, d_ref, y_ref, *, t_chunk):
    for b in range(SSM_LANE_BLOCKS):
        ub = _chunk_lanes(uc_ref, b, t_chunk).astype(BF16)
        yb = jnp.dot(ub, m_ref[b], preferred_element_type=F32)
        sprev = jnp.concatenate([spre_ref[b], spim_ref[b]], axis=1).astype(BF16)
        yb = yb + jnp.dot(sprev, v_ref[b], preferred_element_type=F32)
        d = d_ref[:, b * LANES:(b + 1) * LANES]
        for t in range(t_chunk):
            col = slice(t * SSM_WIDTH + b * LANES, t * SSM_WIDTH + (b + 1) * LANES)
            y_ref[:, col] = yb[:, t * LANES:(t + 1) * LANES] + d * uc_ref[:, col]


def _ssm_prompt(u, n_seq, m_mat, w_mat, v_mat, arow, d_skip):
    rows, t_chunk, nb, sl = u.shape[0], SSM_T, SSM_LANE_BLOCKS, SSM_STATE_LANES
    assert rows % (t_chunk * n_seq) == 0
    n_rows = rows // t_chunk
    n_chunks = n_rows // n_seq
    tr = min(ROW_TILE, n_rows)
    assert n_rows % tr == 0
    uc = u.reshape(n_rows, t_chunk * SSM_WIDTH)
    tl = t_chunk * LANES
    e = pl.pallas_call(
        functools.partial(_ssm_end_state_kernel, t_chunk=t_chunk),
        grid=(n_rows // tr,),
        in_specs=[pl.BlockSpec((tr, t_chunk * SSM_WIDTH), lambda i: (i, 0)), _full(w_mat.shape)],
        out_specs=pl.BlockSpec((nb, tr, sl), lambda i: (0, i, 0)),
        out_shape=jax.ShapeDtypeStruct((nb, n_rows, sl), F32),
        compiler_params=_params("parallel"),
        name="ssm_end_state",
    )(uc, w_mat)
    sp = SSM_STATES_PER_BLOCK
    lane_chunks = sp // LANES

    def re_part(shape_rows):
        return pl.BlockSpec((None, shape_rows, LANES), lambda b, j: (b, 0, j))

    def im_part(shape_rows):
        return pl.BlockSpec((None, shape_rows, LANES), lambda b, j: (b, 0, lane_chunks + j))

    sp_re, sp_im, fin_re, fin_im = pl.pallas_call(
        functools.partial(_ssm_scan_kernel, n_seq=n_seq, n_chunks=n_chunks),
        grid=(nb, lane_chunks),
        in_specs=[re_part(n_rows), im_part(n_rows), re_part(2), im_part(2)],
        out_specs=[re_part(n_rows), re_part(n_rows), re_part(n_seq), re_part(n_seq)],
        out_shape=[jax.ShapeDtypeStruct((nb, n_rows, sp), F32), jax.ShapeDtypeStruct((nb, n_rows, sp), F32),
                   jax.ShapeDtypeStruct((nb, n_seq, sp), F32), jax.ShapeDtypeStruct((nb, n_seq, sp), F32)],
        compiler_params=_params("parallel", "parallel"),
        name="ssm_scan",
    )(e, e, arow, arow)
    sp_spec = pl.BlockSpec((nb, tr, sp), lambda i: (0, i, 0))
    y = pl.pallas_call(
        functools.partial(_ssm_output_kernel, t_chunk=t_chunk),
        grid=(n_rows // tr,),
        in_specs=[pl.BlockSpec((tr, t_chunk * SSM_WIDTH), lambda i: (i, 0)), sp_spec, sp_spec,
                  _full(m_mat.shape), _full(v_mat.shape), _full((1, SSM_WIDTH))],
        out_specs=pl.BlockSpec((tr, t_chunk * SSM_WIDTH), lambda i: (i, 0)),
        out_shape=jax.ShapeDtypeStruct((n_rows, t_chunk * SSM_WIDTH), F32),
        compiler_params=_params("parallel"),
        name="ssm_output",
    )(uc, sp_re, sp_im, m_mat, v_mat, d_skip.reshape(1, SSM_WIDTH))
    return y.reshape(rows, SSM_WIDTH), fin_re, fin_im


def _ssm_step_kernel(u_ref, sre_ref, sim_ref, w0_ref, v0_ref, arow_ref, d_ref, y_ref, nre_ref, nim_ref):
    sp = SSM_STATES_PER_BLOCK
    for b in range(SSM_LANE_BLOCKS):
        lanes = slice(b * LANES, (b + 1) * LANES)
        states = slice(b * sp, (b + 1) * sp)
        u = u_ref[:, lanes]
        e = jnp.dot(u.astype(BF16), w0_ref[b], preferred_element_type=F32)
        a_re, a_im = arow_ref[b, 1:2, 0:sp], arow_ref[b, 1:2, sp:2 * sp]
        s_re, s_im = sre_ref[:, states], sim_ref[:, states]
        n_re = a_re * s_re - a_im * s_im + e[:, 0:sp]
        n_im = a_re * s_im + a_im * s_re + e[:, sp:2 * sp]
        nre_ref[:, states] = n_re
        nim_ref[:, states] = n_im
        sn = jnp.concatenate([n_re, n_im], axis=1).astype(BF16)
        y_ref[:, lanes] = jnp.dot(sn, v0_ref[b], preferred_element_type=F32) + d_ref[:, lanes] * u


def _ssm_step(u, s_re, s_im, w0, v0, arow, d_skip):
    rows = u.shape[0]
    ns = SSM_GROUPS * SSM_STATE
    args = (u, s_re.reshape(rows, ns), s_im.reshape(rows, ns), w0, v0, arow, d_skip.reshape(1, SSM_WIDTH))
    return pl.pallas_call(
        _ssm_step_kernel,
        grid=(1,),
        in_specs=[_full(a.shape) for a in args],
        out_specs=[_full((rows, SSM_WIDTH)), _full((rows, ns)), _full((rows, ns))],
        out_shape=[jax.ShapeDtypeStruct((rows, SSM_WIDTH), F32), jax.ShapeDtypeStruct((rows, ns), F32),
                   jax.ShapeDtypeStruct((rows, ns), F32)],
        compiler_params=_params("arbitrary"),
        name="ssm_step",
    )(*args)


def _dil_attn_kernel(q_ref, kc_ref, kp_ref, vc_ref, vp_ref, o_ref, *, dil, slopes):
    steps, gw, hd = ATT_STEPS, ATT_GROUP_WIDTH, ATT_HEAD_DIM
    j = pl.program_id(2)
    q = q_ref[...]
    kk = jnp.concatenate([kp_ref[...], kc_ref[...]], axis=0).astype(BF16)
    vv = jnp.concatenate([vp_ref[...], vc_ref[...]], axis=0).astype(BF16)
    qi = lax.broadcasted_iota(jnp.int32, (steps, 2 * steps), 0)
    kj = lax.broadcasted_iota(jnp.int32, (steps, 2 * steps), 1)
    dist = qi + steps - kj
    valid = (dist >= 0) & (dist <= steps) & ((j > 0) | (kj >= steps))
    distf = (dist * dil).astype(F32)
    lane = lax.broadcasted_iota(jnp.int32, (steps, gw), 1)
    out = jnp.zeros((steps, gw), F32)
    lse = jnp.zeros((steps, gw), F32)
    for h in range(HEADS_PER_GROUP):
        head = (lane >= h * hd) & (lane < (h + 1) * hd)
        qh = jnp.where(head, q, 0.0).astype(BF16)
        s = lax.dot_general(qh, kk, (((1,), (1,)), ((), ())), preferred_element_type=F32) * (hd ** -0.5)
        s = jnp.where(valid, s - slopes[h] * distf, NEG_BIG)
        m = jnp.max(s, axis=-1, keepdims=True)
        p = jnp.exp(s - m)
        den = jnp.sum(p, axis=-1, keepdims=True)
        oh = jnp.dot(p.astype(BF16), vv, preferred_element_type=F32) / den
        out = jnp.where(head, oh, out)
        lse = jnp.where(head, m + jnp.log(den), lse)
    o_ref[:, 0:gw] = out
    o_ref[:, gw:2 * gw] = lse


def _dil_attn(q, k, v, n_seq, seq_len, group):
    win, dil = DIL_PAIRS[group]
    steps, gw = ATT_STEPS, ATT_GROUP_WIDTH
    assert win // dil == steps and seq_len % (dil * steps) == 0
    sub = seq_len // dil
    blocks_per_token = ATT_WIDTH // gw

    def view(x):
        return x.reshape(n_seq, sub, dil * ATT_WIDTH)

    cur = pl.BlockSpec((None, steps, gw), lambda n, r, j: (n, j, blocks_per_token * r + group))
    prev = pl.BlockSpec((None, steps, gw), lambda n, r, j: (n, jnp.maximum(j - 1, 0), blocks_per_token * r + group))
    slopes = tuple(_alibi_slopes()[group * HEADS_PER_GROUP:(group + 1) * HEADS_PER_GROUP])
    out = pl.pallas_call(
        functools.partial(_dil_attn_kernel, dil=dil, slopes=slopes),
        grid=(n_seq, dil, sub // steps),
        in_specs=[cur, cur, prev, cur, prev],
        out_specs=pl.BlockSpec((None, steps, 2 * gw), lambda n, r, j: (n, j, r)),
        out_shape=jax.ShapeDtypeStruct((n_seq, sub, dil * 2 * gw), F32),
        compiler_params=_params("parallel", "parallel", "parallel"),
        name=f"dil_attn_w{win}",
    )(view(q), view(k), view(k), view(v), view(v))
    return out.reshape(n_seq * seq_len, 2 * gw)


def _mem_attn_kernel(q_ref, mk_ref, mv_ref, o_ref):
    hd = MEM_HEAD_DIM
    for h in range(MEM_HEADS):
        lanes = slice(h * hd, (h + 1) * hd)
        s = lax.dot_general(q_ref[:, lanes].astype(BF16), mk_ref[:, lanes].astype(BF16),
                            (((1,), (1,)), ((), ())), preferred_element_type=F32) * (hd ** -0.5)
        m = jnp.max(s, axis=-1, keepdims=True)
        p = jnp.exp(s - m)
        den = jnp.sum(p, axis=-1, keepdims=True)
        o_ref[:, lanes] = jnp.dot(p.astype(BF16), mv_ref[:, lanes].astype(BF16), preferred_element_type=F32) / den


def _mem_attn(qm, mk, mv, n_seq, seq_len, tm):
    assert seq_len % tm == 0
    tiles = seq_len // tm
    kv = pl.BlockSpec((None, N_MEM, MEM_WIDTH), lambda n, i: (n, 0, 0))
    return pl.pallas_call(
        _mem_attn_kernel,
        grid=(n_seq, tiles),
        in_specs=[pl.BlockSpec((tm, MEM_WIDTH), lambda n, i: (n * tiles + i, 0)), kv, kv],
        out_specs=pl.BlockSpec((tm, MEM_WIDTH), lambda n, i: (n * tiles + i, 0)),
        out_shape=jax.ShapeDtypeStruct(qm.shape, F32),
        compiler_params=_params("parallel", "parallel"),
        name="mem_attn",
    )(qm, mk, mv)


def _row_attention(q, k_blocks, v_blocks, biases, scale):
    qb = q[:, None]
    scores = []
    for k, bias in zip(k_blocks, biases):
        s = jnp.sum(k * qb, axis=-1, keepdims=True) * scale
        scores.append(s if bias is None else s + bias)
    m = functools.reduce(jnp.maximum, [jnp.max(s, axis=1, keepdims=True) for s in scores])
    den, acc = 0.0, 0.0
    for s, v in zip(scores, v_blocks):
        p = jnp.exp(s - m)
        den = den + jnp.sum(p, axis=1, keepdims=True)
        acc = acc + jnp.sum(p * v, axis=1, keepdims=True)
    return acc / den, m + jnp.log(den)


def _cache_attn_kernel(q_ref, kn_ref, vn_ref, k1_ref, v1_ref, k2_ref, v2_ref, k3_ref, v3_ref, o_ref,
                       qt_scr, knt_scr, vnt_scr, *, slopes):
    hpg, hd = HEADS_PER_GROUP, ATT_HEAD_DIM
    j = pl.program_id(1)
    nb = q_ref.shape[0]
    scale = hd ** -0.5
    qt_scr[...] = q_ref[...].T
    knt_scr[...] = kn_ref[...].T
    vnt_scr[...] = vn_ref[...].T
    outs = [[] for _ in range(nb)]
    lses = [[] for _ in range(nb)]
    for g, (k_ref, v_ref) in enumerate(((k1_ref, v1_ref), (k2_ref, v2_ref), (k3_ref, v3_ref))):
        dil = DIL_PAIRS[g][1]
        n_pos = k_ref.shape[-1]
        rows = pl.ds(pl.multiple_of((g * hpg + j) * hd, hd), hd)
        qg, kng, vng = qt_scr[rows, :], knt_scr[rows, :], vnt_scr[rows, :]
        slope = sum(jnp.where(j == h, slopes[g * hpg + h], 0.0) for h in range(hpg))
        back = n_pos - lax.broadcasted_iota(jnp.int32, (1, n_pos), 1)
        bias = jnp.where(back % dil == 0, -slope * back.astype(F32), NEG_BIG)
        for b in range(nb):
            qc = qg[:, b:b + 1]
            s = jnp.sum(k_ref[b] * qc, axis=0, keepdims=True) * scale + bias
            s_new = jnp.sum(qc * kng[:, b:b + 1], axis=0, keepdims=True) * scale
            m = jnp.maximum(jnp.max(s, axis=1, keepdims=True), s_new)
            p = jnp.exp(s - m)
            p_new = jnp.exp(s_new - m)
            den = jnp.sum(p, axis=1, keepdims=True) + p_new
            acc = jnp.sum(v_ref[b] * p, axis=1, keepdims=True) + p_new * vng[:, b:b + 1]
            outs[b].append(acc / den)
            lses[b].append(m + jnp.log(den))
    cols = []
    for b in range(nb):
        top = functools.reduce(jnp.maximum, lses[b])
        ws = [jnp.exp(l - top) for l in lses[b]]
        cols.append(sum(w * o for w, o in zip(ws, outs[b])) / sum(ws))
    o_ref[...] = jnp.concatenate(cols, axis=1)


def _cache_attn(q, k_new, v_new, caches_k, caches_v, nb):
    rows = q.shape[0]
    hpg, hd = HEADS_PER_GROUP, ATT_HEAD_DIM
    assert rows % nb == 0 and nb % SUBLANES == 0
    new = pl.BlockSpec((nb, ATT_WIDTH), lambda i, j: (i, 0))
    cache_specs, cache_args = [], []
    for g, (win, dil) in enumerate(DIL_PAIRS):
        for c in (caches_k[g], caches_v[g]):
            assert c.shape == (rows, win, hpg, hd) and win % dil == 0, c.shape
            cache_args.append(c.transpose(0, 2, 3, 1))
            cache_specs.append(pl.BlockSpec((nb, None, hd, win), lambda i, j: (i, j, 0, 0)))
    out = pl.pallas_call(
        functools.partial(_cache_attn_kernel, slopes=tuple(_alibi_slopes())),
        grid=(rows // nb, hpg),
        in_specs=[new, new, new] + cache_specs,
        out_specs=pl.BlockSpec((None, None, hd, nb), lambda i, j: (j, i, 0, 0)),
        out_shape=jax.ShapeDtypeStruct((hpg, rows // nb, hd, nb), F32),
        scratch_shapes=[pltpu.VMEM((ATT_WIDTH, nb), F32) for _ in range(3)],
        compiler_params=_params("parallel", "arbitrary"),
        name="cache_attn",
    )(q, k_new, v_new, *cache_args)
    return out.transpose(1, 3, 0, 2).reshape(rows, hpg * hd)


def _cache_mem_attn_kernel(q_ref, k_ref, v_ref, o_ref):
    o, _ = _row_attention(q_ref[...], [k_ref[...]], [v_ref[...]], [None], MEM_HEAD_DIM ** -0.5)
    o_ref[...] = o[:, 0]


def _cache_mem_attn(qm, mem_k, mem_v, nb):
    rows = qm.shape[0]
    assert rows % nb == 0
    kv = pl.BlockSpec((nb, N_MEM, MEM_HEADS, MEM_HEAD_DIM), lambda i: (i, 0, 0, 0))
    q = pl.BlockSpec((nb, MEM_HEADS, MEM_HEAD_DIM), lambda i: (i, 0, 0))
    return pl.pallas_call(
        _cache_mem_attn_kernel,
        grid=(rows // nb,),
        in_specs=[q, kv, kv],
        out_specs=q,
        out_shape=jax.ShapeDtypeStruct(qm.shape, F32),
        compiler_params=_params("parallel"),
        name="cache_mem_attn",
    )(qm, mem_k, mem_v)


def _merge_kernel(x_ref, y_ref, *refs, n_att):
    att_refs = refs[:n_att]
    mem_ref, gl_ref, wglu_ref, watt_ref, wmem_ref, wout_ref, o_ref = refs[n_att:]
    d, gw = D_MODEL, ATT_GROUP_WIDTH
    z = jax.nn.gelu(y_ref[...]).astype(BF16)
    glu = jnp.dot(z, wglu_ref[...], preferred_element_type=F32)
    merged = jax.nn.sigmoid(gl_ref[:, 0:d]) * (glu[:, 0:d] * jax.nn.sigmoid(glu[:, d:2 * d]))
    if n_att == 1:
        att = att_refs[0][...]
    else:
        lses = [r[:, gw:2 * gw] for r in att_refs]
        top = functools.reduce(jnp.maximum, lses)
        ws = [jnp.exp(l - top) for l in lses]
        att = sum(w * r[:, 0:gw] for w, r in zip(ws, att_refs)) / sum(ws)
    b_att = jnp.dot(att.astype(BF16), watt_ref[...], preferred_element_type=F32)
    merged = merged + jax.nn.sigmoid(gl_ref[:, d:2 * d]) * b_att
    b_mem = jnp.dot(mem_ref[...].astype(BF16), wmem_ref[...], preferred_element_type=F32)
    merged = merged + jax.nn.sigmoid(gl_ref[:, 2 * d:3 * d]) * b_mem
    o_ref[...] = x_ref[...] + jnp.dot(merged.astype(BF16), wout_ref[...], preferred_element_type=F32)


def _merge(x, y_ssm, atts, o_mem, gl, w_glu, w_att_o, w_mem_o, w_out, tm):
    rows = x.shape[0]
    assert rows % tm == 0

    def tile(a):
        return pl.BlockSpec((tm, a.shape[1]), lambda i: (i, 0))

    acts = (x, y_ssm, *atts, o_mem, gl)
    weights = (w_glu, w_att_o, w_mem_o, w_out)
    return pl.pallas_call(
        functools.partial(_merge_kernel, n_att=len(atts)),
        grid=(rows // tm,),
        in_specs=[tile(a) for a in acts] + [_full(w.shape) for w in weights],
        out_specs=tile(x),
        out_shape=jax.ShapeDtypeStruct(x.shape, F32),
        compiler_params=_params("parallel"),
        name="merge",
    )(*acts, *weights)


def _ffn_kernel(*refs, tm, tiles_per_seq, stepwise):
    if stepwise:
        x_ref, g2_ref, wup_ref, cw_ref, cb_ref, wdn_ref, gf_ref, prev_ref, y_ref, conv_ref = refs
    else:
        x_ref, g2_ref, wup_ref, cw_ref, cb_ref, wdn_ref, gf_ref, y_ref, conv_ref, a_scr = refs
        i = pl.program_id(0)
        first = i % tiles_per_seq == 0

        @pl.when(first)
        def _():
            a_scr[0:SUBLANES, :] = jnp.zeros((SUBLANES, D_FF), F32)

        @pl.when(jnp.logical_not(first))
        def _():
            a_scr[0:SUBLANES, :] = a_scr[tm:tm + SUBLANES, :]

    x = x_ref[...]
    h = _rmsnorm(x, g2_ref[...]).astype(BF16)
    acc = jnp.zeros((tm, D_MODEL), F32)
    for c0 in range(0, D_FF, FFN_CHUNK):
        cols = slice(c0, c0 + FFN_CHUNK)
        a = jnp.dot(h, wup_ref[:, cols], preferred_element_type=F32)
        if stepwise:
            a2, a1 = prev_ref[:, cols], prev_ref[:, D_FF + c0:D_FF + c0 + FFN_CHUNK]
            conv_ref[:, cols] = a1
            conv_ref[:, D_FF + c0:D_FF + c0 + FFN_CHUNK] = a
        else:
            a_scr[SUBLANES:SUBLANES + tm, cols] = a
            a1 = a_scr[SUBLANES - 1:SUBLANES - 1 + tm, cols]
            a2 = a_scr[SUBLANES - 2:SUBLANES - 2 + tm, cols]
        c = a2 * cw_ref[0:1, cols] + a1 * cw_ref[1:2, cols] + a * cw_ref[2:3, cols] + cb_ref[:, cols]
        v = jnp.dot(h, wup_ref[:, D_FF + c0:D_FF + c0 + FFN_CHUNK], preferred_element_type=F32)
        acc = acc + jnp.dot((jax.nn.gelu(c) * v).astype(BF16), wdn_ref[cols, :], preferred_element_type=F32)
    y_ref[...] = _rmsnorm(x + acc, gf_ref[...])
    if not stepwise:
        @pl.when(i % tiles_per_seq == tiles_per_seq - 1)
        def _():
            conv_ref[...] = a_scr[SUBLANES + tm - (CONV_W - 1):SUBLANES + tm, :]


def _ffn(x, g2, w_up, conv_w, conv_b, w_down, gf, tm, n_seq=None, prev=None):
    rows, d = x.shape
    assert rows % tm == 0 and D_FF % FFN_CHUNK == 0 and FFN_CHUNK % LANES == 0
    stepwise = prev is not None
    weights = (g2.reshape(1, d), w_up, conv_w, conv_b.reshape(1, D_FF), w_down, gf.reshape(1, d))
    in_specs = [pl.BlockSpec((tm, d), lambda i: (i, 0))] + [_full(w.shape) for w in weights]
    args = (x,) + weights
    if stepwise:
        tiles_per_seq = 1
        in_specs.append(pl.BlockSpec((tm, 2 * D_FF), lambda i: (i, 0)))
        args += (prev,)
        conv_spec = pl.BlockSpec((tm, 2 * D_FF), lambda i: (i, 0))
        conv_shape = jax.ShapeDtypeStruct((rows, 2 * D_FF), F32)
        scratch = []
    else:
        tiles_per_seq = rows // n_seq // tm
        assert tiles_per_seq * tm * n_seq == rows
        conv_spec = pl.BlockSpec((None, CONV_W - 1, D_FF), lambda i: (i // tiles_per_seq, 0, 0))
        conv_shape = jax.ShapeDtypeStruct((n_seq, CONV_W - 1, D_FF), F32)
        scratch = [pltpu.VMEM((tm + 2 * SUBLANES, D_FF), F32)]
    return pl.pallas_call(
        functools.partial(_ffn_kernel, tm=tm, tiles_per_seq=tiles_per_seq, stepwise=stepwise),
        grid=(rows // tm,),
        in_specs=in_specs,
        out_specs=[pl.BlockSpec((tm, d), lambda i: (i, 0)), conv_spec],
        out_shape=[jax.ShapeDtypeStruct((rows, d), F32), conv_shape],
        scratch_shapes=scratch,
        compiler_params=_params("arbitrary"),
        name="ffn_step" if stepwise else "ffn",
    )(*args)


def kernel(x_prompt, x_sample, state_ssm_re, state_ssm_im, cache_w1_k, cache_w1_v, cache_w2_k, cache_w2_v, cache_w3_k, cache_w3_v, cache_mem_k, cache_mem_v, state_ffn_conv, mem_prompt, norm1_g, w_in, ssm_a_re, ssm_a_im, ssm_log_dt, ssm_b_re, ssm_b_im, ssm_c_re, ssm_c_im, ssm_d, w_ssm_glu, w_att_o, mem_norm_g, w_mem_kv, w_mem_o, w_out, norm2_g, w_up, ffn_conv_w, ffn_conv_b, w_down, final_norm_g):
    n_seq, seq_len, d = x_prompt.shape
    n_dec, dec_len, _ = x_sample.shape
    depth = norm1_g.shape[0]
    assert d == D_MODEL and depth == 1 and dec_len == 1
    assert w_in.shape == (depth, D_MODEL, sum(IN_SPLITS)) and w_up.shape == (depth, D_MODEL, 2 * D_FF)
    assert mem_prompt.shape == (n_seq, N_MEM, D_MODEL)
    assert ssm_a_re.shape == (depth, SSM_GROUPS, SSM_STATE)
    assert seq_len % ROW_TILE == 0 and n_dec % SUBLANES == 0
    hpg, hd = HEADS_PER_GROUP, ATT_HEAD_DIM
    rows_p = n_seq * seq_len
    dec_tile = n_dec if n_dec <= ROW_TILE else ROW_TILE

    wb = lambda w: w[0].astype(BF16)
    w_in_b, w_glu_b, w_att_b, w_memkv_b = wb(w_in), wb(w_ssm_glu), wb(w_att_o), wb(w_mem_kv)
    w_memo_b, w_out_b, w_up_b, w_down_b = wb(w_mem_o), wb(w_out), wb(w_up), wb(w_down)

    m_mat, w_mat, v_mat, v0_mat, arow = _ssm_prep(ssm_a_re[0], ssm_a_im[0], ssm_log_dt[0], ssm_b_re[0],
                                                  ssm_b_im[0], ssm_c_re[0], ssm_c_im[0])
    w0_mat = w_mat[:, (SSM_T - 1) * LANES:, :]

    xp = x_prompt.reshape(rows_p, d)
    u, q, k, v, qm, gl = _norm_proj(xp, norm1_g[0], w_in_b, IN_SPLITS, ROW_TILE)
    y_ssm, fin_re, fin_im = _ssm_prompt(u, n_seq, m_mat, w_mat, v_mat, arow, ssm_d[0])
    atts = [_dil_attn(q, k, v, n_seq, seq_len, g) for g in range(len(DIL_PAIRS))]
    mk, mv = _norm_proj(mem_prompt.reshape(n_seq * N_MEM, d), mem_norm_g[0], w_memkv_b, (MEM_WIDTH, MEM_WIDTH),
                        ROW_TILE)
    o_mem = _mem_attn(qm, mk.reshape(n_seq, N_MEM, MEM_WIDTH), mv.reshape(n_seq, N_MEM, MEM_WIDTH),
                      n_seq, seq_len, 2 * ROW_TILE)
    x1 = _merge(xp, y_ssm, atts, o_mem, gl, w_glu_b, w_att_b, w_memo_b, w_out_b, ROW_TILE)
    y_p, conv_p = _ffn(x1, norm2_g[0], w_up_b, ffn_conv_w[0], ffn_conv_b[0], w_down_b, final_norm_g,
                       ROW_TILE, n_seq=n_seq)

    def final_state(s):
        s = s.reshape(SSM_LANE_BLOCKS, n_seq, SSM_GROUPS_PER_BLOCK, SSM_STATE)
        return s.transpose(1, 0, 2, 3).reshape(1, n_seq, SSM_GROUPS, SSM_STATE)
    k3 = k.reshape(n_seq, seq_len, ATT_WIDTH)
    v3 = v.reshape(n_seq, seq_len, ATT_WIDTH)
    gw = ATT_GROUP_WIDTH
    win_p = []
    for g, (win, _) in enumerate(DIL_PAIRS):
        keep = min(win, seq_len)
        for t in (k3, v3):
            win_p.append(t[:, seq_len - keep:, g * gw:(g + 1) * gw].reshape(1, n_seq, keep, hpg, hd))
    mem_kv = [mk.reshape(1, n_seq, N_MEM, MEM_HEADS, MEM_HEAD_DIM), mv.reshape(1, n_seq, N_MEM, MEM_HEADS, MEM_HEAD_DIM)]

    xs = x_sample.reshape(n_dec, d)
    us, qs, ks, vs, qms, gls = _norm_proj(xs, norm1_g[0], w_in_b, IN_SPLITS, dec_tile)
    ys_ssm, sn_re, sn_im = _ssm_step(us, state_ssm_re[0], state_ssm_im[0], w0_mat, v0_mat, arow, ssm_d[0])
    n_g = len(DIL_PAIRS)
    ks4, vs4 = (t.reshape(n_dec, n_g, hpg, hd) for t in (ks, vs))
    att_s = _cache_attn(qs, ks, vs, (cache_w1_k[0], cache_w2_k[0], cache_w3_k[0]),
                        (cache_w1_v[0], cache_w2_v[0], cache_w3_v[0]), SUBLANES)
    mem_s = _cache_mem_attn(qms.reshape(n_dec, MEM_HEADS, MEM_HEAD_DIM), cache_mem_k[0], cache_mem_v[0],
                            CACHE_ROWS_PER_STEP)
    xs1 = _merge(xs, ys_ssm, [att_s], mem_s.reshape(n_dec, MEM_WIDTH), gls,
                 w_glu_b, w_att_b, w_memo_b, w_out_b, dec_tile)
    y_s, conv_s = _ffn(xs1, norm2_g[0], w_up_b, ffn_conv_w[0], ffn_conv_b[0], w_down_b, final_norm_g,
                       dec_tile, prev=state_ffn_conv[0].reshape(n_dec, (CONV_W - 1) * D_FF))

    win_s = []
    for g in range(len(DIL_PAIRS)):
        win_s += [ks4[None, :, g:g + 1], vs4[None, :, g:g + 1]]

    return (y_p.reshape(n_seq, seq_len, d), y_s.reshape(n_dec, 1, d),
            final_state(fin_re), final_state(fin_im), *win_p, *mem_kv, conv_p[None],
            sn_re.reshape(1, n_dec, SSM_GROUPS, SSM_STATE), sn_im.reshape(1, n_dec, SSM_GROUPS, SSM_STATE),
            *win_s, conv_s.reshape(1, n_dec, CONV_W - 1, D_FF))
```

```python
import functools

import jax
import jax.numpy as jnp
from jax import lax
from jax.experimental import pallas as pl
from jax.experimental.pallas import tpu as pltpu

F32 = jnp.float32
BF16 = jnp.bfloat16

D_MODEL = 1024
SSM_WIDTH = 512
SSM_GROUP = 16
SSM_GROUPS = 32
SSM_STATE = 64
ATT_HEAD_DIM = 64
HEADS_PER_GROUP = 4
DIL_PAIRS = ((128, 1), (512, 4), (2048, 16))
ATT_HEADS = len(DIL_PAIRS) * HEADS_PER_GROUP
ATT_WIDTH = ATT_HEADS * ATT_HEAD_DIM
ATT_GROUP_WIDTH = HEADS_PER_GROUP * ATT_HEAD_DIM
ATT_STEPS = 128
N_MEM = 256
MEM_HEADS = 4
MEM_HEAD_DIM = 128
MEM_WIDTH = MEM_HEADS * MEM_HEAD_DIM
N_BRANCH = 3
D_FF = 2816
CONV_W = 3
EPS = 1e-6
IN_SPLITS = (SSM_WIDTH, ATT_WIDTH, ATT_WIDTH, ATT_WIDTH, MEM_WIDTH, N_BRANCH * D_MODEL)

LANES = 128
SUBLANES = 8
VMEM_LIMIT_BYTES = 56 * 1024 * 1024

SSM_T = 8
SSM_LANE_BLOCKS = SSM_WIDTH // LANES
SSM_GROUPS_PER_BLOCK = LANES // SSM_GROUP
SSM_STATES_PER_BLOCK = SSM_GROUPS_PER_BLOCK * SSM_STATE
SSM_STATE_LANES = 2 * SSM_STATES_PER_BLOCK

ROW_TILE = 256
FFN_CHUNK = 1408
CACHE_ROWS_PER_STEP = 4
ATT_CHUNK = 2048
NEG_BIG = -1e30


def _alibi_slopes():
    return [float(2.0 ** (-8.0 * h / ATT_HEADS)) for h in range(1, ATT_HEADS + 1)]


def _params(*sem):
    return pltpu.CompilerParams(dimension_semantics=sem, vmem_limit_bytes=VMEM_LIMIT_BYTES)


def _rmsnorm(x, g):
    ms = jnp.mean(x * x, axis=-1, keepdims=True)
    return x * lax.rsqrt(ms + EPS) * g


def _full(shape):
    nd = len(shape)
    return pl.BlockSpec(shape, lambda *_: (0,) * nd)


def _norm_proj_kernel(x_ref, g_ref, w_ref, *out_refs, splits):
    h = _rmsnorm(x_ref[...], g_ref[...]).astype(BF16)
    off = 0
    for o_ref, width in zip(out_refs, splits):
        for c0 in range(0, width, 512):
            cw = min(512, width - c0)
            o_ref[:, c0:c0 + cw] = jnp.dot(h, w_ref[:, off + c0:off + c0 + cw], preferred_element_type=F32)
        off += width


def _norm_proj(x, g, w_bf16, splits, tm):
    rows, d = x.shape
    assert rows % tm == 0 and sum(splits) == w_bf16.shape[1]
    return pl.pallas_call(
        functools.partial(_norm_proj_kernel, splits=splits),
        grid=(rows // tm,),
        in_specs=[pl.BlockSpec((tm, d), lambda i: (i, 0)), _full((1, d)), _full(w_bf16.shape)],
        out_specs=[pl.BlockSpec((tm, s), lambda i: (i, 0)) for s in splits],
        out_shape=[jax.ShapeDtypeStruct((rows, s), F32) for s in splits],
        compiler_params=_params("parallel"),
        name="norm_proj",
    )(x, g.reshape(1, d), w_bf16)


def _ssm_layout(a_re, a_im, log_dt, b_re, b_im, c_re, c_im):
    nb, gpb, p, c = SSM_LANE_BLOCKS, SSM_GROUPS_PER_BLOCK, SSM_STATE, SSM_GROUP
    rows = jnp.stack([a_re.reshape(nb, gpb * p), a_im.reshape(nb, gpb * p),
                      jnp.repeat(log_dt, p).reshape(nb, gpb * p)], axis=1)
    cols = rows.transpose(0, 2, 1)
    eye = jnp.eye(gpb, dtype=F32)

    def place_b(b):
        return jnp.einsum('bgpc,gh->bgchp', b.reshape(nb, gpb, p, c), eye).reshape(nb, gpb * c, gpb * p)

    def place_c(m):
        return jnp.einsum('bgcp,gh->bhpgc', m.reshape(nb, gpb, c, p), eye).reshape(nb, gpb * p, gpb * c)

    return rows, cols, place_b(b_re), place_b(b_im), place_c(c_re), place_c(c_im)


def _ssm_prep_kernel(rows_ref, cols_ref, bre_ref, bim_ref, cre_ref, cim_ref,
                     m_ref, w_ref, v_ref, v0_ref, arow_ref, *, t_chunk):
    sp = SSM_STATES_PER_BLOCK

    def apow(k, a_re, a_im, dt):
        mag = jnp.exp(a_re * dt * k)
        ang = a_im * dt * k
        return mag * jnp.cos(ang), mag * jnp.sin(ang)

    a_re, a_im, dt = rows_ref[0:1, :], rows_ref[1:2, :], jnp.exp(rows_ref[2:3, :])
    ab_re, ab_im = apow(1.0, a_re, a_im, dt)
    den = a_re * a_re + a_im * a_im
    q_re = ((ab_re - 1.0) * a_re + ab_im * a_im) / den
    q_im = (ab_im * a_re - (ab_re - 1.0) * a_im) / den
    bre, bim = bre_ref[...], bim_ref[...]
    bb_re = q_re * bre - q_im * bim
    bb_im = q_re * bim + q_im * bre
    cre, cim = cre_ref[...], cim_ref[...]

    m_ref[...] = jnp.zeros(m_ref.shape, m_ref.dtype)
    for k in range(t_chunk):
        pk_re, pk_im = apow(float(k), a_re, a_im, dt)
        bk_re = bb_re * pk_re - bb_im * pk_im
        bk_im = bb_re * pk_im + bb_im * pk_re
        t = t_chunk - 1 - k
        w_ref[t * LANES:(t + 1) * LANES, 0:sp] = bk_re.astype(w_ref.dtype)
        w_ref[t * LANES:(t + 1) * LANES, sp:2 * sp] = bk_im.astype(w_ref.dtype)
        kk = (jnp.dot(bk_re, cre, precision=lax.Precision.HIGHEST, preferred_element_type=F32)
              - jnp.dot(bk_im, cim, precision=lax.Precision.HIGHEST, preferred_element_type=F32))
        kk = kk.astype(m_ref.dtype)
        for t0 in range(t_chunk - k):
            m_ref[t0 * LANES:(t0 + 1) * LANES, (t0 + k) * LANES:(t0 + k + 1) * LANES] = kk

    ca_re, ca_im, cdt = cols_ref[:, 0:1], cols_ref[:, 1:2], jnp.exp(cols_ref[:, 2:3])
    for t in range(t_chunk):
        pc_re, pc_im = apow(float(t + 1), ca_re, ca_im, cdt)
        v_ref[0:sp, t * LANES:(t + 1) * LANES] = (cre * pc_re - cim * pc_im).astype(v_ref.dtype)
        v_ref[sp:2 * sp, t * LANES:(t + 1) * LANES] = (-(cre * pc_im + cim * pc_re)).astype(v_ref.dtype)
    v0_ref[0:sp, :] = cre.astype(v0_ref.dtype)
    v0_ref[sp:2 * sp, :] = (-cim).astype(v0_ref.dtype)

    pt_re, pt_im = apow(float(t_chunk), a_re, a_im, dt)
    arow_ref[0:1, 0:sp] = pt_re
    arow_ref[0:1, sp:2 * sp] = pt_im
    arow_ref[1:2, 0:sp] = ab_re
    arow_ref[1:2, sp:2 * sp] = ab_im


def _ssm_prep(a_re, a_im, log_dt, b_re, b_im, c_re, c_im):
    rows, cols, pbre, pbim, pcre, pcim = _ssm_layout(a_re, a_im, log_dt, b_re, b_im, c_re, c_im)
    nb, sp, sl, tl = SSM_LANE_BLOCKS, SSM_STATES_PER_BLOCK, SSM_STATE_LANES, SSM_T * LANES

    def blk(shape):
        return pl.BlockSpec((None,) + shape, lambda b: (b, 0, 0))

    return pl.pallas_call(
        functools.partial(_ssm_prep_kernel, t_chunk=SSM_T),
        grid=(nb,),
        in_specs=[blk((3, sp)), blk((sp, 3)), blk((LANES, sp)), blk((LANES, sp)), blk((sp, LANES)), blk((sp, LANES))],
        out_specs=[blk((tl, tl)), blk((tl, sl)), blk((sl, tl)), blk((sl, LANES)), blk((2, sl))],
        out_shape=[jax.ShapeDtypeStruct((nb, tl, tl), BF16), jax.ShapeDtypeStruct((nb, tl, sl), BF16),
                   jax.ShapeDtypeStruct((nb, sl, tl), BF16), jax.ShapeDtypeStruct((nb, sl, LANES), BF16),
                   jax.ShapeDtypeStruct((nb, 2, sl), F32)],
        compiler_params=_params("parallel"),
        name="ssm_prep",
    )(rows, cols, pbre, pbim, pcre, pcim)


def _chunk_tokens(u_ref, t, t_chunk):
    return u_ref[pl.ds(t, u_ref.shape[0] // t_chunk, stride=t_chunk), :]


def _chunk_lanes(u_ref, t_chunk):
    return jnp.concatenate([_chunk_tokens(u_ref, t, t_chunk) for t in range(t_chunk)], axis=1)


def _ssm_end_state_kernel(*refs, t_chunk):
    nb = SSM_LANE_BLOCKS
    u_refs, w_ref, e_ref = refs[:nb], refs[nb], refs[nb + 1]
    for b in range(nb):
        ub = _chunk_lanes(u_refs[b], t_chunk).astype(BF16)
        e_ref[b] = jnp.dot(ub, w_ref[b], preferred_element_type=F32)


def _ssm_scan_kernel(ere_ref, eim_ref, are_ref, aim_ref, spre_ref, spim_ref, fre_ref, fim_ref, *, n_seq, n_chunks):
    a_re = are_ref[0:1, :]
    a_im = aim_ref[0:1, :]

    def body(c, carry):
        s_re, s_im = carry
        rows = pl.ds(c, n_seq, stride=n_chunks)
        spre_ref[rows, :] = s_re
        spim_ref[rows, :] = s_im
        return (a_re * s_re - a_im * s_im + ere_ref[rows, :], a_re * s_im + a_im * s_re + eim_ref[rows, :])

    zero = jnp.zeros((n_seq, LANES), F32)
    s_re, s_im = lax.fori_loop(0, n_chunks, body, (zero, zero))
    fre_ref[...] = s_re
    fim_ref[...] = s_im


def _ssm_output_kernel(*refs, t_chunk):
    nb = SSM_LANE_BLOCKS
    u_refs, (spre_ref, spim_ref, m_ref, v_ref, d_ref), y_refs = refs[:nb], refs[nb:nb + 5], refs[nb + 5:]
    for b in range(nb):
        ub = _chunk_lanes(u_refs[b], t_chunk).astype(BF16)
        yb = jnp.dot(ub, m_ref[b], preferred_element_type=F32)
        sprev = jnp.concatenate([spre_ref[b], spim_ref[b]], axis=1).astype(BF16)
        yb = yb + jnp.dot(sprev, v_ref[b], preferred_element_type=F32)
        d = d_ref[:, b * LANES:(b + 1) * LANES]
        for t in range(t_chunk):
            y_t = yb[:, t * LANES:(t + 1) * LANES] + d * _chunk_tokens(u_refs[b], t, t_chunk)
            y_refs[b][pl.ds(t, yb.shape[0], stride=t_chunk), :] = y_t


def _ssm_prompt(u, n_seq, m_mat, w_mat, v_mat, arow, d_skip):
    rows, t_chunk, nb, sl = u.shape[0], SSM_T, SSM_LANE_BLOCKS, SSM_STATE_LANES
    assert rows % (t_chunk * n_seq) == 0
    n_rows = rows // t_chunk
    n_chunks = n_rows // n_seq
    tr = min(ROW_TILE, n_rows)
    assert n_rows % tr == 0
    u_specs = [pl.BlockSpec((tr * t_chunk, LANES), lambda i, b=b: (i, b)) for b in range(nb)]
    e = pl.pallas_call(
        functools.partial(_ssm_end_state_kernel, t_chunk=t_chunk),
        grid=(n_rows // tr,),
        in_specs=u_specs + [_full(w_mat.shape)],
        out_specs=pl.BlockSpec((nb, tr, sl), lambda i: (0, i, 0)),
        out_shape=jax.ShapeDtypeStruct((nb, n_rows, sl), F32),
        compiler_params=_params("parallel"),
        name="ssm_end_state",
    )(*([u] * nb), w_mat)
    sp = SSM_STATES_PER_BLOCK
    lane_chunks = sp // LANES

    def re_part(shape_rows):
        return pl.BlockSpec((None, shape_rows, LANES), lambda b, j: (b, 0, j))

    def im_part(shape_rows):
        return pl.BlockSpec((None, shape_rows, LANES), lambda b, j: (b, 0, lane_chunks + j))

    sp_re, sp_im, fin_re, fin_im = pl.pallas_call(
        functools.partial(_ssm_scan_kernel, n_seq=n_seq, n_chunks=n_chunks),
        grid=(nb, lane_chunks),
        in_specs=[re_part(n_rows), im_part(n_rows), re_part(2), im_part(2)],
        out_specs=[re_part(n_rows), re_part(n_rows), re_part(n_seq), re_part(n_seq)],
        out_shape=[jax.ShapeDtypeStruct((nb, n_rows, sp), F32), jax.ShapeDtypeStruct((nb, n_rows, sp), F32),
                   jax.ShapeDtypeStruct((nb, n_seq, sp), F32), jax.ShapeDtypeStruct((nb, n_seq, sp), F32)],
        compiler_params=_params("parallel", "parallel"),
        name="ssm_scan",
    )(e, e, arow, arow)
    sp_spec = pl.BlockSpec((nb, tr, sp), lambda i: (0, i, 0))
    ys = pl.pallas_call(
        functools.partial(_ssm_output_kernel, t_chunk=t_chunk),
        grid=(n_rows // tr,),
        in_specs=u_specs + [sp_spec, sp_spec, _full(m_mat.shape), _full(v_mat.shape), _full((1, SSM_WIDTH))],
        out_specs=[pl.BlockSpec((tr * t_chunk, LANES), lambda i: (i, 0)) for _ in range(nb)],
        out_shape=[jax.ShapeDtypeStruct((rows, LANES), F32) for _ in range(nb)],
        compiler_params=_params("parallel"),
        name="ssm_output",
    )(*([u] * nb), sp_re, sp_im, m_mat, v_mat, d_skip.reshape(1, SSM_WIDTH))
    return ys, fin_re, fin_im


def _ssm_step_kernel(u_ref, sre_ref, sim_ref, w0_ref, v0_ref, arow_ref, d_ref, y_ref, nre_ref, nim_ref):
    sp = SSM_STATES_PER_BLOCK
    for b in range(SSM_LANE_BLOCKS):
        lanes = slice(b * LANES, (b + 1) * LANES)
        states = slice(b * sp, (b + 1) * sp)
        u = u_ref[:, lanes]
        e = jnp.dot(u.astype(BF16), w0_ref[b], preferred_element_type=F32)
        a_re, a_im = arow_ref[b, 1:2, 0:sp], arow_ref[b, 1:2, sp:2 * sp]
        s_re, s_im = sre_ref[:, states], sim_ref[:, states]
        n_re = a_re * s_re - a_im * s_im + e[:, 0:sp]
        n_im = a_re * s_im + a_im * s_re + e[:, sp:2 * sp]
        nre_ref[:, states] = n_re
        nim_ref[:, states] = n_im
        sn = jnp.concatenate([n_re, n_im], axis=1).astype(BF16)
        y_ref[:, lanes] = jnp.dot(sn, v0_ref[b], preferred_element_type=F32) + d_ref[:, lanes] * u


def _ssm_step(u, s_re, s_im, w0, v0, arow, d_skip):
    rows = u.shape[0]
    ns = SSM_GROUPS * SSM_STATE
    args = (u, s_re.reshape(rows, ns), s_im.reshape(rows, ns), w0, v0, arow, d_skip.reshape(1, SSM_WIDTH))
    return pl.pallas_call(
        _ssm_step_kernel,
        grid=(1,),
        in_specs=[_full(a.shape) for a in args],
        out_specs=[_full((rows, SSM_WIDTH)), _full((rows, ns)), _full((rows, ns))],
        out_shape=[jax.ShapeDtypeStruct((rows, SSM_WIDTH), F32), jax.ShapeDtypeStruct((rows, ns), F32),
                   jax.ShapeDtypeStruct((rows, ns), F32)],
        compiler_params=_params("arbitrary"),
        name="ssm_step",
    )(*args)


def _dil_attn_kernel(q_ref, kc_ref, kp_ref, vc_ref, vp_ref, o_ref, lse_ref, *, dil, slopes):
    steps, hd = ATT_STEPS, ATT_HEAD_DIM
    chunk = q_ref.shape[0]
    span = steps * dil
    first_chunk = pl.program_id(1) == 0
    pair = pl.program_id(2)
    qi = lax.broadcasted_iota(jnp.int32, (steps, 2 * steps), 0)
    kj = lax.broadcasted_iota(jnp.int32, (steps, 2 * steps), 1)
    dist = qi + steps - kj
    band = (dist >= 0) & (dist <= steps)
    distf = (dist * dil).astype(F32)
    lane = lax.broadcasted_iota(jnp.int32, (steps, LANES), 1)
    heads = [lane < hd, lane >= hd]
    biases, first_biases = [], []
    for hh in range(2):
        slope = jnp.where(pair == 0, slopes[hh], slopes[2 + hh])
        bias = jnp.where(band, -slope * distf, NEG_BIG)
        biases.append(bias)
        first_biases.append(jnp.where(first_chunk & (kj < steps), NEG_BIG, bias))

    def rows(ref, start):
        return ref[pl.ds(start, steps, stride=dil), :] if dil > 1 else ref[pl.ds(start, steps), :]

    for r in range(dil):
        for qb in range(chunk // span):
            start = r + qb * span
            q = rows(q_ref, start) * (hd ** -0.5)
            if qb == 0:
                k_prev, v_prev = rows(kp_ref, r), rows(vp_ref, r)
            else:
                k_prev, v_prev = rows(kc_ref, start - span), rows(vc_ref, start - span)
            kk = jnp.concatenate([k_prev, rows(kc_ref, start)], axis=0).astype(BF16)
            vv = jnp.concatenate([v_prev, rows(vc_ref, start)], axis=0).astype(BF16)
            out = jnp.zeros((steps, LANES), F32)
            lse = jnp.zeros((steps, LANES), F32)
            for hh in range(2):
                qh = jnp.where(heads[hh], q, 0.0).astype(BF16)
                s = lax.dot_general(qh, kk, (((1,), (1,)), ((), ())), preferred_element_type=F32)
                s = s + (first_biases[hh] if qb == 0 else biases[hh])
                m = jnp.max(s, axis=-1, keepdims=True)
                p = jnp.exp(s - m)
                den = jnp.sum(p, axis=-1, keepdims=True)
                oh = jnp.dot(p.astype(BF16), vv, preferred_element_type=F32) / den
                out = jnp.where(heads[hh], oh, out)
                lse = jnp.where(heads[hh], m + jnp.log(den), lse)
            if dil > 1:
                o_ref[pl.ds(start, steps, stride=dil), :] = out
                lse_ref[pl.ds(start, steps, stride=dil), :] = lse
            else:
                o_ref[pl.ds(start, steps), :] = out
                lse_ref[pl.ds(start, steps), :] = lse


def _dil_attn(q, k, v, n_seq, seq_len, group):
    win, dil = DIL_PAIRS[group]
    steps, gw = ATT_STEPS, ATT_GROUP_WIDTH
    span = steps * dil
    chunk = ATT_CHUNK
    assert win // dil == steps and chunk % span == 0 and seq_len % chunk == 0
    chunks = seq_len // chunk
    lane_blocks = gw // LANES
    cur = pl.BlockSpec((chunk, LANES), lambda n, c, p: (n * chunks + c, lane_blocks * group + p))
    prev = pl.BlockSpec((span, LANES), lambda n, c, p: (jnp.maximum((n * chunks + c) * (chunk // span) - 1, 0),
                                                        lane_blocks * group + p))
    out = pl.BlockSpec((chunk, LANES), lambda n, c, p: (n * chunks + c, p))
    slopes = tuple(_alibi_slopes()[group * HEADS_PER_GROUP:(group + 1) * HEADS_PER_GROUP])
    rows = n_seq * seq_len
    return pl.pallas_call(
        functools.partial(_dil_attn_kernel, dil=dil, slopes=slopes),
        grid=(n_seq, chunks, lane_blocks),
        in_specs=[cur, cur, prev, cur, prev],
        out_specs=[out, out],
        out_shape=[jax.ShapeDtypeStruct((rows, gw), F32), jax.ShapeDtypeStruct((rows, gw), F32)],
        compiler_params=_params("parallel", "parallel", "parallel"),
        name=f"dil_attn_w{win}",
    )(q, k, k, v, v)


def _mem_attn_kernel(q_ref, mk_ref, mv_ref, o_ref):
    hd = MEM_HEAD_DIM
    for h in range(MEM_HEADS):
        lanes = slice(h * hd, (h + 1) * hd)
        s = lax.dot_general(q_ref[:, lanes].astype(BF16), mk_ref[:, lanes].astype(BF16),
                            (((1,), (1,)), ((), ())), preferred_element_type=F32) * (hd ** -0.5)
        m = jnp.max(s, axis=-1, keepdims=True)
        p = jnp.exp(s - m)
        den = jnp.sum(p, axis=-1, keepdims=True)
        o_ref[:, lanes] = jnp.dot(p.astype(BF16), mv_ref[:, lanes].astype(BF16), preferred_element_type=F32) / den


def _mem_attn(qm, mk, mv, n_seq, seq_len, tm):
    assert seq_len % tm == 0
    tiles = seq_len // tm
    kv = pl.BlockSpec((None, N_MEM, MEM_WIDTH), lambda n, i: (n, 0, 0))
    return pl.pallas_call(
        _mem_attn_kernel,
        grid=(n_seq, tiles),
        in_specs=[pl.BlockSpec((tm, MEM_WIDTH), lambda n, i: (n * tiles + i, 0)), kv, kv],
        out_specs=pl.BlockSpec((tm, MEM_WIDTH), lambda n, i: (n * tiles + i, 0)),
        out_shape=jax.ShapeDtypeStruct(qm.shape, F32),
        compiler_params=_params("parallel", "parallel"),
        name="mem_attn",
    )(qm, mk, mv)


def _row_attention(q, k_blocks, v_blocks, biases, scale):
    qb = q[:, None]
    scores = []
    for k, bias in zip(k_blocks, biases):
        s = jnp.sum(k * qb, axis=-1, keepdims=True) * scale
        scores.append(s if bias is None else s + bias)
    m = functools.reduce(jnp.maximum, [jnp.max(s, axis=1, keepdims=True) for s in scores])
    den, acc = 0.0, 0.0
    for s, v in zip(scores, v_blocks):
        p = jnp.exp(s - m)
        den = den + jnp.sum(p, axis=1, keepdims=True)
        acc = acc + jnp.sum(p * v, axis=1, keepdims=True)
    return acc / den, m + jnp.log(den)


def _cache_attn_kernel(q_ref, kn_ref, vn_ref, k1_ref, v1_ref, k2_ref, v2_ref, k3_ref, v3_ref, o_ref,
                       qt_scr, knt_scr, vnt_scr, *, slopes):
    hpg, hd = HEADS_PER_GROUP, ATT_HEAD_DIM
    j = pl.program_id(1)
    nb = q_ref.shape[0]
    scale = hd ** -0.5
    qt_scr[...] = q_ref[...].T
    knt_scr[...] = kn_ref[...].T
    vnt_scr[...] = vn_ref[...].T
    outs = [[] for _ in range(nb)]
    lses = [[] for _ in range(nb)]
    for g, (k_ref, v_ref) in enumerate(((k1_ref, v1_ref), (k2_ref, v2_ref), (k3_ref, v3_ref))):
        dil = DIL_PAIRS[g][1]
        n_pos = k_ref.shape[-1]
        rows = pl.ds(pl.multiple_of((g * hpg + j) * hd, hd), hd)
        qg, kng, vng = qt_scr[rows, :], knt_scr[rows, :], vnt_scr[rows, :]
        slope = sum(jnp.where(j == h, slopes[g * hpg + h], 0.0) for h in range(hpg))
        back = n_pos - lax.broadcasted_iota(jnp.int32, (1, n_pos), 1)
        bias = jnp.where(back % dil == 0, -slope * back.astype(F32), NEG_BIG)
        for b in range(nb):
            qc = qg[:, b:b + 1]
            s = jnp.sum(k_ref[b] * qc, axis=0, keepdims=True) * scale + bias
            s_new = jnp.sum(qc * kng[:, b:b + 1], axis=0, keepdims=True) * scale
            m = jnp.maximum(jnp.max(s, axis=1, keepdims=True), s_new)
            p = jnp.exp(s - m)
            p_new = jnp.exp(s_new - m)
            den = jnp.sum(p, axis=1, keepdims=True) + p_new
            acc = jnp.sum(v_ref[b] * p, axis=1, keepdims=True) + p_new * vng[:, b:b + 1]
            outs[b].append(acc / den)
            lses[b].append(m + jnp.log(den))
    cols = []
    for b in range(nb):
        top = functools.reduce(jnp.maximum, lses[b])
        ws = [jnp.exp(l - top) for l in lses[b]]
        cols.append(sum(w * o for w, o in zip(ws, outs[b])) / sum(ws))
    o_ref[...] = jnp.concatenate(cols, axis=1)


def _cache_attn(q, k_new, v_new, caches_k, caches_v, nb):
    rows = q.shape[0]
    hpg, hd = HEADS_PER_GROUP, ATT_HEAD_DIM
    assert rows % nb == 0 and nb % SUBLANES == 0
    new = pl.BlockSpec((nb, ATT_WIDTH), lambda i, j: (i, 0))
    cache_specs, cache_args = [], []
    for g, (win, dil) in enumerate(DIL_PAIRS):
        for c in (caches_k[g], caches_v[g]):
            assert c.shape == (rows, win, hpg, hd) and win % dil == 0, c.shape
            cache_args.append(c.transpose(0, 2, 3, 1))
            cache_specs.append(pl.BlockSpec((nb, None, hd, win), lambda i, j: (i, j, 0, 0)))
    out = pl.pallas_call(
        functools.partial(_cache_attn_kernel, slopes=tuple(_alibi_slopes())),
        grid=(rows // nb, hpg),
        in_specs=[new, new, new] + cache_specs,
        out_specs=pl.BlockSpec((None, None, hd, nb), lambda i, j: (j, i, 0, 0)),
        out_shape=jax.ShapeDtypeStruct((hpg, rows // nb, hd, nb), F32),
        scratch_shapes=[pltpu.VMEM((ATT_WIDTH, nb), F32) for _ in range(3)],
        compiler_params=_params("parallel", "arbitrary"),
        name="cache_attn",
    )(q, k_new, v_new, *cache_args)
    return out.transpose(1, 3, 0, 2).reshape(rows, hpg * hd)


def _cache_mem_attn_kernel(q_ref, k_ref, v_ref, o_ref):
    o, _ = _row_attention(q_ref[...], [k_ref[...]], [v_ref[...]], [None], MEM_HEAD_DIM ** -0.5)
    o_ref[...] = o[:, 0]


def _cache_mem_attn(qm, mem_k, mem_v, nb):
    rows = qm.shape[0]
    assert rows % nb == 0
    kv = pl.BlockSpec((nb, N_MEM, MEM_HEADS, MEM_HEAD_DIM), lambda i: (i, 0, 0, 0))
    q = pl.BlockSpec((nb, MEM_HEADS, MEM_HEAD_DIM), lambda i: (i, 0, 0))
    return pl.pallas_call(
        _cache_mem_attn_kernel,
        grid=(rows // nb,),
        in_specs=[q, kv, kv],
        out_specs=q,
        out_shape=jax.ShapeDtypeStruct(qm.shape, F32),
        compiler_params=_params("parallel"),
        name="cache_mem_attn",
    )(qm, mem_k, mem_v)


def _merge_kernel(x_ref, *refs, n_y, n_att):
    y_refs, att_refs = refs[:n_y], refs[n_y:n_y + n_att]
    mem_ref, gl_ref, wglu_ref, watt_ref, wmem_ref, wout_ref, o_ref = refs[n_y + n_att:]
    d = D_MODEL
    y = jnp.concatenate([r[...] for r in y_refs], axis=1) if n_y > 1 else y_refs[0][...]
    z = jax.nn.gelu(y).astype(BF16)
    glu = jnp.dot(z, wglu_ref[...], preferred_element_type=F32)
    merged = jax.nn.sigmoid(gl_ref[:, 0:d]) * (glu[:, 0:d] * jax.nn.sigmoid(glu[:, d:2 * d]))
    if n_att == 1:
        att = att_refs[0][...]
    else:
        lses = [r[...] for r in att_refs[1::2]]
        top = functools.reduce(jnp.maximum, lses)
        ws = [jnp.exp(l - top) for l in lses]
        att = sum(w * r[...] for w, r in zip(ws, att_refs[0::2])) / sum(ws)
    b_att = jnp.dot(att.astype(BF16), watt_ref[...], preferred_element_type=F32)
    merged = merged + jax.nn.sigmoid(gl_ref[:, d:2 * d]) * b_att
    b_mem = jnp.dot(mem_ref[...].astype(BF16), wmem_ref[...], preferred_element_type=F32)
    merged = merged + jax.nn.sigmoid(gl_ref[:, 2 * d:3 * d]) * b_mem
    o_ref[...] = x_ref[...] + jnp.dot(merged.astype(BF16), wout_ref[...], preferred_element_type=F32)


def _merge(x, ys, atts, o_mem, gl, w_glu, w_att_o, w_mem_o, w_out, tm):
    rows = x.shape[0]
    assert rows % tm == 0

    def tile(a):
        return pl.BlockSpec((tm, a.shape[1]), lambda i: (i, 0))

    acts = (x, *ys, *atts, o_mem, gl)
    weights = (w_glu, w_att_o, w_mem_o, w_out)
    return pl.pallas_call(
        functools.partial(_merge_kernel, n_y=len(ys), n_att=len(atts)),
        grid=(rows // tm,),
        in_specs=[tile(a) for a in acts] + [_full(w.shape) for w in weights],
        out_specs=tile(x),
        out_shape=jax.ShapeDtypeStruct(x.shape, F32),
        compiler_params=_params("parallel"),
        name="merge",
    )(*acts, *weights)


def _ffn_kernel(*refs, tm, tiles_per_seq, stepwise):
    if stepwise:
        x_ref, g2_ref, wup_ref, cw_ref, cb_ref, wdn_ref, gf_ref, prev_ref, y_ref, conv_ref = refs
    else:
        x_ref, g2_ref, wup_ref, cw_ref, cb_ref, wdn_ref, gf_ref, y_ref, conv_ref, a_scr = refs
        i = pl.program_id(0)
        first = i % tiles_per_seq == 0

        @pl.when(first)
        def _():
            a_scr[0:SUBLANES, :] = jnp.zeros((SUBLANES, D_FF), F32)

        @pl.when(jnp.logical_not(first))
        def _():
            a_scr[0:SUBLANES, :] = a_scr[tm:tm + SUBLANES, :]

    x = x_ref[...]
    h = _rmsnorm(x, g2_ref[...]).astype(BF16)
    acc = jnp.zeros((tm, D_MODEL), F32)
    for c0 in range(0, D_FF, FFN_CHUNK):
        cols = slice(c0, c0 + FFN_CHUNK)
        a = jnp.dot(h, wup_ref[:, cols], preferred_element_type=F32)
        if stepwise:
            a2, a1 = prev_ref[:, cols], prev_ref[:, D_FF + c0:D_FF + c0 + FFN_CHUNK]
            conv_ref[:, cols] = a1
            conv_ref[:, D_FF + c0:D_FF + c0 + FFN_CHUNK] = a
        else:
            a_scr[SUBLANES:SUBLANES + tm, cols] = a
            a1 = a_scr[SUBLANES - 1:SUBLANES - 1 + tm, cols]
            a2 = a_scr[SUBLANES - 2:SUBLANES - 2 + tm, cols]
        c = a2 * cw_ref[0:1, cols] + a1 * cw_ref[1:2, cols] + a * cw_ref[2:3, cols] + cb_ref[:, cols]
        v = jnp.dot(h, wup_ref[:, D_FF + c0:D_FF + c0 + FFN_CHUNK], preferred_element_type=F32)
        acc = acc + jnp.dot((jax.nn.gelu(c) * v).astype(BF16), wdn_ref[cols, :], preferred_element_type=F32)
    y_ref[...] = _rmsnorm(x + acc, gf_ref[...])
    if not stepwise:
        @pl.when(i % tiles_per_seq == tiles_per_seq - 1)
        def _():
            conv_ref[...] = a_scr[SUBLANES + tm - (CONV_W - 1):SUBLANES + tm, :]


def _ffn(x, g2, w_up, conv_w, conv_b, w_down, gf, tm, n_seq=None, prev=None):
    rows, d = x.shape
    assert rows % tm == 0 and D_FF % FFN_CHUNK == 0 and FFN_CHUNK % LANES == 0
    stepwise = prev is not None
    weights = (g2.reshape(1, d), w_up, conv_w, conv_b.reshape(1, D_FF), w_down, gf.reshape(1, d))
    in_specs = [pl.BlockSpec((tm, d), lambda i: (i, 0))] + [_full(w.shape) for w in weights]
    args = (x,) + weights
    if stepwise:
        tiles_per_seq = 1
        in_specs.append(pl.BlockSpec((tm, 2 * D_FF), lambda i: (i, 0)))
        args += (prev,)
        conv_spec = pl.BlockSpec((tm, 2 * D_FF), lambda i: (i, 0))
        conv_shape = jax.ShapeDtypeStruct((rows, 2 * D_FF), F32)
        scratch = []
    else:
        tiles_per_seq = rows // n_seq // tm
        assert tiles_per_seq * tm * n_seq == rows
        conv_spec = pl.BlockSpec((None, CONV_W - 1, D_FF), lambda i: (i // tiles_per_seq, 0, 0))
        conv_shape = jax.ShapeDtypeStruct((n_seq, CONV_W - 1, D_FF), F32)
        scratch = [pltpu.VMEM((tm + 2 * SUBLANES, D_FF), F32)]
    return pl.pallas_call(
        functools.partial(_ffn_kernel, tm=tm, tiles_per_seq=tiles_per_seq, stepwise=stepwise),
        grid=(rows // tm,),
        in_specs=in_specs,
        out_specs=[pl.BlockSpec((tm, d), lambda i: (i, 0)), conv_spec],
        out_shape=[jax.ShapeDtypeStruct((rows, d), F32), conv_shape],
        scratch_shapes=scratch,
        compiler_params=_params("arbitrary"),
        name="ffn_step" if stepwise else "ffn",
    )(*args)


def kernel(x_prompt, x_sample, state_ssm_re, state_ssm_im, cache_w1_k, cache_w1_v, cache_w2_k, cache_w2_v, cache_w3_k, cache_w3_v, cache_mem_k, cache_mem_v, state_ffn_conv, mem_prompt, norm1_g, w_in, ssm_a_re, ssm_a_im, ssm_log_dt, ssm_b_re, ssm_b_im, ssm_c_re, ssm_c_im, ssm_d, w_ssm_glu, w_att_o, mem_norm_g, w_mem_kv, w_mem_o, w_out, norm2_g, w_up, ffn_conv_w, ffn_conv_b, w_down, final_norm_g):
    n_seq, seq_len, d = x_prompt.shape
    n_dec, dec_len, _ = x_sample.shape
    depth = norm1_g.shape[0]
    assert d == D_MODEL and depth == 1 and dec_len == 1
    assert w_in.shape == (depth, D_MODEL, sum(IN_SPLITS)) and w_up.shape == (depth, D_MODEL, 2 * D_FF)
    assert mem_prompt.shape == (n_seq, N_MEM, D_MODEL)
    assert ssm_a_re.shape == (depth, SSM_GROUPS, SSM_STATE)
    assert seq_len % ROW_TILE == 0 and n_dec % SUBLANES == 0
    hpg, hd = HEADS_PER_GROUP, ATT_HEAD_DIM
    rows_p = n_seq * seq_len
    dec_tile = n_dec if n_dec <= ROW_TILE else ROW_TILE

    wb = lambda w: w[0].astype(BF16)
    w_in_b, w_glu_b, w_att_b, w_memkv_b = wb(w_in), wb(w_ssm_glu), wb(w_att_o), wb(w_mem_kv)
    w_memo_b, w_out_b, w_up_b, w_down_b = wb(w_mem_o), wb(w_out), wb(w_up), wb(w_down)

    m_mat, w_mat, v_mat, v0_mat, arow = _ssm_prep(ssm_a_re[0], ssm_a_im[0], ssm_log_dt[0], ssm_b_re[0],
                                                  ssm_b_im[0], ssm_c_re[0], ssm_c_im[0])
    w0_mat = w_mat[:, (SSM_T - 1) * LANES:, :]

    xp = x_prompt.reshape(rows_p, d)
    u, q, k, v, qm, gl = _norm_proj(xp, norm1_g[0], w_in_b, IN_SPLITS, ROW_TILE)
    y_ssm, fin_re, fin_im = _ssm_prompt(u, n_seq, m_mat, w_mat, v_mat, arow, ssm_d[0])
    atts = [a for g in range(len(DIL_PAIRS)) for a in _dil_attn(q, k, v, n_seq, seq_len, g)]
    mk, mv = _norm_proj(mem_prompt.reshape(n_seq * N_MEM, d), mem_norm_g[0], w_memkv_b, (MEM_WIDTH, MEM_WIDTH),
                        ROW_TILE)
    o_mem = _mem_attn(qm, mk.reshape(n_seq, N_MEM, MEM_WIDTH), mv.reshape(n_seq, N_MEM, MEM_WIDTH),
                      n_seq, seq_len, 2 * ROW_TILE)
    x1 = _merge(xp, y_ssm, atts, o_mem, gl, w_glu_b, w_att_b, w_memo_b, w_out_b, ROW_TILE)
    y_p, conv_p = _ffn(x1, norm2_g[0], w_up_b, ffn_conv_w[0], ffn_conv_b[0], w_down_b, final_norm_g,
                       ROW_TILE, n_seq=n_seq)

    def final_state(s):
        s = s.reshape(SSM_LANE_BLOCKS, n_seq, SSM_GROUPS_PER_BLOCK, SSM_STATE)
        return s.transpose(1, 0, 2, 3).reshape(1, n_seq, SSM_GROUPS, SSM_STATE)
    k3 = k.reshape(n_seq, seq_len, ATT_WIDTH)
    v3 = v.reshape(n_seq, seq_len, ATT_WIDTH)
    gw = ATT_GROUP_WIDTH
    win_p = []
    for g, (win, _) in enumerate(DIL_PAIRS):
        keep = min(win, seq_len)
        for t in (k3, v3):
            win_p.append(t[:, seq_len - keep:, g * gw:(g + 1) * gw].reshape(1, n_seq, keep, hpg, hd))
    mem_kv = [mk.reshape(1, n_seq, N_MEM, MEM_HEADS, MEM_HEAD_DIM), mv.reshape(1, n_seq, N_MEM, MEM_HEADS, MEM_HEAD_DIM)]

    xs = x_sample.reshape(n_dec, d)
    us, qs, ks, vs, qms, gls = _norm_proj(xs, norm1_g[0], w_in_b, IN_SPLITS, dec_tile)
    ys_ssm, sn_re, sn_im = _ssm_step(us, state_ssm_re[0], state_ssm_im[0], w0_mat, v0_mat, arow, ssm_d[0])
    n_g = len(DIL_PAIRS)
    ks4, vs4 = (t.reshape(n_dec, n_g, hpg, hd) for t in (ks, vs))
    att_s = _cache_attn(qs, ks, vs, (cache_w1_k[0], cache_w2_k[0], cache_w3_k[0]),
                        (cache_w1_v[0], cache_w2_v[0], cache_w3_v[0]), SUBLANES)
    mem_s = _cache_mem_attn(qms.reshape(n_dec, MEM_HEADS, MEM_HEAD_DIM), cache_mem_k[0], cache_mem_v[0],
                            CACHE_ROWS_PER_STEP)
    xs1 = _merge(xs, [ys_ssm], [att_s], mem_s.reshape(n_dec, MEM_WIDTH), gls,
                 w_glu_b, w_att_b, w_memo_b, w_out_b, dec_tile)
    y_s, conv_s = _ffn(xs1, norm2_g[0], w_up_b, ffn_conv_w[0], ffn_conv_b[0], w_down_b, final_norm_g,
                       dec_tile, prev=state_ffn_conv[0].reshape(n_dec, (CONV_W - 1) * D_FF))

    win_s = []
    for g in range(len(DIL_PAIRS)):
        win_s += [ks4[None, :, g:g + 1], vs4[None, :, g:g + 1]]

    return (y_p.reshape(n_seq, seq_len, d), y_s.reshape(n_dec, 1, d),
            final_state(fin_re), final_state(fin_im), *win_p, *mem_kv, conv_p[None],
            sn_re.reshape(1, n_dec, SSM_GROUPS, SSM_STATE), sn_im.reshape(1, n_dec, SSM_GROUPS, SSM_STATE),
            *win_s, conv_s.reshape(1, n_dec, CONV_W - 1, D_FF))
```

```python
import functools

import jax
import jax.numpy as jnp
from jax import lax
from jax.experimental import pallas as pl
from jax.experimental.pallas import tpu as pltpu

F32 = jnp.float32
BF16 = jnp.bfloat16

D_MODEL = 1024
SSM_WIDTH = 512
SSM_GROUP = 16
SSM_GROUPS = 32
SSM_STATE = 64
ATT_HEAD_DIM = 64
HEADS_PER_GROUP = 4
DIL_PAIRS = ((128, 1), (512, 4), (2048, 16))
ATT_HEADS = len(DIL_PAIRS) * HEADS_PER_GROUP
ATT_WIDTH = ATT_HEADS * ATT_HEAD_DIM
ATT_GROUP_WIDTH = HEADS_PER_GROUP * ATT_HEAD_DIM
ATT_STEPS = 128
N_MEM = 256
MEM_HEADS = 4
MEM_HEAD_DIM = 128
MEM_WIDTH = MEM_HEADS * MEM_HEAD_DIM
N_BRANCH = 3
D_FF = 2816
CONV_W = 3
EPS = 1e-6
IN_SPLITS = (SSM_WIDTH, ATT_WIDTH, ATT_WIDTH, ATT_WIDTH, MEM_WIDTH, N_BRANCH * D_MODEL)

LANES = 128
SUBLANES = 8
VMEM_LIMIT_BYTES = 56 * 1024 * 1024

SSM_T = 8
SSM_LANE_BLOCKS = SSM_WIDTH // LANES
SSM_GROUPS_PER_BLOCK = LANES // SSM_GROUP
SSM_STATES_PER_BLOCK = SSM_GROUPS_PER_BLOCK * SSM_STATE
SSM_STATE_LANES = 2 * SSM_STATES_PER_BLOCK

ROW_TILE = 256
FFN_CHUNK = 1408
CACHE_ROWS_PER_STEP = 4
ATT_CHUNK = 2048
NEG_BIG = -1e30


def _alibi_slopes():
    return [float(2.0 ** (-8.0 * h / ATT_HEADS)) for h in range(1, ATT_HEADS + 1)]


def _params(*sem):
    return pltpu.CompilerParams(dimension_semantics=sem, vmem_limit_bytes=VMEM_LIMIT_BYTES)


def _rmsnorm(x, g):
    ms = jnp.mean(x * x, axis=-1, keepdims=True)
    return x * lax.rsqrt(ms + EPS) * g


def _full(shape):
    nd = len(shape)
    return pl.BlockSpec(shape, lambda *_: (0,) * nd, pipeline_mode=pl.Buffered(1))


def _norm_proj_kernel(x_ref, g_ref, w_ref, *out_refs, splits):
    h = _rmsnorm(x_ref[...], g_ref[...]).astype(BF16)
    off = 0
    for o_ref, width in zip(out_refs, splits):
        for c0 in range(0, width, 512):
            cw = min(512, width - c0)
            o_ref[:, c0:c0 + cw] = jnp.dot(h, w_ref[:, off + c0:off + c0 + cw], preferred_element_type=F32)
        off += width


def _norm_proj(x, g, w_bf16, splits, tm):
    rows, d = x.shape
    assert rows % tm == 0 and sum(splits) == w_bf16.shape[1]
    return pl.pallas_call(
        functools.partial(_norm_proj_kernel, splits=splits),
        grid=(rows // tm,),
        in_specs=[pl.BlockSpec((tm, d), lambda i: (i, 0)), _full((1, d)), _full(w_bf16.shape)],
        out_specs=[pl.BlockSpec((tm, s), lambda i: (i, 0)) for s in splits],
        out_shape=[jax.ShapeDtypeStruct((rows, s), F32) for s in splits],
        compiler_params=_params("parallel"),
        name="norm_proj",
    )(x, g.reshape(1, d), w_bf16)


def _ssm_layout(a_re, a_im, log_dt, b_re, b_im, c_re, c_im):
    nb, gpb, p, c = SSM_LANE_BLOCKS, SSM_GROUPS_PER_BLOCK, SSM_STATE, SSM_GROUP
    rows = jnp.stack([a_re.reshape(nb, gpb * p), a_im.reshape(nb, gpb * p),
                      jnp.repeat(log_dt, p).reshape(nb, gpb * p)], axis=1)
    eye = jnp.eye(gpb, dtype=F32)

    def place_b(b):
        return jnp.einsum('bgpc,gh->bgchp', b.reshape(nb, gpb, p, c), eye).reshape(nb, gpb * c, gpb * p)

    def place_c(m):
        return jnp.einsum('bgcp,gh->bhpgc', m.reshape(nb, gpb, c, p), eye).reshape(nb, gpb * p, gpb * c)

    return rows, place_b(b_re), place_b(b_im), place_c(c_re), place_c(c_im)


def _ssm_prep_kernel(rows_ref, bre_ref, bim_ref, cre_ref, cim_ref,
                     m_ref, w_ref, v_ref, v0_ref, a1_ref, apow_ref, *, t_chunk):
    sp = SSM_STATES_PER_BLOCK
    a_re, a_im, dt = rows_ref[0:1, :], rows_ref[1:2, :], jnp.exp(rows_ref[2:3, :])

    def powers(k):
        mag = jnp.exp(a_re * dt * k)
        ang = a_im * dt * k
        return mag * jnp.cos(ang), mag * jnp.sin(ang)

    n_pow = 2 * SUBLANES
    assert t_chunk + 1 <= n_pow
    pw_re, pw_im = powers(lax.broadcasted_iota(jnp.int32, (n_pow, 1), 0).astype(F32))
    pwt_re, pwt_im = pw_re.T, pw_im.T
    ab_re, ab_im = pw_re[1:2, :], pw_im[1:2, :]
    den = a_re * a_re + a_im * a_im
    q_re = ((ab_re - 1.0) * a_re + ab_im * a_im) / den
    q_im = (ab_im * a_re - (ab_re - 1.0) * a_im) / den
    bre, bim = bre_ref[...], bim_ref[...]
    bb_re = q_re * bre - q_im * bim
    bb_im = q_re * bim + q_im * bre
    cre, cim = cre_ref[...], cim_ref[...]

    m_ref[...] = jnp.zeros(m_ref.shape, m_ref.dtype)
    for k in range(t_chunk):
        pk_re, pk_im = pw_re[k:k + 1, :], pw_im[k:k + 1, :]
        bk_re = bb_re * pk_re - bb_im * pk_im
        bk_im = bb_re * pk_im + bb_im * pk_re
        t = t_chunk - 1 - k
        w_ref[t * LANES:(t + 1) * LANES, 0:sp] = bk_re.astype(w_ref.dtype)
        w_ref[t * LANES:(t + 1) * LANES, sp:2 * sp] = bk_im.astype(w_ref.dtype)
        kk = (jnp.dot(bk_re, cre, precision=lax.Precision.HIGHEST, preferred_element_type=F32)
              - jnp.dot(bk_im, cim, precision=lax.Precision.HIGHEST, preferred_element_type=F32))
        kk = kk.astype(m_ref.dtype)
        for t0 in range(t_chunk - k):
            m_ref[t0 * LANES:(t0 + 1) * LANES, (t0 + k) * LANES:(t0 + k + 1) * LANES] = kk

    for t in range(t_chunk):
        pc_re, pc_im = pwt_re[:, t + 1:t + 2], pwt_im[:, t + 1:t + 2]
        v_ref[0:sp, t * LANES:(t + 1) * LANES] = (cre * pc_re - cim * pc_im).astype(v_ref.dtype)
        v_ref[sp:2 * sp, t * LANES:(t + 1) * LANES] = (-(cre * pc_im + cim * pc_re)).astype(v_ref.dtype)
    v0_ref[0:sp, :] = cre.astype(v0_ref.dtype)
    v0_ref[sp:2 * sp, :] = (-cim).astype(v0_ref.dtype)

    a1_ref[:, 0:sp] = ab_re
    a1_ref[:, sp:2 * sp] = ab_im
    steps = (lax.broadcasted_iota(jnp.int32, (SUBLANES, 1), 0) + 1) * t_chunk
    ap_re, ap_im = powers(steps.astype(F32))
    apow_ref[:, 0:sp] = ap_re
    apow_ref[:, sp:2 * sp] = ap_im


def _ssm_prep(a_re, a_im, log_dt, b_re, b_im, c_re, c_im):
    rows, pbre, pbim, pcre, pcim = _ssm_layout(a_re, a_im, log_dt, b_re, b_im, c_re, c_im)
    nb, sp, sl, tl = SSM_LANE_BLOCKS, SSM_STATES_PER_BLOCK, SSM_STATE_LANES, SSM_T * LANES

    def blk(shape):
        return pl.BlockSpec((None,) + shape, lambda b: (b, 0, 0))

    return pl.pallas_call(
        functools.partial(_ssm_prep_kernel, t_chunk=SSM_T),
        grid=(nb,),
        in_specs=[blk((3, sp)), blk((LANES, sp)), blk((LANES, sp)), blk((sp, LANES)), blk((sp, LANES))],
        out_specs=[blk((tl, tl)), blk((tl, sl)), blk((sl, tl)), blk((sl, LANES)), blk((1, sl)), blk((SUBLANES, sl))],
        out_shape=[jax.ShapeDtypeStruct((nb, tl, tl), BF16), jax.ShapeDtypeStruct((nb, tl, sl), BF16),
                   jax.ShapeDtypeStruct((nb, sl, tl), BF16), jax.ShapeDtypeStruct((nb, sl, LANES), BF16),
                   jax.ShapeDtypeStruct((nb, 1, sl), F32), jax.ShapeDtypeStruct((nb, SUBLANES, sl), F32)],
        compiler_params=_params("parallel"),
        name="ssm_prep",
    )(rows, pbre, pbim, pcre, pcim)


def _chunk_tokens(u_ref, t, t_chunk):
    return u_ref[pl.ds(t, u_ref.shape[0] // t_chunk, stride=t_chunk), :]


def _chunk_lanes(u_ref, t_chunk):
    return jnp.concatenate([_chunk_tokens(u_ref, t, t_chunk) for t in range(t_chunk)], axis=1)


def _ssm_end_state_kernel(*refs, t_chunk):
    nb, sl = SSM_LANE_BLOCKS, SSM_STATE_LANES
    u_refs, w_ref, e_ref = refs[:nb], refs[nb], refs[nb + 1]
    for b in range(nb):
        ub = _chunk_lanes(u_refs[b], t_chunk).astype(BF16)
        e_ref[:, b * sl:(b + 1) * sl] = jnp.dot(ub, w_ref[b], preferred_element_type=F32)


def _ssm_scan_kernel(ere_ref, eim_ref, pre_ref, pim_ref, spre_ref, spim_ref, fre_ref, fim_ref):
    n_tiles = ere_ref.shape[0] // SUBLANES
    width = ere_ref.shape[1]
    p_re, p_im = pre_ref[...], pim_ref[...]
    row = lax.broadcasted_iota(jnp.int32, (SUBLANES, width), 0)

    def shift_down(x, k):
        return jnp.where(row >= k, pltpu.roll(x, k, 0), 0.0)

    def body(i, carry):
        c_re, c_im = carry
        rows = pl.ds(pl.multiple_of(i * SUBLANES, SUBLANES), SUBLANES)
        x_re, x_im = ere_ref[rows, :], eim_ref[rows, :]
        for k in (1, 2, 4):
            a_re, a_im = p_re[k - 1:k, :], p_im[k - 1:k, :]
            s_re, s_im = shift_down(x_re, k), shift_down(x_im, k)
            x_re, x_im = x_re + a_re * s_re - a_im * s_im, x_im + a_re * s_im + a_im * s_re
        t_re = x_re + p_re * c_re - p_im * c_im
        t_im = x_im + p_re * c_im + p_im * c_re
        spre_ref[rows, :] = jnp.where(row >= 1, pltpu.roll(t_re, 1, 0), c_re)
        spim_ref[rows, :] = jnp.where(row >= 1, pltpu.roll(t_im, 1, 0), c_im)
        return t_re[SUBLANES - 1:SUBLANES, :], t_im[SUBLANES - 1:SUBLANES, :]

    zero = jnp.zeros((1, width), F32)
    f_re, f_im = lax.fori_loop(0, n_tiles, body, (zero, zero))
    fre_ref[...] = f_re
    fim_ref[...] = f_im


def _ssm_output_kernel(*refs, t_chunk):
    nb, sp = SSM_LANE_BLOCKS, SSM_STATES_PER_BLOCK
    u_refs, (spre_ref, spim_ref, m_ref, v_ref, d_ref), y_refs = refs[:nb], refs[nb:nb + 5], refs[nb + 5:]
    for b in range(nb):
        ub = _chunk_lanes(u_refs[b], t_chunk).astype(BF16)
        yb = jnp.dot(ub, m_ref[b], preferred_element_type=F32)
        states = slice(b * sp, (b + 1) * sp)
        sprev = jnp.concatenate([spre_ref[:, states], spim_ref[:, states]], axis=1).astype(BF16)
        yb = yb + jnp.dot(sprev, v_ref[b], preferred_element_type=F32)
        d = d_ref[:, b * LANES:(b + 1) * LANES]
        for t in range(t_chunk):
            y_t = yb[:, t * LANES:(t + 1) * LANES] + d * _chunk_tokens(u_refs[b], t, t_chunk)
            y_refs[b][pl.ds(t, yb.shape[0], stride=t_chunk), :] = y_t


def _ssm_prompt(u, n_seq, m_mat, w_mat, v_mat, apow, d_skip):
    rows, t_chunk, nb, sl, sp = u.shape[0], SSM_T, SSM_LANE_BLOCKS, SSM_STATE_LANES, SSM_STATES_PER_BLOCK
    assert rows % (t_chunk * n_seq) == 0
    n_chunks = rows // t_chunk // n_seq
    tr = min(ROW_TILE, n_chunks)
    assert n_chunks % tr == 0 and n_chunks % SUBLANES == 0
    tiles = n_chunks // tr
    u_specs = [pl.BlockSpec((tr * t_chunk, LANES), lambda i, b=b: (i, b)) for b in range(nb)]
    e = pl.pallas_call(
        functools.partial(_ssm_end_state_kernel, t_chunk=t_chunk),
        grid=(n_seq * tiles,),
        in_specs=u_specs + [_full(w_mat.shape)],
        out_specs=pl.BlockSpec((tr, nb * sl), lambda i: (i % tiles, i // tiles)),
        out_shape=jax.ShapeDtypeStruct((n_chunks, n_seq * nb * sl), F32),
        compiler_params=_params("parallel"),
        name="ssm_end_state",
    )(*([u] * nb), w_mat)
    col = pl.BlockSpec((n_chunks, sp), lambda g: (0, g))
    fin = pl.BlockSpec((1, sp), lambda g: (0, g))
    sp_re, sp_im, fin_re, fin_im = pl.pallas_call(
        _ssm_scan_kernel,
        grid=(n_seq * nb,),
        in_specs=[pl.BlockSpec((n_chunks, sp), lambda g: (0, 2 * g)),
                  pl.BlockSpec((n_chunks, sp), lambda g: (0, 2 * g + 1)),
                  pl.BlockSpec((None, SUBLANES, sp), lambda g: (g % nb, 0, 0)),
                  pl.BlockSpec((None, SUBLANES, sp), lambda g: (g % nb, 0, 1))],
        out_specs=[col, col, fin, fin],
        out_shape=[jax.ShapeDtypeStruct((n_chunks, n_seq * nb * sp), F32)] * 2
        + [jax.ShapeDtypeStruct((1, n_seq * nb * sp), F32)] * 2,
        compiler_params=_params("parallel"),
        name="ssm_scan",
    )(e, e, apow, apow)
    sp_spec = pl.BlockSpec((tr, nb * sp), lambda i: (i % tiles, i // tiles))
    ys = pl.pallas_call(
        functools.partial(_ssm_output_kernel, t_chunk=t_chunk),
        grid=(n_seq * tiles,),
        in_specs=u_specs + [sp_spec, sp_spec, _full(m_mat.shape), _full(v_mat.shape), _full((1, SSM_WIDTH))],
        out_specs=[pl.BlockSpec((tr * t_chunk, LANES), lambda i: (i, 0)) for _ in range(nb)],
        out_shape=[jax.ShapeDtypeStruct((rows, LANES), F32) for _ in range(nb)],
        compiler_params=_params("parallel"),
        name="ssm_output",
    )(*([u] * nb), sp_re, sp_im, m_mat, v_mat, d_skip.reshape(1, SSM_WIDTH))
    return ys, fin_re, fin_im


def _ssm_step_kernel(u_ref, sre_ref, sim_ref, w0_ref, v0_ref, a1_ref, d_ref, y_ref, nre_ref, nim_ref):
    sp = SSM_STATES_PER_BLOCK
    for b in range(SSM_LANE_BLOCKS):
        lanes = slice(b * LANES, (b + 1) * LANES)
        states = slice(b * sp, (b + 1) * sp)
        u = u_ref[:, lanes]
        e = jnp.dot(u.astype(BF16), w0_ref[b], preferred_element_type=F32)
        a_re, a_im = a1_ref[b, :, 0:sp], a1_ref[b, :, sp:2 * sp]
        s_re, s_im = sre_ref[:, states], sim_ref[:, states]
        n_re = a_re * s_re - a_im * s_im + e[:, 0:sp]
        n_im = a_re * s_im + a_im * s_re + e[:, sp:2 * sp]
        nre_ref[:, states] = n_re
        nim_ref[:, states] = n_im
        sn = jnp.concatenate([n_re, n_im], axis=1).astype(BF16)
        y_ref[:, lanes] = jnp.dot(sn, v0_ref[b], preferred_element_type=F32) + d_ref[:, lanes] * u


def _ssm_step(u, s_re, s_im, w0, v0, a1, d_skip):
    rows = u.shape[0]
    ns = SSM_GROUPS * SSM_STATE
    args = (u, s_re.reshape(rows, ns), s_im.reshape(rows, ns), w0, v0, a1, d_skip.reshape(1, SSM_WIDTH))
    return pl.pallas_call(
        _ssm_step_kernel,
        grid=(1,),
        in_specs=[_full(a.shape) for a in args],
        out_specs=[_full((rows, SSM_WIDTH)), _full((rows, ns)), _full((rows, ns))],
        out_shape=[jax.ShapeDtypeStruct((rows, SSM_WIDTH), F32), jax.ShapeDtypeStruct((rows, ns), F32),
                   jax.ShapeDtypeStruct((rows, ns), F32)],
        compiler_params=_params("arbitrary"),
        name="ssm_step",
    )(*args)


def _dil_attn_kernel(q_ref, kc_ref, kp_ref, vc_ref, vp_ref, o_ref, lse_ref, *, dil, slopes):
    steps, hd = ATT_STEPS, ATT_HEAD_DIM
    chunk = q_ref.shape[0]
    span = steps * dil
    first_chunk = pl.program_id(1) == 0
    pair = pl.program_id(2)
    qi = lax.broadcasted_iota(jnp.int32, (steps, 2 * steps), 0)
    kj = lax.broadcasted_iota(jnp.int32, (steps, 2 * steps), 1)
    dist = qi + steps - kj
    band = (dist >= 0) & (dist <= steps)
    distf = (dist * dil).astype(F32)
    lane = lax.broadcasted_iota(jnp.int32, (steps, LANES), 1)
    heads = [lane < hd, lane >= hd]
    biases, first_biases = [], []
    for hh in range(2):
        slope = jnp.where(pair == 0, slopes[hh], slopes[2 + hh])
        bias = jnp.where(band, -slope * distf, NEG_BIG)
        biases.append(bias)
        first_biases.append(jnp.where(first_chunk & (kj < steps), NEG_BIG, bias))

    def rows(ref, start):
        return ref[pl.ds(start, steps, stride=dil), :] if dil > 1 else ref[pl.ds(start, steps), :]

    for r in range(dil):
        for qb in range(chunk // span):
            start = r + qb * span
            q = rows(q_ref, start) * (hd ** -0.5)
            if qb == 0:
                k_prev, v_prev = rows(kp_ref, r), rows(vp_ref, r)
            else:
                k_prev, v_prev = rows(kc_ref, start - span), rows(vc_ref, start - span)
            kk = jnp.concatenate([k_prev, rows(kc_ref, start)], axis=0).astype(BF16)
            vv = jnp.concatenate([v_prev, rows(vc_ref, start)], axis=0).astype(BF16)
            out = jnp.zeros((steps, LANES), F32)
            lse = jnp.zeros((steps, LANES), F32)
            for hh in range(2):
                qh = jnp.where(heads[hh], q, 0.0).astype(BF16)
                s = lax.dot_general(qh, kk, (((1,), (1,)), ((), ())), preferred_element_type=F32)
                s = s + (first_biases[hh] if qb == 0 else biases[hh])
                m = jnp.max(s, axis=-1, keepdims=True)
                p = jnp.exp(s - m)
                den = jnp.sum(p, axis=-1, keepdims=True)
                oh = jnp.dot(p.astype(BF16), vv, preferred_element_type=F32) / den
                out = jnp.where(heads[hh], oh, out)
                lse = jnp.where(heads[hh], m + jnp.log(den), lse)
            if dil > 1:
                o_ref[pl.ds(start, steps, stride=dil), :] = out
                lse_ref[pl.ds(start, steps, stride=dil), :] = lse
            else:
                o_ref[pl.ds(start, steps), :] = out
                lse_ref[pl.ds(start, steps), :] = lse


def _dil_attn(q, k, v, n_seq, seq_len, group):
    win, dil = DIL_PAIRS[group]
    steps, gw = ATT_STEPS, ATT_GROUP_WIDTH
    span = steps * dil
    chunk = ATT_CHUNK
    assert win // dil == steps and chunk % span == 0 and seq_len % chunk == 0
    chunks = seq_len // chunk
    lane_blocks = gw // LANES
    cur = pl.BlockSpec((chunk, LANES), lambda n, c, p: (n * chunks + c, lane_blocks * group + p))
    prev = pl.BlockSpec((span, LANES), lambda n, c, p: (jnp.maximum((n * chunks + c) * (chunk // span) - 1, 0),
                                                        lane_blocks * group + p))
    out = pl.BlockSpec((chunk, LANES), lambda n, c, p: (n * chunks + c, p))
    slopes = tuple(_alibi_slopes()[group * HEADS_PER_GROUP:(group + 1) * HEADS_PER_GROUP])
    rows = n_seq * seq_len
    return pl.pallas_call(
        functools.partial(_dil_attn_kernel, dil=dil, slopes=slopes),
        grid=(n_seq, chunks, lane_blocks),
        in_specs=[cur, cur, prev, cur, prev],
        out_specs=[out, out],
        out_shape=[jax.ShapeDtypeStruct((rows, gw), F32), jax.ShapeDtypeStruct((rows, gw), F32)],
        compiler_params=_params("parallel", "parallel", "parallel"),
        name=f"dil_attn_w{win}",
    )(q, k, k, v, v)


def _mem_attn_kernel(q_ref, mk_ref, mv_ref, o_ref):
    hd = MEM_HEAD_DIM
    for h in range(MEM_HEADS):
        lanes = slice(h * hd, (h + 1) * hd)
        s = lax.dot_general(q_ref[:, lanes].astype(BF16), mk_ref[:, lanes].astype(BF16),
                            (((1,), (1,)), ((), ())), preferred_element_type=F32) * (hd ** -0.5)
        m = jnp.max(s, axis=-1, keepdims=True)
        p = jnp.exp(s - m)
        den = jnp.sum(p, axis=-1, keepdims=True)
        o_ref[:, lanes] = jnp.dot(p.astype(BF16), mv_ref[:, lanes].astype(BF16), preferred_element_type=F32) / den


def _mem_attn(qm, mk, mv, n_seq, seq_len, tm):
    assert seq_len % tm == 0
    tiles = seq_len // tm
    kv = pl.BlockSpec((None, N_MEM, MEM_WIDTH), lambda n, i: (n, 0, 0))
    return pl.pallas_call(
        _mem_attn_kernel,
        grid=(n_seq, tiles),
        in_specs=[pl.BlockSpec((tm, MEM_WIDTH), lambda n, i: (n * tiles + i, 0)), kv, kv],
        out_specs=pl.BlockSpec((tm, MEM_WIDTH), lambda n, i: (n * tiles + i, 0)),
        out_shape=jax.ShapeDtypeStruct(qm.shape, F32),
        compiler_params=_params("parallel", "parallel"),
        name="mem_attn",
    )(qm, mk, mv)


def _row_attention(q, k_blocks, v_blocks, biases, scale):
    qb = q[:, None]
    scores = []
    for k, bias in zip(k_blocks, biases):
        s = jnp.sum(k * qb, axis=-1, keepdims=True) * scale
        scores.append(s if bias is None else s + bias)
    m = functools.reduce(jnp.maximum, [jnp.max(s, axis=1, keepdims=True) for s in scores])
    den, acc = 0.0, 0.0
    for s, v in zip(scores, v_blocks):
        p = jnp.exp(s - m)
        den = den + jnp.sum(p, axis=1, keepdims=True)
        acc = acc + jnp.sum(p * v, axis=1, keepdims=True)
    return acc / den, m + jnp.log(den)


def _cache_attn_kernel(q_ref, kn_ref, vn_ref, k1_ref, v1_ref, k2_ref, v2_ref, k3_ref, v3_ref, o_ref,
                       qt_scr, vnt_scr, *, slopes):
    hpg, hd = HEADS_PER_GROUP, ATT_HEAD_DIM
    j = pl.program_id(1)
    nb = q_ref.shape[0]
    scale = hd ** -0.5
    q, kn = q_ref[...], kn_ref[...]
    qt_scr[...] = q.T
    vnt_scr[...] = vn_ref[...].T
    lane_head = lax.broadcasted_iota(jnp.int32, q.shape, 1) // hd
    outs, lses = [], []
    for g, (k_ref, v_ref) in enumerate(((k1_ref, v1_ref), (k2_ref, v2_ref), (k3_ref, v3_ref))):
        dil = DIL_PAIRS[g][1]
        n_pos = k_ref.shape[-1]
        rows = pl.ds(pl.multiple_of((g * hpg + j) * hd, hd), hd)
        qg, vng = qt_scr[rows, :], vnt_scr[rows, :]
        slope = sum(jnp.where(j == h, slopes[g * hpg + h], 0.0) for h in range(hpg))
        back = n_pos - lax.broadcasted_iota(jnp.int32, (1, n_pos), 1)
        bias = jnp.where(back % dil == 0, -slope * back.astype(F32), NEG_BIG)
        row = lax.broadcasted_iota(jnp.int32, (nb, n_pos), 0)
        s = jnp.zeros((nb, n_pos), F32)
        for b in range(nb):
            s = jnp.where(row == b, jnp.sum(k_ref[b] * qg[:, b:b + 1], axis=0, keepdims=True), s)
        s = s * scale + bias
        s_new = jnp.sum(jnp.where(lane_head == g * hpg + j, q * kn, 0.0), axis=1, keepdims=True) * scale
        m = jnp.maximum(jnp.max(s, axis=1, keepdims=True), s_new)
        p = jnp.exp(s - m)
        p_new = jnp.exp(s_new - m)
        den = jnp.sum(p, axis=1, keepdims=True) + p_new
        cols = []
        for b in range(nb):
            acc = jnp.sum(v_ref[b] * p[b:b + 1, :], axis=1, keepdims=True) + p_new[b:b + 1, :] * vng[:, b:b + 1]
            cols.append(acc / den[b:b + 1, :])
        outs.append(cols)
        lses.append(m + jnp.log(den))
    top = functools.reduce(jnp.maximum, lses)
    ws = [jnp.exp(l - top) for l in lses]
    total = sum(ws)
    o_ref[...] = jnp.concatenate(
        [sum(w[b:b + 1, :] * cols[b] for w, cols in zip(ws, outs)) / total[b:b + 1, :] for b in range(nb)], axis=1)


def _cache_attn(q, k_new, v_new, caches_k, caches_v, nb):
    rows = q.shape[0]
    hpg, hd = HEADS_PER_GROUP, ATT_HEAD_DIM
    assert rows % nb == 0 and nb % SUBLANES == 0
    new = pl.BlockSpec((nb, ATT_WIDTH), lambda i, j: (i, 0))
    cache_specs, cache_args = [], []
    for g, (win, dil) in enumerate(DIL_PAIRS):
        for c in (caches_k[g], caches_v[g]):
            assert c.shape == (rows, win, hpg, hd) and win % dil == 0, c.shape
            cache_args.append(c.transpose(0, 2, 3, 1))
            cache_specs.append(pl.BlockSpec((nb, None, hd, win), lambda i, j: (i, j, 0, 0)))
    out = pl.pallas_call(
        functools.partial(_cache_attn_kernel, slopes=tuple(_alibi_slopes())),
        grid=(rows // nb, hpg),
        in_specs=[new, new, new] + cache_specs,
        out_specs=pl.BlockSpec((None, None, hd, nb), lambda i, j: (j, i, 0, 0)),
        out_shape=jax.ShapeDtypeStruct((hpg, rows // nb, hd, nb), F32),
        scratch_shapes=[pltpu.VMEM((ATT_WIDTH, nb), F32) for _ in range(2)],
        compiler_params=_params("parallel", "arbitrary"),
        name="cache_attn",
    )(q, k_new, v_new, *cache_args)
    return out.transpose(1, 3, 0, 2).reshape(rows, hpg * hd)


def _cache_mem_attn_kernel(q_ref, k_ref, v_ref, o_ref):
    o, _ = _row_attention(q_ref[...], [k_ref[...]], [v_ref[...]], [None], MEM_HEAD_DIM ** -0.5)
    o_ref[...] = o[:, 0]


def _cache_mem_attn(qm, mem_k, mem_v, nb):
    rows = qm.shape[0]
    assert rows % nb == 0
    kv = pl.BlockSpec((nb, N_MEM, MEM_HEADS, MEM_HEAD_DIM), lambda i: (i, 0, 0, 0))
    q = pl.BlockSpec((nb, MEM_HEADS, MEM_HEAD_DIM), lambda i: (i, 0, 0))
    return pl.pallas_call(
        _cache_mem_attn_kernel,
        grid=(rows // nb,),
        in_specs=[q, kv, kv],
        out_specs=q,
        out_shape=jax.ShapeDtypeStruct(qm.shape, F32),
        compiler_params=_params("parallel"),
        name="cache_mem_attn",
    )(qm, mem_k, mem_v)


def _merge_kernel(x_ref, *refs, n_y, n_att):
    y_refs, att_refs = refs[:n_y], refs[n_y:n_y + n_att]
    mem_ref, gl_ref, wglu_ref, watt_ref, wmem_ref, wout_ref, o_ref = refs[n_y + n_att:]
    d = D_MODEL
    y = jnp.concatenate([r[...] for r in y_refs], axis=1) if n_y > 1 else y_refs[0][...]
    z = jax.nn.gelu(y).astype(BF16)
    glu = jnp.dot(z, wglu_ref[...], preferred_element_type=F32)
    merged = jax.nn.sigmoid(gl_ref[:, 0:d]) * (glu[:, 0:d] * jax.nn.sigmoid(glu[:, d:2 * d]))
    if n_att == 1:
        att = att_refs[0][...]
    else:
        lses = [r[...] for r in att_refs[1::2]]
        top = functools.reduce(jnp.maximum, lses)
        ws = [jnp.exp(l - top) for l in lses]
        att = sum(w * r[...] for w, r in zip(ws, att_refs[0::2])) / sum(ws)
    b_att = jnp.dot(att.astype(BF16), watt_ref[...], preferred_element_type=F32)
    merged = merged + jax.nn.sigmoid(gl_ref[:, d:2 * d]) * b_att
    b_mem = jnp.dot(mem_ref[...].astype(BF16), wmem_ref[...], preferred_element_type=F32)
    merged = merged + jax.nn.sigmoid(gl_ref[:, 2 * d:3 * d]) * b_mem
    o_ref[...] = x_ref[...] + jnp.dot(merged.astype(BF16), wout_ref[...], preferred_element_type=F32)


def _merge(x, ys, atts, o_mem, gl, w_glu, w_att_o, w_mem_o, w_out, tm):
    rows = x.shape[0]
    assert rows % tm == 0

    def tile(a):
        return pl.BlockSpec((tm, a.shape[1]), lambda i: (i, 0))

    acts = (x, *ys, *atts, o_mem, gl)
    weights = (w_glu, w_att_o, w_mem_o, w_out)
    return pl.pallas_call(
        functools.partial(_merge_kernel, n_y=len(ys), n_att=len(atts)),
        grid=(rows // tm,),
        in_specs=[tile(a) for a in acts] + [_full(w.shape) for w in weights],
        out_specs=tile(x),
        out_shape=jax.ShapeDtypeStruct(x.shape, F32),
        compiler_params=_params("parallel"),
        name="merge",
    )(*acts, *weights)


def _ffn_kernel(*refs, tm, tiles_per_seq, stepwise):
    if stepwise:
        x_ref, g2_ref, wup_ref, cw_ref, cb_ref, wdn_ref, gf_ref, prev_ref, y_ref, conv_ref = refs
    else:
        x_ref, g2_ref, wup_ref, cw_ref, cb_ref, wdn_ref, gf_ref, y_ref, conv_ref, a_scr = refs
        i = pl.program_id(0)
        first = i % tiles_per_seq == 0

        @pl.when(first)
        def _():
            a_scr[0:SUBLANES, :] = jnp.zeros((SUBLANES, D_FF), F32)

        @pl.when(jnp.logical_not(first))
        def _():
            a_scr[0:SUBLANES, :] = a_scr[tm:tm + SUBLANES, :]

    x = x_ref[...]
    h = _rmsnorm(x, g2_ref[...]).astype(BF16)
    acc = jnp.zeros((tm, D_MODEL), F32)
    for c0 in range(0, D_FF, FFN_CHUNK):
        cols = slice(c0, c0 + FFN_CHUNK)
        a = jnp.dot(h, wup_ref[:, cols], preferred_element_type=F32)
        if stepwise:
            a2, a1 = prev_ref[:, cols], prev_ref[:, D_FF + c0:D_FF + c0 + FFN_CHUNK]
            conv_ref[:, cols] = a1
            conv_ref[:, D_FF + c0:D_FF + c0 + FFN_CHUNK] = a
        else:
            a_scr[SUBLANES:SUBLANES + tm, cols] = a
            a1 = a_scr[SUBLANES - 1:SUBLANES - 1 + tm, cols]
            a2 = a_scr[SUBLANES - 2:SUBLANES - 2 + tm, cols]
        c = a2 * cw_ref[0:1, cols] + a1 * cw_ref[1:2, cols] + a * cw_ref[2:3, cols] + cb_ref[:, cols]
        v = jnp.dot(h, wup_ref[:, D_FF + c0:D_FF + c0 + FFN_CHUNK], preferred_element_type=F32)
        acc = acc + jnp.dot((jax.nn.gelu(c) * v).astype(BF16), wdn_ref[cols, :], preferred_element_type=F32)
    y_ref[...] = _rmsnorm(x + acc, gf_ref[...])
    if not stepwise:
        @pl.when(i % tiles_per_seq == tiles_per_seq - 1)
        def _():
            conv_ref[...] = a_scr[SUBLANES + tm - (CONV_W - 1):SUBLANES + tm, :]


def _ffn(x, g2, w_up, conv_w, conv_b, w_down, gf, tm, n_seq=None, prev=None):
    rows, d = x.shape
    assert rows % tm == 0 and D_FF % FFN_CHUNK == 0 and FFN_CHUNK % LANES == 0
    stepwise = prev is not None
    weights = (g2.reshape(1, d), w_up, conv_w, conv_b.reshape(1, D_FF), w_down, gf.reshape(1, d))
    in_specs = [pl.BlockSpec((tm, d), lambda i: (i, 0))] + [_full(w.shape) for w in weights]
    args = (x,) + weights
    if stepwise:
        tiles_per_seq = 1
        in_specs.append(pl.BlockSpec((tm, 2 * D_FF), lambda i: (i, 0)))
        args += (prev,)
        conv_spec = pl.BlockSpec((tm, 2 * D_FF), lambda i: (i, 0))
        conv_shape = jax.ShapeDtypeStruct((rows, 2 * D_FF), F32)
        scratch = []
    else:
        tiles_per_seq = rows // n_seq // tm
        assert tiles_per_seq * tm * n_seq == rows
        conv_spec = pl.BlockSpec((None, CONV_W - 1, D_FF), lambda i: (i // tiles_per_seq, 0, 0))
        conv_shape = jax.ShapeDtypeStruct((n_seq, CONV_W - 1, D_FF), F32)
        scratch = [pltpu.VMEM((tm + 2 * SUBLANES, D_FF), F32)]
    return pl.pallas_call(
        functools.partial(_ffn_kernel, tm=tm, tiles_per_seq=tiles_per_seq, stepwise=stepwise),
        grid=(rows // tm,),
        in_specs=in_specs,
        out_specs=[pl.BlockSpec((tm, d), lambda i: (i, 0)), conv_spec],
        out_shape=[jax.ShapeDtypeStruct((rows, d), F32), conv_shape],
        scratch_shapes=scratch,
        compiler_params=_params("arbitrary"),
        name="ffn_step" if stepwise else "ffn",
    )(*args)


def kernel(x_prompt, x_sample, state_ssm_re, state_ssm_im, cache_w1_k, cache_w1_v, cache_w2_k, cache_w2_v, cache_w3_k, cache_w3_v, cache_mem_k, cache_mem_v, state_ffn_conv, mem_prompt, norm1_g, w_in, ssm_a_re, ssm_a_im, ssm_log_dt, ssm_b_re, ssm_b_im, ssm_c_re, ssm_c_im, ssm_d, w_ssm_glu, w_att_o, mem_norm_g, w_mem_kv, w_mem_o, w_out, norm2_g, w_up, ffn_conv_w, ffn_conv_b, w_down, final_norm_g):
    n_seq, seq_len, d = x_prompt.shape
    n_dec, dec_len, _ = x_sample.shape
    depth = norm1_g.shape[0]
    assert d == D_MODEL and depth == 1 and dec_len == 1
    assert w_in.shape == (depth, D_MODEL, sum(IN_SPLITS)) and w_up.shape == (depth, D_MODEL, 2 * D_FF)
    assert mem_prompt.shape == (n_seq, N_MEM, D_MODEL)
    assert ssm_a_re.shape == (depth, SSM_GROUPS, SSM_STATE)
    assert seq_len % ROW_TILE == 0 and n_dec % SUBLANES == 0
    hpg, hd = HEADS_PER_GROUP, ATT_HEAD_DIM
    rows_p = n_seq * seq_len
    dec_tile = n_dec if n_dec <= ROW_TILE else ROW_TILE

    wb = lambda w: w[0].astype(BF16)
    w_in_b, w_glu_b, w_att_b, w_memkv_b = wb(w_in), wb(w_ssm_glu), wb(w_att_o), wb(w_mem_kv)
    w_memo_b, w_out_b, w_up_b, w_down_b = wb(w_mem_o), wb(w_out), wb(w_up), wb(w_down)

    m_mat, w_mat, v_mat, v0_mat, a1, apow = _ssm_prep(ssm_a_re[0], ssm_a_im[0], ssm_log_dt[0], ssm_b_re[0],
                                                  ssm_b_im[0], ssm_c_re[0], ssm_c_im[0])
    w0_mat = w_mat[:, (SSM_T - 1) * LANES:, :]

    xp = x_prompt.reshape(rows_p, d)
    u, q, k, v, qm, gl = _norm_proj(xp, norm1_g[0], w_in_b, IN_SPLITS, ROW_TILE)
    y_ssm, fin_re, fin_im = _ssm_prompt(u, n_seq, m_mat, w_mat, v_mat, apow, ssm_d[0])
    atts = [a for g in range(len(DIL_PAIRS)) for a in _dil_attn(q, k, v, n_seq, seq_len, g)]
    mk, mv = _norm_proj(mem_prompt.reshape(n_seq * N_MEM, d), mem_norm_g[0], w_memkv_b, (MEM_WIDTH, MEM_WIDTH),
                        ROW_TILE)
    o_mem = _mem_attn(qm, mk.reshape(n_seq, N_MEM, MEM_WIDTH), mv.reshape(n_seq, N_MEM, MEM_WIDTH),
                      n_seq, seq_len, 2 * ROW_TILE)
    x1 = _merge(xp, y_ssm, atts, o_mem, gl, w_glu_b, w_att_b, w_memo_b, w_out_b, ROW_TILE)
    y_p, conv_p = _ffn(x1, norm2_g[0], w_up_b, ffn_conv_w[0], ffn_conv_b[0], w_down_b, final_norm_g,
                       ROW_TILE, n_seq=n_seq)

    def final_state(s):
        return s.reshape(1, n_seq, SSM_GROUPS, SSM_STATE)
    k3 = k.reshape(n_seq, seq_len, ATT_WIDTH)
    v3 = v.reshape(n_seq, seq_len, ATT_WIDTH)
    gw = ATT_GROUP_WIDTH
    win_p = []
    for g, (win, _) in enumerate(DIL_PAIRS):
        keep = min(win, seq_len)
        for t in (k3, v3):
            win_p.append(t[:, seq_len - keep:, g * gw:(g + 1) * gw].reshape(1, n_seq, keep, hpg, hd))
    mem_kv = [mk.reshape(1, n_seq, N_MEM, MEM_HEADS, MEM_HEAD_DIM), mv.reshape(1, n_seq, N_MEM, MEM_HEADS, MEM_HEAD_DIM)]

    xs = x_sample.reshape(n_dec, d)
    us, qs, ks, vs, qms, gls = _norm_proj(xs, norm1_g[0], w_in_b, IN_SPLITS, dec_tile)
    ys_ssm, sn_re, sn_im = _ssm_step(us, state_ssm_re[0], state_ssm_im[0], w0_mat, v0_mat, a1, ssm_d[0])
    n_g = len(DIL_PAIRS)
    ks4, vs4 = (t.reshape(n_dec, n_g, hpg, hd) for t in (ks, vs))
    att_s = _cache_attn(qs, ks, vs, (cache_w1_k[0], cache_w2_k[0], cache_w3_k[0]),
                        (cache_w1_v[0], cache_w2_v[0], cache_w3_v[0]), SUBLANES)
    mem_s = _cache_mem_attn(qms.reshape(n_dec, MEM_HEADS, MEM_HEAD_DIM), cache_mem_k[0], cache_mem_v[0],
                            CACHE_ROWS_PER_STEP)
    xs1 = _merge(xs, [ys_ssm], [att_s], mem_s.reshape(n_dec, MEM_WIDTH), gls,
                 w_glu_b, w_att_b, w_memo_b, w_out_b, dec_tile)
    y_s, conv_s = _ffn(xs1, norm2_g[0], w_up_b, ffn_conv_w[0], ffn_conv_b[0], w_down_b, final_norm_g,
                       dec_tile, prev=state_ffn_conv[0].reshape(n_dec, (CONV_W - 1) * D_FF))

    win_s = []
    for g in range(len(DIL_PAIRS)):
        win_s += [ks4[None, :, g:g + 1], vs4[None, :, g:g + 1]]

    return (y_p.reshape(n_seq, seq_len, d), y_s.reshape(n_dec, 1, d),
            final_state(fin_re), final_state(fin_im), *win_p, *mem_kv, conv_p[None],
            sn_re.reshape(1, n_dec, SSM_GROUPS, SSM_STATE), sn_im.reshape(1, n_dec, SSM_GROUPS, SSM_STATE),
            *win_s, conv_s.reshape(1, n_dec, CONV_W - 1, D_FF))
```

```python
import functools

import jax
import jax.numpy as jnp
from jax import lax
from jax.experimental import pallas as pl
from jax.experimental.pallas import tpu as pltpu

F32 = jnp.float32
BF16 = jnp.bfloat16

D_MODEL = 1024
SSM_WIDTH = 512
SSM_GROUP = 16
SSM_GROUPS = 32
SSM_STATE = 64
ATT_HEAD_DIM = 64
HEADS_PER_GROUP = 4
DIL_PAIRS = ((128, 1), (512, 4), (2048, 16))
ATT_HEADS = len(DIL_PAIRS) * HEADS_PER_GROUP
ATT_WIDTH = ATT_HEADS * ATT_HEAD_DIM
ATT_GROUP_WIDTH = HEADS_PER_GROUP * ATT_HEAD_DIM
ATT_STEPS = 128
N_MEM = 256
MEM_HEADS = 4
MEM_HEAD_DIM = 128
MEM_WIDTH = MEM_HEADS * MEM_HEAD_DIM
N_BRANCH = 3
D_FF = 2816
CONV_W = 3
EPS = 1e-6
PROJ_SPLITS = (SSM_WIDTH, ATT_WIDTH, ATT_WIDTH, ATT_WIDTH, MEM_WIDTH)
IN_WIDTH = sum(PROJ_SPLITS) + N_BRANCH * D_MODEL

LANES = 128
SUBLANES = 8
VMEM_LIMIT_BYTES = 56 * 1024 * 1024

SSM_T = 8
SSM_LANE_BLOCKS = SSM_WIDTH // LANES
SSM_GROUPS_PER_BLOCK = LANES // SSM_GROUP
SSM_STATES_PER_BLOCK = SSM_GROUPS_PER_BLOCK * SSM_STATE
SSM_STATE_LANES = 2 * SSM_STATES_PER_BLOCK

ROW_TILE = 512
SSM_ROW_TILE = 256
FFN_CHUNK = 2816
CACHE_ROWS_PER_STEP = 4
ATT_CHUNK = 2048
NEG_BIG = -1e30


def _alibi_slopes():
    return [float(2.0 ** (-8.0 * h / ATT_HEADS)) for h in range(1, ATT_HEADS + 1)]


def _params(*sem):
    return pltpu.CompilerParams(dimension_semantics=sem, vmem_limit_bytes=VMEM_LIMIT_BYTES)


def _rmsnorm(x, g):
    ms = jnp.mean(x * x, axis=-1, keepdims=True)
    return x * lax.rsqrt(ms + EPS) * g


def _full(shape):
    nd = len(shape)
    return pl.BlockSpec(shape, lambda *_: (0,) * nd, pipeline_mode=pl.Buffered(1))


def _norm_proj_kernel(x_ref, g_ref, w_ref, *out_refs, splits):
    h = _rmsnorm(x_ref[...], g_ref[...]).astype(BF16)
    off = 0
    for o_ref, width in zip(out_refs, splits):
        for c0 in range(0, width, 512):
            cw = min(512, width - c0)
            o_ref[:, c0:c0 + cw] = jnp.dot(h, w_ref[:, off + c0:off + c0 + cw], preferred_element_type=F32)
        off += width


def _norm_proj(x, g, w_bf16, splits, tm):
    rows, d = x.shape
    assert rows % tm == 0 and sum(splits) == w_bf16.shape[1]
    return pl.pallas_call(
        functools.partial(_norm_proj_kernel, splits=splits),
        grid=(rows // tm,),
        in_specs=[pl.BlockSpec((tm, d), lambda i: (i, 0)), _full((1, d)), _full(w_bf16.shape)],
        out_specs=[pl.BlockSpec((tm, s), lambda i: (i, 0)) for s in splits],
        out_shape=[jax.ShapeDtypeStruct((rows, s), F32) for s in splits],
        compiler_params=_params("parallel"),
        name="norm_proj",
    )(x, g.reshape(1, d), w_bf16)


def _ssm_layout(a_re, a_im, log_dt, b_re, b_im, c_re, c_im):
    nb, gpb, p, c = SSM_LANE_BLOCKS, SSM_GROUPS_PER_BLOCK, SSM_STATE, SSM_GROUP
    rows = jnp.stack([a_re.reshape(nb, gpb * p), a_im.reshape(nb, gpb * p),
                      jnp.repeat(log_dt, p).reshape(nb, gpb * p)], axis=1)
    eye = jnp.eye(gpb, dtype=F32)

    def place_b(b):
        return jnp.einsum('bgpc,gh->bgchp', b.reshape(nb, gpb, p, c), eye).reshape(nb, gpb * c, gpb * p)

    def place_c(m):
        return jnp.einsum('bgcp,gh->bhpgc', m.reshape(nb, gpb, c, p), eye).reshape(nb, gpb * p, gpb * c)

    return rows, place_b(b_re), place_b(b_im), place_c(c_re), place_c(c_im)


def _ssm_prep_kernel(rows_ref, bre_ref, bim_ref, cre_ref, cim_ref,
                     m_ref, w_ref, v_ref, v0_ref, a1_ref, apow_ref, *, t_chunk):
    sp = SSM_STATES_PER_BLOCK
    a_re, a_im, dt = rows_ref[0:1, :], rows_ref[1:2, :], jnp.exp(rows_ref[2:3, :])

    def powers(k):
        mag = jnp.exp(a_re * dt * k)
        ang = a_im * dt * k
        return mag * jnp.cos(ang), mag * jnp.sin(ang)

    n_pow = 2 * SUBLANES
    assert t_chunk + 1 <= n_pow
    pw_re, pw_im = powers(lax.broadcasted_iota(jnp.int32, (n_pow, 1), 0).astype(F32))
    pwt_re, pwt_im = pw_re.T, pw_im.T
    ab_re, ab_im = pw_re[1:2, :], pw_im[1:2, :]
    den = a_re * a_re + a_im * a_im
    q_re = ((ab_re - 1.0) * a_re + ab_im * a_im) / den
    q_im = (ab_im * a_re - (ab_re - 1.0) * a_im) / den
    bre, bim = bre_ref[...], bim_ref[...]
    bb_re = q_re * bre - q_im * bim
    bb_im = q_re * bim + q_im * bre
    cre, cim = cre_ref[...], cim_ref[...]

    m_ref[...] = jnp.zeros(m_ref.shape, m_ref.dtype)
    for k in range(t_chunk):
        pk_re, pk_im = pw_re[k:k + 1, :], pw_im[k:k + 1, :]
        bk_re = bb_re * pk_re - bb_im * pk_im
        bk_im = bb_re * pk_im + bb_im * pk_re
        t = t_chunk - 1 - k
        w_ref[t * LANES:(t + 1) * LANES, 0:sp] = bk_re.astype(w_ref.dtype)
        w_ref[t * LANES:(t + 1) * LANES, sp:2 * sp] = bk_im.astype(w_ref.dtype)
        kk = (jnp.dot(bk_re, cre, precision=lax.Precision.HIGHEST, preferred_element_type=F32)
              - jnp.dot(bk_im, cim, precision=lax.Precision.HIGHEST, preferred_element_type=F32))
        kk = kk.astype(m_ref.dtype)
        for t0 in range(t_chunk - k):
            m_ref[t0 * LANES:(t0 + 1) * LANES, (t0 + k) * LANES:(t0 + k + 1) * LANES] = kk

    for t in range(t_chunk):
        pc_re, pc_im = pwt_re[:, t + 1:t + 2], pwt_im[:, t + 1:t + 2]
        v_ref[0:sp, t * LANES:(t + 1) * LANES] = (cre * pc_re - cim * pc_im).astype(v_ref.dtype)
        v_ref[sp:2 * sp, t * LANES:(t + 1) * LANES] = (-(cre * pc_im + cim * pc_re)).astype(v_ref.dtype)
    v0_ref[0:sp, :] = cre.astype(v0_ref.dtype)
    v0_ref[sp:2 * sp, :] = (-cim).astype(v0_ref.dtype)

    a1_ref[:, 0:sp] = ab_re
    a1_ref[:, sp:2 * sp] = ab_im
    steps = (lax.broadcasted_iota(jnp.int32, (SUBLANES, 1), 0) + 1) * t_chunk
    ap_re, ap_im = powers(steps.astype(F32))
    apow_ref[:, 0:sp] = ap_re
    apow_ref[:, sp:2 * sp] = ap_im


def _ssm_prep(a_re, a_im, log_dt, b_re, b_im, c_re, c_im):
    rows, pbre, pbim, pcre, pcim = _ssm_layout(a_re, a_im, log_dt, b_re, b_im, c_re, c_im)
    nb, sp, sl, tl = SSM_LANE_BLOCKS, SSM_STATES_PER_BLOCK, SSM_STATE_LANES, SSM_T * LANES

    def blk(shape):
        return pl.BlockSpec((None,) + shape, lambda b: (b, 0, 0))

    return pl.pallas_call(
        functools.partial(_ssm_prep_kernel, t_chunk=SSM_T),
        grid=(nb,),
        in_specs=[blk((3, sp)), blk((LANES, sp)), blk((LANES, sp)), blk((sp, LANES)), blk((sp, LANES))],
        out_specs=[blk((tl, tl)), blk((tl, sl)), blk((sl, tl)), blk((sl, LANES)), blk((1, sl)), blk((SUBLANES, sl))],
        out_shape=[jax.ShapeDtypeStruct((nb, tl, tl), BF16), jax.ShapeDtypeStruct((nb, tl, sl), BF16),
                   jax.ShapeDtypeStruct((nb, sl, tl), BF16), jax.ShapeDtypeStruct((nb, sl, LANES), BF16),
                   jax.ShapeDtypeStruct((nb, 1, sl), F32), jax.ShapeDtypeStruct((nb, SUBLANES, sl), F32)],
        compiler_params=_params("parallel"),
        name="ssm_prep",
    )(rows, pbre, pbim, pcre, pcim)


def _chunk_tokens(u_ref, t, t_chunk):
    return u_ref[pl.ds(t, u_ref.shape[0] // t_chunk, stride=t_chunk), :]


def _chunk_lanes(u_ref, t_chunk):
    return jnp.concatenate([_chunk_tokens(u_ref, t, t_chunk) for t in range(t_chunk)], axis=1)


def _ssm_end_state_kernel(*refs, t_chunk):
    nb, sl = SSM_LANE_BLOCKS, SSM_STATE_LANES
    u_refs, w_ref, e_ref = refs[:nb], refs[nb], refs[nb + 1]
    for b in range(nb):
        ub = _chunk_lanes(u_refs[b], t_chunk).astype(BF16)
        e_ref[:, b * sl:(b + 1) * sl] = jnp.dot(ub, w_ref[b], preferred_element_type=F32)


def _ssm_scan_kernel(ere_ref, eim_ref, pre_ref, pim_ref, spre_ref, spim_ref, fre_ref, fim_ref):
    n_tiles = ere_ref.shape[0] // SUBLANES
    width = ere_ref.shape[1]
    p_re, p_im = pre_ref[...], pim_ref[...]
    row = lax.broadcasted_iota(jnp.int32, (SUBLANES, width), 0)

    def shift_down(x, k):
        return jnp.where(row >= k, pltpu.roll(x, k, 0), 0.0)

    def body(i, carry):
        c_re, c_im = carry
        rows = pl.ds(pl.multiple_of(i * SUBLANES, SUBLANES), SUBLANES)
        x_re, x_im = ere_ref[rows, :], eim_ref[rows, :]
        for k in (1, 2, 4):
            a_re, a_im = p_re[k - 1:k, :], p_im[k - 1:k, :]
            s_re, s_im = shift_down(x_re, k), shift_down(x_im, k)
            x_re, x_im = x_re + a_re * s_re - a_im * s_im, x_im + a_re * s_im + a_im * s_re
        t_re = x_re + p_re * c_re - p_im * c_im
        t_im = x_im + p_re * c_im + p_im * c_re
        spre_ref[rows, :] = jnp.where(row >= 1, pltpu.roll(t_re, 1, 0), c_re)
        spim_ref[rows, :] = jnp.where(row >= 1, pltpu.roll(t_im, 1, 0), c_im)
        return t_re[SUBLANES - 1:SUBLANES, :], t_im[SUBLANES - 1:SUBLANES, :]

    zero = jnp.zeros((1, width), F32)
    f_re, f_im = lax.fori_loop(0, n_tiles, body, (zero, zero))
    fre_ref[...] = f_re
    fim_ref[...] = f_im


def _ssm_output_kernel(*refs, t_chunk):
    nb, sp = SSM_LANE_BLOCKS, SSM_STATES_PER_BLOCK
    u_refs, (spre_ref, spim_ref, m_ref, v_ref, d_ref), y_refs = refs[:nb], refs[nb:nb + 5], refs[nb + 5:]
    for b in range(nb):
        ub = _chunk_lanes(u_refs[b], t_chunk).astype(BF16)
        yb = jnp.dot(ub, m_ref[b], preferred_element_type=F32)
        states = slice(b * sp, (b + 1) * sp)
        sprev = jnp.concatenate([spre_ref[:, states], spim_ref[:, states]], axis=1).astype(BF16)
        yb = yb + jnp.dot(sprev, v_ref[b], preferred_element_type=F32)
        d = d_ref[:, b * LANES:(b + 1) * LANES]
        for t in range(t_chunk):
            y_t = yb[:, t * LANES:(t + 1) * LANES] + d * _chunk_tokens(u_refs[b], t, t_chunk)
            y_refs[b][pl.ds(t, yb.shape[0], stride=t_chunk), :] = y_t


def _ssm_prompt(u, n_seq, m_mat, w_mat, v_mat, apow, d_skip):
    rows, t_chunk, nb, sl, sp = u.shape[0], SSM_T, SSM_LANE_BLOCKS, SSM_STATE_LANES, SSM_STATES_PER_BLOCK
    assert rows % (t_chunk * n_seq) == 0
    n_chunks = rows // t_chunk // n_seq
    tr = min(SSM_ROW_TILE, n_chunks)
    assert n_chunks % tr == 0 and n_chunks % SUBLANES == 0
    tiles = n_chunks // tr
    u_specs = [pl.BlockSpec((tr * t_chunk, LANES), lambda i, b=b: (i, b)) for b in range(nb)]
    e = pl.pallas_call(
        functools.partial(_ssm_end_state_kernel, t_chunk=t_chunk),
        grid=(n_seq * tiles,),
        in_specs=u_specs + [_full(w_mat.shape)],
        out_specs=pl.BlockSpec((tr, nb * sl), lambda i: (i % tiles, i // tiles)),
        out_shape=jax.ShapeDtypeStruct((n_chunks, n_seq * nb * sl), F32),
        compiler_params=_params("parallel"),
        name="ssm_end_state",
    )(*([u] * nb), w_mat)
    col = pl.BlockSpec((n_chunks, sp), lambda g: (0, g))
    fin = pl.BlockSpec((1, sp), lambda g: (0, g))
    sp_re, sp_im, fin_re, fin_im = pl.pallas_call(
        _ssm_scan_kernel,
        grid=(n_seq * nb,),
        in_specs=[pl.BlockSpec((n_chunks, sp), lambda g: (0, 2 * g)),
                  pl.BlockSpec((n_chunks, sp), lambda g: (0, 2 * g + 1)),
                  pl.BlockSpec((None, SUBLANES, sp), lambda g: (g % nb, 0, 0)),
                  pl.BlockSpec((None, SUBLANES, sp), lambda g: (g % nb, 0, 1))],
        out_specs=[col, col, fin, fin],
        out_shape=[jax.ShapeDtypeStruct((n_chunks, n_seq * nb * sp), F32)] * 2
        + [jax.ShapeDtypeStruct((1, n_seq * nb * sp), F32)] * 2,
        compiler_params=_params("parallel"),
        name="ssm_scan",
    )(e, e, apow, apow)
    sp_spec = pl.BlockSpec((tr, nb * sp), lambda i: (i % tiles, i // tiles))
    ys = pl.pallas_call(
        functools.partial(_ssm_output_kernel, t_chunk=t_chunk),
        grid=(n_seq * tiles,),
        in_specs=u_specs + [sp_spec, sp_spec, _full(m_mat.shape), _full(v_mat.shape), _full((1, SSM_WIDTH))],
        out_specs=[pl.BlockSpec((tr * t_chunk, LANES), lambda i: (i, 0)) for _ in range(nb)],
        out_shape=[jax.ShapeDtypeStruct((rows, LANES), F32) for _ in range(nb)],
        compiler_params=_params("parallel"),
        name="ssm_output",
    )(*([u] * nb), sp_re, sp_im, m_mat, v_mat, d_skip.reshape(1, SSM_WIDTH))
    return ys, fin_re, fin_im


def _ssm_step_kernel(u_ref, sre_ref, sim_ref, w0_ref, v0_ref, a1_ref, d_ref, y_ref, nre_ref, nim_ref):
    sp = SSM_STATES_PER_BLOCK
    for b in range(SSM_LANE_BLOCKS):
        lanes = slice(b * LANES, (b + 1) * LANES)
        states = slice(b * sp, (b + 1) * sp)
        u = u_ref[:, lanes]
        e = jnp.dot(u.astype(BF16), w0_ref[b], preferred_element_type=F32)
        a_re, a_im = a1_ref[b, :, 0:sp], a1_ref[b, :, sp:2 * sp]
        s_re, s_im = sre_ref[:, states], sim_ref[:, states]
        n_re = a_re * s_re - a_im * s_im + e[:, 0:sp]
        n_im = a_re * s_im + a_im * s_re + e[:, sp:2 * sp]
        nre_ref[:, states] = n_re
        nim_ref[:, states] = n_im
        sn = jnp.concatenate([n_re, n_im], axis=1).astype(BF16)
        y_ref[:, lanes] = jnp.dot(sn, v0_ref[b], preferred_element_type=F32) + d_ref[:, lanes] * u


def _ssm_step(u, s_re, s_im, w0, v0, a1, d_skip):
    rows = u.shape[0]
    ns = SSM_GROUPS * SSM_STATE
    args = (u, s_re.reshape(rows, ns), s_im.reshape(rows, ns), w0, v0, a1, d_skip.reshape(1, SSM_WIDTH))
    return pl.pallas_call(
        _ssm_step_kernel,
        grid=(1,),
        in_specs=[_full(a.shape) for a in args],
        out_specs=[_full((rows, SSM_WIDTH)), _full((rows, ns)), _full((rows, ns))],
        out_shape=[jax.ShapeDtypeStruct((rows, SSM_WIDTH), F32), jax.ShapeDtypeStruct((rows, ns), F32),
                   jax.ShapeDtypeStruct((rows, ns), F32)],
        compiler_params=_params("arbitrary"),
        name="ssm_step",
    )(*args)


def _dil_attn_kernel(q_ref, kc_ref, kp_ref, vc_ref, vp_ref, o_ref, lse_ref, *, dil, slopes):
    steps, hd = ATT_STEPS, ATT_HEAD_DIM
    chunk = q_ref.shape[0]
    span = steps * dil
    first_chunk = pl.program_id(1) == 0
    pair = pl.program_id(2)
    qi = lax.broadcasted_iota(jnp.int32, (steps, 2 * steps), 0)
    kj = lax.broadcasted_iota(jnp.int32, (steps, 2 * steps), 1)
    dist = qi + steps - kj
    band = (dist >= 0) & (dist <= steps)
    distf = (dist * dil).astype(F32)
    lane = lax.broadcasted_iota(jnp.int32, (steps, LANES), 1)
    heads = [lane < hd, lane >= hd]
    biases, first_biases = [], []
    for hh in range(2):
        slope = jnp.where(pair == 0, slopes[hh], slopes[2 + hh])
        bias = jnp.where(band, -slope * distf, NEG_BIG)
        biases.append(bias)
        first_biases.append(jnp.where(first_chunk & (kj < steps), NEG_BIG, bias))

    def rows(ref, start):
        return ref[pl.ds(start, steps, stride=dil), :] if dil > 1 else ref[pl.ds(start, steps), :]

    for r in range(dil):
        for qb in range(chunk // span):
            start = r + qb * span
            q = rows(q_ref, start) * (hd ** -0.5)
            if qb == 0:
                k_prev, v_prev = rows(kp_ref, r), rows(vp_ref, r)
            else:
                k_prev, v_prev = rows(kc_ref, start - span), rows(vc_ref, start - span)
            kk = jnp.concatenate([k_prev, rows(kc_ref, start)], axis=0).astype(BF16)
            vv = jnp.concatenate([v_prev, rows(vc_ref, start)], axis=0).astype(BF16)
            out = jnp.zeros((steps, LANES), F32)
            lse = jnp.zeros((steps, LANES), F32)
            for hh in range(2):
                qh = jnp.where(heads[hh], q, 0.0).astype(BF16)
                s = lax.dot_general(qh, kk, (((1,), (1,)), ((), ())), preferred_element_type=F32)
                s = s + (first_biases[hh] if qb == 0 else biases[hh])
                m = jnp.max(s, axis=-1, keepdims=True)
                p = jnp.exp(s - m)
                den = jnp.sum(p, axis=-1, keepdims=True)
                oh = jnp.dot(p.astype(BF16), vv, preferred_element_type=F32) / den
                out = jnp.where(heads[hh], oh, out)
                lse = jnp.where(heads[hh], m + jnp.log(den), lse)
            if dil > 1:
                o_ref[pl.ds(start, steps, stride=dil), :] = out
                lse_ref[pl.ds(start, steps, stride=dil), :] = lse
            else:
                o_ref[pl.ds(start, steps), :] = out
                lse_ref[pl.ds(start, steps), :] = lse


def _dil_attn(q, k, v, n_seq, seq_len, group):
    win, dil = DIL_PAIRS[group]
    steps, gw = ATT_STEPS, ATT_GROUP_WIDTH
    span = steps * dil
    chunk = ATT_CHUNK
    assert win // dil == steps and chunk % span == 0 and seq_len % chunk == 0
    chunks = seq_len // chunk
    lane_blocks = gw // LANES
    cur = pl.BlockSpec((chunk, LANES), lambda n, c, p: (n * chunks + c, lane_blocks * group + p))
    prev = pl.BlockSpec((span, LANES), lambda n, c, p: (jnp.maximum((n * chunks + c) * (chunk // span) - 1, 0),
                                                        lane_blocks * group + p))
    out = pl.BlockSpec((chunk, LANES), lambda n, c, p: (n * chunks + c, p))
    slopes = tuple(_alibi_slopes()[group * HEADS_PER_GROUP:(group + 1) * HEADS_PER_GROUP])
    rows = n_seq * seq_len
    return pl.pallas_call(
        functools.partial(_dil_attn_kernel, dil=dil, slopes=slopes),
        grid=(n_seq, chunks, lane_blocks),
        in_specs=[cur, cur, prev, cur, prev],
        out_specs=[out, out],
        out_shape=[jax.ShapeDtypeStruct((rows, gw), F32), jax.ShapeDtypeStruct((rows, gw), F32)],
        compiler_params=_params("parallel", "parallel", "parallel"),
        name=f"dil_attn_w{win}",
    )(q, k, k, v, v)


def _mem_attn_kernel(q_ref, mk_ref, mv_ref, o_ref):
    hd = MEM_HEAD_DIM
    for h in range(MEM_HEADS):
        lanes = slice(h * hd, (h + 1) * hd)
        s = lax.dot_general(q_ref[:, lanes].astype(BF16), mk_ref[:, lanes].astype(BF16),
                            (((1,), (1,)), ((), ())), preferred_element_type=F32) * (hd ** -0.5)
        m = jnp.max(s, axis=-1, keepdims=True)
        p = jnp.exp(s - m)
        den = jnp.sum(p, axis=-1, keepdims=True)
        o_ref[:, lanes] = jnp.dot(p.astype(BF16), mv_ref[:, lanes].astype(BF16), preferred_element_type=F32) / den


def _mem_attn(qm, mk, mv, n_seq, seq_len, tm):
    assert seq_len % tm == 0
    tiles = seq_len // tm
    kv = pl.BlockSpec((None, N_MEM, MEM_WIDTH), lambda n, i: (n, 0, 0))
    return pl.pallas_call(
        _mem_attn_kernel,
        grid=(n_seq, tiles),
        in_specs=[pl.BlockSpec((tm, MEM_WIDTH), lambda n, i: (n * tiles + i, 0)), kv, kv],
        out_specs=pl.BlockSpec((tm, MEM_WIDTH), lambda n, i: (n * tiles + i, 0)),
        out_shape=jax.ShapeDtypeStruct(qm.shape, F32),
        compiler_params=_params("parallel", "parallel"),
        name="mem_attn",
    )(qm, mk, mv)


def _row_attention(q, k_blocks, v_blocks, biases, scale):
    qb = q[:, None]
    scores = []
    for k, bias in zip(k_blocks, biases):
        s = jnp.sum(k * qb, axis=-1, keepdims=True) * scale
        scores.append(s if bias is None else s + bias)
    m = functools.reduce(jnp.maximum, [jnp.max(s, axis=1, keepdims=True) for s in scores])
    den, acc = 0.0, 0.0
    for s, v in zip(scores, v_blocks):
        p = jnp.exp(s - m)
        den = den + jnp.sum(p, axis=1, keepdims=True)
        acc = acc + jnp.sum(p * v, axis=1, keepdims=True)
    return acc / den, m + jnp.log(den)


def _cache_attn_kernel(q_ref, kn_ref, vn_ref, k1_ref, v1_ref, k2_ref, v2_ref, k3_ref, v3_ref, o_ref,
                       qt_scr, vnt_scr, *, slopes):
    hpg, hd = HEADS_PER_GROUP, ATT_HEAD_DIM
    j = pl.program_id(1)
    nb = q_ref.shape[0]
    scale = hd ** -0.5
    q, kn = q_ref[...], kn_ref[...]
    qt_scr[...] = q.T
    vnt_scr[...] = vn_ref[...].T
    lane_head = lax.broadcasted_iota(jnp.int32, q.shape, 1) // hd
    outs, lses = [], []
    for g, (k_ref, v_ref) in enumerate(((k1_ref, v1_ref), (k2_ref, v2_ref), (k3_ref, v3_ref))):
        dil = DIL_PAIRS[g][1]
        n_pos = k_ref.shape[-1]
        rows = pl.ds(pl.multiple_of((g * hpg + j) * hd, hd), hd)
        qg, vng = qt_scr[rows, :], vnt_scr[rows, :]
        slope = sum(jnp.where(j == h, slopes[g * hpg + h], 0.0) for h in range(hpg))
        back = n_pos - lax.broadcasted_iota(jnp.int32, (1, n_pos), 1)
        bias = jnp.where(back % dil == 0, -slope * back.astype(F32), NEG_BIG)
        row = lax.broadcasted_iota(jnp.int32, (nb, n_pos), 0)
        s = jnp.zeros((nb, n_pos), F32)
        for b in range(nb):
            s = jnp.where(row == b, jnp.sum(k_ref[b] * qg[:, b:b + 1], axis=0, keepdims=True), s)
        s = s * scale + bias
        s_new = jnp.sum(jnp.where(lane_head == g * hpg + j, q * kn, 0.0), axis=1, keepdims=True) * scale
        m = jnp.maximum(jnp.max(s, axis=1, keepdims=True), s_new)
        p = jnp.exp(s - m)
        p_new = jnp.exp(s_new - m)
        den = jnp.sum(p, axis=1, keepdims=True) + p_new
        cols = []
        for b in range(nb):
            acc = jnp.sum(v_ref[b] * p[b:b + 1, :], axis=1, keepdims=True) + p_new[b:b + 1, :] * vng[:, b:b + 1]
            cols.append(acc / den[b:b + 1, :])
        outs.append(cols)
        lses.append(m + jnp.log(den))
    top = functools.reduce(jnp.maximum, lses)
    ws = [jnp.exp(l - top) for l in lses]
    total = sum(ws)
    o_ref[...] = jnp.concatenate(
        [sum(w[b:b + 1, :] * cols[b] for w, cols in zip(ws, outs)) / total[b:b + 1, :] for b in range(nb)], axis=1)


def _cache_attn(q, k_new, v_new, caches_k, caches_v, nb):
    rows = q.shape[0]
    hpg, hd = HEADS_PER_GROUP, ATT_HEAD_DIM
    assert rows % nb == 0 and nb % SUBLANES == 0
    new = pl.BlockSpec((nb, ATT_WIDTH), lambda i, j: (i, 0))
    cache_specs, cache_args = [], []
    for g, (win, dil) in enumerate(DIL_PAIRS):
        for c in (caches_k[g], caches_v[g]):
            assert c.shape == (rows, win, hpg, hd) and win % dil == 0, c.shape
            cache_args.append(c.transpose(0, 2, 3, 1))
            cache_specs.append(pl.BlockSpec((nb, None, hd, win), lambda i, j: (i, j, 0, 0)))
    out = pl.pallas_call(
        functools.partial(_cache_attn_kernel, slopes=tuple(_alibi_slopes())),
        grid=(rows // nb, hpg),
        in_specs=[new, new, new] + cache_specs,
        out_specs=pl.BlockSpec((None, None, hd, nb), lambda i, j: (j, i, 0, 0)),
        out_shape=jax.ShapeDtypeStruct((hpg, rows // nb, hd, nb), F32),
        scratch_shapes=[pltpu.VMEM((ATT_WIDTH, nb), F32) for _ in range(2)],
        compiler_params=_params("parallel", "arbitrary"),
        name="cache_attn",
    )(q, k_new, v_new, *cache_args)
    return out.transpose(1, 3, 0, 2).reshape(rows, hpg * hd)


def _cache_mem_attn_kernel(q_ref, k_ref, v_ref, o_ref):
    o, _ = _row_attention(q_ref[...], [k_ref[...]], [v_ref[...]], [None], MEM_HEAD_DIM ** -0.5)
    o_ref[...] = o[:, 0]


def _cache_mem_attn(qm, mem_k, mem_v, nb):
    rows = qm.shape[0]
    assert rows % nb == 0
    kv = pl.BlockSpec((nb, N_MEM, MEM_HEADS, MEM_HEAD_DIM), lambda i: (i, 0, 0, 0))
    q = pl.BlockSpec((nb, MEM_HEADS, MEM_HEAD_DIM), lambda i: (i, 0, 0))
    return pl.pallas_call(
        _cache_mem_attn_kernel,
        grid=(rows // nb,),
        in_specs=[q, kv, kv],
        out_specs=q,
        out_shape=jax.ShapeDtypeStruct(qm.shape, F32),
        compiler_params=_params("parallel"),
        name="cache_mem_attn",
    )(qm, mem_k, mem_v)


def _merge_kernel(x_ref, *refs, n_y, n_att):
    y_refs, att_refs = refs[:n_y], refs[n_y:n_y + n_att]
    mem_ref, g1_ref, wgate_ref, wglu_ref, watt_ref, wmem_ref, wout_ref, o_ref = refs[n_y + n_att:]
    d = D_MODEL
    x = x_ref[...]
    h = _rmsnorm(x, g1_ref[...]).astype(BF16)

    def gate(i):
        return jax.nn.sigmoid(jnp.dot(h, wgate_ref[:, i * d:(i + 1) * d], preferred_element_type=F32))

    y = jnp.concatenate([r[...] for r in y_refs], axis=1) if n_y > 1 else y_refs[0][...]
    z = jax.nn.gelu(y).astype(BF16)
    glu = jnp.dot(z, wglu_ref[...], preferred_element_type=F32)
    merged = gate(0) * (glu[:, 0:d] * jax.nn.sigmoid(glu[:, d:2 * d]))
    if n_att == 1:
        att = att_refs[0][...]
    else:
        lses = [r[...] for r in att_refs[1::2]]
        top = functools.reduce(jnp.maximum, lses)
        ws = [jnp.exp(l - top) for l in lses]
        att = sum(w * r[...] for w, r in zip(ws, att_refs[0::2])) / sum(ws)
    b_att = jnp.dot(att.astype(BF16), watt_ref[...], preferred_element_type=F32)
    merged = merged + gate(1) * b_att
    b_mem = jnp.dot(mem_ref[...].astype(BF16), wmem_ref[...], preferred_element_type=F32)
    merged = merged + gate(2) * b_mem
    o_ref[...] = x + jnp.dot(merged.astype(BF16), wout_ref[...], preferred_element_type=F32)


def _merge(x, ys, atts, o_mem, g1, w_gate, w_glu, w_att_o, w_mem_o, w_out, tm):
    rows = x.shape[0]
    assert rows % tm == 0 and w_gate.shape == (D_MODEL, N_BRANCH * D_MODEL)

    def tile(a):
        return pl.BlockSpec((tm, a.shape[1]), lambda i: (i, 0))

    acts = (x, *ys, *atts, o_mem)
    weights = (g1.reshape(1, D_MODEL), w_gate, w_glu, w_att_o, w_mem_o, w_out)
    return pl.pallas_call(
        functools.partial(_merge_kernel, n_y=len(ys), n_att=len(atts)),
        grid=(rows // tm,),
        in_specs=[tile(a) for a in acts] + [_full(w.shape) for w in weights],
        out_specs=tile(x),
        out_shape=jax.ShapeDtypeStruct(x.shape, F32),
        compiler_params=_params("parallel"),
        name="merge",
    )(*acts, *weights)


def _ffn_kernel(*refs, tm, tiles_per_seq, stepwise):
    if stepwise:
        x_ref, g2_ref, wup_ref, cw_ref, cb_ref, wdn_ref, gf_ref, prev_ref, y_ref, conv_ref = refs
    else:
        x_ref, g2_ref, wup_ref, cw_ref, cb_ref, wdn_ref, gf_ref, y_ref, conv_ref, a_scr = refs
        i = pl.program_id(0)
        first = i % tiles_per_seq == 0

        @pl.when(first)
        def _():
            a_scr[0:SUBLANES, :] = jnp.zeros((SUBLANES, D_FF), F32)

        @pl.when(jnp.logical_not(first))
        def _():
            a_scr[0:SUBLANES, :] = a_scr[tm:tm + SUBLANES, :]

    x = x_ref[...]
    h = _rmsnorm(x, g2_ref[...]).astype(BF16)
    acc = jnp.zeros((tm, D_MODEL), F32)
    for c0 in range(0, D_FF, FFN_CHUNK):
        cols = slice(c0, c0 + FFN_CHUNK)
        a = jnp.dot(h, wup_ref[:, cols], preferred_element_type=F32)
        if stepwise:
            a2, a1 = prev_ref[:, cols], prev_ref[:, D_FF + c0:D_FF + c0 + FFN_CHUNK]
            conv_ref[:, cols] = a1
            conv_ref[:, D_FF + c0:D_FF + c0 + FFN_CHUNK] = a
        else:
            a_scr[SUBLANES:SUBLANES + tm, cols] = a
            a1 = a_scr[SUBLANES - 1:SUBLANES - 1 + tm, cols]
            a2 = a_scr[SUBLANES - 2:SUBLANES - 2 + tm, cols]
        c = a2 * cw_ref[0:1, cols] + a1 * cw_ref[1:2, cols] + a * cw_ref[2:3, cols] + cb_ref[:, cols]
        v = jnp.dot(h, wup_ref[:, D_FF + c0:D_FF + c0 + FFN_CHUNK], preferred_element_type=F32)
        acc = acc + jnp.dot((jax.nn.gelu(c) * v).astype(BF16), wdn_ref[cols, :], preferred_element_type=F32)
    y_ref[...] = _rmsnorm(x + acc, gf_ref[...])
    if not stepwise:
        @pl.when(i % tiles_per_seq == tiles_per_seq - 1)
        def _():
            conv_ref[...] = a_scr[SUBLANES + tm - (CONV_W - 1):SUBLANES + tm, :]


def _ffn(x, g2, w_up, conv_w, conv_b, w_down, gf, tm, n_seq=None, prev=None):
    rows, d = x.shape
    assert rows % tm == 0 and D_FF % FFN_CHUNK == 0 and FFN_CHUNK % LANES == 0
    stepwise = prev is not None
    weights = (g2.reshape(1, d), w_up, conv_w, conv_b.reshape(1, D_FF), w_down, gf.reshape(1, d))
    in_specs = [pl.BlockSpec((tm, d), lambda i: (i, 0))] + [_full(w.shape) for w in weights]
    args = (x,) + weights
    if stepwise:
        tiles_per_seq = 1
        in_specs.append(pl.BlockSpec((tm, 2 * D_FF), lambda i: (i, 0)))
        args += (prev,)
        conv_spec = pl.BlockSpec((tm, 2 * D_FF), lambda i: (i, 0))
        conv_shape = jax.ShapeDtypeStruct((rows, 2 * D_FF), F32)
        scratch = []
    else:
        tiles_per_seq = rows // n_seq // tm
        assert tiles_per_seq * tm * n_seq == rows
        conv_spec = pl.BlockSpec((None, CONV_W - 1, D_FF), lambda i: (i // tiles_per_seq, 0, 0))
        conv_shape = jax.ShapeDtypeStruct((n_seq, CONV_W - 1, D_FF), F32)
        scratch = [pltpu.VMEM((tm + 2 * SUBLANES, D_FF), F32)]
    return pl.pallas_call(
        functools.partial(_ffn_kernel, tm=tm, tiles_per_seq=tiles_per_seq, stepwise=stepwise),
        grid=(rows // tm,),
        in_specs=in_specs,
        out_specs=[pl.BlockSpec((tm, d), lambda i: (i, 0)), conv_spec],
        out_shape=[jax.ShapeDtypeStruct((rows, d), F32), conv_shape],
        scratch_shapes=scratch,
        compiler_params=_params("arbitrary"),
        name="ffn_step" if stepwise else "ffn",
    )(*args)


def kernel(x_prompt, x_sample, state_ssm_re, state_ssm_im, cache_w1_k, cache_w1_v, cache_w2_k, cache_w2_v, cache_w3_k, cache_w3_v, cache_mem_k, cache_mem_v, state_ffn_conv, mem_prompt, norm1_g, w_in, ssm_a_re, ssm_a_im, ssm_log_dt, ssm_b_re, ssm_b_im, ssm_c_re, ssm_c_im, ssm_d, w_ssm_glu, w_att_o, mem_norm_g, w_mem_kv, w_mem_o, w_out, norm2_g, w_up, ffn_conv_w, ffn_conv_b, w_down, final_norm_g):
    n_seq, seq_len, d = x_prompt.shape
    n_dec, dec_len, _ = x_sample.shape
    depth = norm1_g.shape[0]
    assert d == D_MODEL and depth == 1 and dec_len == 1
    assert w_in.shape == (depth, D_MODEL, IN_WIDTH) and w_up.shape == (depth, D_MODEL, 2 * D_FF)
    assert mem_prompt.shape == (n_seq, N_MEM, D_MODEL)
    assert ssm_a_re.shape == (depth, SSM_GROUPS, SSM_STATE)
    assert seq_len % ROW_TILE == 0 and n_dec % SUBLANES == 0
    hpg, hd = HEADS_PER_GROUP, ATT_HEAD_DIM
    rows_p = n_seq * seq_len
    dec_tile = n_dec if n_dec <= ROW_TILE else ROW_TILE

    wb = lambda w: w[0].astype(BF16)
    w_in_b, w_glu_b, w_att_b, w_memkv_b = wb(w_in), wb(w_ssm_glu), wb(w_att_o), wb(w_mem_kv)
    w_memo_b, w_out_b, w_up_b, w_down_b = wb(w_mem_o), wb(w_out), wb(w_up), wb(w_down)
    n_proj = sum(PROJ_SPLITS)
    w_proj_b, w_gate_b = w_in_b[:, :n_proj], w_in_b[:, n_proj:]

    m_mat, w_mat, v_mat, v0_mat, a1, apow = _ssm_prep(ssm_a_re[0], ssm_a_im[0], ssm_log_dt[0], ssm_b_re[0],
                                                  ssm_b_im[0], ssm_c_re[0], ssm_c_im[0])
    w0_mat = w_mat[:, (SSM_T - 1) * LANES:, :]

    xp = x_prompt.reshape(rows_p, d)
    u, q, k, v, qm = _norm_proj(xp, norm1_g[0], w_proj_b, PROJ_SPLITS, ROW_TILE)
    y_ssm, fin_re, fin_im = _ssm_prompt(u, n_seq, m_mat, w_mat, v_mat, apow, ssm_d[0])
    atts = [a for g in range(len(DIL_PAIRS)) for a in _dil_attn(q, k, v, n_seq, seq_len, g)]
    mk, mv = _norm_proj(mem_prompt.reshape(n_seq * N_MEM, d), mem_norm_g[0], w_memkv_b, (MEM_WIDTH, MEM_WIDTH),
                        ROW_TILE)
    o_mem = _mem_attn(qm, mk.reshape(n_seq, N_MEM, MEM_WIDTH), mv.reshape(n_seq, N_MEM, MEM_WIDTH),
                      n_seq, seq_len, 2 * ROW_TILE)
    x1 = _merge(xp, y_ssm, atts, o_mem, norm1_g[0], w_gate_b, w_glu_b, w_att_b, w_memo_b, w_out_b, ROW_TILE)
    y_p, conv_p = _ffn(x1, norm2_g[0], w_up_b, ffn_conv_w[0], ffn_conv_b[0], w_down_b, final_norm_g,
                       ROW_TILE, n_seq=n_seq)

    def final_state(s):
        return s.reshape(1, n_seq, SSM_GROUPS, SSM_STATE)
    k3 = k.reshape(n_seq, seq_len, ATT_WIDTH)
    v3 = v.reshape(n_seq, seq_len, ATT_WIDTH)
    gw = ATT_GROUP_WIDTH
    win_p = []
    for g, (win, _) in enumerate(DIL_PAIRS):
        keep = min(win, seq_len)
        for t in (k3, v3):
            win_p.append(t[:, seq_len - keep:, g * gw:(g + 1) * gw].reshape(1, n_seq, keep, hpg, hd))
    mem_kv = [mk.reshape(1, n_seq, N_MEM, MEM_HEADS, MEM_HEAD_DIM), mv.reshape(1, n_seq, N_MEM, MEM_HEADS, MEM_HEAD_DIM)]

    xs = x_sample.reshape(n_dec, d)
    us, qs, ks, vs, qms = _norm_proj(xs, norm1_g[0], w_proj_b, PROJ_SPLITS, dec_tile)
    ys_ssm, sn_re, sn_im = _ssm_step(us, state_ssm_re[0], state_ssm_im[0], w0_mat, v0_mat, a1, ssm_d[0])
    n_g = len(DIL_PAIRS)
    ks4, vs4 = (t.reshape(n_dec, n_g, hpg, hd) for t in (ks, vs))
    att_s = _cache_attn(qs, ks, vs, (cache_w1_k[0], cache_w2_k[0], cache_w3_k[0]),
                        (cache_w1_v[0], cache_w2_v[0], cache_w3_v[0]), SUBLANES)
    mem_s = _cache_mem_attn(qms.reshape(n_dec, MEM_HEADS, MEM_HEAD_DIM), cache_mem_k[0], cache_mem_v[0],
                            CACHE_ROWS_PER_STEP)
    xs1 = _merge(xs, [ys_ssm], [att_s], mem_s.reshape(n_dec, MEM_WIDTH), norm1_g[0], w_gate_b,
                 w_glu_b, w_att_b, w_memo_b, w_out_b, dec_tile)
    y_s, conv_s = _ffn(xs1, norm2_g[0], w_up_b, ffn_conv_w[0], ffn_conv_b[0], w_down_b, final_norm_g,
                       dec_tile, prev=state_ffn_conv[0].reshape(n_dec, (CONV_W - 1) * D_FF))

    win_s = []
    for g in range(len(DIL_PAIRS)):
        win_s += [ks4[None, :, g:g + 1], vs4[None, :, g:g + 1]]

    return (y_p.reshape(n_seq, seq_len, d), y_s.reshape(n_dec, 1, d),
            final_state(fin_re), final_state(fin_im), *win_p, *mem_kv, conv_p[None],
            sn_re.reshape(1, n_dec, SSM_GROUPS, SSM_STATE), sn_im.reshape(1, n_dec, SSM_GROUPS, SSM_STATE),
            *win_s, conv_s.reshape(1, n_dec, CONV_W - 1, D_FF))
```

```python
import functools

import jax
import jax.numpy as jnp
from jax import lax
from jax.experimental import pallas as pl
from jax.experimental.pallas import tpu as pltpu

F32 = jnp.float32
BF16 = jnp.bfloat16

D_MODEL = 1024
SSM_WIDTH = 512
SSM_GROUP = 16
SSM_GROUPS = 32
SSM_STATE = 64
ATT_HEAD_DIM = 64
HEADS_PER_GROUP = 4
DIL_PAIRS = ((128, 1), (512, 4), (2048, 16))
ATT_HEADS = len(DIL_PAIRS) * HEADS_PER_GROUP
ATT_WIDTH = ATT_HEADS * ATT_HEAD_DIM
ATT_GROUP_WIDTH = HEADS_PER_GROUP * ATT_HEAD_DIM
ATT_STEPS = 128
N_MEM = 256
MEM_HEADS = 4
MEM_HEAD_DIM = 128
MEM_WIDTH = MEM_HEADS * MEM_HEAD_DIM
N_BRANCH = 3
D_FF = 2816
CONV_W = 3
EPS = 1e-6
PROJ_SPLITS = (SSM_WIDTH, ATT_WIDTH, ATT_WIDTH, ATT_WIDTH, MEM_WIDTH)
IN_WIDTH = sum(PROJ_SPLITS) + N_BRANCH * D_MODEL

LANES = 128
SUBLANES = 8
VMEM_LIMIT_BYTES = 56 * 1024 * 1024

SSM_T = 8
SSM_LANE_BLOCKS = SSM_WIDTH // LANES
SSM_GROUPS_PER_BLOCK = LANES // SSM_GROUP
SSM_STATES_PER_BLOCK = SSM_GROUPS_PER_BLOCK * SSM_STATE
SSM_STATE_LANES = 2 * SSM_STATES_PER_BLOCK

ROW_TILE = 512
SSM_ROW_TILE = 256
FFN_CHUNK = 2816
CACHE_ROWS_PER_STEP = 8
ATT_CHUNK = 2048
NEG_BIG = -1e30


def _alibi_slopes():
    return [float(2.0 ** (-8.0 * h / ATT_HEADS)) for h in range(1, ATT_HEADS + 1)]


def _params(*sem):
    return pltpu.CompilerParams(dimension_semantics=sem, vmem_limit_bytes=VMEM_LIMIT_BYTES)


def _rmsnorm(x, g):
    ms = jnp.mean(x * x, axis=-1, keepdims=True)
    return x * lax.rsqrt(ms + EPS) * g


def _full(shape):
    nd = len(shape)
    return pl.BlockSpec(shape, lambda *_: (0,) * nd, pipeline_mode=pl.Buffered(1))


def _norm_proj_kernel(x_ref, g_ref, w_ref, *out_refs, splits):
    h = _rmsnorm(x_ref[...], g_ref[...]).astype(BF16)
    off = 0
    for o_ref, width in zip(out_refs, splits):
        for c0 in range(0, width, 512):
            cw = min(512, width - c0)
            o_ref[:, c0:c0 + cw] = jnp.dot(h, w_ref[:, off + c0:off + c0 + cw], preferred_element_type=F32)
        off += width


def _norm_proj(x, g, w_bf16, splits, tm):
    rows, d = x.shape
    assert rows % tm == 0 and sum(splits) == w_bf16.shape[1]
    return pl.pallas_call(
        functools.partial(_norm_proj_kernel, splits=splits),
        grid=(rows // tm,),
        in_specs=[pl.BlockSpec((tm, d), lambda i: (i, 0)), _full((1, d)), _full(w_bf16.shape)],
        out_specs=[pl.BlockSpec((tm, s), lambda i: (i, 0)) for s in splits],
        out_shape=[jax.ShapeDtypeStruct((rows, s), F32) for s in splits],
        compiler_params=_params("parallel"),
        name="norm_proj",
    )(x, g.reshape(1, d), w_bf16)


def _ssm_layout(a_re, a_im, log_dt, b_re, b_im, c_re, c_im):
    nb, gpb, p, c = SSM_LANE_BLOCKS, SSM_GROUPS_PER_BLOCK, SSM_STATE, SSM_GROUP
    rows = jnp.stack([a_re.reshape(nb, gpb * p), a_im.reshape(nb, gpb * p),
                      jnp.repeat(log_dt, p).reshape(nb, gpb * p)], axis=1)
    eye = jnp.eye(gpb, dtype=F32)

    def place_b(b):
        return jnp.einsum('bgpc,gh->bgchp', b.reshape(nb, gpb, p, c), eye).reshape(nb, gpb * c, gpb * p)

    def place_c(m):
        return jnp.einsum('bgcp,gh->bhpgc', m.reshape(nb, gpb, c, p), eye).reshape(nb, gpb * p, gpb * c)

    return rows, place_b(b_re), place_b(b_im), place_c(c_re), place_c(c_im)


def _ssm_prep_kernel(rows_ref, bre_ref, bim_ref, cre_ref, cim_ref,
                     m_ref, w_ref, v_ref, v0_ref, a1_ref, apow_ref, *, t_chunk):
    sp = SSM_STATES_PER_BLOCK
    a_re, a_im, dt = rows_ref[0:1, :], rows_ref[1:2, :], jnp.exp(rows_ref[2:3, :])

    def powers(k):
        mag = jnp.exp(a_re * dt * k)
        ang = a_im * dt * k
        return mag * jnp.cos(ang), mag * jnp.sin(ang)

    n_pow = 2 * SUBLANES
    assert t_chunk + 1 <= n_pow
    pw_re, pw_im = powers(lax.broadcasted_iota(jnp.int32, (n_pow, 1), 0).astype(F32))
    pwt_re, pwt_im = pw_re.T, pw_im.T
    ab_re, ab_im = pw_re[1:2, :], pw_im[1:2, :]
    den = a_re * a_re + a_im * a_im
    q_re = ((ab_re - 1.0) * a_re + ab_im * a_im) / den
    q_im = (ab_im * a_re - (ab_re - 1.0) * a_im) / den
    bre, bim = bre_ref[...], bim_ref[...]
    bb_re = q_re * bre - q_im * bim
    bb_im = q_re * bim + q_im * bre
    cre, cim = cre_ref[...], cim_ref[...]

    m_ref[...] = jnp.zeros(m_ref.shape, m_ref.dtype)
    for k in range(t_chunk):
        pk_re, pk_im = pw_re[k:k + 1, :], pw_im[k:k + 1, :]
        bk_re = bb_re * pk_re - bb_im * pk_im
        bk_im = bb_re * pk_im + bb_im * pk_re
        t = t_chunk - 1 - k
        w_ref[t * LANES:(t + 1) * LANES, 0:sp] = bk_re.astype(w_ref.dtype)
        w_ref[t * LANES:(t + 1) * LANES, sp:2 * sp] = bk_im.astype(w_ref.dtype)
        kk = (jnp.dot(bk_re, cre, precision=lax.Precision.HIGHEST, preferred_element_type=F32)
              - jnp.dot(bk_im, cim, precision=lax.Precision.HIGHEST, preferred_element_type=F32))
        kk = kk.astype(m_ref.dtype)
        for t0 in range(t_chunk - k):
            m_ref[t0 * LANES:(t0 + 1) * LANES, (t0 + k) * LANES:(t0 + k + 1) * LANES] = kk

    for t in range(t_chunk):
        pc_re, pc_im = pwt_re[:, t + 1:t + 2], pwt_im[:, t + 1:t + 2]
        v_ref[0:sp, t * LANES:(t + 1) * LANES] = (cre * pc_re - cim * pc_im).astype(v_ref.dtype)
        v_ref[sp:2 * sp, t * LANES:(t + 1) * LANES] = (-(cre * pc_im + cim * pc_re)).astype(v_ref.dtype)
    v0_ref[0:sp, :] = cre.astype(v0_ref.dtype)
    v0_ref[sp:2 * sp, :] = (-cim).astype(v0_ref.dtype)

    a1_ref[:, 0:sp] = ab_re
    a1_ref[:, sp:2 * sp] = ab_im
    steps = (lax.broadcasted_iota(jnp.int32, (SUBLANES, 1), 0) + 1) * t_chunk
    ap_re, ap_im = powers(steps.astype(F32))
    apow_ref[:, 0:sp] = ap_re
    apow_ref[:, sp:2 * sp] = ap_im


def _ssm_prep(a_re, a_im, log_dt, b_re, b_im, c_re, c_im):
    rows, pbre, pbim, pcre, pcim = _ssm_layout(a_re, a_im, log_dt, b_re, b_im, c_re, c_im)
    nb, sp, sl, tl = SSM_LANE_BLOCKS, SSM_STATES_PER_BLOCK, SSM_STATE_LANES, SSM_T * LANES

    def blk(shape):
        return pl.BlockSpec((None,) + shape, lambda b: (b, 0, 0))

    return pl.pallas_call(
        functools.partial(_ssm_prep_kernel, t_chunk=SSM_T),
        grid=(nb,),
        in_specs=[blk((3, sp)), blk((LANES, sp)), blk((LANES, sp)), blk((sp, LANES)), blk((sp, LANES))],
        out_specs=[blk((tl, tl)), blk((tl, sl)), blk((sl, tl)), blk((sl, LANES)), blk((1, sl)), blk((SUBLANES, sl))],
        out_shape=[jax.ShapeDtypeStruct((nb, tl, tl), BF16), jax.ShapeDtypeStruct((nb, tl, sl), BF16),
                   jax.ShapeDtypeStruct((nb, sl, tl), BF16), jax.ShapeDtypeStruct((nb, sl, LANES), BF16),
                   jax.ShapeDtypeStruct((nb, 1, sl), F32), jax.ShapeDtypeStruct((nb, SUBLANES, sl), F32)],
        compiler_params=_params("parallel"),
        name="ssm_prep",
    )(rows, pbre, pbim, pcre, pcim)


def _chunk_tokens(u_ref, t, t_chunk):
    return u_ref[pl.ds(t, u_ref.shape[0] // t_chunk, stride=t_chunk), :]


def _chunk_lanes(u_ref, t_chunk):
    return jnp.concatenate([_chunk_tokens(u_ref, t, t_chunk) for t in range(t_chunk)], axis=1)


def _ssm_end_state_kernel(*refs, t_chunk):
    nb, sl = SSM_LANE_BLOCKS, SSM_STATE_LANES
    u_refs, w_ref, e_ref = refs[:nb], refs[nb], refs[nb + 1]
    for b in range(nb):
        ub = _chunk_lanes(u_refs[b], t_chunk).astype(BF16)
        e_ref[:, b * sl:(b + 1) * sl] = jnp.dot(ub, w_ref[b], preferred_element_type=F32)


def _ssm_scan_kernel(ere_ref, eim_ref, pre_ref, pim_ref, spre_ref, spim_ref, fre_ref, fim_ref):
    n_tiles = ere_ref.shape[0] // SUBLANES
    width = ere_ref.shape[1]
    p_re, p_im = pre_ref[...], pim_ref[...]
    row = lax.broadcasted_iota(jnp.int32, (SUBLANES, width), 0)

    def shift_down(x, k):
        return jnp.where(row >= k, pltpu.roll(x, k, 0), 0.0)

    def body(i, carry):
        c_re, c_im = carry
        rows = pl.ds(pl.multiple_of(i * SUBLANES, SUBLANES), SUBLANES)
        x_re, x_im = ere_ref[rows, :], eim_ref[rows, :]
        for k in (1, 2, 4):
            a_re, a_im = p_re[k - 1:k, :], p_im[k - 1:k, :]
            s_re, s_im = shift_down(x_re, k), shift_down(x_im, k)
            x_re, x_im = x_re + a_re * s_re - a_im * s_im, x_im + a_re * s_im + a_im * s_re
        t_re = x_re + p_re * c_re - p_im * c_im
        t_im = x_im + p_re * c_im + p_im * c_re
        spre_ref[rows, :] = jnp.where(row >= 1, pltpu.roll(t_re, 1, 0), c_re)
        spim_ref[rows, :] = jnp.where(row >= 1, pltpu.roll(t_im, 1, 0), c_im)
        return t_re[SUBLANES - 1:SUBLANES, :], t_im[SUBLANES - 1:SUBLANES, :]

    zero = jnp.zeros((1, width), F32)
    f_re, f_im = lax.fori_loop(0, n_tiles, body, (zero, zero))
    fre_ref[...] = f_re
    fim_ref[...] = f_im


def _ssm_output_kernel(*refs, t_chunk):
    nb, sp = SSM_LANE_BLOCKS, SSM_STATES_PER_BLOCK
    u_refs, (spre_ref, spim_ref, m_ref, v_ref, d_ref), y_refs = refs[:nb], refs[nb:nb + 5], refs[nb + 5:]
    for b in range(nb):
        ub = _chunk_lanes(u_refs[b], t_chunk).astype(BF16)
        yb = jnp.dot(ub, m_ref[b], preferred_element_type=F32)
        states = slice(b * sp, (b + 1) * sp)
        sprev = jnp.concatenate([spre_ref[:, states], spim_ref[:, states]], axis=1).astype(BF16)
        yb = yb + jnp.dot(sprev, v_ref[b], preferred_element_type=F32)
        d = d_ref[:, b * LANES:(b + 1) * LANES]
        for t in range(t_chunk):
            y_t = yb[:, t * LANES:(t + 1) * LANES] + d * _chunk_tokens(u_refs[b], t, t_chunk)
            y_refs[b][pl.ds(t, yb.shape[0], stride=t_chunk), :] = y_t


def _ssm_prompt(u, n_seq, m_mat, w_mat, v_mat, apow, d_skip):
    rows, t_chunk, nb, sl, sp = u.shape[0], SSM_T, SSM_LANE_BLOCKS, SSM_STATE_LANES, SSM_STATES_PER_BLOCK
    assert rows % (t_chunk * n_seq) == 0
    n_chunks = rows // t_chunk // n_seq
    tr = min(SSM_ROW_TILE, n_chunks)
    assert n_chunks % tr == 0 and n_chunks % SUBLANES == 0
    tiles = n_chunks // tr
    u_specs = [pl.BlockSpec((tr * t_chunk, LANES), lambda i, b=b: (i, b)) for b in range(nb)]
    e = pl.pallas_call(
        functools.partial(_ssm_end_state_kernel, t_chunk=t_chunk),
        grid=(n_seq * tiles,),
        in_specs=u_specs + [_full(w_mat.shape)],
        out_specs=pl.BlockSpec((tr, nb * sl), lambda i: (i % tiles, i // tiles)),
        out_shape=jax.ShapeDtypeStruct((n_chunks, n_seq * nb * sl), F32),
        compiler_params=_params("parallel"),
        name="ssm_end_state",
    )(*([u] * nb), w_mat)
    col = pl.BlockSpec((n_chunks, sp), lambda g: (0, g))
    fin = pl.BlockSpec((1, sp), lambda g: (0, g))
    sp_re, sp_im, fin_re, fin_im = pl.pallas_call(
        _ssm_scan_kernel,
        grid=(n_seq * nb,),
        in_specs=[pl.BlockSpec((n_chunks, sp), lambda g: (0, 2 * g)),
                  pl.BlockSpec((n_chunks, sp), lambda g: (0, 2 * g + 1)),
                  pl.BlockSpec((None, SUBLANES, sp), lambda g: (g % nb, 0, 0)),
                  pl.BlockSpec((None, SUBLANES, sp), lambda g: (g % nb, 0, 1))],
        out_specs=[col, col, fin, fin],
        out_shape=[jax.ShapeDtypeStruct((n_chunks, n_seq * nb * sp), F32)] * 2
        + [jax.ShapeDtypeStruct((1, n_seq * nb * sp), F32)] * 2,
        compiler_params=_params("parallel"),
        name="ssm_scan",
    )(e, e, apow, apow)
    sp_spec = pl.BlockSpec((tr, nb * sp), lambda i: (i % tiles, i // tiles))
    ys = pl.pallas_call(
        functools.partial(_ssm_output_kernel, t_chunk=t_chunk),
        grid=(n_seq * tiles,),
        in_specs=u_specs + [sp_spec, sp_spec, _full(m_mat.shape), _full(v_mat.shape), _full((1, SSM_WIDTH))],
        out_specs=[pl.BlockSpec((tr * t_chunk, LANES), lambda i: (i, 0)) for _ in range(nb)],
        out_shape=[jax.ShapeDtypeStruct((rows, LANES), F32) for _ in range(nb)],
        compiler_params=_params("parallel"),
        name="ssm_output",
    )(*([u] * nb), sp_re, sp_im, m_mat, v_mat, d_skip.reshape(1, SSM_WIDTH))
    return ys, fin_re, fin_im


def _ssm_step_kernel(u_ref, sre_ref, sim_ref, w0_ref, v0_ref, a1_ref, d_ref, y_ref, nre_ref, nim_ref):
    sp = SSM_STATES_PER_BLOCK
    for b in range(SSM_LANE_BLOCKS):
        lanes = slice(b * LANES, (b + 1) * LANES)
        states = slice(b * sp, (b + 1) * sp)
        u = u_ref[:, lanes]
        e = jnp.dot(u.astype(BF16), w0_ref[b], preferred_element_type=F32)
        a_re, a_im = a1_ref[b, :, 0:sp], a1_ref[b, :, sp:2 * sp]
        s_re, s_im = sre_ref[:, states], sim_ref[:, states]
        n_re = a_re * s_re - a_im * s_im + e[:, 0:sp]
        n_im = a_re * s_im + a_im * s_re + e[:, sp:2 * sp]
        nre_ref[:, states] = n_re
        nim_ref[:, states] = n_im
        sn = jnp.concatenate([n_re, n_im], axis=1).astype(BF16)
        y_ref[:, lanes] = jnp.dot(sn, v0_ref[b], preferred_element_type=F32) + d_ref[:, lanes] * u


def _ssm_step(u, s_re, s_im, w0, v0, a1, d_skip):
    rows = u.shape[0]
    ns = SSM_GROUPS * SSM_STATE
    args = (u, s_re.reshape(rows, ns), s_im.reshape(rows, ns), w0, v0, a1, d_skip.reshape(1, SSM_WIDTH))
    return pl.pallas_call(
        _ssm_step_kernel,
        grid=(1,),
        in_specs=[_full(a.shape) for a in args],
        out_specs=[_full((rows, SSM_WIDTH)), _full((rows, ns)), _full((rows, ns))],
        out_shape=[jax.ShapeDtypeStruct((rows, SSM_WIDTH), F32), jax.ShapeDtypeStruct((rows, ns), F32),
                   jax.ShapeDtypeStruct((rows, ns), F32)],
        compiler_params=_params("arbitrary"),
        name="ssm_step",
    )(*args)


def _dil_attn_kernel(q_ref, kc_ref, kp_ref, vc_ref, vp_ref, o_ref, lse_ref, *, dil, slopes):
    steps, hd = ATT_STEPS, ATT_HEAD_DIM
    chunk = q_ref.shape[0]
    span = steps * dil
    first_chunk = pl.program_id(1) == 0
    pair = pl.program_id(2)
    qi = lax.broadcasted_iota(jnp.int32, (steps, 2 * steps), 0)
    kj = lax.broadcasted_iota(jnp.int32, (steps, 2 * steps), 1)
    dist = qi + steps - kj
    band = (dist >= 0) & (dist <= steps)
    distf = (dist * dil).astype(F32)
    lane = lax.broadcasted_iota(jnp.int32, (steps, LANES), 1)
    heads = [lane < hd, lane >= hd]
    biases, first_biases = [], []
    for hh in range(2):
        slope = jnp.where(pair == 0, slopes[hh], slopes[2 + hh])
        bias = jnp.where(band, -slope * distf, NEG_BIG)
        biases.append(bias)
        first_biases.append(jnp.where(first_chunk & (kj < steps), NEG_BIG, bias))

    def rows(ref, start):
        return ref[pl.ds(start, steps, stride=dil), :] if dil > 1 else ref[pl.ds(start, steps), :]

    for r in range(dil):
        for qb in range(chunk // span):
            start = r + qb * span
            q = rows(q_ref, start) * (hd ** -0.5)
            if qb == 0:
                k_prev, v_prev = rows(kp_ref, r), rows(vp_ref, r)
            else:
                k_prev, v_prev = rows(kc_ref, start - span), rows(vc_ref, start - span)
            kk = jnp.concatenate([k_prev, rows(kc_ref, start)], axis=0).astype(BF16)
            vv = jnp.concatenate([v_prev, rows(vc_ref, start)], axis=0).astype(BF16)
            out = jnp.zeros((steps, LANES), F32)
            lse = jnp.zeros((steps, LANES), F32)
            for hh in range(2):
                qh = jnp.where(heads[hh], q, 0.0).astype(BF16)
                s = lax.dot_general(qh, kk, (((1,), (1,)), ((), ())), preferred_element_type=F32)
                s = s + (first_biases[hh] if qb == 0 else biases[hh])
                m = jnp.max(s, axis=-1, keepdims=True)
                p = jnp.exp(s - m)
                den = jnp.sum(p, axis=-1, keepdims=True)
                oh = jnp.dot(p.astype(BF16), vv, preferred_element_type=F32) / den
                out = jnp.where(heads[hh], oh, out)
                lse = jnp.where(heads[hh], m + jnp.log(den), lse)
            if dil > 1:
                o_ref[pl.ds(start, steps, stride=dil), :] = out
                lse_ref[pl.ds(start, steps, stride=dil), :] = lse
            else:
                o_ref[pl.ds(start, steps), :] = out
                lse_ref[pl.ds(start, steps), :] = lse


def _dil_attn(q, k, v, n_seq, seq_len, group):
    win, dil = DIL_PAIRS[group]
    steps, gw = ATT_STEPS, ATT_GROUP_WIDTH
    span = steps * dil
    chunk = ATT_CHUNK
    assert win // dil == steps and chunk % span == 0 and seq_len % chunk == 0
    chunks = seq_len // chunk
    lane_blocks = gw // LANES
    cur = pl.BlockSpec((chunk, LANES), lambda n, c, p: (n * chunks + c, lane_blocks * group + p))
    prev = pl.BlockSpec((span, LANES), lambda n, c, p: (jnp.maximum((n * chunks + c) * (chunk // span) - 1, 0),
                                                        lane_blocks * group + p))
    out = pl.BlockSpec((chunk, LANES), lambda n, c, p: (n * chunks + c, p))
    slopes = tuple(_alibi_slopes()[group * HEADS_PER_GROUP:(group + 1) * HEADS_PER_GROUP])
    rows = n_seq * seq_len
    return pl.pallas_call(
        functools.partial(_dil_attn_kernel, dil=dil, slopes=slopes),
        grid=(n_seq, chunks, lane_blocks),
        in_specs=[cur, cur, prev, cur, prev],
        out_specs=[out, out],
        out_shape=[jax.ShapeDtypeStruct((rows, gw), F32), jax.ShapeDtypeStruct((rows, gw), F32)],
        compiler_params=_params("parallel", "parallel", "parallel"),
        name=f"dil_attn_w{win}",
    )(q, k, k, v, v)


def _mem_attn_kernel(q_ref, mk_ref, mv_ref, o_ref):
    hd = MEM_HEAD_DIM
    for h in range(MEM_HEADS):
        lanes = slice(h * hd, (h + 1) * hd)
        s = lax.dot_general(q_ref[:, lanes].astype(BF16), mk_ref[:, lanes].astype(BF16),
                            (((1,), (1,)), ((), ())), preferred_element_type=F32) * (hd ** -0.5)
        m = jnp.max(s, axis=-1, keepdims=True)
        p = jnp.exp(s - m)
        den = jnp.sum(p, axis=-1, keepdims=True)
        o_ref[:, lanes] = jnp.dot(p.astype(BF16), mv_ref[:, lanes].astype(BF16), preferred_element_type=F32) / den


def _mem_attn(qm, mk, mv, n_seq, seq_len, tm):
    assert seq_len % tm == 0
    tiles = seq_len // tm
    kv = pl.BlockSpec((None, N_MEM, MEM_WIDTH), lambda n, i: (n, 0, 0))
    return pl.pallas_call(
        _mem_attn_kernel,
        grid=(n_seq, tiles),
        in_specs=[pl.BlockSpec((tm, MEM_WIDTH), lambda n, i: (n * tiles + i, 0)), kv, kv],
        out_specs=pl.BlockSpec((tm, MEM_WIDTH), lambda n, i: (n * tiles + i, 0)),
        out_shape=jax.ShapeDtypeStruct(qm.shape, F32),
        compiler_params=_params("parallel", "parallel"),
        name="mem_attn",
    )(qm, mk, mv)


def _cache_attn_kernel(q_ref, kn_ref, vn_ref, k1_ref, v1_ref, k2_ref, v2_ref, k3_ref, v3_ref, o_ref,
                       qt_scr, vnt_scr, *, slopes):
    hpg, hd = HEADS_PER_GROUP, ATT_HEAD_DIM
    j = pl.program_id(1)
    nb = q_ref.shape[0]
    scale = hd ** -0.5
    q, kn = q_ref[...], kn_ref[...]
    qt_scr[...] = q.T
    vnt_scr[...] = vn_ref[...].T
    lane_head = lax.broadcasted_iota(jnp.int32, q.shape, 1) // hd
    outs, lses = [], []
    for g, (k_ref, v_ref) in enumerate(((k1_ref, v1_ref), (k2_ref, v2_ref), (k3_ref, v3_ref))):
        dil = DIL_PAIRS[g][1]
        n_pos = k_ref.shape[-1]
        rows = pl.ds(pl.multiple_of((g * hpg + j) * hd, hd), hd)
        qg, vng = qt_scr[rows, :], vnt_scr[rows, :]
        slope = sum(jnp.where(j == h, slopes[g * hpg + h], 0.0) for h in range(hpg))
        back = n_pos - lax.broadcasted_iota(jnp.int32, (1, n_pos), 1)
        bias = jnp.where(back % dil == 0, -slope * back.astype(F32), NEG_BIG)
        row = lax.broadcasted_iota(jnp.int32, (nb, n_pos), 0)
        s = jnp.zeros((nb, n_pos), F32)
        for b in range(nb):
            s = jnp.where(row == b, jnp.sum(k_ref[b] * qg[:, b:b + 1], axis=0, keepdims=True), s)
        s = s * scale + bias
        s_new = jnp.sum(jnp.where(lane_head == g * hpg + j, q * kn, 0.0), axis=1, keepdims=True) * scale
        m = jnp.maximum(jnp.max(s, axis=1, keepdims=True), s_new)
        p = jnp.exp(s - m)
        p_new = jnp.exp(s_new - m)
        den = jnp.sum(p, axis=1, keepdims=True) + p_new
        cols = []
        for b in range(nb):
            acc = jnp.sum(v_ref[b] * p[b:b + 1, :], axis=1, keepdims=True) + p_new[b:b + 1, :] * vng[:, b:b + 1]
            cols.append(acc / den[b:b + 1, :])
        outs.append(cols)
        lses.append(m + jnp.log(den))
    top = functools.reduce(jnp.maximum, lses)
    ws = [jnp.exp(l - top) for l in lses]
    total = sum(ws)
    o_ref[...] = jnp.concatenate(
        [sum(w[b:b + 1, :] * cols[b] for w, cols in zip(ws, outs)) / total[b:b + 1, :] for b in range(nb)], axis=1)


def _cache_attn(q, k_new, v_new, caches_k, caches_v, nb):
    rows = q.shape[0]
    hpg, hd = HEADS_PER_GROUP, ATT_HEAD_DIM
    assert rows % nb == 0 and nb % SUBLANES == 0
    new = pl.BlockSpec((nb, ATT_WIDTH), lambda i, j: (i, 0))
    cache_specs, cache_args = [], []
    for g, (win, dil) in enumerate(DIL_PAIRS):
        for c in (caches_k[g], caches_v[g]):
            assert c.shape == (rows, win, hpg, hd) and win % dil == 0, c.shape
            cache_args.append(c.transpose(0, 2, 3, 1))
            cache_specs.append(pl.BlockSpec((nb, None, hd, win), lambda i, j: (i, j, 0, 0)))
    out = pl.pallas_call(
        functools.partial(_cache_attn_kernel, slopes=tuple(_alibi_slopes())),
        grid=(rows // nb, hpg),
        in_specs=[new, new, new] + cache_specs,
        out_specs=pl.BlockSpec((None, None, hd, nb), lambda i, j: (j, i, 0, 0)),
        out_shape=jax.ShapeDtypeStruct((hpg, rows // nb, hd, nb), F32),
        scratch_shapes=[pltpu.VMEM((ATT_WIDTH, nb), F32) for _ in range(2)],
        compiler_params=_params("parallel", "arbitrary"),
        name="cache_attn",
    )(q, k_new, v_new, *cache_args)
    return out.transpose(1, 3, 0, 2).reshape(rows, hpg * hd)


def _cache_mem_attn_kernel(q_ref, k_ref, v_ref, o_ref):
    nb, rows, hd = k_ref.shape
    tiles = rows // SUBLANES

    def fold(x):
        return x, pltpu.roll(x, MEM_HEADS, 2)

    q = q_ref[...][:, None]
    k = k_ref[...].reshape(nb, tiles, SUBLANES, hd)
    v = v_ref[...].reshape(nb, tiles, SUBLANES, hd)
    s = jnp.sum(k * q, axis=-1, keepdims=True) * (MEM_HEAD_DIM ** -0.5)
    m = jnp.maximum(*fold(jnp.max(s, axis=1, keepdims=True)))
    p = jnp.exp(s - m)
    den = sum(fold(jnp.sum(p, axis=1, keepdims=True)))
    acc = sum(fold(jnp.sum(p * v, axis=1, keepdims=True)))
    o_ref[...] = (acc / den)[:, 0]


def _cache_mem_attn(qm, mem_k, mem_v, nb):
    rows = qm.shape[0]
    assert rows % nb == 0 and SUBLANES == 2 * MEM_HEADS
    q4 = qm.reshape(rows, MEM_HEADS, MEM_HEAD_DIM)
    kv = pl.BlockSpec((nb, N_MEM * MEM_HEADS, MEM_HEAD_DIM), lambda i: (i, 0, 0))
    q = pl.BlockSpec((nb, SUBLANES, MEM_HEAD_DIM), lambda i: (i, 0, 0))
    out = pl.pallas_call(
        _cache_mem_attn_kernel,
        grid=(rows // nb,),
        in_specs=[q, kv, kv],
        out_specs=q,
        out_shape=jax.ShapeDtypeStruct((rows, SUBLANES, MEM_HEAD_DIM), F32),
        compiler_params=_params("parallel"),
        name="cache_mem_attn",
    )(jnp.concatenate([q4, q4], axis=1), mem_k.reshape(rows, N_MEM * MEM_HEADS, MEM_HEAD_DIM),
      mem_v.reshape(rows, N_MEM * MEM_HEADS, MEM_HEAD_DIM))
    return out[:, :MEM_HEADS].reshape(rows, MEM_WIDTH)


def _merge_kernel(x_ref, *refs, n_y, n_att):
    y_refs, att_refs = refs[:n_y], refs[n_y:n_y + n_att]
    mem_ref, g1_ref, wgate_ref, wglu_ref, watt_ref, wmem_ref, wout_ref, o_ref = refs[n_y + n_att:]
    d = D_MODEL
    x = x_ref[...]
    h = _rmsnorm(x, g1_ref[...]).astype(BF16)

    def gate(i):
        return jax.nn.sigmoid(jnp.dot(h, wgate_ref[:, i * d:(i + 1) * d], preferred_element_type=F32))

    y = jnp.concatenate([r[...] for r in y_refs], axis=1) if n_y > 1 else y_refs[0][...]
    z = jax.nn.gelu(y).astype(BF16)
    glu = jnp.dot(z, wglu_ref[...], preferred_element_type=F32)
    merged = gate(0) * (glu[:, 0:d] * jax.nn.sigmoid(glu[:, d:2 * d]))
    if n_att == 1:
        att = att_refs[0][...]
    else:
        lses = [r[...] for r in att_refs[1::2]]
        top = functools.reduce(jnp.maximum, lses)
        ws = [jnp.exp(l - top) for l in lses]
        att = sum(w * r[...] for w, r in zip(ws, att_refs[0::2])) / sum(ws)
    b_att = jnp.dot(att.astype(BF16), watt_ref[...], preferred_element_type=F32)
    merged = merged + gate(1) * b_att
    b_mem = jnp.dot(mem_ref[...].astype(BF16), wmem_ref[...], preferred_element_type=F32)
    merged = merged + gate(2) * b_mem
    o_ref[...] = x + jnp.dot(merged.astype(BF16), wout_ref[...], preferred_element_type=F32)


def _merge(x, ys, atts, o_mem, g1, w_gate, w_glu, w_att_o, w_mem_o, w_out, tm):
    rows = x.shape[0]
    assert rows % tm == 0 and w_gate.shape == (D_MODEL, N_BRANCH * D_MODEL)

    def tile(a):
        return pl.BlockSpec((tm, a.shape[1]), lambda i: (i, 0))

    acts = (x, *ys, *atts, o_mem)
    weights = (g1.reshape(1, D_MODEL), w_gate, w_glu, w_att_o, w_mem_o, w_out)
    return pl.pallas_call(
        functools.partial(_merge_kernel, n_y=len(ys), n_att=len(atts)),
        grid=(rows // tm,),
        in_specs=[tile(a) for a in acts] + [_full(w.shape) for w in weights],
        out_specs=tile(x),
        out_shape=jax.ShapeDtypeStruct(x.shape, F32),
        compiler_params=_params("parallel"),
        name="merge",
    )(*acts, *weights)


def _ffn_kernel(*refs, tm, tiles_per_seq, stepwise):
    if stepwise:
        x_ref, g2_ref, wup_ref, cw_ref, cb_ref, wdn_ref, gf_ref, prev_ref, y_ref, conv_ref = refs
    else:
        x_ref, g2_ref, wup_ref, cw_ref, cb_ref, wdn_ref, gf_ref, y_ref, conv_ref, a_scr = refs
        i = pl.program_id(0)
        first = i % tiles_per_seq == 0

        @pl.when(first)
        def _():
            a_scr[0:SUBLANES, :] = jnp.zeros((SUBLANES, D_FF), F32)

        @pl.when(jnp.logical_not(first))
        def _():
            a_scr[0:SUBLANES, :] = a_scr[tm:tm + SUBLANES, :]

    x = x_ref[...]
    h = _rmsnorm(x, g2_ref[...]).astype(BF16)
    acc = jnp.zeros((tm, D_MODEL), F32)
    for c0 in range(0, D_FF, FFN_CHUNK):
        cols = slice(c0, c0 + FFN_CHUNK)
        a = jnp.dot(h, wup_ref[:, cols], preferred_element_type=F32)
        if stepwise:
            a2, a1 = prev_ref[:, cols], prev_ref[:, D_FF + c0:D_FF + c0 + FFN_CHUNK]
            conv_ref[:, cols] = a1
            conv_ref[:, D_FF + c0:D_FF + c0 + FFN_CHUNK] = a
        else:
            a_scr[SUBLANES:SUBLANES + tm, cols] = a
            a1 = a_scr[SUBLANES - 1:SUBLANES - 1 + tm, cols]
            a2 = a_scr[SUBLANES - 2:SUBLANES - 2 + tm, cols]
        c = a2 * cw_ref[0:1, cols] + a1 * cw_ref[1:2, cols] + a * cw_ref[2:3, cols] + cb_ref[:, cols]
        v = jnp.dot(h, wup_ref[:, D_FF + c0:D_FF + c0 + FFN_CHUNK], preferred_element_type=F32)
        acc = acc + jnp.dot((jax.nn.gelu(c) * v).astype(BF16), wdn_ref[cols, :], preferred_element_type=F32)
    y_ref[...] = _rmsnorm(x + acc, gf_ref[...])
    if not stepwise:
        @pl.when(i % tiles_per_seq == tiles_per_seq - 1)
        def _():
            conv_ref[...] = a_scr[SUBLANES + tm - (CONV_W - 1):SUBLANES + tm, :]


def _ffn(x, g2, w_up, conv_w, conv_b, w_down, gf, tm, n_seq=None, prev=None):
    rows, d = x.shape
    assert rows % tm == 0 and D_FF % FFN_CHUNK == 0 and FFN_CHUNK % LANES == 0
    stepwise = prev is not None
    weights = (g2.reshape(1, d), w_up, conv_w, conv_b.reshape(1, D_FF), w_down, gf.reshape(1, d))
    in_specs = [pl.BlockSpec((tm, d), lambda i: (i, 0))] + [_full(w.shape) for w in weights]
    args = (x,) + weights
    if stepwise:
        tiles_per_seq = 1
        in_specs.append(pl.BlockSpec((tm, 2 * D_FF), lambda i: (i, 0)))
        args += (prev,)
        conv_spec = pl.BlockSpec((tm, 2 * D_FF), lambda i: (i, 0))
        conv_shape = jax.ShapeDtypeStruct((rows, 2 * D_FF), F32)
        scratch = []
    else:
        tiles_per_seq = rows // n_seq // tm
        assert tiles_per_seq * tm * n_seq == rows
        conv_spec = pl.BlockSpec((None, CONV_W - 1, D_FF), lambda i: (i // tiles_per_seq, 0, 0))
        conv_shape = jax.ShapeDtypeStruct((n_seq, CONV_W - 1, D_FF), F32)
        scratch = [pltpu.VMEM((tm + 2 * SUBLANES, D_FF), F32)]
    return pl.pallas_call(
        functools.partial(_ffn_kernel, tm=tm, tiles_per_seq=tiles_per_seq, stepwise=stepwise),
        grid=(rows // tm,),
        in_specs=in_specs,
        out_specs=[pl.BlockSpec((tm, d), lambda i: (i, 0)), conv_spec],
        out_shape=[jax.ShapeDtypeStruct((rows, d), F32), conv_shape],
        scratch_shapes=scratch,
        compiler_params=_params("arbitrary"),
        name="ffn_step" if stepwise else "ffn",
    )(*args)


def kernel(x_prompt, x_sample, state_ssm_re, state_ssm_im, cache_w1_k, cache_w1_v, cache_w2_k, cache_w2_v, cache_w3_k, cache_w3_v, cache_mem_k, cache_mem_v, state_ffn_conv, mem_prompt, norm1_g, w_in, ssm_a_re, ssm_a_im, ssm_log_dt, ssm_b_re, ssm_b_im, ssm_c_re, ssm_c_im, ssm_d, w_ssm_glu, w_att_o, mem_norm_g, w_mem_kv, w_mem_o, w_out, norm2_g, w_up, ffn_conv_w, ffn_conv_b, w_down, final_norm_g):
    n_seq, seq_len, d = x_prompt.shape
    n_dec, dec_len, _ = x_sample.shape
    depth = norm1_g.shape[0]
    assert d == D_MODEL and depth == 1 and dec_len == 1
    assert w_in.shape == (depth, D_MODEL, IN_WIDTH) and w_up.shape == (depth, D_MODEL, 2 * D_FF)
    assert mem_prompt.shape == (n_seq, N_MEM, D_MODEL)
    assert ssm_a_re.shape == (depth, SSM_GROUPS, SSM_STATE)
    assert seq_len % ROW_TILE == 0 and n_dec % SUBLANES == 0
    hpg, hd = HEADS_PER_GROUP, ATT_HEAD_DIM
    rows_p = n_seq * seq_len
    dec_tile = n_dec if n_dec <= ROW_TILE else ROW_TILE

    wb = lambda w: w[0].astype(BF16)
    w_glu_b, w_att_b, w_memkv_b = wb(w_ssm_glu), wb(w_att_o), wb(w_mem_kv)
    w_memo_b, w_out_b, w_up_b, w_down_b = wb(w_mem_o), wb(w_out), wb(w_up), wb(w_down)
    n_proj = sum(PROJ_SPLITS)
    w_proj_b, w_gate_b = w_in[0, :, :n_proj].astype(BF16), w_in[0, :, n_proj:].astype(BF16)

    m_mat, w_mat, v_mat, v0_mat, a1, apow = _ssm_prep(ssm_a_re[0], ssm_a_im[0], ssm_log_dt[0], ssm_b_re[0],
                                                  ssm_b_im[0], ssm_c_re[0], ssm_c_im[0])
    w0_mat = w_mat[:, (SSM_T - 1) * LANES:, :]

    xp = x_prompt.reshape(rows_p, d)
    u, q, k, v, qm = _norm_proj(xp, norm1_g[0], w_proj_b, PROJ_SPLITS, ROW_TILE)
    y_ssm, fin_re, fin_im = _ssm_prompt(u, n_seq, m_mat, w_mat, v_mat, apow, ssm_d[0])
    atts = [a for g in range(len(DIL_PAIRS)) for a in _dil_attn(q, k, v, n_seq, seq_len, g)]
    mk, mv = _norm_proj(mem_prompt.reshape(n_seq * N_MEM, d), mem_norm_g[0], w_memkv_b, (MEM_WIDTH, MEM_WIDTH),
                        ROW_TILE)
    o_mem = _mem_attn(qm, mk.reshape(n_seq, N_MEM, MEM_WIDTH), mv.reshape(n_seq, N_MEM, MEM_WIDTH),
                      n_seq, seq_len, 2 * ROW_TILE)
    x1 = _merge(xp, y_ssm, atts, o_mem, norm1_g[0], w_gate_b, w_glu_b, w_att_b, w_memo_b, w_out_b, ROW_TILE)
    y_p, conv_p = _ffn(x1, norm2_g[0], w_up_b, ffn_conv_w[0], ffn_conv_b[0], w_down_b, final_norm_g,
                       ROW_TILE, n_seq=n_seq)

    def final_state(s):
        return s.reshape(1, n_seq, SSM_GROUPS, SSM_STATE)
    k3 = k.reshape(n_seq, seq_len, ATT_WIDTH)
    v3 = v.reshape(n_seq, seq_len, ATT_WIDTH)
    gw = ATT_GROUP_WIDTH
    win_p = []
    for g, (win, _) in enumerate(DIL_PAIRS):
        keep = min(win, seq_len)
        for t in (k3, v3):
            win_p.append(t[:, seq_len - keep:, g * gw:(g + 1) * gw].reshape(1, n_seq, keep, hpg, hd))
    mem_kv = [mk.reshape(1, n_seq, N_MEM, MEM_HEADS, MEM_HEAD_DIM), mv.reshape(1, n_seq, N_MEM, MEM_HEADS, MEM_HEAD_DIM)]

    xs = x_sample.reshape(n_dec, d)
    us, qs, ks, vs, qms = _norm_proj(xs, norm1_g[0], w_proj_b, PROJ_SPLITS, dec_tile)
    ys_ssm, sn_re, sn_im = _ssm_step(us, state_ssm_re[0], state_ssm_im[0], w0_mat, v0_mat, a1, ssm_d[0])
    n_g = len(DIL_PAIRS)
    ks4, vs4 = (t.reshape(n_dec, n_g, hpg, hd) for t in (ks, vs))
    att_s = _cache_attn(qs, ks, vs, (cache_w1_k[0], cache_w2_k[0], cache_w3_k[0]),
                        (cache_w1_v[0], cache_w2_v[0], cache_w3_v[0]), SUBLANES)
    mem_s = _cache_mem_attn(qms, cache_mem_k[0], cache_mem_v[0], CACHE_ROWS_PER_STEP)
    xs1 = _merge(xs, [ys_ssm], [att_s], mem_s, norm1_g[0], w_gate_b,
                 w_glu_b, w_att_b, w_memo_b, w_out_b, dec_tile)
    y_s, conv_s = _ffn(xs1, norm2_g[0], w_up_b, ffn_conv_w[0], ffn_conv_b[0], w_down_b, final_norm_g,
                       dec_tile, prev=state_ffn_conv[0].reshape(n_dec, (CONV_W - 1) * D_FF))

    win_s = []
    for g in range(len(DIL_PAIRS)):
        win_s += [ks4[None, :, g:g + 1], vs4[None, :, g:g + 1]]

    return (y_p.reshape(n_seq, seq_len, d), y_s.reshape(n_dec, 1, d),
            final_state(fin_re), final_state(fin_im), *win_p, *mem_kv, conv_p[None],
            sn_re.reshape(1, n_dec, SSM_GROUPS, SSM_STATE), sn_im.reshape(1, n_dec, SSM_GROUPS, SSM_STATE),
            *win_s, conv_s.reshape(1, n_dec, CONV_W - 1, D_FF))
```

```python
import functools

import jax
import jax.numpy as jnp
from jax import lax
from jax.experimental import pallas as pl
from jax.experimental.pallas import tpu as pltpu

F32 = jnp.float32
BF16 = jnp.bfloat16

D_MODEL = 1024
SSM_WIDTH = 512
SSM_GROUP = 16
SSM_GROUPS = 32
SSM_STATE = 64
ATT_HEAD_DIM = 64
HEADS_PER_GROUP = 4
DIL_PAIRS = ((128, 1), (512, 4), (2048, 16))
ATT_HEADS = len(DIL_PAIRS) * HEADS_PER_GROUP
ATT_WIDTH = ATT_HEADS * ATT_HEAD_DIM
ATT_GROUP_WIDTH = HEADS_PER_GROUP * ATT_HEAD_DIM
ATT_STEPS = 128
N_MEM = 256
MEM_HEADS = 4
MEM_HEAD_DIM = 128
MEM_WIDTH = MEM_HEADS * MEM_HEAD_DIM
N_BRANCH = 3
D_FF = 2816
CONV_W = 3
EPS = 1e-6
PROJ_SPLITS = (SSM_WIDTH, ATT_WIDTH, ATT_WIDTH, ATT_WIDTH, MEM_WIDTH)
IN_WIDTH = sum(PROJ_SPLITS) + N_BRANCH * D_MODEL

LANES = 128
SUBLANES = 8
VMEM_LIMIT_BYTES = 56 * 1024 * 1024

SSM_T = 8
SSM_LANE_BLOCKS = SSM_WIDTH // LANES
SSM_GROUPS_PER_BLOCK = LANES // SSM_GROUP
SSM_STATES_PER_BLOCK = SSM_GROUPS_PER_BLOCK * SSM_STATE
SSM_STATE_LANES = 2 * SSM_STATES_PER_BLOCK

ROW_TILE = 512
SSM_ROW_TILE = 256
FFN_CHUNK = 2816
CACHE_ROWS_PER_STEP = 8
ATT_CHUNK = 2048
NEG_BIG = -1e30


def _alibi_slopes():
    return [float(2.0 ** (-8.0 * h / ATT_HEADS)) for h in range(1, ATT_HEADS + 1)]


def _params(*sem):
    return pltpu.CompilerParams(dimension_semantics=sem, vmem_limit_bytes=VMEM_LIMIT_BYTES)


def _rmsnorm(x, g):
    ms = jnp.mean(x * x, axis=-1, keepdims=True)
    return x * lax.rsqrt(ms + EPS) * g


def _full(shape):
    nd = len(shape)
    return pl.BlockSpec(shape, lambda *_: (0,) * nd, pipeline_mode=pl.Buffered(1))


def _norm_proj_kernel(x_ref, g_ref, w_ref, *out_refs, splits, windows, tiles_per_seq):
    tm = x_ref.shape[0]
    h = _rmsnorm(x_ref[...], g_ref[...]).astype(BF16)
    off = 0
    for o_ref, width in zip(out_refs, splits):
        for c0 in range(0, width, 512):
            cw = min(512, width - c0)
            o_ref[:, c0:c0 + cw] = jnp.dot(h, w_ref[:, off + c0:off + c0 + cw], preferred_element_type=F32)
        off += width
    tile = pl.program_id(0) % tiles_per_seq
    for win_ref, (src, col0, cols, win) in zip(out_refs[len(splits):], windows):
        rows = min(win, tm)

        @pl.when(tile >= tiles_per_seq - max(win // tm, 1))
        def _(win_ref=win_ref, src=src, col0=col0, cols=cols, rows=rows):
            win_ref[...] = out_refs[src][tm - rows:tm, col0:col0 + cols].T


def _resident_cols(w, start, width):
    return pl.BlockSpec((pl.Element(w.shape[0]), pl.Element(width)), lambda *_: (0, start),
                        pipeline_mode=pl.Buffered(1))


def _norm_proj(x, g, w_bf16, splits, tm, n_seq=1, windows=()):
    rows, d = x.shape
    assert rows % (tm * n_seq) == 0 and sum(splits) <= w_bf16.shape[1]
    tps = rows // n_seq // tm
    win_specs, win_shapes = [], []
    for _, _, cols, win in windows:
        assert (win % tm == 0 or tm % win == 0) and win <= tps * tm
        n_tiles = max(win // tm, 1)
        win_specs.append(pl.BlockSpec((None, cols, min(win, tm)), lambda i, n_tiles=n_tiles: (
            i // tps, 0, jnp.maximum(i % tps - (tps - n_tiles), 0))))
        win_shapes.append(jax.ShapeDtypeStruct((n_seq, cols, win), F32))
    return pl.pallas_call(
        functools.partial(_norm_proj_kernel, splits=splits, windows=tuple(windows), tiles_per_seq=tps),
        grid=(rows // tm,),
        in_specs=[pl.BlockSpec((tm, d), lambda i: (i, 0)), _full((1, d)), _resident_cols(w_bf16, 0, sum(splits))],
        out_specs=[pl.BlockSpec((tm, s), lambda i: (i, 0)) for s in splits] + win_specs,
        out_shape=[jax.ShapeDtypeStruct((rows, s), F32) for s in splits] + win_shapes,
        compiler_params=_params("arbitrary"),
        name="norm_proj",
    )(x, g.reshape(1, d), w_bf16)


def _ssm_layout(a_re, a_im, log_dt, b_re, b_im, c_re, c_im):
    nb, gpb, p, c = SSM_LANE_BLOCKS, SSM_GROUPS_PER_BLOCK, SSM_STATE, SSM_GROUP
    rows = jnp.stack([a_re.reshape(nb, gpb * p), a_im.reshape(nb, gpb * p),
                      jnp.repeat(log_dt, p).reshape(nb, gpb * p)], axis=1)
    eye = jnp.eye(gpb, dtype=F32)

    def place_b(b):
        return jnp.einsum('bgpc,gh->bgchp', b.reshape(nb, gpb, p, c), eye).reshape(nb, gpb * c, gpb * p)

    def place_c(m):
        return jnp.einsum('bgcp,gh->bhpgc', m.reshape(nb, gpb, c, p), eye).reshape(nb, gpb * p, gpb * c)

    return rows, place_b(b_re), place_b(b_im), place_c(c_re), place_c(c_im)


def _ssm_prep_kernel(rows_ref, bre_ref, bim_ref, cre_ref, cim_ref,
                     m_ref, w_ref, v_ref, v0_ref, a1_ref, apow_ref, *, t_chunk):
    sp = SSM_STATES_PER_BLOCK
    a_re, a_im, dt = rows_ref[0:1, :], rows_ref[1:2, :], jnp.exp(rows_ref[2:3, :])

    def powers(k):
        mag = jnp.exp(a_re * dt * k)
        ang = a_im * dt * k
        return mag * jnp.cos(ang), mag * jnp.sin(ang)

    n_pow = 2 * SUBLANES
    assert t_chunk + 1 <= n_pow
    pw_re, pw_im = powers(lax.broadcasted_iota(jnp.int32, (n_pow, 1), 0).astype(F32))
    pwt_re, pwt_im = pw_re.T, pw_im.T
    ab_re, ab_im = pw_re[1:2, :], pw_im[1:2, :]
    den = a_re * a_re + a_im * a_im
    q_re = ((ab_re - 1.0) * a_re + ab_im * a_im) / den
    q_im = (ab_im * a_re - (ab_re - 1.0) * a_im) / den
    bre, bim = bre_ref[...], bim_ref[...]
    bb_re = q_re * bre - q_im * bim
    bb_im = q_re * bim + q_im * bre
    cre, cim = cre_ref[...], cim_ref[...]

    m_ref[...] = jnp.zeros(m_ref.shape, m_ref.dtype)
    for k in range(t_chunk):
        pk_re, pk_im = pw_re[k:k + 1, :], pw_im[k:k + 1, :]
        bk_re = bb_re * pk_re - bb_im * pk_im
        bk_im = bb_re * pk_im + bb_im * pk_re
        t = t_chunk - 1 - k
        w_ref[t * LANES:(t + 1) * LANES, 0:sp] = bk_re.astype(w_ref.dtype)
        w_ref[t * LANES:(t + 1) * LANES, sp:2 * sp] = bk_im.astype(w_ref.dtype)
        kk = (jnp.dot(bk_re, cre, precision=lax.Precision.HIGHEST, preferred_element_type=F32)
              - jnp.dot(bk_im, cim, precision=lax.Precision.HIGHEST, preferred_element_type=F32))
        kk = kk.astype(m_ref.dtype)
        for t0 in range(t_chunk - k):
            m_ref[t0 * LANES:(t0 + 1) * LANES, (t0 + k) * LANES:(t0 + k + 1) * LANES] = kk

    for t in range(t_chunk):
        pc_re, pc_im = pwt_re[:, t + 1:t + 2], pwt_im[:, t + 1:t + 2]
        v_ref[0:sp, t * LANES:(t + 1) * LANES] = (cre * pc_re - cim * pc_im).astype(v_ref.dtype)
        v_ref[sp:2 * sp, t * LANES:(t + 1) * LANES] = (-(cre * pc_im + cim * pc_re)).astype(v_ref.dtype)
    v0_ref[0:sp, :] = cre.astype(v0_ref.dtype)
    v0_ref[sp:2 * sp, :] = (-cim).astype(v0_ref.dtype)

    a1_ref[:, 0:sp] = ab_re
    a1_ref[:, sp:2 * sp] = ab_im
    steps = (lax.broadcasted_iota(jnp.int32, (SUBLANES, 1), 0) + 1) * t_chunk
    ap_re, ap_im = powers(steps.astype(F32))
    apow_ref[:, 0:sp] = ap_re
    apow_ref[:, sp:2 * sp] = ap_im


def _ssm_prep(a_re, a_im, log_dt, b_re, b_im, c_re, c_im):
    rows, pbre, pbim, pcre, pcim = _ssm_layout(a_re, a_im, log_dt, b_re, b_im, c_re, c_im)
    nb, sp, sl, tl = SSM_LANE_BLOCKS, SSM_STATES_PER_BLOCK, SSM_STATE_LANES, SSM_T * LANES

    def blk(shape):
        return pl.BlockSpec((None,) + shape, lambda b: (b, 0, 0))

    return pl.pallas_call(
        functools.partial(_ssm_prep_kernel, t_chunk=SSM_T),
        grid=(nb,),
        in_specs=[blk((3, sp)), blk((LANES, sp)), blk((LANES, sp)), blk((sp, LANES)), blk((sp, LANES))],
        out_specs=[blk((tl, tl)), blk((tl, sl)), blk((sl, tl)), blk((sl, LANES)), blk((1, sl)), blk((SUBLANES, sl))],
        out_shape=[jax.ShapeDtypeStruct((nb, tl, tl), BF16), jax.ShapeDtypeStruct((nb, tl, sl), BF16),
                   jax.ShapeDtypeStruct((nb, sl, tl), BF16), jax.ShapeDtypeStruct((nb, sl, LANES), BF16),
                   jax.ShapeDtypeStruct((nb, 1, sl), F32), jax.ShapeDtypeStruct((nb, SUBLANES, sl), F32)],
        compiler_params=_params("parallel"),
        name="ssm_prep",
    )(rows, pbre, pbim, pcre, pcim)


def _chunk_tokens(u_ref, t, t_chunk):
    return u_ref[pl.ds(t, u_ref.shape[0] // t_chunk, stride=t_chunk), :]


def _chunk_lanes(u_ref, t_chunk):
    return jnp.concatenate([_chunk_tokens(u_ref, t, t_chunk) for t in range(t_chunk)], axis=1)


def _ssm_end_state_kernel(*refs, t_chunk):
    nb, sl = SSM_LANE_BLOCKS, SSM_STATE_LANES
    u_refs, w_ref, e_ref = refs[:nb], refs[nb], refs[nb + 1]
    for b in range(nb):
        ub = _chunk_lanes(u_refs[b], t_chunk).astype(BF16)
        e_ref[:, b * sl:(b + 1) * sl] = jnp.dot(ub, w_ref[b], preferred_element_type=F32)


def _ssm_scan_kernel(ere_ref, eim_ref, pre_ref, pim_ref, spre_ref, spim_ref, fre_ref, fim_ref):
    n_tiles = ere_ref.shape[0] // SUBLANES
    width = ere_ref.shape[1]
    p_re, p_im = pre_ref[...], pim_ref[...]
    row = lax.broadcasted_iota(jnp.int32, (SUBLANES, width), 0)

    def shift_down(x, k):
        return jnp.where(row >= k, pltpu.roll(x, k, 0), 0.0)

    def body(i, carry):
        c_re, c_im = carry
        rows = pl.ds(pl.multiple_of(i * SUBLANES, SUBLANES), SUBLANES)
        x_re, x_im = ere_ref[rows, :], eim_ref[rows, :]
        for k in (1, 2, 4):
            a_re, a_im = p_re[k - 1:k, :], p_im[k - 1:k, :]
            s_re, s_im = shift_down(x_re, k), shift_down(x_im, k)
            x_re, x_im = x_re + a_re * s_re - a_im * s_im, x_im + a_re * s_im + a_im * s_re
        t_re = x_re + p_re * c_re - p_im * c_im
        t_im = x_im + p_re * c_im + p_im * c_re
        spre_ref[rows, :] = jnp.where(row >= 1, pltpu.roll(t_re, 1, 0), c_re)
        spim_ref[rows, :] = jnp.where(row >= 1, pltpu.roll(t_im, 1, 0), c_im)
        return t_re[SUBLANES - 1:SUBLANES, :], t_im[SUBLANES - 1:SUBLANES, :]

    zero = jnp.zeros((1, width), F32)
    f_re, f_im = lax.fori_loop(0, n_tiles, body, (zero, zero))
    fre_ref[...] = f_re
    fim_ref[...] = f_im


def _ssm_output_kernel(*refs, t_chunk):
    nb, sp = SSM_LANE_BLOCKS, SSM_STATES_PER_BLOCK
    u_refs, (spre_ref, spim_ref, m_ref, v_ref, d_ref), y_refs = refs[:nb], refs[nb:nb + 5], refs[nb + 5:]
    for b in range(nb):
        ub = _chunk_lanes(u_refs[b], t_chunk).astype(BF16)
        yb = jnp.dot(ub, m_ref[b], preferred_element_type=F32)
        states = slice(b * sp, (b + 1) * sp)
        sprev = jnp.concatenate([spre_ref[:, states], spim_ref[:, states]], axis=1).astype(BF16)
        yb = yb + jnp.dot(sprev, v_ref[b], preferred_element_type=F32)
        d = d_ref[:, b * LANES:(b + 1) * LANES]
        for t in range(t_chunk):
            y_t = yb[:, t * LANES:(t + 1) * LANES] + d * _chunk_tokens(u_refs[b], t, t_chunk)
            y_refs[b][pl.ds(t, yb.shape[0], stride=t_chunk), :] = y_t


def _ssm_prompt(u, n_seq, m_mat, w_mat, v_mat, apow, d_skip):
    rows, t_chunk, nb, sl, sp = u.shape[0], SSM_T, SSM_LANE_BLOCKS, SSM_STATE_LANES, SSM_STATES_PER_BLOCK
    assert rows % (t_chunk * n_seq) == 0
    n_chunks = rows // t_chunk // n_seq
    tr = min(SSM_ROW_TILE, n_chunks)
    assert n_chunks % tr == 0 and n_chunks % SUBLANES == 0
    tiles = n_chunks // tr
    u_specs = [pl.BlockSpec((tr * t_chunk, LANES), lambda i, b=b: (i, b)) for b in range(nb)]
    e = pl.pallas_call(
        functools.partial(_ssm_end_state_kernel, t_chunk=t_chunk),
        grid=(n_seq * tiles,),
        in_specs=u_specs + [_full(w_mat.shape)],
        out_specs=pl.BlockSpec((tr, nb * sl), lambda i: (i % tiles, i // tiles)),
        out_shape=jax.ShapeDtypeStruct((n_chunks, n_seq * nb * sl), F32),
        compiler_params=_params("parallel"),
        name="ssm_end_state",
    )(*([u] * nb), w_mat)
    col = pl.BlockSpec((n_chunks, sp), lambda g: (0, g))
    fin = pl.BlockSpec((1, sp), lambda g: (0, g))
    sp_re, sp_im, fin_re, fin_im = pl.pallas_call(
        _ssm_scan_kernel,
        grid=(n_seq * nb,),
        in_specs=[pl.BlockSpec((n_chunks, sp), lambda g: (0, 2 * g)),
                  pl.BlockSpec((n_chunks, sp), lambda g: (0, 2 * g + 1)),
                  pl.BlockSpec((None, SUBLANES, sp), lambda g: (g % nb, 0, 0)),
                  pl.BlockSpec((None, SUBLANES, sp), lambda g: (g % nb, 0, 1))],
        out_specs=[col, col, fin, fin],
        out_shape=[jax.ShapeDtypeStruct((n_chunks, n_seq * nb * sp), F32)] * 2
        + [jax.ShapeDtypeStruct((1, n_seq * nb * sp), F32)] * 2,
        compiler_params=_params("parallel"),
        name="ssm_scan",
    )(e, e, apow, apow)
    sp_spec = pl.BlockSpec((tr, nb * sp), lambda i: (i % tiles, i // tiles))
    ys = pl.pallas_call(
        functools.partial(_ssm_output_kernel, t_chunk=t_chunk),
        grid=(n_seq * tiles,),
        in_specs=u_specs + [sp_spec, sp_spec, _full(m_mat.shape), _full(v_mat.shape), _full((1, SSM_WIDTH))],
        out_specs=[pl.BlockSpec((tr * t_chunk, LANES), lambda i: (i, 0)) for _ in range(nb)],
        out_shape=[jax.ShapeDtypeStruct((rows, LANES), F32) for _ in range(nb)],
        compiler_params=_params("parallel"),
        name="ssm_output",
    )(*([u] * nb), sp_re, sp_im, m_mat, v_mat, d_skip.reshape(1, SSM_WIDTH))
    return ys, fin_re, fin_im


def _ssm_step_kernel(u_ref, sre_ref, sim_ref, w0_ref, v0_ref, a1_ref, d_ref, y_ref, nre_ref, nim_ref):
    sp = SSM_STATES_PER_BLOCK
    for b in range(SSM_LANE_BLOCKS):
        lanes = slice(b * LANES, (b + 1) * LANES)
        states = slice(b * sp, (b + 1) * sp)
        u = u_ref[:, lanes]
        e = jnp.dot(u.astype(BF16), w0_ref[b], preferred_element_type=F32)
        a_re, a_im = a1_ref[b, :, 0:sp], a1_ref[b, :, sp:2 * sp]
        s_re, s_im = sre_ref[:, states], sim_ref[:, states]
        n_re = a_re * s_re - a_im * s_im + e[:, 0:sp]
        n_im = a_re * s_im + a_im * s_re + e[:, sp:2 * sp]
        nre_ref[:, states] = n_re
        nim_ref[:, states] = n_im
        sn = jnp.concatenate([n_re, n_im], axis=1).astype(BF16)
        y_ref[:, lanes] = jnp.dot(sn, v0_ref[b], preferred_element_type=F32) + d_ref[:, lanes] * u


def _ssm_step(u, s_re, s_im, w0, v0, a1, d_skip):
    rows = u.shape[0]
    ns = SSM_GROUPS * SSM_STATE
    args = (u, s_re.reshape(rows, ns), s_im.reshape(rows, ns), w0, v0, a1, d_skip.reshape(1, SSM_WIDTH))
    return pl.pallas_call(
        _ssm_step_kernel,
        grid=(1,),
        in_specs=[_full(a.shape) for a in args],
        out_specs=[_full((rows, SSM_WIDTH)), _full((rows, ns)), _full((rows, ns))],
        out_shape=[jax.ShapeDtypeStruct((rows, SSM_WIDTH), F32), jax.ShapeDtypeStruct((rows, ns), F32),
                   jax.ShapeDtypeStruct((rows, ns), F32)],
        compiler_params=_params("arbitrary"),
        name="ssm_step",
    )(*args)


def _dil_attn_kernel(q_ref, kc_ref, kp_ref, vc_ref, vp_ref, o_ref, lse_ref, *, dil, slopes):
    steps, hd = ATT_STEPS, ATT_HEAD_DIM
    chunk = q_ref.shape[0]
    span = steps * dil
    first_chunk = pl.program_id(1) == 0
    pair = pl.program_id(2)
    qi = lax.broadcasted_iota(jnp.int32, (steps, 2 * steps), 0)
    kj = lax.broadcasted_iota(jnp.int32, (steps, 2 * steps), 1)
    dist = qi + steps - kj
    band = (dist >= 0) & (dist <= steps)
    distf = (dist * dil).astype(F32)
    lane = lax.broadcasted_iota(jnp.int32, (steps, LANES), 1)
    heads = [lane < hd, lane >= hd]
    biases, first_biases = [], []
    for hh in range(2):
        slope = jnp.where(pair == 0, slopes[hh], slopes[2 + hh])
        bias = jnp.where(band, -slope * distf, NEG_BIG)
        biases.append(bias)
        first_biases.append(jnp.where(first_chunk & (kj < steps), NEG_BIG, bias))

    def rows(ref, start):
        return ref[pl.ds(start, steps, stride=dil), :] if dil > 1 else ref[pl.ds(start, steps), :]

    for r in range(dil):
        for qb in range(chunk // span):
            start = r + qb * span
            q = rows(q_ref, start) * (hd ** -0.5)
            if qb == 0:
                k_prev, v_prev = rows(kp_ref, r), rows(vp_ref, r)
            else:
                k_prev, v_prev = rows(kc_ref, start - span), rows(vc_ref, start - span)
            kk = jnp.concatenate([k_prev, rows(kc_ref, start)], axis=0).astype(BF16)
            vv = jnp.concatenate([v_prev, rows(vc_ref, start)], axis=0).astype(BF16)
            out = jnp.zeros((steps, LANES), F32)
            lse = jnp.zeros((steps, LANES), F32)
            for hh in range(2):
                qh = jnp.where(heads[hh], q, 0.0).astype(BF16)
                s = lax.dot_general(qh, kk, (((1,), (1,)), ((), ())), preferred_element_type=F32)
                s = s + (first_biases[hh] if qb == 0 else biases[hh])
                m = jnp.max(s, axis=-1, keepdims=True)
                p = jnp.exp(s - m)
                den = jnp.sum(p, axis=-1, keepdims=True)
                oh = jnp.dot(p.astype(BF16), vv, preferred_element_type=F32) / den
                out = jnp.where(heads[hh], oh, out)
                lse = jnp.where(heads[hh], m + jnp.log(den), lse)
            if dil > 1:
                o_ref[pl.ds(start, steps, stride=dil), :] = out
                lse_ref[pl.ds(start, steps, stride=dil), :] = lse
            else:
                o_ref[pl.ds(start, steps), :] = out
                lse_ref[pl.ds(start, steps), :] = lse


def _dil_attn(q, k, v, n_seq, seq_len, group):
    win, dil = DIL_PAIRS[group]
    steps, gw = ATT_STEPS, ATT_GROUP_WIDTH
    span = steps * dil
    chunk = ATT_CHUNK
    assert win // dil == steps and chunk % span == 0 and seq_len % chunk == 0
    chunks = seq_len // chunk
    lane_blocks = gw // LANES
    cur = pl.BlockSpec((chunk, LANES), lambda n, c, p: (n * chunks + c, lane_blocks * group + p))
    prev = pl.BlockSpec((span, LANES), lambda n, c, p: (jnp.maximum((n * chunks + c) * (chunk // span) - 1, 0),
                                                        lane_blocks * group + p))
    out = pl.BlockSpec((chunk, LANES), lambda n, c, p: (n * chunks + c, p))
    slopes = tuple(_alibi_slopes()[group * HEADS_PER_GROUP:(group + 1) * HEADS_PER_GROUP])
    rows = n_seq * seq_len
    return pl.pallas_call(
        functools.partial(_dil_attn_kernel, dil=dil, slopes=slopes),
        grid=(n_seq, chunks, lane_blocks),
        in_specs=[cur, cur, prev, cur, prev],
        out_specs=[out, out],
        out_shape=[jax.ShapeDtypeStruct((rows, gw), F32), jax.ShapeDtypeStruct((rows, gw), F32)],
        compiler_params=_params("parallel", "parallel", "parallel"),
        name=f"dil_attn_w{win}",
    )(q, k, k, v, v)


def _mem_attn_kernel(q_ref, mk_ref, mv_ref, o_ref):
    hd = MEM_HEAD_DIM
    for h in range(MEM_HEADS):
        lanes = slice(h * hd, (h + 1) * hd)
        s = lax.dot_general(q_ref[:, lanes].astype(BF16), mk_ref[:, lanes].astype(BF16),
                            (((1,), (1,)), ((), ())), preferred_element_type=F32) * (hd ** -0.5)
        m = jnp.max(s, axis=-1, keepdims=True)
        p = jnp.exp(s - m)
        den = jnp.sum(p, axis=-1, keepdims=True)
        o_ref[:, lanes] = jnp.dot(p.astype(BF16), mv_ref[:, lanes].astype(BF16), preferred_element_type=F32) / den


def _mem_attn(qm, mk, mv, n_seq, seq_len, tm):
    assert seq_len % tm == 0
    tiles = seq_len // tm
    kv = pl.BlockSpec((None, N_MEM, MEM_WIDTH), lambda n, i: (n, 0, 0))
    return pl.pallas_call(
        _mem_attn_kernel,
        grid=(n_seq, tiles),
        in_specs=[pl.BlockSpec((tm, MEM_WIDTH), lambda n, i: (n * tiles + i, 0)), kv, kv],
        out_specs=pl.BlockSpec((tm, MEM_WIDTH), lambda n, i: (n * tiles + i, 0)),
        out_shape=jax.ShapeDtypeStruct(qm.shape, F32),
        compiler_params=_params("parallel", "parallel"),
        name="mem_attn",
    )(qm, mk, mv)


def _cache_attn_kernel(q_ref, kn_ref, vn_ref, k1_ref, v1_ref, k2_ref, v2_ref, k3_ref, v3_ref, o_ref,
                       qt_scr, vnt_scr, *, slopes):
    hpg, hd = HEADS_PER_GROUP, ATT_HEAD_DIM
    j = pl.program_id(1)
    nb = q_ref.shape[0]
    scale = hd ** -0.5
    q, kn = q_ref[...], kn_ref[...]
    qt_scr[...] = q.T
    vnt_scr[...] = vn_ref[...].T
    lane_head = lax.broadcasted_iota(jnp.int32, q.shape, 1) // hd
    outs, lses = [], []
    for g, (k_ref, v_ref) in enumerate(((k1_ref, v1_ref), (k2_ref, v2_ref), (k3_ref, v3_ref))):
        dil = DIL_PAIRS[g][1]
        n_pos = k_ref.shape[-1]
        rows = pl.ds(pl.multiple_of((g * hpg + j) * hd, hd), hd)
        qg, vng = qt_scr[rows, :], vnt_scr[rows, :]
        slope = sum(jnp.where(j == h, slopes[g * hpg + h], 0.0) for h in range(hpg))
        back = n_pos - lax.broadcasted_iota(jnp.int32, (1, n_pos), 1)
        bias = jnp.where(back % dil == 0, -slope * back.astype(F32), NEG_BIG)
        row = lax.broadcasted_iota(jnp.int32, (nb, n_pos), 0)
        s = jnp.zeros((nb, n_pos), F32)
        for b in range(nb):
            s = jnp.where(row == b, jnp.sum(k_ref[b] * qg[:, b:b + 1], axis=0, keepdims=True), s)
        s = s * scale + bias
        s_new = jnp.sum(jnp.where(lane_head == g * hpg + j, q * kn, 0.0), axis=1, keepdims=True) * scale
        m = jnp.maximum(jnp.max(s, axis=1, keepdims=True), s_new)
        p = jnp.exp(s - m)
        p_new = jnp.exp(s_new - m)
        den = jnp.sum(p, axis=1, keepdims=True) + p_new
        cols = []
        for b in range(nb):
            acc = jnp.sum(v_ref[b] * p[b:b + 1, :], axis=1, keepdims=True) + p_new[b:b + 1, :] * vng[:, b:b + 1]
            cols.append(acc / den[b:b + 1, :])
        outs.append(cols)
        lses.append(m + jnp.log(den))
    top = functools.reduce(jnp.maximum, lses)
    ws = [jnp.exp(l - top) for l in lses]
    total = sum(ws)
    o_ref[...] = jnp.concatenate(
        [sum(w[b:b + 1, :] * cols[b] for w, cols in zip(ws, outs)) / total[b:b + 1, :] for b in range(nb)], axis=1)


def _cache_attn(q, k_new, v_new, caches_k, caches_v, nb):
    rows = q.shape[0]
    hpg, hd = HEADS_PER_GROUP, ATT_HEAD_DIM
    assert rows % nb == 0 and nb % SUBLANES == 0
    new = pl.BlockSpec((nb, ATT_WIDTH), lambda i, j: (i, 0))
    cache_specs, cache_args = [], []
    for g, (win, dil) in enumerate(DIL_PAIRS):
        for c in (caches_k[g], caches_v[g]):
            assert c.shape == (rows, win, hpg, hd) and win % dil == 0, c.shape
            cache_args.append(c.transpose(0, 2, 3, 1))
            cache_specs.append(pl.BlockSpec((nb, None, hd, win), lambda i, j: (i, j, 0, 0)))
    out = pl.pallas_call(
        functools.partial(_cache_attn_kernel, slopes=tuple(_alibi_slopes())),
        grid=(rows // nb, hpg),
        in_specs=[new, new, new] + cache_specs,
        out_specs=pl.BlockSpec((None, None, hd, nb), lambda i, j: (j, i, 0, 0)),
        out_shape=jax.ShapeDtypeStruct((hpg, rows // nb, hd, nb), F32),
        scratch_shapes=[pltpu.VMEM((ATT_WIDTH, nb), F32) for _ in range(2)],
        compiler_params=_params("parallel", "arbitrary"),
        name="cache_attn",
    )(q, k_new, v_new, *cache_args)
    return out.transpose(1, 3, 0, 2).reshape(rows, hpg * hd)


def _cache_mem_attn_kernel(q_ref, k_ref, v_ref, o_ref):
    nb, rows, hd = k_ref.shape
    tiles = rows // SUBLANES

    def fold(x):
        return x, pltpu.roll(x, MEM_HEADS, 2)

    q = q_ref[...][:, None]
    k = k_ref[...].reshape(nb, tiles, SUBLANES, hd)
    v = v_ref[...].reshape(nb, tiles, SUBLANES, hd)
    s = jnp.sum(k * q, axis=-1, keepdims=True) * (MEM_HEAD_DIM ** -0.5)
    m = jnp.maximum(*fold(jnp.max(s, axis=1, keepdims=True)))
    p = jnp.exp(s - m)
    den = sum(fold(jnp.sum(p, axis=1, keepdims=True)))
    acc = sum(fold(jnp.sum(p * v, axis=1, keepdims=True)))
    o_ref[...] = (acc / den)[:, 0]


def _cache_mem_attn(qm, mem_k, mem_v, nb):
    rows = qm.shape[0]
    assert rows % nb == 0 and SUBLANES == 2 * MEM_HEADS
    q4 = qm.reshape(rows, MEM_HEADS, MEM_HEAD_DIM)
    kv = pl.BlockSpec((nb, N_MEM * MEM_HEADS, MEM_HEAD_DIM), lambda i: (i, 0, 0))
    q = pl.BlockSpec((nb, SUBLANES, MEM_HEAD_DIM), lambda i: (i, 0, 0))
    out = pl.pallas_call(
        _cache_mem_attn_kernel,
        grid=(rows // nb,),
        in_specs=[q, kv, kv],
        out_specs=q,
        out_shape=jax.ShapeDtypeStruct((rows, SUBLANES, MEM_HEAD_DIM), F32),
        compiler_params=_params("parallel"),
        name="cache_mem_attn",
    )(jnp.concatenate([q4, q4], axis=1), mem_k.reshape(rows, N_MEM * MEM_HEADS, MEM_HEAD_DIM),
      mem_v.reshape(rows, N_MEM * MEM_HEADS, MEM_HEAD_DIM))
    return out[:, :MEM_HEADS].reshape(rows, MEM_WIDTH)


def _merge_kernel(x_ref, *refs, n_y, n_att):
    y_refs, att_refs = refs[:n_y], refs[n_y:n_y + n_att]
    mem_ref, g1_ref, wgate_ref, wglu_ref, watt_ref, wmem_ref, wout_ref, o_ref = refs[n_y + n_att:]
    d = D_MODEL
    x = x_ref[...]
    h = _rmsnorm(x, g1_ref[...]).astype(BF16)

    def gate(i):
        return jax.nn.sigmoid(jnp.dot(h, wgate_ref[:, i * d:(i + 1) * d], preferred_element_type=F32))

    y = jnp.concatenate([r[...] for r in y_refs], axis=1) if n_y > 1 else y_refs[0][...]
    z = jax.nn.gelu(y).astype(BF16)
    glu = jnp.dot(z, wglu_ref[...], preferred_element_type=F32)
    merged = gate(0) * (glu[:, 0:d] * jax.nn.sigmoid(glu[:, d:2 * d]))
    if n_att == 1:
        att = att_refs[0][...]
    else:
        lses = [r[...] for r in att_refs[1::2]]
        top = functools.reduce(jnp.maximum, lses)
        ws = [jnp.exp(l - top) for l in lses]
        att = sum(w * r[...] for w, r in zip(ws, att_refs[0::2])) / sum(ws)
    b_att = jnp.dot(att.astype(BF16), watt_ref[...], preferred_element_type=F32)
    merged = merged + gate(1) * b_att
    b_mem = jnp.dot(mem_ref[...].astype(BF16), wmem_ref[...], preferred_element_type=F32)
    merged = merged + gate(2) * b_mem
    o_ref[...] = x + jnp.dot(merged.astype(BF16), wout_ref[...], preferred_element_type=F32)


def _merge(x, ys, atts, o_mem, g1, w_in, w_glu, w_att_o, w_mem_o, w_out, tm):
    rows = x.shape[0]
    gate_cols = N_BRANCH * D_MODEL
    assert rows % tm == 0 and w_in.shape == (D_MODEL, IN_WIDTH)

    def tile(a):
        return pl.BlockSpec((tm, a.shape[1]), lambda i: (i, 0))

    acts = (x, *ys, *atts, o_mem)
    weights = (w_glu, w_att_o, w_mem_o, w_out)
    return pl.pallas_call(
        functools.partial(_merge_kernel, n_y=len(ys), n_att=len(atts)),
        grid=(rows // tm,),
        in_specs=[tile(a) for a in acts] + [_full((1, D_MODEL)), _resident_cols(w_in, IN_WIDTH - gate_cols, gate_cols)]
        + [_full(w.shape) for w in weights],
        out_specs=tile(x),
        out_shape=jax.ShapeDtypeStruct(x.shape, F32),
        compiler_params=_params("parallel"),
        name="merge",
    )(*acts, g1.reshape(1, D_MODEL), w_in, *weights)


def _ffn_kernel(*refs, tm, tiles_per_seq, stepwise):
    if stepwise:
        x_ref, g2_ref, wup_ref, cw_ref, cb_ref, wdn_ref, gf_ref, prev_ref, y_ref, conv_ref = refs
    else:
        x_ref, g2_ref, wup_ref, cw_ref, cb_ref, wdn_ref, gf_ref, y_ref, conv_ref, a_scr = refs
        i = pl.program_id(0)
        first = i % tiles_per_seq == 0

        @pl.when(first)
        def _():
            a_scr[0:SUBLANES, :] = jnp.zeros((SUBLANES, D_FF), F32)

        @pl.when(jnp.logical_not(first))
        def _():
            a_scr[0:SUBLANES, :] = a_scr[tm:tm + SUBLANES, :]

    x = x_ref[...]
    h = _rmsnorm(x, g2_ref[...]).astype(BF16)
    acc = jnp.zeros((tm, D_MODEL), F32)
    for c0 in range(0, D_FF, FFN_CHUNK):
        cols = slice(c0, c0 + FFN_CHUNK)
        a = jnp.dot(h, wup_ref[:, cols], preferred_element_type=F32)
        if stepwise:
            a2, a1 = prev_ref[:, cols], prev_ref[:, D_FF + c0:D_FF + c0 + FFN_CHUNK]
            conv_ref[:, cols] = a1
            conv_ref[:, D_FF + c0:D_FF + c0 + FFN_CHUNK] = a
        else:
            a_scr[SUBLANES:SUBLANES + tm, cols] = a
            a1 = a_scr[SUBLANES - 1:SUBLANES - 1 + tm, cols]
            a2 = a_scr[SUBLANES - 2:SUBLANES - 2 + tm, cols]
        c = a2 * cw_ref[0:1, cols] + a1 * cw_ref[1:2, cols] + a * cw_ref[2:3, cols] + cb_ref[:, cols]
        v = jnp.dot(h, wup_ref[:, D_FF + c0:D_FF + c0 + FFN_CHUNK], preferred_element_type=F32)
        acc = acc + jnp.dot((jax.nn.gelu(c) * v).astype(BF16), wdn_ref[cols, :], preferred_element_type=F32)
    y_ref[...] = _rmsnorm(x + acc, gf_ref[...])
    if not stepwise:
        @pl.when(i % tiles_per_seq == tiles_per_seq - 1)
        def _():
            conv_ref[...] = a_scr[SUBLANES + tm - (CONV_W - 1):SUBLANES + tm, :]


def _ffn(x, g2, w_up, conv_w, conv_b, w_down, gf, tm, n_seq=None, prev=None):
    rows, d = x.shape
    assert rows % tm == 0 and D_FF % FFN_CHUNK == 0 and FFN_CHUNK % LANES == 0
    stepwise = prev is not None
    weights = (g2.reshape(1, d), w_up, conv_w, conv_b.reshape(1, D_FF), w_down, gf.reshape(1, d))
    in_specs = [pl.BlockSpec((tm, d), lambda i: (i, 0))] + [_full(w.shape) for w in weights]
    args = (x,) + weights
    if stepwise:
        tiles_per_seq = 1
        in_specs.append(pl.BlockSpec((tm, 2 * D_FF), lambda i: (i, 0)))
        args += (prev,)
        conv_spec = pl.BlockSpec((tm, 2 * D_FF), lambda i: (i, 0))
        conv_shape = jax.ShapeDtypeStruct((rows, 2 * D_FF), F32)
        scratch = []
    else:
        tiles_per_seq = rows // n_seq // tm
        assert tiles_per_seq * tm * n_seq == rows
        conv_spec = pl.BlockSpec((None, CONV_W - 1, D_FF), lambda i: (i // tiles_per_seq, 0, 0))
        conv_shape = jax.ShapeDtypeStruct((n_seq, CONV_W - 1, D_FF), F32)
        scratch = [pltpu.VMEM((tm + 2 * SUBLANES, D_FF), F32)]
    return pl.pallas_call(
        functools.partial(_ffn_kernel, tm=tm, tiles_per_seq=tiles_per_seq, stepwise=stepwise),
        grid=(rows // tm,),
        in_specs=in_specs,
        out_specs=[pl.BlockSpec((tm, d), lambda i: (i, 0)), conv_spec],
        out_shape=[jax.ShapeDtypeStruct((rows, d), F32), conv_shape],
        scratch_shapes=scratch,
        compiler_params=_params("arbitrary"),
        name="ffn_step" if stepwise else "ffn",
    )(*args)


def kernel(x_prompt, x_sample, state_ssm_re, state_ssm_im, cache_w1_k, cache_w1_v, cache_w2_k, cache_w2_v, cache_w3_k, cache_w3_v, cache_mem_k, cache_mem_v, state_ffn_conv, mem_prompt, norm1_g, w_in, ssm_a_re, ssm_a_im, ssm_log_dt, ssm_b_re, ssm_b_im, ssm_c_re, ssm_c_im, ssm_d, w_ssm_glu, w_att_o, mem_norm_g, w_mem_kv, w_mem_o, w_out, norm2_g, w_up, ffn_conv_w, ffn_conv_b, w_down, final_norm_g):
    n_seq, seq_len, d = x_prompt.shape
    n_dec, dec_len, _ = x_sample.shape
    depth = norm1_g.shape[0]
    assert d == D_MODEL and depth == 1 and dec_len == 1
    assert w_in.shape == (depth, D_MODEL, IN_WIDTH) and w_up.shape == (depth, D_MODEL, 2 * D_FF)
    assert mem_prompt.shape == (n_seq, N_MEM, D_MODEL)
    assert ssm_a_re.shape == (depth, SSM_GROUPS, SSM_STATE)
    assert seq_len % ROW_TILE == 0 and n_dec % SUBLANES == 0
    hpg, hd = HEADS_PER_GROUP, ATT_HEAD_DIM
    rows_p = n_seq * seq_len
    dec_tile = n_dec if n_dec <= ROW_TILE else ROW_TILE

    wb = lambda w: w[0].astype(BF16)
    w_in_b, w_glu_b, w_att_b, w_memkv_b = wb(w_in), wb(w_ssm_glu), wb(w_att_o), wb(w_mem_kv)
    w_memo_b, w_out_b, w_up_b, w_down_b = wb(w_mem_o), wb(w_out), wb(w_up), wb(w_down)

    m_mat, w_mat, v_mat, v0_mat, a1, apow = _ssm_prep(ssm_a_re[0], ssm_a_im[0], ssm_log_dt[0], ssm_b_re[0],
                                                  ssm_b_im[0], ssm_c_re[0], ssm_c_im[0])
    w0_mat = w_mat[:, (SSM_T - 1) * LANES:, :]

    xp = x_prompt.reshape(rows_p, d)
    gw = ATT_GROUP_WIDTH
    keeps = [min(win, seq_len) for win, _ in DIL_PAIRS]
    windows = [(src, g * gw, gw, keep) for g, keep in enumerate(keeps) for src in (2, 3)]
    u, q, k, v, qm, *kv_win = _norm_proj(xp, norm1_g[0], w_in_b, PROJ_SPLITS, ROW_TILE, n_seq, windows)
    y_ssm, fin_re, fin_im = _ssm_prompt(u, n_seq, m_mat, w_mat, v_mat, apow, ssm_d[0])
    atts = [a for g in range(len(DIL_PAIRS)) for a in _dil_attn(q, k, v, n_seq, seq_len, g)]
    mk, mv = _norm_proj(mem_prompt.reshape(n_seq * N_MEM, d), mem_norm_g[0], w_memkv_b, (MEM_WIDTH, MEM_WIDTH),
                        ROW_TILE)
    o_mem = _mem_attn(qm, mk.reshape(n_seq, N_MEM, MEM_WIDTH), mv.reshape(n_seq, N_MEM, MEM_WIDTH),
                      n_seq, seq_len, 2 * ROW_TILE)
    x1 = _merge(xp, y_ssm, atts, o_mem, norm1_g[0], w_in_b, w_glu_b, w_att_b, w_memo_b, w_out_b, ROW_TILE)
    y_p, conv_p = _ffn(x1, norm2_g[0], w_up_b, ffn_conv_w[0], ffn_conv_b[0], w_down_b, final_norm_g,
                       ROW_TILE, n_seq=n_seq)

    def final_state(s):
        return s.reshape(1, n_seq, SSM_GROUPS, SSM_STATE)
    win_p = [t.reshape(n_seq, hpg, hd, t.shape[-1]).transpose(0, 3, 1, 2)[None] for t in kv_win]
    mem_kv = [mk.reshape(1, n_seq, N_MEM, MEM_HEADS, MEM_HEAD_DIM), mv.reshape(1, n_seq, N_MEM, MEM_HEADS, MEM_HEAD_DIM)]

    xs = x_sample.reshape(n_dec, d)
    us, qs, ks, vs, qms = _norm_proj(xs, norm1_g[0], w_in_b, PROJ_SPLITS, dec_tile)
    ys_ssm, sn_re, sn_im = _ssm_step(us, state_ssm_re[0], state_ssm_im[0], w0_mat, v0_mat, a1, ssm_d[0])
    n_g = len(DIL_PAIRS)
    ks4, vs4 = (t.reshape(n_dec, n_g, hpg, hd) for t in (ks, vs))
    att_s = _cache_attn(qs, ks, vs, (cache_w1_k[0], cache_w2_k[0], cache_w3_k[0]),
                        (cache_w1_v[0], cache_w2_v[0], cache_w3_v[0]), SUBLANES)
    mem_s = _cache_mem_attn(qms, cache_mem_k[0], cache_mem_v[0], CACHE_ROWS_PER_STEP)
    xs1 = _merge(xs, [ys_ssm], [att_s], mem_s, norm1_g[0], w_in_b,
                 w_glu_b, w_att_b, w_memo_b, w_out_b, dec_tile)
    y_s, conv_s = _ffn(xs1, norm2_g[0], w_up_b, ffn_conv_w[0], ffn_conv_b[0], w_down_b, final_norm_g,
                       dec_tile, prev=state_ffn_conv[0].reshape(n_dec, (CONV_W - 1) * D_FF))

    win_s = []
    for g in range(len(DIL_PAIRS)):
        win_s += [ks4[None, :, g:g + 1], vs4[None, :, g:g + 1]]

    return (y_p.reshape(n_seq, seq_len, d), y_s.reshape(n_dec, 1, d),
            final_state(fin_re), final_state(fin_im), *win_p, *mem_kv, conv_p[None],
            sn_re.reshape(1, n_dec, SSM_GROUPS, SSM_STATE), sn_im.reshape(1, n_dec, SSM_GROUPS, SSM_STATE),
            *win_s, conv_s.reshape(1, n_dec, CONV_W - 1, D_FF))
```

```python
import functools

import jax
import jax.numpy as jnp
from jax import lax
from jax.experimental import pallas as pl
from jax.experimental.pallas import tpu as pltpu

F32 = jnp.float32
BF16 = jnp.bfloat16

D_MODEL = 1024
SSM_WIDTH = 512
SSM_GROUP = 16
SSM_GROUPS = 32
SSM_STATE = 64
ATT_HEAD_DIM = 64
HEADS_PER_GROUP = 4
DIL_PAIRS = ((128, 1), (512, 4), (2048, 16))
ATT_HEADS = len(DIL_PAIRS) * HEADS_PER_GROUP
ATT_WIDTH = ATT_HEADS * ATT_HEAD_DIM
ATT_GROUP_WIDTH = HEADS_PER_GROUP * ATT_HEAD_DIM
ATT_STEPS = 128
N_MEM = 256
MEM_HEADS = 4
MEM_HEAD_DIM = 128
MEM_WIDTH = MEM_HEADS * MEM_HEAD_DIM
N_BRANCH = 3
D_FF = 2816
CONV_W = 3
EPS = 1e-6
PROJ_SPLITS = (SSM_WIDTH, ATT_WIDTH, ATT_WIDTH, ATT_WIDTH, MEM_WIDTH)
IN_WIDTH = sum(PROJ_SPLITS) + N_BRANCH * D_MODEL

LANES = 128
SUBLANES = 8
VMEM_LIMIT_BYTES = 56 * 1024 * 1024

SSM_T = 8
SSM_LANE_BLOCKS = SSM_WIDTH // LANES
SSM_GROUPS_PER_BLOCK = LANES // SSM_GROUP
SSM_STATES_PER_BLOCK = SSM_GROUPS_PER_BLOCK * SSM_STATE
SSM_STATE_LANES = 2 * SSM_STATES_PER_BLOCK

ROW_TILE = 512
SSM_ROW_TILE = 256
FFN_CHUNK = 2816
CACHE_ROWS_PER_STEP = 8
ATT_CHUNK = 2048
NEG_BIG = -1e30


def _alibi_slopes():
    return [float(2.0 ** (-8.0 * h / ATT_HEADS)) for h in range(1, ATT_HEADS + 1)]


def _params(*sem):
    return pltpu.CompilerParams(dimension_semantics=sem, vmem_limit_bytes=VMEM_LIMIT_BYTES)


def _rmsnorm(x, g):
    ms = jnp.mean(x * x, axis=-1, keepdims=True)
    return x * lax.rsqrt(ms + EPS) * g


def _full(shape):
    nd = len(shape)
    return pl.BlockSpec(shape, lambda *_: (0,) * nd, pipeline_mode=pl.Buffered(1))


def _norm_proj_kernel(x_ref, g_ref, w_ref, *out_refs, splits, windows, tiles_per_seq):
    tm = x_ref.shape[0]
    h = _rmsnorm(x_ref[...], g_ref[...]).astype(BF16)
    off = 0
    for o_ref, width in zip(out_refs, splits):
        for c0 in range(0, width, 512):
            cw = min(512, width - c0)
            o_ref[:, c0:c0 + cw] = jnp.dot(h, w_ref[:, off + c0:off + c0 + cw], preferred_element_type=F32)
        off += width
    tile = pl.program_id(0) % tiles_per_seq
    for win_ref, (src, col0, cols, win) in zip(out_refs[len(splits):], windows):
        rows = min(win, tm)

        @pl.when(tile >= tiles_per_seq - max(win // tm, 1))
        def _(win_ref=win_ref, src=src, col0=col0, cols=cols, rows=rows):
            win_ref[...] = out_refs[src][tm - rows:tm, col0:col0 + cols].T


def _resident_cols(w, start, width):
    return pl.BlockSpec((pl.Element(w.shape[0]), pl.Element(width)), lambda *_: (0, start),
                        pipeline_mode=pl.Buffered(1))


def _norm_proj_rider_kernel(*refs, n_rider_in, rider, **kw):
    main_in, rider_in = refs[:3], refs[3:3 + n_rider_in]
    rider(*rider_in, refs[-1])
    _norm_proj_kernel(*main_in, *refs[3 + n_rider_in:-1], **kw)


def _norm_proj(x, g, w_bf16, splits, tm, n_seq=1, windows=(), rider=None):
    rows, d = x.shape
    assert rows % (tm * n_seq) == 0 and sum(splits) <= w_bf16.shape[1]
    tps = rows // n_seq // tm
    win_specs, win_shapes = [], []
    for _, _, cols, win in windows:
        assert (win % tm == 0 or tm % win == 0) and win <= tps * tm
        n_tiles = max(win // tm, 1)
        win_specs.append(pl.BlockSpec((None, cols, min(win, tm)), lambda i, n_tiles=n_tiles: (
            i // tps, 0, jnp.maximum(i % tps - (tps - n_tiles), 0))))
        win_shapes.append(jax.ShapeDtypeStruct((n_seq, cols, win), F32))
    kw = dict(splits=splits, windows=tuple(windows), tiles_per_seq=tps)
    in_specs = [pl.BlockSpec((tm, d), lambda i: (i, 0)), _full((1, d)), _resident_cols(w_bf16, 0, sum(splits))]
    out_specs = [pl.BlockSpec((tm, s), lambda i: (i, 0)) for s in splits] + win_specs
    out_shape = [jax.ShapeDtypeStruct((rows, s), F32) for s in splits] + win_shapes
    args = (x, g.reshape(1, d), w_bf16)
    body, name = functools.partial(_norm_proj_kernel, **kw), "norm_proj"
    if rider is not None:
        r_kernel, r_args, r_specs, r_out_spec, r_out_shape, r_steps = rider
        assert r_steps == rows // tm, (r_steps, rows // tm)
        body = functools.partial(_norm_proj_rider_kernel, n_rider_in=len(r_args), rider=r_kernel, **kw)
        in_specs, out_specs, out_shape = in_specs + r_specs, out_specs + [r_out_spec], out_shape + [r_out_shape]
        args, name = args + tuple(r_args), "norm_proj_rider"
    return pl.pallas_call(
        body,
        grid=(rows // tm,),
        in_specs=in_specs,
        out_specs=out_specs,
        out_shape=out_shape,
        compiler_params=_params("arbitrary"),
        name=name,
    )(*args)


def _ssm_layout(a_re, a_im, log_dt, b_re, b_im, c_re, c_im):
    nb, gpb, p, c = SSM_LANE_BLOCKS, SSM_GROUPS_PER_BLOCK, SSM_STATE, SSM_GROUP
    rows = jnp.stack([a_re.reshape(nb, gpb * p), a_im.reshape(nb, gpb * p),
                      jnp.repeat(log_dt, p).reshape(nb, gpb * p)], axis=1)
    eye = jnp.eye(gpb, dtype=F32)

    def place_b(b):
        return jnp.einsum('bgpc,gh->bgchp', b.reshape(nb, gpb, p, c), eye).reshape(nb, gpb * c, gpb * p)

    def place_c(m):
        return jnp.einsum('bgcp,gh->bhpgc', m.reshape(nb, gpb, c, p), eye).reshape(nb, gpb * p, gpb * c)

    return rows, place_b(b_re), place_b(b_im), place_c(c_re), place_c(c_im)


def _ssm_prep_kernel(rows_ref, bre_ref, bim_ref, cre_ref, cim_ref,
                     m_ref, w_ref, v_ref, v0_ref, a1_ref, apow_ref, *, t_chunk):
    sp = SSM_STATES_PER_BLOCK
    a_re, a_im, dt = rows_ref[0:1, :], rows_ref[1:2, :], jnp.exp(rows_ref[2:3, :])

    def powers(k):
        mag = jnp.exp(a_re * dt * k)
        ang = a_im * dt * k
        return mag * jnp.cos(ang), mag * jnp.sin(ang)

    n_pow = 2 * SUBLANES
    assert t_chunk + 1 <= n_pow
    pw_re, pw_im = powers(lax.broadcasted_iota(jnp.int32, (n_pow, 1), 0).astype(F32))
    pwt_re, pwt_im = pw_re.T, pw_im.T
    ab_re, ab_im = pw_re[1:2, :], pw_im[1:2, :]
    den = a_re * a_re + a_im * a_im
    q_re = ((ab_re - 1.0) * a_re + ab_im * a_im) / den
    q_im = (ab_im * a_re - (ab_re - 1.0) * a_im) / den
    bre, bim = bre_ref[...], bim_ref[...]
    bb_re = q_re * bre - q_im * bim
    bb_im = q_re * bim + q_im * bre
    cre, cim = cre_ref[...], cim_ref[...]

    m_ref[...] = jnp.zeros(m_ref.shape, m_ref.dtype)
    for k in range(t_chunk):
        pk_re, pk_im = pw_re[k:k + 1, :], pw_im[k:k + 1, :]
        bk_re = bb_re * pk_re - bb_im * pk_im
        bk_im = bb_re * pk_im + bb_im * pk_re
        t = t_chunk - 1 - k
        w_ref[t * LANES:(t + 1) * LANES, 0:sp] = bk_re.astype(w_ref.dtype)
        w_ref[t * LANES:(t + 1) * LANES, sp:2 * sp] = bk_im.astype(w_ref.dtype)
        kk = (jnp.dot(bk_re, cre, precision=lax.Precision.HIGHEST, preferred_element_type=F32)
              - jnp.dot(bk_im, cim, precision=lax.Precision.HIGHEST, preferred_element_type=F32))
        kk = kk.astype(m_ref.dtype)
        for t0 in range(t_chunk - k):
            m_ref[t0 * LANES:(t0 + 1) * LANES, (t0 + k) * LANES:(t0 + k + 1) * LANES] = kk

    for t in range(t_chunk):
        pc_re, pc_im = pwt_re[:, t + 1:t + 2], pwt_im[:, t + 1:t + 2]
        v_ref[0:sp, t * LANES:(t + 1) * LANES] = (cre * pc_re - cim * pc_im).astype(v_ref.dtype)
        v_ref[sp:2 * sp, t * LANES:(t + 1) * LANES] = (-(cre * pc_im + cim * pc_re)).astype(v_ref.dtype)
    v0_ref[0:sp, :] = cre.astype(v0_ref.dtype)
    v0_ref[sp:2 * sp, :] = (-cim).astype(v0_ref.dtype)

    a1_ref[:, 0:sp] = ab_re
    a1_ref[:, sp:2 * sp] = ab_im
    steps = (lax.broadcasted_iota(jnp.int32, (SUBLANES, 1), 0) + 1) * t_chunk
    ap_re, ap_im = powers(steps.astype(F32))
    apow_ref[:, 0:sp] = ap_re
    apow_ref[:, sp:2 * sp] = ap_im


def _ssm_prep(a_re, a_im, log_dt, b_re, b_im, c_re, c_im):
    rows, pbre, pbim, pcre, pcim = _ssm_layout(a_re, a_im, log_dt, b_re, b_im, c_re, c_im)
    nb, sp, sl, tl = SSM_LANE_BLOCKS, SSM_STATES_PER_BLOCK, SSM_STATE_LANES, SSM_T * LANES

    def blk(shape):
        return pl.BlockSpec((None,) + shape, lambda b: (b, 0, 0))

    return pl.pallas_call(
        functools.partial(_ssm_prep_kernel, t_chunk=SSM_T),
        grid=(nb,),
        in_specs=[blk((3, sp)), blk((LANES, sp)), blk((LANES, sp)), blk((sp, LANES)), blk((sp, LANES))],
        out_specs=[blk((tl, tl)), blk((tl, sl)), blk((sl, tl)), blk((sl, LANES)), blk((1, sl)), blk((SUBLANES, sl))],
        out_shape=[jax.ShapeDtypeStruct((nb, tl, tl), BF16), jax.ShapeDtypeStruct((nb, tl, sl), BF16),
                   jax.ShapeDtypeStruct((nb, sl, tl), BF16), jax.ShapeDtypeStruct((nb, sl, LANES), BF16),
                   jax.ShapeDtypeStruct((nb, 1, sl), F32), jax.ShapeDtypeStruct((nb, SUBLANES, sl), F32)],
        compiler_params=_params("parallel"),
        name="ssm_prep",
    )(rows, pbre, pbim, pcre, pcim)


def _chunk_tokens(u_ref, t, t_chunk):
    return u_ref[pl.ds(t, u_ref.shape[0] // t_chunk, stride=t_chunk), :]


def _chunk_lanes(u_ref, t_chunk):
    return jnp.concatenate([_chunk_tokens(u_ref, t, t_chunk) for t in range(t_chunk)], axis=1)


def _ssm_end_state_kernel(*refs, t_chunk):
    nb, sl = SSM_LANE_BLOCKS, SSM_STATE_LANES
    u_refs, w_ref, e_ref = refs[:nb], refs[nb], refs[nb + 1]
    for b in range(nb):
        ub = _chunk_lanes(u_refs[b], t_chunk).astype(BF16)
        e_ref[:, b * sl:(b + 1) * sl] = jnp.dot(ub, w_ref[b], preferred_element_type=F32)


def _ssm_scan_kernel(ere_ref, eim_ref, pre_ref, pim_ref, spre_ref, spim_ref, fre_ref, fim_ref):
    n_tiles = ere_ref.shape[0] // SUBLANES
    width = ere_ref.shape[1]
    p_re, p_im = pre_ref[...], pim_ref[...]
    row = lax.broadcasted_iota(jnp.int32, (SUBLANES, width), 0)

    def shift_down(x, k):
        return jnp.where(row >= k, pltpu.roll(x, k, 0), 0.0)

    def body(i, carry):
        c_re, c_im = carry
        rows = pl.ds(pl.multiple_of(i * SUBLANES, SUBLANES), SUBLANES)
        x_re, x_im = ere_ref[rows, :], eim_ref[rows, :]
        for k in (1, 2, 4):
            a_re, a_im = p_re[k - 1:k, :], p_im[k - 1:k, :]
            s_re, s_im = shift_down(x_re, k), shift_down(x_im, k)
            x_re, x_im = x_re + a_re * s_re - a_im * s_im, x_im + a_re * s_im + a_im * s_re
        t_re = x_re + p_re * c_re - p_im * c_im
        t_im = x_im + p_re * c_im + p_im * c_re
        spre_ref[rows, :] = jnp.where(row >= 1, pltpu.roll(t_re, 1, 0), c_re)
        spim_ref[rows, :] = jnp.where(row >= 1, pltpu.roll(t_im, 1, 0), c_im)
        return t_re[SUBLANES - 1:SUBLANES, :], t_im[SUBLANES - 1:SUBLANES, :]

    zero = jnp.zeros((1, width), F32)
    f_re, f_im = lax.fori_loop(0, n_tiles, body, (zero, zero))
    fre_ref[...] = f_re
    fim_ref[...] = f_im


def _ssm_output_kernel(*refs, t_chunk):
    nb, sp = SSM_LANE_BLOCKS, SSM_STATES_PER_BLOCK
    u_refs, (spre_ref, spim_ref, m_ref, v_ref, d_ref), y_refs = refs[:nb], refs[nb:nb + 5], refs[nb + 5:]
    for b in range(nb):
        ub = _chunk_lanes(u_refs[b], t_chunk).astype(BF16)
        yb = jnp.dot(ub, m_ref[b], preferred_element_type=F32)
        states = slice(b * sp, (b + 1) * sp)
        sprev = jnp.concatenate([spre_ref[:, states], spim_ref[:, states]], axis=1).astype(BF16)
        yb = yb + jnp.dot(sprev, v_ref[b], preferred_element_type=F32)
        d = d_ref[:, b * LANES:(b + 1) * LANES]
        for t in range(t_chunk):
            y_t = yb[:, t * LANES:(t + 1) * LANES] + d * _chunk_tokens(u_refs[b], t, t_chunk)
            y_refs[b][pl.ds(t, yb.shape[0], stride=t_chunk), :] = y_t


def _ssm_prompt(u, n_seq, m_mat, w_mat, v_mat, apow, d_skip):
    rows, t_chunk, nb, sl, sp = u.shape[0], SSM_T, SSM_LANE_BLOCKS, SSM_STATE_LANES, SSM_STATES_PER_BLOCK
    assert rows % (t_chunk * n_seq) == 0
    n_chunks = rows // t_chunk // n_seq
    tr = min(SSM_ROW_TILE, n_chunks)
    assert n_chunks % tr == 0 and n_chunks % SUBLANES == 0
    tiles = n_chunks // tr
    u_specs = [pl.BlockSpec((tr * t_chunk, LANES), lambda i, b=b: (i, b)) for b in range(nb)]
    e = pl.pallas_call(
        functools.partial(_ssm_end_state_kernel, t_chunk=t_chunk),
        grid=(n_seq * tiles,),
        in_specs=u_specs + [_full(w_mat.shape)],
        out_specs=pl.BlockSpec((tr, nb * sl), lambda i: (i % tiles, i // tiles)),
        out_shape=jax.ShapeDtypeStruct((n_chunks, n_seq * nb * sl), F32),
        compiler_params=_params("parallel"),
        name="ssm_end_state",
    )(*([u] * nb), w_mat)
    col = pl.BlockSpec((n_chunks, sp), lambda g: (0, g))
    fin = pl.BlockSpec((1, sp), lambda g: (0, g))
    sp_re, sp_im, fin_re, fin_im = pl.pallas_call(
        _ssm_scan_kernel,
        grid=(n_seq * nb,),
        in_specs=[pl.BlockSpec((n_chunks, sp), lambda g: (0, 2 * g)),
                  pl.BlockSpec((n_chunks, sp), lambda g: (0, 2 * g + 1)),
                  pl.BlockSpec((None, SUBLANES, sp), lambda g: (g % nb, 0, 0)),
                  pl.BlockSpec((None, SUBLANES, sp), lambda g: (g % nb, 0, 1))],
        out_specs=[col, col, fin, fin],
        out_shape=[jax.ShapeDtypeStruct((n_chunks, n_seq * nb * sp), F32)] * 2
        + [jax.ShapeDtypeStruct((1, n_seq * nb * sp), F32)] * 2,
        compiler_params=_params("parallel"),
        name="ssm_scan",
    )(e, e, apow, apow)
    sp_spec = pl.BlockSpec((tr, nb * sp), lambda i: (i % tiles, i // tiles))
    ys = pl.pallas_call(
        functools.partial(_ssm_output_kernel, t_chunk=t_chunk),
        grid=(n_seq * tiles,),
        in_specs=u_specs + [sp_spec, sp_spec, _full(m_mat.shape), _full(v_mat.shape), _full((1, SSM_WIDTH))],
        out_specs=[pl.BlockSpec((tr * t_chunk, LANES), lambda i: (i, 0)) for _ in range(nb)],
        out_shape=[jax.ShapeDtypeStruct((rows, LANES), F32) for _ in range(nb)],
        compiler_params=_params("parallel"),
        name="ssm_output",
    )(*([u] * nb), sp_re, sp_im, m_mat, v_mat, d_skip.reshape(1, SSM_WIDTH))
    return ys, fin_re, fin_im


def _ssm_step_kernel(u_ref, sre_ref, sim_ref, w0_ref, v0_ref, a1_ref, d_ref, y_ref, nre_ref, nim_ref):
    sp = SSM_STATES_PER_BLOCK
    for b in range(SSM_LANE_BLOCKS):
        lanes = slice(b * LANES, (b + 1) * LANES)
        states = slice(b * sp, (b + 1) * sp)
        u = u_ref[:, lanes]
        e = jnp.dot(u.astype(BF16), w0_ref[b], preferred_element_type=F32)
        a_re, a_im = a1_ref[b, :, 0:sp], a1_ref[b, :, sp:2 * sp]
        s_re, s_im = sre_ref[:, states], sim_ref[:, states]
        n_re = a_re * s_re - a_im * s_im + e[:, 0:sp]
        n_im = a_re * s_im + a_im * s_re + e[:, sp:2 * sp]
        nre_ref[:, states] = n_re
        nim_ref[:, states] = n_im
        sn = jnp.concatenate([n_re, n_im], axis=1).astype(BF16)
        y_ref[:, lanes] = jnp.dot(sn, v0_ref[b], preferred_element_type=F32) + d_ref[:, lanes] * u


def _ssm_step(u, s_re, s_im, w0, v0, a1, d_skip):
    rows = u.shape[0]
    ns = SSM_GROUPS * SSM_STATE
    args = (u, s_re.reshape(rows, ns), s_im.reshape(rows, ns), w0, v0, a1, d_skip.reshape(1, SSM_WIDTH))
    return pl.pallas_call(
        _ssm_step_kernel,
        grid=(1,),
        in_specs=[_full(a.shape) for a in args],
        out_specs=[_full((rows, SSM_WIDTH)), _full((rows, ns)), _full((rows, ns))],
        out_shape=[jax.ShapeDtypeStruct((rows, SSM_WIDTH), F32), jax.ShapeDtypeStruct((rows, ns), F32),
                   jax.ShapeDtypeStruct((rows, ns), F32)],
        compiler_params=_params("arbitrary"),
        name="ssm_step",
    )(*args)


def _dil_attn_kernel(q_ref, kc_ref, kp_ref, vc_ref, vp_ref, o_ref, lse_ref, *, dil, slopes):
    steps, hd = ATT_STEPS, ATT_HEAD_DIM
    chunk = q_ref.shape[0]
    span = steps * dil
    first_chunk = pl.program_id(1) == 0
    pair = pl.program_id(2)
    qi = lax.broadcasted_iota(jnp.int32, (steps, 2 * steps), 0)
    kj = lax.broadcasted_iota(jnp.int32, (steps, 2 * steps), 1)
    dist = qi + steps - kj
    band = (dist >= 0) & (dist <= steps)
    distf = (dist * dil).astype(F32)
    lane = lax.broadcasted_iota(jnp.int32, (steps, LANES), 1)
    heads = [lane < hd, lane >= hd]
    biases, first_biases = [], []
    for hh in range(2):
        slope = jnp.where(pair == 0, slopes[hh], slopes[2 + hh])
        bias = jnp.where(band, -slope * distf, NEG_BIG)
        biases.append(bias)
        first_biases.append(jnp.where(first_chunk & (kj < steps), NEG_BIG, bias))

    def rows(ref, start):
        return ref[pl.ds(start, steps, stride=dil), :] if dil > 1 else ref[pl.ds(start, steps), :]

    for r in range(dil):
        for qb in range(chunk // span):
            start = r + qb * span
            q = rows(q_ref, start) * (hd ** -0.5)
            if qb == 0:
                k_prev, v_prev = rows(kp_ref, r), rows(vp_ref, r)
            else:
                k_prev, v_prev = rows(kc_ref, start - span), rows(vc_ref, start - span)
            kk = jnp.concatenate([k_prev, rows(kc_ref, start)], axis=0).astype(BF16)
            vv = jnp.concatenate([v_prev, rows(vc_ref, start)], axis=0).astype(BF16)
            out = jnp.zeros((steps, LANES), F32)
            lse = jnp.zeros((steps, LANES), F32)
            for hh in range(2):
                qh = jnp.where(heads[hh], q, 0.0).astype(BF16)
                s = lax.dot_general(qh, kk, (((1,), (1,)), ((), ())), preferred_element_type=F32)
                s = s + (first_biases[hh] if qb == 0 else biases[hh])
                m = jnp.max(s, axis=-1, keepdims=True)
                p = jnp.exp(s - m)
                den = jnp.sum(p, axis=-1, keepdims=True)
                oh = jnp.dot(p.astype(BF16), vv, preferred_element_type=F32) / den
                out = jnp.where(heads[hh], oh, out)
                lse = jnp.where(heads[hh], m + jnp.log(den), lse)
            if dil > 1:
                o_ref[pl.ds(start, steps, stride=dil), :] = out
                lse_ref[pl.ds(start, steps, stride=dil), :] = lse
            else:
                o_ref[pl.ds(start, steps), :] = out
                lse_ref[pl.ds(start, steps), :] = lse


def _dil_attn(q, k, v, n_seq, seq_len, group):
    win, dil = DIL_PAIRS[group]
    steps, gw = ATT_STEPS, ATT_GROUP_WIDTH
    span = steps * dil
    chunk = ATT_CHUNK
    assert win // dil == steps and chunk % span == 0 and seq_len % chunk == 0
    chunks = seq_len // chunk
    lane_blocks = gw // LANES
    cur = pl.BlockSpec((chunk, LANES), lambda n, c, p: (n * chunks + c, lane_blocks * group + p))
    prev = pl.BlockSpec((span, LANES), lambda n, c, p: (jnp.maximum((n * chunks + c) * (chunk // span) - 1, 0),
                                                        lane_blocks * group + p))
    out = pl.BlockSpec((chunk, LANES), lambda n, c, p: (n * chunks + c, p))
    slopes = tuple(_alibi_slopes()[group * HEADS_PER_GROUP:(group + 1) * HEADS_PER_GROUP])
    rows = n_seq * seq_len
    return pl.pallas_call(
        functools.partial(_dil_attn_kernel, dil=dil, slopes=slopes),
        grid=(n_seq, chunks, lane_blocks),
        in_specs=[cur, cur, prev, cur, prev],
        out_specs=[out, out],
        out_shape=[jax.ShapeDtypeStruct((rows, gw), F32), jax.ShapeDtypeStruct((rows, gw), F32)],
        compiler_params=_params("parallel", "parallel", "parallel"),
        name=f"dil_attn_w{win}",
    )(q, k, k, v, v)


def _mem_attention(q_ref, mk_ref, mv_ref):
    hd = MEM_HEAD_DIM
    outs = []
    for h in range(MEM_HEADS):
        lanes = slice(h * hd, (h + 1) * hd)
        s = lax.dot_general(q_ref[:, lanes].astype(BF16), mk_ref[:, lanes].astype(BF16),
                            (((1,), (1,)), ((), ())), preferred_element_type=F32) * (hd ** -0.5)
        m = jnp.max(s, axis=-1, keepdims=True)
        p = jnp.exp(s - m)
        den = jnp.sum(p, axis=-1, keepdims=True)
        outs.append(jnp.dot(p.astype(BF16), mv_ref[:, lanes].astype(BF16), preferred_element_type=F32) / den)
    return jnp.concatenate(outs, axis=1)


def _cache_attn_body(j, q_ref, kn_ref, vn_ref, k1_ref, v1_ref, k2_ref, v2_ref, k3_ref, v3_ref, o_ref,
                     qt_scr, vnt_scr, slopes):
    hpg, hd = HEADS_PER_GROUP, ATT_HEAD_DIM
    nb = q_ref.shape[0]
    scale = hd ** -0.5
    q, kn = q_ref[...], kn_ref[...]
    qt_scr[...] = q.T
    vnt_scr[...] = vn_ref[...].T
    lane_head = lax.broadcasted_iota(jnp.int32, q.shape, 1) // hd
    outs, lses = [], []
    for g, (k_ref, v_ref) in enumerate(((k1_ref, v1_ref), (k2_ref, v2_ref), (k3_ref, v3_ref))):
        dil = DIL_PAIRS[g][1]
        n_pos = k_ref.shape[-1]
        rows = pl.ds(pl.multiple_of((g * hpg + j) * hd, hd), hd)
        qg, vng = qt_scr[rows, :], vnt_scr[rows, :]
        slope = sum(jnp.where(j == h, slopes[g * hpg + h], 0.0) for h in range(hpg))
        back = n_pos - lax.broadcasted_iota(jnp.int32, (1, n_pos), 1)
        bias = jnp.where(back % dil == 0, -slope * back.astype(F32), NEG_BIG)
        row = lax.broadcasted_iota(jnp.int32, (nb, n_pos), 0)
        s = jnp.zeros((nb, n_pos), F32)
        for b in range(nb):
            s = jnp.where(row == b, jnp.sum(k_ref[b] * qg[:, b:b + 1], axis=0, keepdims=True), s)
        s = s * scale + bias
        s_new = jnp.sum(jnp.where(lane_head == g * hpg + j, q * kn, 0.0), axis=1, keepdims=True) * scale
        m = jnp.maximum(jnp.max(s, axis=1, keepdims=True), s_new)
        p = jnp.exp(s - m)
        p_new = jnp.exp(s_new - m)
        den = jnp.sum(p, axis=1, keepdims=True) + p_new
        cols = []
        for b in range(nb):
            acc = jnp.sum(v_ref[b] * p[b:b + 1, :], axis=1, keepdims=True) + p_new[b:b + 1, :] * vng[:, b:b + 1]
            cols.append(acc / den[b:b + 1, :])
        outs.append(cols)
        lses.append(m + jnp.log(den))
    top = functools.reduce(jnp.maximum, lses)
    ws = [jnp.exp(l - top) for l in lses]
    total = sum(ws)
    o_ref[...] = jnp.concatenate(
        [sum(w[b:b + 1, :] * cols[b] for w, cols in zip(ws, outs)) / total[b:b + 1, :] for b in range(nb)], axis=1)


def _cache_attn_plan(q, k_new, v_new, caches_k, caches_v, nb):
    rows = q.shape[0]
    hpg, hd = HEADS_PER_GROUP, ATT_HEAD_DIM
    assert rows % nb == 0 and nb % SUBLANES == 0
    new = pl.BlockSpec((nb, ATT_WIDTH), lambda i: (i // hpg, 0))
    specs, args = [new, new, new], [q, k_new, v_new]
    for g, (win, dil) in enumerate(DIL_PAIRS):
        for c in (caches_k[g], caches_v[g]):
            assert c.shape == (rows, win, hpg, hd) and win % dil == 0, c.shape
            args.append(c.transpose(0, 2, 3, 1))
            specs.append(pl.BlockSpec((nb, None, hd, win), lambda i: (i // hpg, i % hpg, 0, 0)))
    out_spec = pl.BlockSpec((None, None, hd, nb), lambda i: (i % hpg, i // hpg, 0, 0))
    out_shape = jax.ShapeDtypeStruct((hpg, rows // nb, hd, nb), F32)
    scratch = [pltpu.VMEM((ATT_WIDTH, nb), F32) for _ in range(2)]
    return args, specs, out_spec, out_shape, scratch, (rows // nb) * hpg


def _cache_attn_kernel(*refs, slopes):
    _cache_attn_body(pl.program_id(0) % HEADS_PER_GROUP, *refs, slopes)


def _cache_attn(plan):
    args, specs, out_spec, out_shape, scratch, steps = plan
    out = pl.pallas_call(
        functools.partial(_cache_attn_kernel, slopes=tuple(_alibi_slopes())),
        grid=(steps,),
        in_specs=specs,
        out_specs=out_spec,
        out_shape=out_shape,
        scratch_shapes=scratch,
        compiler_params=_params("arbitrary"),
        name="cache_attn",
    )(*args)
    return _cache_attn_result(out)


def _cache_attn_result(out):
    hpg, blocks, hd, nb = out.shape
    return out.transpose(1, 3, 0, 2).reshape(blocks * nb, hpg * hd)


def _cache_mem_attn_kernel(q_ref, k_ref, v_ref, o_ref):
    nb, rows, hd = k_ref.shape
    tiles = rows // SUBLANES

    def fold(x):
        return x, pltpu.roll(x, MEM_HEADS, 2)

    q = q_ref[...][:, None]
    k = k_ref[...].reshape(nb, tiles, SUBLANES, hd)
    v = v_ref[...].reshape(nb, tiles, SUBLANES, hd)
    s = jnp.sum(k * q, axis=-1, keepdims=True) * (MEM_HEAD_DIM ** -0.5)
    m = jnp.maximum(*fold(jnp.max(s, axis=1, keepdims=True)))
    p = jnp.exp(s - m)
    den = sum(fold(jnp.sum(p, axis=1, keepdims=True)))
    acc = sum(fold(jnp.sum(p * v, axis=1, keepdims=True)))
    o_ref[...] = (acc / den)[:, 0]


def _cache_mem_plan(qm, mem_k, mem_v, nb):
    rows = qm.shape[0]
    assert rows % nb == 0 and SUBLANES == 2 * MEM_HEADS
    q4 = qm.reshape(rows, MEM_HEADS, MEM_HEAD_DIM)
    kv = pl.BlockSpec((nb, N_MEM * MEM_HEADS, MEM_HEAD_DIM), lambda i: (i, 0, 0))
    q = pl.BlockSpec((nb, SUBLANES, MEM_HEAD_DIM), lambda i: (i, 0, 0))
    args = [jnp.concatenate([q4, q4], axis=1), mem_k.reshape(rows, N_MEM * MEM_HEADS, MEM_HEAD_DIM),
            mem_v.reshape(rows, N_MEM * MEM_HEADS, MEM_HEAD_DIM)]
    out_shape = jax.ShapeDtypeStruct((rows, SUBLANES, MEM_HEAD_DIM), F32)
    return _cache_mem_attn_kernel, args, [q, kv, kv], q, out_shape, rows // nb


def _cache_mem_attn(plan):
    kernel_fn, args, specs, out_spec, out_shape, steps = plan
    return pl.pallas_call(
        kernel_fn,
        grid=(steps,),
        in_specs=specs,
        out_specs=out_spec,
        out_shape=out_shape,
        compiler_params=_params("parallel"),
        name="cache_mem_attn",
    )(*args)


def _cache_mem_result(out):
    return out[:, :MEM_HEADS].reshape(out.shape[0], MEM_WIDTH)


def _merge_kernel(x_ref, *refs, n_y, n_att, n_mem):
    y_refs, att_refs = refs[:n_y], refs[n_y:n_y + n_att]
    mem_refs = refs[n_y + n_att:n_y + n_att + n_mem]
    g1_ref, wgate_ref, wglu_ref, watt_ref, wmem_ref, wout_ref, o_ref = refs[n_y + n_att + n_mem:]
    d = D_MODEL
    x = x_ref[...]
    h = _rmsnorm(x, g1_ref[...]).astype(BF16)

    def gate(i):
        return jax.nn.sigmoid(jnp.dot(h, wgate_ref[:, i * d:(i + 1) * d], preferred_element_type=F32))

    y = jnp.concatenate([r[...] for r in y_refs], axis=1) if n_y > 1 else y_refs[0][...]
    z = jax.nn.gelu(y).astype(BF16)
    glu = jnp.dot(z, wglu_ref[...], preferred_element_type=F32)
    merged = gate(0) * (glu[:, 0:d] * jax.nn.sigmoid(glu[:, d:2 * d]))
    if n_att == 1:
        att = att_refs[0][...]
    else:
        lses = [r[...] for r in att_refs[1::2]]
        top = functools.reduce(jnp.maximum, lses)
        ws = [jnp.exp(l - top) for l in lses]
        att = sum(w * r[...] for w, r in zip(ws, att_refs[0::2])) / sum(ws)
    b_att = jnp.dot(att.astype(BF16), watt_ref[...], preferred_element_type=F32)
    merged = merged + gate(1) * b_att
    o_mem = mem_refs[0][...] if n_mem == 1 else _mem_attention(*mem_refs)
    b_mem = jnp.dot(o_mem.astype(BF16), wmem_ref[...], preferred_element_type=F32)
    merged = merged + gate(2) * b_mem
    o_ref[...] = x + jnp.dot(merged.astype(BF16), wout_ref[...], preferred_element_type=F32)


def _merge_cache_kernel(*refs, n_in, merge_kw, slopes):
    n_cache = 3 + 2 * len(DIL_PAIRS)
    merge_in, cache_in = refs[:n_in], refs[n_in:n_in + n_cache]
    merge_out, cache_out, qt_scr, vnt_scr = refs[n_in + n_cache:]
    _merge_kernel(*merge_in, merge_out, **merge_kw)
    _cache_attn_body(pl.program_id(0) % HEADS_PER_GROUP, *cache_in, cache_out, qt_scr, vnt_scr, slopes)


def _merge(x, ys, atts, mems, g1, w_in, w_glu, w_att_o, w_mem_o, w_out, tm, cache_attn=None):
    rows = x.shape[0]
    gate_cols = N_BRANCH * D_MODEL
    assert rows % tm == 0 and w_in.shape == (D_MODEL, IN_WIDTH)

    def tile(a):
        return pl.BlockSpec((tm, a.shape[1]), lambda i: (i, 0))

    acts = (x, *ys, *atts, mems[0])
    act_specs = [tile(a) for a in acts]
    if len(mems) > 1:
        tiles_per_seq = rows // mems[1].shape[0] // tm
        assert tiles_per_seq * tm * mems[1].shape[0] == rows
        act_specs += [pl.BlockSpec((None, N_MEM, MEM_WIDTH), lambda i: (i // tiles_per_seq, 0, 0))] * 2
        acts += tuple(mems[1:])
    weights = (w_glu, w_att_o, w_mem_o, w_out)
    merge_kw = dict(n_y=len(ys), n_att=len(atts), n_mem=len(mems))
    in_specs = (act_specs + [_full((1, D_MODEL)), _resident_cols(w_in, IN_WIDTH - gate_cols, gate_cols)]
                + [_full(w.shape) for w in weights])
    args = (*acts, g1.reshape(1, D_MODEL), w_in, *weights)
    out_shape = jax.ShapeDtypeStruct(x.shape, F32)
    if cache_attn is None:
        return pl.pallas_call(
            functools.partial(_merge_kernel, **merge_kw),
            grid=(rows // tm,),
            in_specs=in_specs,
            out_specs=tile(x),
            out_shape=out_shape,
            compiler_params=_params("parallel"),
            name="merge",
        )(*args)
    c_args, c_specs, c_out_spec, c_out_shape, c_scratch, c_steps = cache_attn
    assert c_steps == rows // tm, (c_steps, rows // tm)
    merged, att = pl.pallas_call(
        functools.partial(_merge_cache_kernel, n_in=len(args), merge_kw=merge_kw, slopes=tuple(_alibi_slopes())),
        grid=(rows // tm,),
        in_specs=in_specs + c_specs,
        out_specs=[tile(x), c_out_spec],
        out_shape=[out_shape, c_out_shape],
        scratch_shapes=c_scratch,
        compiler_params=_params("arbitrary"),
        name="merge_cache_attn",
    )(*args, *c_args)
    return merged, _cache_attn_result(att)


def _ffn_kernel(*refs, tm, tiles_per_seq, stepwise):
    if stepwise:
        x_ref, g2_ref, wup_ref, cw_ref, cb_ref, wdn_ref, gf_ref, prev_ref, y_ref, conv_ref = refs
    else:
        x_ref, g2_ref, wup_ref, cw_ref, cb_ref, wdn_ref, gf_ref, y_ref, conv_ref, a_scr = refs
        i = pl.program_id(0)
        first = i % tiles_per_seq == 0

        @pl.when(first)
        def _():
            a_scr[0:SUBLANES, :] = jnp.zeros((SUBLANES, D_FF), F32)

        @pl.when(jnp.logical_not(first))
        def _():
            a_scr[0:SUBLANES, :] = a_scr[tm:tm + SUBLANES, :]

    x = x_ref[...]
    h = _rmsnorm(x, g2_ref[...]).astype(BF16)
    acc = jnp.zeros((tm, D_MODEL), F32)
    for c0 in range(0, D_FF, FFN_CHUNK):
        cols = slice(c0, c0 + FFN_CHUNK)
        a = jnp.dot(h, wup_ref[:, cols], preferred_element_type=F32)
        if stepwise:
            a2, a1 = prev_ref[:, cols], prev_ref[:, D_FF + c0:D_FF + c0 + FFN_CHUNK]
            conv_ref[:, cols] = a1
            conv_ref[:, D_FF + c0:D_FF + c0 + FFN_CHUNK] = a
        else:
            a_scr[SUBLANES:SUBLANES + tm, cols] = a
            a1 = a_scr[SUBLANES - 1:SUBLANES - 1 + tm, cols]
            a2 = a_scr[SUBLANES - 2:SUBLANES - 2 + tm, cols]
        c = a2 * cw_ref[0:1, cols] + a1 * cw_ref[1:2, cols] + a * cw_ref[2:3, cols] + cb_ref[:, cols]
        v = jnp.dot(h, wup_ref[:, D_FF + c0:D_FF + c0 + FFN_CHUNK], preferred_element_type=F32)
        acc = acc + jnp.dot((jax.nn.gelu(c) * v).astype(BF16), wdn_ref[cols, :], preferred_element_type=F32)
    y_ref[...] = _rmsnorm(x + acc, gf_ref[...])
    if not stepwise:
        @pl.when(i % tiles_per_seq == tiles_per_seq - 1)
        def _():
            conv_ref[...] = a_scr[SUBLANES + tm - (CONV_W - 1):SUBLANES + tm, :]


def _ffn(x, g2, w_up, conv_w, conv_b, w_down, gf, tm, n_seq=None, prev=None):
    rows, d = x.shape
    assert rows % tm == 0 and D_FF % FFN_CHUNK == 0 and FFN_CHUNK % LANES == 0
    stepwise = prev is not None
    weights = (g2.reshape(1, d), w_up, conv_w, conv_b.reshape(1, D_FF), w_down, gf.reshape(1, d))
    in_specs = [pl.BlockSpec((tm, d), lambda i: (i, 0))] + [_full(w.shape) for w in weights]
    args = (x,) + weights
    if stepwise:
        tiles_per_seq = 1
        in_specs.append(pl.BlockSpec((tm, 2 * D_FF), lambda i: (i, 0)))
        args += (prev,)
        conv_spec = pl.BlockSpec((tm, 2 * D_FF), lambda i: (i, 0))
        conv_shape = jax.ShapeDtypeStruct((rows, 2 * D_FF), F32)
        scratch = []
    else:
        tiles_per_seq = rows // n_seq // tm
        assert tiles_per_seq * tm * n_seq == rows
        conv_spec = pl.BlockSpec((None, CONV_W - 1, D_FF), lambda i: (i // tiles_per_seq, 0, 0))
        conv_shape = jax.ShapeDtypeStruct((n_seq, CONV_W - 1, D_FF), F32)
        scratch = [pltpu.VMEM((tm + 2 * SUBLANES, D_FF), F32)]
    return pl.pallas_call(
        functools.partial(_ffn_kernel, tm=tm, tiles_per_seq=tiles_per_seq, stepwise=stepwise),
        grid=(rows // tm,),
        in_specs=in_specs,
        out_specs=[pl.BlockSpec((tm, d), lambda i: (i, 0)), conv_spec],
        out_shape=[jax.ShapeDtypeStruct((rows, d), F32), conv_shape],
        scratch_shapes=scratch,
        compiler_params=_params("arbitrary"),
        name="ffn_step" if stepwise else "ffn",
    )(*args)


def kernel(x_prompt, x_sample, state_ssm_re, state_ssm_im, cache_w1_k, cache_w1_v, cache_w2_k, cache_w2_v, cache_w3_k, cache_w3_v, cache_mem_k, cache_mem_v, state_ffn_conv, mem_prompt, norm1_g, w_in, ssm_a_re, ssm_a_im, ssm_log_dt, ssm_b_re, ssm_b_im, ssm_c_re, ssm_c_im, ssm_d, w_ssm_glu, w_att_o, mem_norm_g, w_mem_kv, w_mem_o, w_out, norm2_g, w_up, ffn_conv_w, ffn_conv_b, w_down, final_norm_g):
    n_seq, seq_len, d = x_prompt.shape
    n_dec, dec_len, _ = x_sample.shape
    depth = norm1_g.shape[0]
    assert d == D_MODEL and depth == 1 and dec_len == 1
    assert w_in.shape == (depth, D_MODEL, IN_WIDTH) and w_up.shape == (depth, D_MODEL, 2 * D_FF)
    assert mem_prompt.shape == (n_seq, N_MEM, D_MODEL)
    assert ssm_a_re.shape == (depth, SSM_GROUPS, SSM_STATE)
    assert seq_len % ROW_TILE == 0 and n_dec % SUBLANES == 0
    hpg, hd = HEADS_PER_GROUP, ATT_HEAD_DIM
    rows_p = n_seq * seq_len
    dec_tile = n_dec if n_dec <= ROW_TILE else ROW_TILE

    wb = lambda w: w[0].astype(BF16)
    w_in_b, w_glu_b, w_att_b, w_memkv_b = wb(w_in), wb(w_ssm_glu), wb(w_att_o), wb(w_mem_kv)
    w_memo_b, w_out_b, w_up_b, w_down_b = wb(w_mem_o), wb(w_out), wb(w_up), wb(w_down)

    m_mat, w_mat, v_mat, v0_mat, a1, apow = _ssm_prep(ssm_a_re[0], ssm_a_im[0], ssm_log_dt[0], ssm_b_re[0],
                                                  ssm_b_im[0], ssm_c_re[0], ssm_c_im[0])
    w0_mat = w_mat[:, (SSM_T - 1) * LANES:, :]

    xs = x_sample.reshape(n_dec, d)
    us, qs, ks, vs, qms = _norm_proj(xs, norm1_g[0], w_in_b, PROJ_SPLITS, dec_tile)
    cache_plan = _cache_attn_plan(qs, ks, vs, (cache_w1_k[0], cache_w2_k[0], cache_w3_k[0]),
                                  (cache_w1_v[0], cache_w2_v[0], cache_w3_v[0]), SUBLANES)
    merge_tile = rows_p // cache_plan[-1]
    fuse_cache = (rows_p % cache_plan[-1] == 0 and merge_tile % LANES == 0 and merge_tile <= ROW_TILE
                  and seq_len % merge_tile == 0)

    xp = x_prompt.reshape(rows_p, d)
    gw = ATT_GROUP_WIDTH
    keeps = [min(win, seq_len) for win, _ in DIL_PAIRS]
    windows = [(src, g * gw, gw, keep) for g, keep in enumerate(keeps) for src in (2, 3)]
    proj_steps = rows_p // ROW_TILE
    ride_mem = n_dec % proj_steps == 0
    mem_plan = _cache_mem_plan(qms, cache_mem_k[0], cache_mem_v[0],
                               n_dec // proj_steps if ride_mem else CACHE_ROWS_PER_STEP)
    u, q, k, v, qm, *kv_win = _norm_proj(xp, norm1_g[0], w_in_b, PROJ_SPLITS, ROW_TILE, n_seq, windows,
                                         rider=mem_plan if ride_mem else None)
    mem_s = _cache_mem_result(kv_win.pop() if ride_mem else _cache_mem_attn(mem_plan))
    y_ssm, fin_re, fin_im = _ssm_prompt(u, n_seq, m_mat, w_mat, v_mat, apow, ssm_d[0])
    atts = [a for g in range(len(DIL_PAIRS)) for a in _dil_attn(q, k, v, n_seq, seq_len, g)]
    mk, mv = _norm_proj(mem_prompt.reshape(n_seq * N_MEM, d), mem_norm_g[0], w_memkv_b, (MEM_WIDTH, MEM_WIDTH),
                        ROW_TILE)
    mems = [qm, mk.reshape(n_seq, N_MEM, MEM_WIDTH), mv.reshape(n_seq, N_MEM, MEM_WIDTH)]
    merge_w = (norm1_g[0], w_in_b, w_glu_b, w_att_b, w_memo_b, w_out_b)
    if fuse_cache:
        x1, att_s = _merge(xp, y_ssm, atts, mems, *merge_w, merge_tile, cache_attn=cache_plan)
    else:
        x1, att_s = _merge(xp, y_ssm, atts, mems, *merge_w, ROW_TILE), _cache_attn(cache_plan)
    y_p, conv_p = _ffn(x1, norm2_g[0], w_up_b, ffn_conv_w[0], ffn_conv_b[0], w_down_b, final_norm_g,
                       ROW_TILE, n_seq=n_seq)

    def final_state(s):
        return s.reshape(1, n_seq, SSM_GROUPS, SSM_STATE)
    win_p = [t.reshape(n_seq, hpg, hd, t.shape[-1]).transpose(0, 3, 1, 2)[None] for t in kv_win]
    mem_kv = [mk.reshape(1, n_seq, N_MEM, MEM_HEADS, MEM_HEAD_DIM), mv.reshape(1, n_seq, N_MEM, MEM_HEADS, MEM_HEAD_DIM)]

    ys_ssm, sn_re, sn_im = _ssm_step(us, state_ssm_re[0], state_ssm_im[0], w0_mat, v0_mat, a1, ssm_d[0])
    n_g = len(DIL_PAIRS)
    ks4, vs4 = (t.reshape(n_dec, n_g, hpg, hd) for t in (ks, vs))
    xs1 = _merge(xs, [ys_ssm], [att_s], [mem_s], *merge_w, dec_tile)
    y_s, conv_s = _ffn(xs1, norm2_g[0], w_up_b, ffn_conv_w[0], ffn_conv_b[0], w_down_b, final_norm_g,
                       dec_tile, prev=state_ffn_conv[0].reshape(n_dec, (CONV_W - 1) * D_FF))

    win_s = []
    for g in range(len(DIL_PAIRS)):
        win_s += [ks4[None, :, g:g + 1], vs4[None, :, g:g + 1]]

    return (y_p.reshape(n_seq, seq_len, d), y_s.reshape(n_dec, 1, d),
            final_state(fin_re), final_state(fin_im), *win_p, *mem_kv, conv_p[None],
            sn_re.reshape(1, n_dec, SSM_GROUPS, SSM_STATE), sn_im.reshape(1, n_dec, SSM_GROUPS, SSM_STATE),
            *win_s, conv_s.reshape(1, n_dec, CONV_W - 1, D_FF))
```

```python
import collections
import functools

import jax
import jax.numpy as jnp
from jax import lax
from jax.experimental import pallas as pl
from jax.experimental.pallas import tpu as pltpu

F32 = jnp.float32
BF16 = jnp.bfloat16

D_MODEL = 1024
SSM_WIDTH = 512
SSM_GROUP = 16
SSM_GROUPS = 32
SSM_STATE = 64
ATT_HEAD_DIM = 64
HEADS_PER_GROUP = 4
DIL_PAIRS = ((128, 1), (512, 4), (2048, 16))
ATT_HEADS = len(DIL_PAIRS) * HEADS_PER_GROUP
ATT_WIDTH = ATT_HEADS * ATT_HEAD_DIM
ATT_GROUP_WIDTH = HEADS_PER_GROUP * ATT_HEAD_DIM
ATT_STEPS = 128
N_MEM = 256
MEM_HEADS = 4
MEM_HEAD_DIM = 128
MEM_WIDTH = MEM_HEADS * MEM_HEAD_DIM
N_BRANCH = 3
D_FF = 2816
CONV_W = 3
EPS = 1e-6
PROJ_SPLITS = (SSM_WIDTH, ATT_WIDTH, ATT_WIDTH, ATT_WIDTH, MEM_WIDTH)
IN_WIDTH = sum(PROJ_SPLITS) + N_BRANCH * D_MODEL

LANES = 128
SUBLANES = 8
VMEM_LIMIT_BYTES = 56 * 1024 * 1024

SSM_T = 8
SSM_LANE_BLOCKS = SSM_WIDTH // LANES
SSM_GROUPS_PER_BLOCK = LANES // SSM_GROUP
SSM_STATES_PER_BLOCK = SSM_GROUPS_PER_BLOCK * SSM_STATE
SSM_STATE_LANES = 2 * SSM_STATES_PER_BLOCK

ROW_TILE = 512
SSM_ROW_TILE = 256
CACHE_ROWS_PER_STEP = 8
ATT_CHUNK = 2048
NEG_BIG = -1e30


def _alibi_slopes():
    return [float(2.0 ** (-8.0 * h / ATT_HEADS)) for h in range(1, ATT_HEADS + 1)]


def _params(*sem):
    return pltpu.CompilerParams(dimension_semantics=sem, vmem_limit_bytes=VMEM_LIMIT_BYTES)


def _rmsnorm(x, g):
    ms = jnp.mean(x * x, axis=-1, keepdims=True)
    return x * lax.rsqrt(ms + EPS) * g


def _full(shape):
    nd = len(shape)
    return pl.BlockSpec(shape, lambda *_: (0,) * nd, pipeline_mode=pl.Buffered(1))


def _resident_cols(w, start, width):
    return pl.BlockSpec((pl.Element(w.shape[0]), pl.Element(width)), lambda *_: (0, start),
                        pipeline_mode=pl.Buffered(1))


Job = collections.namedtuple("Job", "kernel args in_specs out_specs out_shapes scratch steps")


def _jobs_kernel(*refs, kernels, layout):
    n_in, n_out = sum(l[0] for l in layout), sum(l[1] for l in layout)
    ins, outs, scratch = refs[:n_in], refs[n_in:n_in + n_out], refs[n_in + n_out:]
    i0 = o0 = s0 = 0
    for kernel, (ni, no, ns) in zip(kernels, layout):
        kernel(*ins[i0:i0 + ni], *outs[o0:o0 + no], *scratch[s0:s0 + ns])
        i0, o0, s0 = i0 + ni, o0 + no, s0 + ns


def _run(jobs, name):
    steps = jobs[0].steps
    assert all(j.steps == steps for j in jobs)
    layout = tuple((len(j.args), len(j.out_shapes), len(j.scratch)) for j in jobs)
    outs = pl.pallas_call(
        functools.partial(_jobs_kernel, kernels=tuple(j.kernel for j in jobs), layout=layout),
        grid=(steps,),
        in_specs=[s for j in jobs for s in j.in_specs],
        out_specs=[s for j in jobs for s in j.out_specs],
        out_shape=[s for j in jobs for s in j.out_shapes],
        scratch_shapes=[s for j in jobs for s in j.scratch],
        compiler_params=_params("arbitrary"),
        name=name,
    )(*[a for j in jobs for a in j.args])
    split, o0 = [], 0
    for _, no, _ in layout:
        split.append(list(outs[o0:o0 + no]))
        o0 += no
    return split


def _run_pair(main, rider, name):
    if main.steps == rider.steps:
        r, m = _run([rider, main], name)
        return m, r
    return _run([main], name)[0], _run([rider], name + "_rider")[0]


def _norm_proj_kernel(x_ref, g_ref, w_ref, *out_refs, splits, windows, tiles_per_seq):
    tm = x_ref.shape[0]
    h = _rmsnorm(x_ref[...], g_ref[...]).astype(BF16)
    off = 0
    for o_ref, width in zip(out_refs, splits):
        for c0 in range(0, width, 512):
            cw = min(512, width - c0)
            o_ref[:, c0:c0 + cw] = jnp.dot(h, w_ref[:, off + c0:off + c0 + cw], preferred_element_type=F32)
        off += width
    tile = pl.program_id(0) % tiles_per_seq
    for win_ref, (src, col0, cols, win) in zip(out_refs[len(splits):], windows):
        rows = min(win, tm)

        @pl.when(tile >= tiles_per_seq - max(win // tm, 1))
        def _(win_ref=win_ref, src=src, col0=col0, cols=cols, rows=rows):
            win_ref[...] = out_refs[src][tm - rows:tm, col0:col0 + cols].T


def _norm_proj_job(x, g, w_bf16, splits, tm, n_seq=1, windows=()):
    rows, d = x.shape
    assert rows % (tm * n_seq) == 0 and sum(splits) <= w_bf16.shape[1]
    tps = rows // n_seq // tm
    win_specs, win_shapes = [], []
    for _, _, cols, win in windows:
        assert (win % tm == 0 or tm % win == 0) and win <= tps * tm
        n_tiles = max(win // tm, 1)
        win_specs.append(pl.BlockSpec((None, cols, min(win, tm)), lambda i, n_tiles=n_tiles: (
            i // tps, 0, jnp.maximum(i % tps - (tps - n_tiles), 0))))
        win_shapes.append(jax.ShapeDtypeStruct((n_seq, cols, win), F32))
    return Job(
        kernel=functools.partial(_norm_proj_kernel, splits=splits, windows=tuple(windows), tiles_per_seq=tps),
        args=[x, g.reshape(1, d), w_bf16],
        in_specs=[pl.BlockSpec((tm, d), lambda i: (i, 0)), _full((1, d)), _resident_cols(w_bf16, 0, sum(splits))],
        out_specs=[pl.BlockSpec((tm, s), lambda i: (i, 0)) for s in splits] + win_specs,
        out_shapes=[jax.ShapeDtypeStruct((rows, s), F32) for s in splits] + win_shapes,
        scratch=[],
        steps=rows // tm,
    )


def _ssm_layout(a_re, a_im, log_dt, b_re, b_im, c_re, c_im):
    nb, gpb, p, c = SSM_LANE_BLOCKS, SSM_GROUPS_PER_BLOCK, SSM_STATE, SSM_GROUP
    rows = jnp.stack([a_re.reshape(nb, gpb * p), a_im.reshape(nb, gpb * p),
                      jnp.repeat(log_dt, p).reshape(nb, gpb * p)], axis=1)
    eye = jnp.eye(gpb, dtype=F32)

    def place_b(b):
        return jnp.einsum('bgpc,gh->bgchp', b.reshape(nb, gpb, p, c), eye).reshape(nb, gpb * c, gpb * p)

    def place_c(m):
        return jnp.einsum('bgcp,gh->bhpgc', m.reshape(nb, gpb, c, p), eye).reshape(nb, gpb * p, gpb * c)

    return rows, place_b(b_re), place_b(b_im), place_c(c_re), place_c(c_im)


def _ssm_prep_kernel(rows_ref, bre_ref, bim_ref, cre_ref, cim_ref,
                     m_ref, w_ref, v_ref, v0_ref, a1_ref, apow_ref, *, t_chunk):
    sp = SSM_STATES_PER_BLOCK
    a_re, a_im, dt = rows_ref[0:1, :], rows_ref[1:2, :], jnp.exp(rows_ref[2:3, :])

    def powers(k):
        mag = jnp.exp(a_re * dt * k)
        ang = a_im * dt * k
        return mag * jnp.cos(ang), mag * jnp.sin(ang)

    n_pow = 2 * SUBLANES
    assert t_chunk + 1 <= n_pow
    pw_re, pw_im = powers(lax.broadcasted_iota(jnp.int32, (n_pow, 1), 0).astype(F32))
    pwt_re, pwt_im = pw_re.T, pw_im.T
    ab_re, ab_im = pw_re[1:2, :], pw_im[1:2, :]
    den = a_re * a_re + a_im * a_im
    q_re = ((ab_re - 1.0) * a_re + ab_im * a_im) / den
    q_im = (ab_im * a_re - (ab_re - 1.0) * a_im) / den
    bre, bim = bre_ref[...], bim_ref[...]
    bb_re = q_re * bre - q_im * bim
    bb_im = q_re * bim + q_im * bre
    cre, cim = cre_ref[...], cim_ref[...]

    m_ref[...] = jnp.zeros(m_ref.shape, m_ref.dtype)
    for k in range(t_chunk):
        pk_re, pk_im = pw_re[k:k + 1, :], pw_im[k:k + 1, :]
        bk_re = bb_re * pk_re - bb_im * pk_im
        bk_im = bb_re * pk_im + bb_im * pk_re
        t = t_chunk - 1 - k
        w_ref[t * LANES:(t + 1) * LANES, 0:sp] = bk_re.astype(w_ref.dtype)
        w_ref[t * LANES:(t + 1) * LANES, sp:2 * sp] = bk_im.astype(w_ref.dtype)
        kk = (jnp.dot(bk_re, cre, precision=lax.Precision.HIGHEST, preferred_element_type=F32)
              - jnp.dot(bk_im, cim, precision=lax.Precision.HIGHEST, preferred_element_type=F32))
        kk = kk.astype(m_ref.dtype)
        for t0 in range(t_chunk - k):
            m_ref[t0 * LANES:(t0 + 1) * LANES, (t0 + k) * LANES:(t0 + k + 1) * LANES] = kk

    for t in range(t_chunk):
        pc_re, pc_im = pwt_re[:, t + 1:t + 2], pwt_im[:, t + 1:t + 2]
        v_ref[0:sp, t * LANES:(t + 1) * LANES] = (cre * pc_re - cim * pc_im).astype(v_ref.dtype)
        v_ref[sp:2 * sp, t * LANES:(t + 1) * LANES] = (-(cre * pc_im + cim * pc_re)).astype(v_ref.dtype)
    v0_ref[0:sp, :] = cre.astype(v0_ref.dtype)
    v0_ref[sp:2 * sp, :] = (-cim).astype(v0_ref.dtype)

    a1_ref[:, 0:sp] = ab_re
    a1_ref[:, sp:2 * sp] = ab_im
    steps = (lax.broadcasted_iota(jnp.int32, (SUBLANES, 1), 0) + 1) * t_chunk
    ap_re, ap_im = powers(steps.astype(F32))
    apow_ref[:, 0:sp] = ap_re
    apow_ref[:, sp:2 * sp] = ap_im


def _ssm_prep(a_re, a_im, log_dt, b_re, b_im, c_re, c_im):
    rows, pbre, pbim, pcre, pcim = _ssm_layout(a_re, a_im, log_dt, b_re, b_im, c_re, c_im)
    nb, sp, sl, tl = SSM_LANE_BLOCKS, SSM_STATES_PER_BLOCK, SSM_STATE_LANES, SSM_T * LANES

    def blk(shape):
        return pl.BlockSpec((None,) + shape, lambda b: (b, 0, 0))

    return pl.pallas_call(
        functools.partial(_ssm_prep_kernel, t_chunk=SSM_T),
        grid=(nb,),
        in_specs=[blk((3, sp)), blk((LANES, sp)), blk((LANES, sp)), blk((sp, LANES)), blk((sp, LANES))],
        out_specs=[blk((tl, tl)), blk((tl, sl)), blk((sl, tl)), blk((sl, LANES)), blk((1, sl)), blk((SUBLANES, sl))],
        out_shape=[jax.ShapeDtypeStruct((nb, tl, tl), BF16), jax.ShapeDtypeStruct((nb, tl, sl), BF16),
                   jax.ShapeDtypeStruct((nb, sl, tl), BF16), jax.ShapeDtypeStruct((nb, sl, LANES), BF16),
                   jax.ShapeDtypeStruct((nb, 1, sl), F32), jax.ShapeDtypeStruct((nb, SUBLANES, sl), F32)],
        compiler_params=_params("parallel"),
        name="ssm_prep",
    )(rows, pbre, pbim, pcre, pcim)


def _chunk_tokens(u_ref, t, t_chunk):
    return u_ref[pl.ds(t, u_ref.shape[0] // t_chunk, stride=t_chunk), :]


def _chunk_lanes(u_ref, t_chunk):
    return jnp.concatenate([_chunk_tokens(u_ref, t, t_chunk) for t in range(t_chunk)], axis=1)


def _ssm_end_state_kernel(*refs, t_chunk):
    nb, sl = SSM_LANE_BLOCKS, SSM_STATE_LANES
    u_refs, w_ref, e_ref = refs[:nb], refs[nb], refs[nb + 1]
    for b in range(nb):
        ub = _chunk_lanes(u_refs[b], t_chunk).astype(BF16)
        e_ref[:, b * sl:(b + 1) * sl] = jnp.dot(ub, w_ref[b], preferred_element_type=F32)


def _ssm_scan_kernel(ere_ref, eim_ref, pre_ref, pim_ref, spre_ref, spim_ref, fre_ref, fim_ref):
    n_tiles = ere_ref.shape[0] // SUBLANES
    width = ere_ref.shape[1]
    p_re, p_im = pre_ref[...], pim_ref[...]
    row = lax.broadcasted_iota(jnp.int32, (SUBLANES, width), 0)

    def shift_down(x, k):
        return jnp.where(row >= k, pltpu.roll(x, k, 0), 0.0)

    def body(i, carry):
        c_re, c_im = carry
        rows = pl.ds(pl.multiple_of(i * SUBLANES, SUBLANES), SUBLANES)
        x_re, x_im = ere_ref[rows, :], eim_ref[rows, :]
        for k in (1, 2, 4):
            a_re, a_im = p_re[k - 1:k, :], p_im[k - 1:k, :]
            s_re, s_im = shift_down(x_re, k), shift_down(x_im, k)
            x_re, x_im = x_re + a_re * s_re - a_im * s_im, x_im + a_re * s_im + a_im * s_re
        t_re = x_re + p_re * c_re - p_im * c_im
        t_im = x_im + p_re * c_im + p_im * c_re
        spre_ref[rows, :] = jnp.where(row >= 1, pltpu.roll(t_re, 1, 0), c_re)
        spim_ref[rows, :] = jnp.where(row >= 1, pltpu.roll(t_im, 1, 0), c_im)
        return t_re[SUBLANES - 1:SUBLANES, :], t_im[SUBLANES - 1:SUBLANES, :]

    zero = jnp.zeros((1, width), F32)
    f_re, f_im = lax.fori_loop(0, n_tiles, body, (zero, zero))
    fre_ref[...] = f_re
    fim_ref[...] = f_im


def _ssm_output_kernel(*refs, t_chunk):
    nb, sp = SSM_LANE_BLOCKS, SSM_STATES_PER_BLOCK
    u_refs, (spre_ref, spim_ref, m_ref, v_ref, d_ref), y_refs = refs[:nb], refs[nb:nb + 5], refs[nb + 5:]
    for b in range(nb):
        ub = _chunk_lanes(u_refs[b], t_chunk).astype(BF16)
        yb = jnp.dot(ub, m_ref[b], preferred_element_type=F32)
        states = slice(b * sp, (b + 1) * sp)
        sprev = jnp.concatenate([spre_ref[:, states], spim_ref[:, states]], axis=1).astype(BF16)
        yb = yb + jnp.dot(sprev, v_ref[b], preferred_element_type=F32)
        d = d_ref[:, b * LANES:(b + 1) * LANES]
        for t in range(t_chunk):
            y_t = yb[:, t * LANES:(t + 1) * LANES] + d * _chunk_tokens(u_refs[b], t, t_chunk)
            y_refs[b][pl.ds(t, yb.shape[0], stride=t_chunk), :] = y_t


def _ssm_tiling(rows, n_seq, steps_wanted):
    assert rows % (SSM_T * n_seq) == 0
    n_chunks = rows // SSM_T // n_seq
    assert n_chunks % SUBLANES == 0
    tr = min(SSM_ROW_TILE, n_chunks)
    if steps_wanted and (n_seq * n_chunks) % steps_wanted == 0:
        want = n_seq * n_chunks // steps_wanted
        if want % SUBLANES == 0 and n_chunks % want == 0 and want <= SSM_ROW_TILE:
            tr = want
    assert n_chunks % tr == 0
    return n_chunks, tr


def _ssm_end_state_job(u, n_seq, w_mat, steps_wanted=None):
    nb, sl = SSM_LANE_BLOCKS, SSM_STATE_LANES
    n_chunks, tr = _ssm_tiling(u.shape[0], n_seq, steps_wanted)
    tiles = n_chunks // tr
    return Job(
        kernel=functools.partial(_ssm_end_state_kernel, t_chunk=SSM_T),
        args=[u] * nb + [w_mat],
        in_specs=[pl.BlockSpec((tr * SSM_T, LANES), lambda i, b=b: (i, b)) for b in range(nb)] + [_full(w_mat.shape)],
        out_specs=[pl.BlockSpec((tr, nb * sl), lambda i: (i % tiles, i // tiles))],
        out_shapes=[jax.ShapeDtypeStruct((n_chunks, n_seq * nb * sl), F32)],
        scratch=[],
        steps=n_seq * tiles,
    )


def _ssm_scan(e, apow, n_seq):
    nb, sp = SSM_LANE_BLOCKS, SSM_STATES_PER_BLOCK
    n_chunks = e.shape[0]
    col = pl.BlockSpec((n_chunks, sp), lambda g: (0, g))
    fin = pl.BlockSpec((1, sp), lambda g: (0, g))
    return pl.pallas_call(
        _ssm_scan_kernel,
        grid=(n_seq * nb,),
        in_specs=[pl.BlockSpec((n_chunks, sp), lambda g: (0, 2 * g)),
                  pl.BlockSpec((n_chunks, sp), lambda g: (0, 2 * g + 1)),
                  pl.BlockSpec((None, SUBLANES, sp), lambda g: (g % nb, 0, 0)),
                  pl.BlockSpec((None, SUBLANES, sp), lambda g: (g % nb, 0, 1))],
        out_specs=[col, col, fin, fin],
        out_shape=[jax.ShapeDtypeStruct((n_chunks, n_seq * nb * sp), F32)] * 2
        + [jax.ShapeDtypeStruct((1, n_seq * nb * sp), F32)] * 2,
        compiler_params=_params("parallel"),
        name="ssm_scan",
    )(e, e, apow, apow)


def _ssm_output_job(u, n_seq, sp_re, sp_im, m_mat, v_mat, d_skip, steps_wanted=None):
    nb, sp = SSM_LANE_BLOCKS, SSM_STATES_PER_BLOCK
    rows = u.shape[0]
    n_chunks, tr = _ssm_tiling(rows, n_seq, steps_wanted)
    tiles = n_chunks // tr
    sp_spec = pl.BlockSpec((tr, nb * sp), lambda i: (i % tiles, i // tiles))
    return Job(
        kernel=functools.partial(_ssm_output_kernel, t_chunk=SSM_T),
        args=[u] * nb + [sp_re, sp_im, m_mat, v_mat, d_skip.reshape(1, SSM_WIDTH)],
        in_specs=[pl.BlockSpec((tr * SSM_T, LANES), lambda i, b=b: (i, b)) for b in range(nb)]
        + [sp_spec, sp_spec, _full(m_mat.shape), _full(v_mat.shape), _full((1, SSM_WIDTH))],
        out_specs=[pl.BlockSpec((tr * SSM_T, LANES), lambda i: (i, 0)) for _ in range(nb)],
        out_shapes=[jax.ShapeDtypeStruct((rows, LANES), F32) for _ in range(nb)],
        scratch=[],
        steps=n_seq * tiles,
    )


def _ssm_step_kernel(u_ref, sre_ref, sim_ref, w0_ref, v0_ref, a1_ref, d_ref, y_ref, nre_ref, nim_ref):
    sp = SSM_STATES_PER_BLOCK
    for b in range(SSM_LANE_BLOCKS):
        lanes = slice(b * LANES, (b + 1) * LANES)
        states = slice(b * sp, (b + 1) * sp)
        u = u_ref[:, lanes]
        e = jnp.dot(u.astype(BF16), w0_ref[b], preferred_element_type=F32)
        a_re, a_im = a1_ref[b, :, 0:sp], a1_ref[b, :, sp:2 * sp]
        s_re, s_im = sre_ref[:, states], sim_ref[:, states]
        n_re = a_re * s_re - a_im * s_im + e[:, 0:sp]
        n_im = a_re * s_im + a_im * s_re + e[:, sp:2 * sp]
        nre_ref[:, states] = n_re
        nim_ref[:, states] = n_im
        sn = jnp.concatenate([n_re, n_im], axis=1).astype(BF16)
        y_ref[:, lanes] = jnp.dot(sn, v0_ref[b], preferred_element_type=F32) + d_ref[:, lanes] * u


def _ssm_step(u, s_re, s_im, w0, v0, a1, d_skip):
    rows = u.shape[0]
    ns = SSM_GROUPS * SSM_STATE
    args = (u, s_re.reshape(rows, ns), s_im.reshape(rows, ns), w0, v0, a1, d_skip.reshape(1, SSM_WIDTH))
    return pl.pallas_call(
        _ssm_step_kernel,
        grid=(1,),
        in_specs=[_full(a.shape) for a in args],
        out_specs=[_full((rows, SSM_WIDTH)), _full((rows, ns)), _full((rows, ns))],
        out_shape=[jax.ShapeDtypeStruct((rows, SSM_WIDTH), F32), jax.ShapeDtypeStruct((rows, ns), F32),
                   jax.ShapeDtypeStruct((rows, ns), F32)],
        compiler_params=_params("arbitrary"),
        name="ssm_step",
    )(*args)


def _dil_attn_kernel(q_ref, kc_ref, kp_ref, vc_ref, vp_ref, o_ref, lse_ref, *, dil, slopes, chunks, pairs):
    steps, hd = ATT_STEPS, ATT_HEAD_DIM
    chunk = q_ref.shape[0]
    span = steps * dil
    step = pl.program_id(0)
    first_chunk = (step // pairs) % chunks == 0
    pair = step % pairs
    qi = lax.broadcasted_iota(jnp.int32, (steps, 2 * steps), 0)
    kj = lax.broadcasted_iota(jnp.int32, (steps, 2 * steps), 1)
    dist = qi + steps - kj
    band = (dist >= 0) & (dist <= steps)
    distf = (dist * dil).astype(F32)
    lane = lax.broadcasted_iota(jnp.int32, (steps, LANES), 1)
    heads = [lane < hd, lane >= hd]
    biases, first_biases = [], []
    for hh in range(2):
        slope = sum(jnp.where(pair == p, slopes[2 * p + hh], 0.0) for p in range(pairs))
        bias = jnp.where(band, -slope * distf, NEG_BIG)
        biases.append(bias)
        first_biases.append(jnp.where(first_chunk & (kj < steps), NEG_BIG, bias))

    def rows(ref, start):
        return ref[pl.ds(start, steps, stride=dil), :] if dil > 1 else ref[pl.ds(start, steps), :]

    for r in range(dil):
        for qb in range(chunk // span):
            start = r + qb * span
            q = rows(q_ref, start) * (hd ** -0.5)
            if qb == 0:
                k_prev, v_prev = rows(kp_ref, r), rows(vp_ref, r)
            else:
                k_prev, v_prev = rows(kc_ref, start - span), rows(vc_ref, start - span)
            kk = jnp.concatenate([k_prev, rows(kc_ref, start)], axis=0).astype(BF16)
            vv = jnp.concatenate([v_prev, rows(vc_ref, start)], axis=0).astype(BF16)
            out = jnp.zeros((steps, LANES), F32)
            lse = jnp.zeros((steps, LANES), F32)
            for hh in range(2):
                qh = jnp.where(heads[hh], q, 0.0).astype(BF16)
                s = lax.dot_general(qh, kk, (((1,), (1,)), ((), ())), preferred_element_type=F32)
                s = s + (first_biases[hh] if qb == 0 else biases[hh])
                m = jnp.max(s, axis=-1, keepdims=True)
                p = jnp.exp(s - m)
                den = jnp.sum(p, axis=-1, keepdims=True)
                oh = jnp.dot(p.astype(BF16), vv, preferred_element_type=F32) / den
                out = jnp.where(heads[hh], oh, out)
                lse = jnp.where(heads[hh], m + jnp.log(den), lse)
            if dil > 1:
                o_ref[pl.ds(start, steps, stride=dil), :] = out
                lse_ref[pl.ds(start, steps, stride=dil), :] = lse
            else:
                o_ref[pl.ds(start, steps), :] = out
                lse_ref[pl.ds(start, steps), :] = lse


def _dil_attn_job(q, k, v, n_seq, seq_len, group):
    win, dil = DIL_PAIRS[group]
    steps, gw = ATT_STEPS, ATT_GROUP_WIDTH
    span = steps * dil
    chunk = ATT_CHUNK
    assert win // dil == steps and chunk % span == 0 and seq_len % chunk == 0
    chunks = seq_len // chunk
    pairs = gw // LANES
    cur = pl.BlockSpec((chunk, LANES), lambda i: (i // pairs, pairs * group + i % pairs))
    prev = pl.BlockSpec((span, LANES), lambda i: (jnp.maximum((i // pairs) * (chunk // span) - 1, 0),
                                                  pairs * group + i % pairs))
    out = pl.BlockSpec((chunk, LANES), lambda i: (i // pairs, i % pairs))
    slopes = tuple(_alibi_slopes()[group * HEADS_PER_GROUP:(group + 1) * HEADS_PER_GROUP])
    rows = n_seq * seq_len
    return Job(
        kernel=functools.partial(_dil_attn_kernel, dil=dil, slopes=slopes, chunks=chunks, pairs=pairs),
        args=[q, k, k, v, v],
        in_specs=[cur, cur, prev, cur, prev],
        out_specs=[out, out],
        out_shapes=[jax.ShapeDtypeStruct((rows, gw), F32), jax.ShapeDtypeStruct((rows, gw), F32)],
        scratch=[],
        steps=n_seq * chunks * pairs,
    )


def _mem_attention(q_ref, mk_ref, mv_ref):
    hd = MEM_HEAD_DIM
    outs = []
    for h in range(MEM_HEADS):
        lanes = slice(h * hd, (h + 1) * hd)
        s = lax.dot_general(q_ref[:, lanes].astype(BF16), mk_ref[:, lanes].astype(BF16),
                            (((1,), (1,)), ((), ())), preferred_element_type=F32) * (hd ** -0.5)
        m = jnp.max(s, axis=-1, keepdims=True)
        p = jnp.exp(s - m)
        den = jnp.sum(p, axis=-1, keepdims=True)
        outs.append(jnp.dot(p.astype(BF16), mv_ref[:, lanes].astype(BF16), preferred_element_type=F32) / den)
    return jnp.concatenate(outs, axis=1)


def _cache_attn_kernel(q_ref, kn_ref, vn_ref, k1_ref, v1_ref, k2_ref, v2_ref, k3_ref, v3_ref, o_ref,
                       qt_scr, vnt_scr, *, slopes):
    hpg, hd = HEADS_PER_GROUP, ATT_HEAD_DIM
    j = pl.program_id(0) % hpg
    nb = q_ref.shape[0]
    scale = hd ** -0.5
    q, kn = q_ref[...], kn_ref[...]
    qt_scr[...] = q.T
    vnt_scr[...] = vn_ref[...].T
    lane_head = lax.broadcasted_iota(jnp.int32, q.shape, 1) // hd
    outs, lses = [], []
    for g, (k_ref, v_ref) in enumerate(((k1_ref, v1_ref), (k2_ref, v2_ref), (k3_ref, v3_ref))):
        dil = DIL_PAIRS[g][1]
        n_pos = k_ref.shape[-1]
        rows = pl.ds(pl.multiple_of((g * hpg + j) * hd, hd), hd)
        qg, vng = qt_scr[rows, :], vnt_scr[rows, :]
        slope = sum(jnp.where(j == h, slopes[g * hpg + h], 0.0) for h in range(hpg))
        back = n_pos - lax.broadcasted_iota(jnp.int32, (1, n_pos), 1)
        bias = jnp.where(back % dil == 0, -slope * back.astype(F32), NEG_BIG)
        row = lax.broadcasted_iota(jnp.int32, (nb, n_pos), 0)
        s = jnp.zeros((nb, n_pos), F32)
        for b in range(nb):
            s = jnp.where(row == b, jnp.sum(k_ref[b] * qg[:, b:b + 1], axis=0, keepdims=True), s)
        s = s * scale + bias
        s_new = jnp.sum(jnp.where(lane_head == g * hpg + j, q * kn, 0.0), axis=1, keepdims=True) * scale
        m = jnp.maximum(jnp.max(s, axis=1, keepdims=True), s_new)
        p = jnp.exp(s - m)
        p_new = jnp.exp(s_new - m)
        den = jnp.sum(p, axis=1, keepdims=True) + p_new
        cols = []
        for b in range(nb):
            acc = jnp.sum(v_ref[b] * p[b:b + 1, :], axis=1, keepdims=True) + p_new[b:b + 1, :] * vng[:, b:b + 1]
            cols.append(acc / den[b:b + 1, :])
        outs.append(cols)
        lses.append(m + jnp.log(den))
    top = functools.reduce(jnp.maximum, lses)
    ws = [jnp.exp(l - top) for l in lses]
    total = sum(ws)
    o_ref[...] = jnp.concatenate(
        [sum(w[b:b + 1, :] * cols[b] for w, cols in zip(ws, outs)) / total[b:b + 1, :] for b in range(nb)], axis=1)


def _cache_attn_job(q, k_new, v_new, caches_k, caches_v, nb):
    rows = q.shape[0]
    hpg, hd = HEADS_PER_GROUP, ATT_HEAD_DIM
    assert rows % nb == 0 and nb % SUBLANES == 0
    new = pl.BlockSpec((nb, ATT_WIDTH), lambda i: (i // hpg, 0))
    specs, args = [new, new, new], [q, k_new, v_new]
    for g, (win, dil) in enumerate(DIL_PAIRS):
        for c in (caches_k[g], caches_v[g]):
            assert c.shape == (rows, win, hpg, hd) and win % dil == 0, c.shape
            args.append(c.transpose(0, 2, 3, 1))
            specs.append(pl.BlockSpec((nb, None, hd, win), lambda i: (i // hpg, i % hpg, 0, 0)))
    return Job(
        kernel=functools.partial(_cache_attn_kernel, slopes=tuple(_alibi_slopes())),
        args=args,
        in_specs=specs,
        out_specs=[pl.BlockSpec((None, None, hd, nb), lambda i: (i % hpg, i // hpg, 0, 0))],
        out_shapes=[jax.ShapeDtypeStruct((hpg, rows // nb, hd, nb), F32)],
        scratch=[pltpu.VMEM((ATT_WIDTH, nb), F32) for _ in range(2)],
        steps=(rows // nb) * hpg,
    )


def _cache_attn_result(out):
    hpg, blocks, hd, nb = out.shape
    return out.transpose(1, 3, 0, 2).reshape(blocks * nb, hpg * hd)


def _cache_mem_attn_kernel(q_ref, k_ref, v_ref, o_ref):
    nb, rows, hd = k_ref.shape
    tiles = rows // SUBLANES

    def fold(x):
        return x, pltpu.roll(x, MEM_HEADS, 2)

    q = q_ref[...][:, None]
    k = k_ref[...].reshape(nb, tiles, SUBLANES, hd)
    v = v_ref[...].reshape(nb, tiles, SUBLANES, hd)
    s = jnp.sum(k * q, axis=-1, keepdims=True) * (MEM_HEAD_DIM ** -0.5)
    m = jnp.maximum(*fold(jnp.max(s, axis=1, keepdims=True)))
    p = jnp.exp(s - m)
    den = sum(fold(jnp.sum(p, axis=1, keepdims=True)))
    acc = sum(fold(jnp.sum(p * v, axis=1, keepdims=True)))
    o_ref[...] = (acc / den)[:, 0]


def _cache_mem_job(qm, mem_k, mem_v, nb):
    rows = qm.shape[0]
    assert rows % nb == 0 and SUBLANES == 2 * MEM_HEADS
    q4 = qm.reshape(rows, MEM_HEADS, MEM_HEAD_DIM)
    kv = pl.BlockSpec((nb, N_MEM * MEM_HEADS, MEM_HEAD_DIM), lambda i: (i, 0, 0))
    q = pl.BlockSpec((nb, SUBLANES, MEM_HEAD_DIM), lambda i: (i, 0, 0))
    return Job(
        kernel=_cache_mem_attn_kernel,
        args=[jnp.concatenate([q4, q4], axis=1), mem_k.reshape(rows, N_MEM * MEM_HEADS, MEM_HEAD_DIM),
              mem_v.reshape(rows, N_MEM * MEM_HEADS, MEM_HEAD_DIM)],
        in_specs=[q, kv, kv],
        out_specs=[q],
        out_shapes=[jax.ShapeDtypeStruct((rows, SUBLANES, MEM_HEAD_DIM), F32)],
        scratch=[],
        steps=rows // nb,
    )


def _cache_mem_result(out):
    return out[:, :MEM_HEADS].reshape(out.shape[0], MEM_WIDTH)


def _merge_kernel(x_ref, *refs, n_y, n_att, n_mem):
    y_refs, att_refs = refs[:n_y], refs[n_y:n_y + n_att]
    mem_refs = refs[n_y + n_att:n_y + n_att + n_mem]
    g1_ref, wgate_ref, wglu_ref, watt_ref, wmem_ref, wout_ref, o_ref = refs[n_y + n_att + n_mem:]
    d = D_MODEL
    x = x_ref[...]
    h = _rmsnorm(x, g1_ref[...]).astype(BF16)

    def gate(i):
        return jax.nn.sigmoid(jnp.dot(h, wgate_ref[:, i * d:(i + 1) * d], preferred_element_type=F32))

    y = jnp.concatenate([r[...] for r in y_refs], axis=1) if n_y > 1 else y_refs[0][...]
    z = jax.nn.gelu(y).astype(BF16)
    glu = jnp.dot(z, wglu_ref[...], preferred_element_type=F32)
    merged = gate(0) * (glu[:, 0:d] * jax.nn.sigmoid(glu[:, d:2 * d]))
    if n_att == 1:
        att = att_refs[0][...]
    else:
        lses = [r[...] for r in att_refs[1::2]]
        top = functools.reduce(jnp.maximum, lses)
        ws = [jnp.exp(l - top) for l in lses]
        att = sum(w * r[...] for w, r in zip(ws, att_refs[0::2])) / sum(ws)
    b_att = jnp.dot(att.astype(BF16), watt_ref[...], preferred_element_type=F32)
    merged = merged + gate(1) * b_att
    o_mem = mem_refs[0][...] if n_mem == 1 else _mem_attention(*mem_refs)
    b_mem = jnp.dot(o_mem.astype(BF16), wmem_ref[...], preferred_element_type=F32)
    merged = merged + gate(2) * b_mem
    o_ref[...] = x + jnp.dot(merged.astype(BF16), wout_ref[...], preferred_element_type=F32)


def _merge_job(x, ys, atts, mems, g1, w_in, w_glu, w_att_o, w_mem_o, w_out, tm):
    rows = x.shape[0]
    gate_cols = N_BRANCH * D_MODEL
    assert rows % tm == 0 and w_in.shape == (D_MODEL, IN_WIDTH)

    def tile(a):
        return pl.BlockSpec((tm, a.shape[1]), lambda i: (i, 0))

    acts = [x, *ys, *atts, mems[0]]
    act_specs = [tile(a) for a in acts]
    if len(mems) > 1:
        tiles_per_seq = rows // mems[1].shape[0] // tm
        assert tiles_per_seq * tm * mems[1].shape[0] == rows
        act_specs += [pl.BlockSpec((None, N_MEM, MEM_WIDTH), lambda i: (i // tiles_per_seq, 0, 0))] * 2
        acts += list(mems[1:])
    weights = [w_glu, w_att_o, w_mem_o, w_out]
    return Job(
        kernel=functools.partial(_merge_kernel, n_y=len(ys), n_att=len(atts), n_mem=len(mems)),
        args=acts + [g1.reshape(1, D_MODEL), w_in] + weights,
        in_specs=act_specs + [_full((1, D_MODEL)), _resident_cols(w_in, IN_WIDTH - gate_cols, gate_cols)]
        + [_full(w.shape) for w in weights],
        out_specs=[tile(x)],
        out_shapes=[jax.ShapeDtypeStruct(x.shape, F32)],
        scratch=[],
        steps=rows // tm,
    )


def _ffn_kernel(*refs, tm, tiles_per_seq, stepwise):
    if stepwise:
        x_ref, g2_ref, wup_ref, cw_ref, cb_ref, wdn_ref, gf_ref, prev_ref, y_ref, conv_ref = refs
    else:
        x_ref, g2_ref, wup_ref, cw_ref, cb_ref, wdn_ref, gf_ref, y_ref, conv_ref, a_scr = refs
        i = pl.program_id(0)
        first = i % tiles_per_seq == 0

        @pl.when(first)
        def _():
            a_scr[0:SUBLANES, :] = jnp.zeros((SUBLANES, D_FF), F32)

        @pl.when(jnp.logical_not(first))
        def _():
            a_scr[0:SUBLANES, :] = a_scr[tm:tm + SUBLANES, :]

    x = x_ref[...]
    h = _rmsnorm(x, g2_ref[...]).astype(BF16)
    a = jnp.dot(h, wup_ref[:, 0:D_FF], preferred_element_type=F32)
    if stepwise:
        a2, a1 = prev_ref[:, 0:D_FF], prev_ref[:, D_FF:2 * D_FF]
        conv_ref[:, 0:D_FF] = a1
        conv_ref[:, D_FF:2 * D_FF] = a
    else:
        a_scr[SUBLANES:SUBLANES + tm, :] = a
        a1 = a_scr[SUBLANES - 1:SUBLANES - 1 + tm, :]
        a2 = a_scr[SUBLANES - 2:SUBLANES - 2 + tm, :]
    c = a2 * cw_ref[0:1, :] + a1 * cw_ref[1:2, :] + a * cw_ref[2:3, :] + cb_ref[...]
    v = jnp.dot(h, wup_ref[:, D_FF:2 * D_FF], preferred_element_type=F32)
    y = jnp.dot((jax.nn.gelu(c) * v).astype(BF16), wdn_ref[...], preferred_element_type=F32)
    y_ref[...] = _rmsnorm(x + y, gf_ref[...])
    if not stepwise:
        @pl.when(i % tiles_per_seq == tiles_per_seq - 1)
        def _():
            conv_ref[...] = a_scr[SUBLANES + tm - (CONV_W - 1):SUBLANES + tm, :]


def _ffn_job(x, g2, w_up, conv_w, conv_b, w_down, gf, tm, n_seq=None, prev=None):
    rows, d = x.shape
    assert rows % tm == 0
    stepwise = prev is not None
    weights = [g2.reshape(1, d), w_up, conv_w, conv_b.reshape(1, D_FF), w_down, gf.reshape(1, d)]
    in_specs = [pl.BlockSpec((tm, d), lambda i: (i, 0))] + [_full(w.shape) for w in weights]
    args = [x] + weights
    if stepwise:
        tiles_per_seq = 1
        in_specs.append(pl.BlockSpec((tm, 2 * D_FF), lambda i: (i, 0)))
        args.append(prev)
        conv_spec = pl.BlockSpec((tm, 2 * D_FF), lambda i: (i, 0))
        conv_shape = jax.ShapeDtypeStruct((rows, 2 * D_FF), F32)
        scratch = []
    else:
        tiles_per_seq = rows // n_seq // tm
        assert tiles_per_seq * tm * n_seq == rows
        conv_spec = pl.BlockSpec((None, CONV_W - 1, D_FF), lambda i: (i // tiles_per_seq, 0, 0))
        conv_shape = jax.ShapeDtypeStruct((n_seq, CONV_W - 1, D_FF), F32)
        scratch = [pltpu.VMEM((tm + 2 * SUBLANES, D_FF), F32)]
    return Job(
        kernel=functools.partial(_ffn_kernel, tm=tm, tiles_per_seq=tiles_per_seq, stepwise=stepwise),
        args=args,
        in_specs=in_specs,
        out_specs=[pl.BlockSpec((tm, d), lambda i: (i, 0)), conv_spec],
        out_shapes=[jax.ShapeDtypeStruct((rows, d), F32), conv_shape],
        scratch=scratch,
        steps=rows // tm,
    )


def kernel(x_prompt, x_sample, state_ssm_re, state_ssm_im, cache_w1_k, cache_w1_v, cache_w2_k, cache_w2_v, cache_w3_k, cache_w3_v, cache_mem_k, cache_mem_v, state_ffn_conv, mem_prompt, norm1_g, w_in, ssm_a_re, ssm_a_im, ssm_log_dt, ssm_b_re, ssm_b_im, ssm_c_re, ssm_c_im, ssm_d, w_ssm_glu, w_att_o, mem_norm_g, w_mem_kv, w_mem_o, w_out, norm2_g, w_up, ffn_conv_w, ffn_conv_b, w_down, final_norm_g):
    n_seq, seq_len, d = x_prompt.shape
    n_dec, dec_len, _ = x_sample.shape
    depth = norm1_g.shape[0]
    assert d == D_MODEL and depth == 1 and dec_len == 1
    assert w_in.shape == (depth, D_MODEL, IN_WIDTH) and w_up.shape == (depth, D_MODEL, 2 * D_FF)
    assert mem_prompt.shape == (n_seq, N_MEM, D_MODEL)
    assert ssm_a_re.shape == (depth, SSM_GROUPS, SSM_STATE)
    assert seq_len % ROW_TILE == 0 and n_dec % SUBLANES == 0
    hpg, hd = HEADS_PER_GROUP, ATT_HEAD_DIM
    rows_p = n_seq * seq_len
    dec_tile = n_dec if n_dec <= ROW_TILE else ROW_TILE

    wb = lambda w: w[0].astype(BF16)
    w_in_b, w_glu_b, w_att_b, w_memkv_b = wb(w_in), wb(w_ssm_glu), wb(w_att_o), wb(w_mem_kv)
    w_memo_b, w_out_b, w_up_b, w_down_b = wb(w_mem_o), wb(w_out), wb(w_up), wb(w_down)
    merge_w = (norm1_g[0], w_in_b, w_glu_b, w_att_b, w_memo_b, w_out_b)
    ffn_w = (norm2_g[0], w_up_b, ffn_conv_w[0], ffn_conv_b[0], w_down_b, final_norm_g)

    m_mat, w_mat, v_mat, v0_mat, a1, apow = _ssm_prep(ssm_a_re[0], ssm_a_im[0], ssm_log_dt[0], ssm_b_re[0],
                                                      ssm_b_im[0], ssm_c_re[0], ssm_c_im[0])
    w0_mat = w_mat[:, (SSM_T - 1) * LANES:, :]

    xs = x_sample.reshape(n_dec, d)
    us, qs, ks, vs, qms = _run([_norm_proj_job(xs, norm1_g[0], w_in_b, PROJ_SPLITS, dec_tile)], "norm_proj")[0]
    cache_job = _cache_attn_job(qs, ks, vs, (cache_w1_k[0], cache_w2_k[0], cache_w3_k[0]),
                                (cache_w1_v[0], cache_w2_v[0], cache_w3_v[0]), CACHE_ROWS_PER_STEP)
    proj_steps = rows_p // ROW_TILE
    mem_rows = n_dec // proj_steps if n_dec % proj_steps == 0 else CACHE_ROWS_PER_STEP
    mem_job = _cache_mem_job(qms, cache_mem_k[0], cache_mem_v[0], mem_rows)

    xp = x_prompt.reshape(rows_p, d)
    gw = ATT_GROUP_WIDTH
    keeps = [min(win, seq_len) for win, _ in DIL_PAIRS]
    windows = [(src, g * gw, gw, keep) for g, keep in enumerate(keeps) for src in (2, 3)]
    proj_job = _norm_proj_job(xp, norm1_g[0], w_in_b, PROJ_SPLITS, ROW_TILE, n_seq, windows)
    (u, q, k, v, qm, *kv_win), (mem_out,) = _run_pair(proj_job, mem_job, "norm_proj_cache_mem")
    mem_s = _cache_mem_result(mem_out)

    dil_jobs = [_dil_attn_job(q, k, v, n_seq, seq_len, g) for g in range(len(DIL_PAIRS))]
    (e,), att0 = _run_pair(_ssm_end_state_job(u, n_seq, w_mat, dil_jobs[0].steps), dil_jobs[0], "ssm_end_state_dil")
    sp_re, sp_im, fin_re, fin_im = _ssm_scan(e, apow, n_seq)
    y_ssm, att1 = _run_pair(_ssm_output_job(u, n_seq, sp_re, sp_im, m_mat, v_mat, ssm_d[0], dil_jobs[1].steps),
                            dil_jobs[1], "ssm_output_dil")
    att2 = _run([dil_jobs[2]], "dil_attn")[0]
    atts = att0 + att1 + att2

    mk, mv = _run([_norm_proj_job(mem_prompt.reshape(n_seq * N_MEM, d), mem_norm_g[0], w_memkv_b,
                                  (MEM_WIDTH, MEM_WIDTH), ROW_TILE)], "mem_kv")[0]
    mems = [qm, mk.reshape(n_seq, N_MEM, MEM_WIDTH), mv.reshape(n_seq, N_MEM, MEM_WIDTH)]
    merge_tile = rows_p // cache_job.steps
    if not (rows_p % cache_job.steps == 0 and merge_tile % LANES == 0 and merge_tile <= ROW_TILE
            and seq_len % merge_tile == 0):
        merge_tile = ROW_TILE
    (x1,), (att_out,) = _run_pair(_merge_job(xp, y_ssm, atts, mems, *merge_w, merge_tile), cache_job,
                                  "merge_cache_attn")
    att_s = _cache_attn_result(att_out)
    y_p, conv_p = _run([_ffn_job(x1, *ffn_w, ROW_TILE, n_seq=n_seq)], "ffn")[0]

    def final_state(s):
        return s.reshape(1, n_seq, SSM_GROUPS, SSM_STATE)
    win_p = [t.reshape(n_seq, hpg, hd, t.shape[-1]).transpose(0, 3, 1, 2)[None] for t in kv_win]
    mem_kv = [mk.reshape(1, n_seq, N_MEM, MEM_HEADS, MEM_HEAD_DIM), mv.reshape(1, n_seq, N_MEM, MEM_HEADS, MEM_HEAD_DIM)]

    ys_ssm, sn_re, sn_im = _ssm_step(us, state_ssm_re[0], state_ssm_im[0], w0_mat, v0_mat, a1, ssm_d[0])
    n_g = len(DIL_PAIRS)
    ks4, vs4 = (t.reshape(n_dec, n_g, hpg, hd) for t in (ks, vs))
    xs1 = _run([_merge_job(xs, [ys_ssm], [att_s], [mem_s], *merge_w, dec_tile)], "merge")[0][0]
    y_s, conv_s = _run([_ffn_job(xs1, *ffn_w, dec_tile,
                                 prev=state_ffn_conv[0].reshape(n_dec, (CONV_W - 1) * D_FF))], "ffn_step")[0]

    win_s = []
    for g in range(len(DIL_PAIRS)):
        win_s += [ks4[None, :, g:g + 1], vs4[None, :, g:g + 1]]

    return (y_p.reshape(n_seq, seq_len, d), y_s.reshape(n_dec, 1, d),
            final_state(fin_re), final_state(fin_im), *win_p, *mem_kv, conv_p[None],
            sn_re.reshape(1, n_dec, SSM_GROUPS, SSM_STATE), sn_im.reshape(1, n_dec, SSM_GROUPS, SSM_STATE),
            *win_s, conv_s.reshape(1, n_dec, CONV_W - 1, D_FF))
```

```python
import collections
import functools

import jax
import jax.numpy as jnp
from jax import lax
from jax.experimental import pallas as pl
from jax.experimental.pallas import tpu as pltpu

F32 = jnp.float32
BF16 = jnp.bfloat16

D_MODEL = 1024
SSM_WIDTH = 512
SSM_GROUP = 16
SSM_GROUPS = 32
SSM_STATE = 64
ATT_HEAD_DIM = 64
HEADS_PER_GROUP = 4
DIL_PAIRS = ((128, 1), (512, 4), (2048, 16))
ATT_HEADS = len(DIL_PAIRS) * HEADS_PER_GROUP
ATT_WIDTH = ATT_HEADS * ATT_HEAD_DIM
ATT_GROUP_WIDTH = HEADS_PER_GROUP * ATT_HEAD_DIM
ATT_STEPS = 128
N_MEM = 256
MEM_HEADS = 4
MEM_HEAD_DIM = 128
MEM_WIDTH = MEM_HEADS * MEM_HEAD_DIM
N_BRANCH = 3
D_FF = 2816
CONV_W = 3
EPS = 1e-6
PROJ_SPLITS = (SSM_WIDTH, ATT_WIDTH, ATT_WIDTH, ATT_WIDTH, MEM_WIDTH)
IN_WIDTH = sum(PROJ_SPLITS) + N_BRANCH * D_MODEL

LANES = 128
SUBLANES = 8
VMEM_LIMIT_BYTES = 56 * 1024 * 1024

SSM_T = 8
SSM_LANE_BLOCKS = SSM_WIDTH // LANES
SSM_GROUPS_PER_BLOCK = LANES // SSM_GROUP
SSM_STATES_PER_BLOCK = SSM_GROUPS_PER_BLOCK * SSM_STATE
SSM_STATE_LANES = 2 * SSM_STATES_PER_BLOCK

ROW_TILE = 512
SSM_ROW_TILE = 256
CACHE_ROWS_PER_STEP = 8
ATT_CHUNK = 2048
NEG_BIG = -1e30


def _alibi_slopes():
    return [float(2.0 ** (-8.0 * h / ATT_HEADS)) for h in range(1, ATT_HEADS + 1)]


def _params(*sem):
    return pltpu.CompilerParams(dimension_semantics=sem, vmem_limit_bytes=VMEM_LIMIT_BYTES)


def _rmsnorm(x, g):
    ms = jnp.mean(x * x, axis=-1, keepdims=True)
    return x * lax.rsqrt(ms + EPS) * g


def _full(shape):
    nd = len(shape)
    return pl.BlockSpec(shape, lambda *_: (0,) * nd, pipeline_mode=pl.Buffered(1))


def _resident_cols(w, start, width):
    return pl.BlockSpec((pl.Element(w.shape[0]), pl.Element(width)), lambda *_: (0, start),
                        pipeline_mode=pl.Buffered(1))


Job = collections.namedtuple("Job", "kernel args in_specs out_specs out_shapes scratch steps")


def _jobs_kernel(*refs, kernels, layout):
    n_in, n_out = sum(l[0] for l in layout), sum(l[1] for l in layout)
    ins, outs, scratch = refs[:n_in], refs[n_in:n_in + n_out], refs[n_in + n_out:]
    i0 = o0 = s0 = 0
    for kernel, (ni, no, ns) in zip(kernels, layout):
        kernel(*ins[i0:i0 + ni], *outs[o0:o0 + no], *scratch[s0:s0 + ns])
        i0, o0, s0 = i0 + ni, o0 + no, s0 + ns


def _run(jobs, name):
    steps = jobs[0].steps
    assert all(j.steps == steps for j in jobs)
    layout = tuple((len(j.args), len(j.out_shapes), len(j.scratch)) for j in jobs)
    outs = pl.pallas_call(
        functools.partial(_jobs_kernel, kernels=tuple(j.kernel for j in jobs), layout=layout),
        grid=(steps,),
        in_specs=[s for j in jobs for s in j.in_specs],
        out_specs=[s for j in jobs for s in j.out_specs],
        out_shape=[s for j in jobs for s in j.out_shapes],
        scratch_shapes=[s for j in jobs for s in j.scratch],
        compiler_params=_params("arbitrary"),
        name=name,
    )(*[a for j in jobs for a in j.args])
    split, o0 = [], 0
    for _, no, _ in layout:
        split.append(list(outs[o0:o0 + no]))
        o0 += no
    return split


def _run_pair(main, rider, name):
    m, (r,) = _run_with(main, [rider], name)
    return m, r


def _run_with(main, riders, name):
    fused = [r for r in riders if r.steps == main.steps]
    outs = _run(fused + [main], name)
    by_rider = {id(r): o for r, o in zip(fused, outs)}
    return outs[-1], [by_rider[id(r)] if id(r) in by_rider else _run([r], name + "_rider")[0] for r in riders]


def _cast_kernel(*refs):
    n = len(refs) // 2
    for src, dst in zip(refs[:n], refs[n:]):
        dst[...] = src[...].astype(dst.dtype)


def _cast_job(weights, steps):
    specs = []
    for w in weights:
        rows = w.shape[0] // steps
        assert rows * steps == w.shape[0] and rows % (2 * SUBLANES) == 0, (w.shape, steps)
        specs.append(pl.BlockSpec((rows, w.shape[1]), lambda i: (i, 0)))
    return Job(kernel=_cast_kernel, args=list(weights), in_specs=specs, out_specs=specs,
               out_shapes=[jax.ShapeDtypeStruct(w.shape, BF16) for w in weights], scratch=[], steps=steps)


def _norm_proj_kernel(x_ref, g_ref, w_ref, *out_refs, splits, windows, tiles_per_seq):
    tm = x_ref.shape[0]
    h = _rmsnorm(x_ref[...], g_ref[...]).astype(BF16)
    off = 0
    for o_ref, width in zip(out_refs, splits):
        for c0 in range(0, width, 512):
            cw = min(512, width - c0)
            o_ref[:, c0:c0 + cw] = jnp.dot(h, w_ref[:, off + c0:off + c0 + cw], preferred_element_type=F32)
        off += width
    tile = pl.program_id(0) % tiles_per_seq
    for win_ref, (src, col0, cols, win) in zip(out_refs[len(splits):], windows):
        rows = min(win, tm)

        @pl.when(tile >= tiles_per_seq - max(win // tm, 1))
        def _(win_ref=win_ref, src=src, col0=col0, cols=cols, rows=rows):
            win_ref[...] = out_refs[src][tm - rows:tm, col0:col0 + cols].T


def _norm_proj_job(x, g, w_bf16, splits, tm, n_seq=1, windows=()):
    rows, d = x.shape
    assert rows % (tm * n_seq) == 0 and sum(splits) <= w_bf16.shape[1]
    tps = rows // n_seq // tm
    win_specs, win_shapes = [], []
    for _, _, cols, win in windows:
        assert (win % tm == 0 or tm % win == 0) and win <= tps * tm
        n_tiles = max(win // tm, 1)
        win_specs.append(pl.BlockSpec((None, cols, min(win, tm)), lambda i, n_tiles=n_tiles: (
            i // tps, 0, jnp.maximum(i % tps - (tps - n_tiles), 0))))
        win_shapes.append(jax.ShapeDtypeStruct((n_seq, cols, win), F32))
    return Job(
        kernel=functools.partial(_norm_proj_kernel, splits=splits, windows=tuple(windows), tiles_per_seq=tps),
        args=[x, g.reshape(1, d), w_bf16],
        in_specs=[pl.BlockSpec((tm, d), lambda i: (i, 0)), _full((1, d)), _resident_cols(w_bf16, 0, sum(splits))],
        out_specs=[pl.BlockSpec((tm, s), lambda i: (i, 0)) for s in splits] + win_specs,
        out_shapes=[jax.ShapeDtypeStruct((rows, s), F32) for s in splits] + win_shapes,
        scratch=[],
        steps=rows // tm,
    )


def _ssm_layout(a_re, a_im, log_dt, b_re, b_im, c_re, c_im):
    nb, gpb, p, c = SSM_LANE_BLOCKS, SSM_GROUPS_PER_BLOCK, SSM_STATE, SSM_GROUP
    rows = jnp.stack([a_re.reshape(nb, gpb * p), a_im.reshape(nb, gpb * p),
                      jnp.repeat(log_dt, p).reshape(nb, gpb * p)], axis=1)
    eye = jnp.eye(gpb, dtype=F32)

    def place_b(b):
        return jnp.einsum('bgpc,gh->bgchp', b.reshape(nb, gpb, p, c), eye).reshape(nb, gpb * c, gpb * p)

    def place_c(m):
        return jnp.einsum('bgcp,gh->bhpgc', m.reshape(nb, gpb, c, p), eye).reshape(nb, gpb * p, gpb * c)

    return rows, place_b(b_re), place_b(b_im), place_c(c_re), place_c(c_im)


def _ssm_prep_kernel(rows_ref, bre_ref, bim_ref, cre_ref, cim_ref,
                     m_ref, w_ref, v_ref, v0_ref, a1_ref, apow_ref, *, t_chunk):
    sp = SSM_STATES_PER_BLOCK
    a_re, a_im, dt = rows_ref[0:1, :], rows_ref[1:2, :], jnp.exp(rows_ref[2:3, :])

    def powers(k):
        mag = jnp.exp(a_re * dt * k)
        ang = a_im * dt * k
        return mag * jnp.cos(ang), mag * jnp.sin(ang)

    n_pow = 2 * SUBLANES
    assert t_chunk + 1 <= n_pow
    pw_re, pw_im = powers(lax.broadcasted_iota(jnp.int32, (n_pow, 1), 0).astype(F32))
    pwt_re, pwt_im = pw_re.T, pw_im.T
    ab_re, ab_im = pw_re[1:2, :], pw_im[1:2, :]
    den = a_re * a_re + a_im * a_im
    q_re = ((ab_re - 1.0) * a_re + ab_im * a_im) / den
    q_im = (ab_im * a_re - (ab_re - 1.0) * a_im) / den
    bre, bim = bre_ref[...], bim_ref[...]
    bb_re = q_re * bre - q_im * bim
    bb_im = q_re * bim + q_im * bre
    cre, cim = cre_ref[...], cim_ref[...]

    m_ref[...] = jnp.zeros(m_ref.shape, m_ref.dtype)
    for k in range(t_chunk):
        pk_re, pk_im = pw_re[k:k + 1, :], pw_im[k:k + 1, :]
        bk_re = bb_re * pk_re - bb_im * pk_im
        bk_im = bb_re * pk_im + bb_im * pk_re
        t = t_chunk - 1 - k
        w_ref[t * LANES:(t + 1) * LANES, 0:sp] = bk_re.astype(w_ref.dtype)
        w_ref[t * LANES:(t + 1) * LANES, sp:2 * sp] = bk_im.astype(w_ref.dtype)
        kk = (jnp.dot(bk_re, cre, precision=lax.Precision.HIGHEST, preferred_element_type=F32)
              - jnp.dot(bk_im, cim, precision=lax.Precision.HIGHEST, preferred_element_type=F32))
        kk = kk.astype(m_ref.dtype)
        for t0 in range(t_chunk - k):
            m_ref[t0 * LANES:(t0 + 1) * LANES, (t0 + k) * LANES:(t0 + k + 1) * LANES] = kk

    for t in range(t_chunk):
        pc_re, pc_im = pwt_re[:, t + 1:t + 2], pwt_im[:, t + 1:t + 2]
        v_ref[0:sp, t * LANES:(t + 1) * LANES] = (cre * pc_re - cim * pc_im).astype(v_ref.dtype)
        v_ref[sp:2 * sp, t * LANES:(t + 1) * LANES] = (-(cre * pc_im + cim * pc_re)).astype(v_ref.dtype)
    v0_ref[0:sp, :] = cre.astype(v0_ref.dtype)
    v0_ref[sp:2 * sp, :] = (-cim).astype(v0_ref.dtype)

    a1_ref[:, 0:sp] = ab_re
    a1_ref[:, sp:2 * sp] = ab_im
    steps = (lax.broadcasted_iota(jnp.int32, (SUBLANES, 1), 0) + 1) * t_chunk
    ap_re, ap_im = powers(steps.astype(F32))
    apow_ref[:, 0:sp] = ap_re
    apow_ref[:, sp:2 * sp] = ap_im


def _ssm_prep(a_re, a_im, log_dt, b_re, b_im, c_re, c_im):
    rows, pbre, pbim, pcre, pcim = _ssm_layout(a_re, a_im, log_dt, b_re, b_im, c_re, c_im)
    nb, sp, sl, tl = SSM_LANE_BLOCKS, SSM_STATES_PER_BLOCK, SSM_STATE_LANES, SSM_T * LANES

    def blk(shape):
        return pl.BlockSpec((None,) + shape, lambda b: (b, 0, 0))

    return pl.pallas_call(
        functools.partial(_ssm_prep_kernel, t_chunk=SSM_T),
        grid=(nb,),
        in_specs=[blk((3, sp)), blk((LANES, sp)), blk((LANES, sp)), blk((sp, LANES)), blk((sp, LANES))],
        out_specs=[blk((tl, tl)), blk((tl, sl)), blk((sl, tl)), blk((sl, LANES)), blk((1, sl)), blk((SUBLANES, sl))],
        out_shape=[jax.ShapeDtypeStruct((nb, tl, tl), BF16), jax.ShapeDtypeStruct((nb, tl, sl), BF16),
                   jax.ShapeDtypeStruct((nb, sl, tl), BF16), jax.ShapeDtypeStruct((nb, sl, LANES), BF16),
                   jax.ShapeDtypeStruct((nb, 1, sl), F32), jax.ShapeDtypeStruct((nb, SUBLANES, sl), F32)],
        compiler_params=_params("parallel"),
        name="ssm_prep",
    )(rows, pbre, pbim, pcre, pcim)


def _chunk_tokens(u_ref, t, t_chunk):
    return u_ref[pl.ds(t, u_ref.shape[0] // t_chunk, stride=t_chunk), :]


def _chunk_lanes(u_ref, t_chunk):
    return jnp.concatenate([_chunk_tokens(u_ref, t, t_chunk) for t in range(t_chunk)], axis=1)


def _ssm_end_state_kernel(*refs, t_chunk):
    nb, sl = SSM_LANE_BLOCKS, SSM_STATE_LANES
    u_refs, w_ref, e_ref = refs[:nb], refs[nb], refs[nb + 1]
    for b in range(nb):
        ub = _chunk_lanes(u_refs[b], t_chunk).astype(BF16)
        e_ref[:, b * sl:(b + 1) * sl] = jnp.dot(ub, w_ref[b], preferred_element_type=F32)


def _ssm_scan_kernel(ere_ref, eim_ref, pre_ref, pim_ref, spre_ref, spim_ref, fre_ref, fim_ref):
    n_tiles = ere_ref.shape[0] // SUBLANES
    width = ere_ref.shape[1]
    p_re, p_im = pre_ref[...], pim_ref[...]
    row = lax.broadcasted_iota(jnp.int32, (SUBLANES, width), 0)

    def shift_down(x, k):
        return jnp.where(row >= k, pltpu.roll(x, k, 0), 0.0)

    def body(i, carry):
        c_re, c_im = carry
        rows = pl.ds(pl.multiple_of(i * SUBLANES, SUBLANES), SUBLANES)
        x_re, x_im = ere_ref[rows, :], eim_ref[rows, :]
        for k in (1, 2, 4):
            a_re, a_im = p_re[k - 1:k, :], p_im[k - 1:k, :]
            s_re, s_im = shift_down(x_re, k), shift_down(x_im, k)
            x_re, x_im = x_re + a_re * s_re - a_im * s_im, x_im + a_re * s_im + a_im * s_re
        t_re = x_re + p_re * c_re - p_im * c_im
        t_im = x_im + p_re * c_im + p_im * c_re
        spre_ref[rows, :] = jnp.where(row >= 1, pltpu.roll(t_re, 1, 0), c_re)
        spim_ref[rows, :] = jnp.where(row >= 1, pltpu.roll(t_im, 1, 0), c_im)
        return t_re[SUBLANES - 1:SUBLANES, :], t_im[SUBLANES - 1:SUBLANES, :]

    zero = jnp.zeros((1, width), F32)
    f_re, f_im = lax.fori_loop(0, n_tiles, body, (zero, zero))
    fre_ref[...] = f_re
    fim_ref[...] = f_im


def _ssm_output_kernel(*refs, t_chunk):
    nb, sp = SSM_LANE_BLOCKS, SSM_STATES_PER_BLOCK
    u_refs, (spre_ref, spim_ref, m_ref, v_ref, d_ref), y_refs = refs[:nb], refs[nb:nb + 5], refs[nb + 5:]
    for b in range(nb):
        ub = _chunk_lanes(u_refs[b], t_chunk).astype(BF16)
        yb = jnp.dot(ub, m_ref[b], preferred_element_type=F32)
        states = slice(b * sp, (b + 1) * sp)
        sprev = jnp.concatenate([spre_ref[:, states], spim_ref[:, states]], axis=1).astype(BF16)
        yb = yb + jnp.dot(sprev, v_ref[b], preferred_element_type=F32)
        d = d_ref[:, b * LANES:(b + 1) * LANES]
        for t in range(t_chunk):
            y_t = yb[:, t * LANES:(t + 1) * LANES] + d * _chunk_tokens(u_refs[b], t, t_chunk)
            y_refs[b][pl.ds(t, yb.shape[0], stride=t_chunk), :] = y_t


def _ssm_tiling(rows, n_seq, steps_wanted):
    assert rows % (SSM_T * n_seq) == 0
    n_chunks = rows // SSM_T // n_seq
    assert n_chunks % SUBLANES == 0
    tr = min(SSM_ROW_TILE, n_chunks)
    if steps_wanted and (n_seq * n_chunks) % steps_wanted == 0:
        want = n_seq * n_chunks // steps_wanted
        if want % SUBLANES == 0 and n_chunks % want == 0 and want <= SSM_ROW_TILE:
            tr = want
    assert n_chunks % tr == 0
    return n_chunks, tr


def _ssm_end_state_job(u, n_seq, w_mat, steps_wanted=None):
    nb, sl = SSM_LANE_BLOCKS, SSM_STATE_LANES
    n_chunks, tr = _ssm_tiling(u.shape[0], n_seq, steps_wanted)
    tiles = n_chunks // tr
    return Job(
        kernel=functools.partial(_ssm_end_state_kernel, t_chunk=SSM_T),
        args=[u] * nb + [w_mat],
        in_specs=[pl.BlockSpec((tr * SSM_T, LANES), lambda i, b=b: (i, b)) for b in range(nb)] + [_full(w_mat.shape)],
        out_specs=[pl.BlockSpec((tr, nb * sl), lambda i: (i % tiles, i // tiles))],
        out_shapes=[jax.ShapeDtypeStruct((n_chunks, n_seq * nb * sl), F32)],
        scratch=[],
        steps=n_seq * tiles,
    )


def _ssm_scan(e, apow, n_seq):
    nb, sp = SSM_LANE_BLOCKS, SSM_STATES_PER_BLOCK
    n_chunks = e.shape[0]
    col = pl.BlockSpec((n_chunks, sp), lambda g: (0, g))
    fin = pl.BlockSpec((1, sp), lambda g: (0, g))
    return pl.pallas_call(
        _ssm_scan_kernel,
        grid=(n_seq * nb,),
        in_specs=[pl.BlockSpec((n_chunks, sp), lambda g: (0, 2 * g)),
                  pl.BlockSpec((n_chunks, sp), lambda g: (0, 2 * g + 1)),
                  pl.BlockSpec((None, SUBLANES, sp), lambda g: (g % nb, 0, 0)),
                  pl.BlockSpec((None, SUBLANES, sp), lambda g: (g % nb, 0, 1))],
        out_specs=[col, col, fin, fin],
        out_shape=[jax.ShapeDtypeStruct((n_chunks, n_seq * nb * sp), F32)] * 2
        + [jax.ShapeDtypeStruct((1, n_seq * nb * sp), F32)] * 2,
        compiler_params=_params("parallel"),
        name="ssm_scan",
    )(e, e, apow, apow)


def _ssm_output_job(u, n_seq, sp_re, sp_im, m_mat, v_mat, d_skip, steps_wanted=None):
    nb, sp = SSM_LANE_BLOCKS, SSM_STATES_PER_BLOCK
    rows = u.shape[0]
    n_chunks, tr = _ssm_tiling(rows, n_seq, steps_wanted)
    tiles = n_chunks // tr
    sp_spec = pl.BlockSpec((tr, nb * sp), lambda i: (i % tiles, i // tiles))
    return Job(
        kernel=functools.partial(_ssm_output_kernel, t_chunk=SSM_T),
        args=[u] * nb + [sp_re, sp_im, m_mat, v_mat, d_skip.reshape(1, SSM_WIDTH)],
        in_specs=[pl.BlockSpec((tr * SSM_T, LANES), lambda i, b=b: (i, b)) for b in range(nb)]
        + [sp_spec, sp_spec, _full(m_mat.shape), _full(v_mat.shape), _full((1, SSM_WIDTH))],
        out_specs=[pl.BlockSpec((tr * SSM_T, LANES), lambda i: (i, 0)) for _ in range(nb)],
        out_shapes=[jax.ShapeDtypeStruct((rows, LANES), F32) for _ in range(nb)],
        scratch=[],
        steps=n_seq * tiles,
    )


def _ssm_step_kernel(u_ref, sre_ref, sim_ref, w0_ref, v0_ref, a1_ref, d_ref, y_ref, nre_ref, nim_ref):
    sp = SSM_STATES_PER_BLOCK
    for b in range(SSM_LANE_BLOCKS):
        lanes = slice(b * LANES, (b + 1) * LANES)
        states = slice(b * sp, (b + 1) * sp)
        u = u_ref[:, lanes]
        e = jnp.dot(u.astype(BF16), w0_ref[b], preferred_element_type=F32)
        a_re, a_im = a1_ref[b, :, 0:sp], a1_ref[b, :, sp:2 * sp]
        s_re, s_im = sre_ref[:, states], sim_ref[:, states]
        n_re = a_re * s_re - a_im * s_im + e[:, 0:sp]
        n_im = a_re * s_im + a_im * s_re + e[:, sp:2 * sp]
        nre_ref[:, states] = n_re
        nim_ref[:, states] = n_im
        sn = jnp.concatenate([n_re, n_im], axis=1).astype(BF16)
        y_ref[:, lanes] = jnp.dot(sn, v0_ref[b], preferred_element_type=F32) + d_ref[:, lanes] * u


def _ssm_step(u, s_re, s_im, w0, v0, a1, d_skip):
    rows = u.shape[0]
    ns = SSM_GROUPS * SSM_STATE
    args = (u, s_re.reshape(rows, ns), s_im.reshape(rows, ns), w0, v0, a1, d_skip.reshape(1, SSM_WIDTH))
    return pl.pallas_call(
        _ssm_step_kernel,
        grid=(1,),
        in_specs=[_full(a.shape) for a in args],
        out_specs=[_full((rows, SSM_WIDTH)), _full((rows, ns)), _full((rows, ns))],
        out_shape=[jax.ShapeDtypeStruct((rows, SSM_WIDTH), F32), jax.ShapeDtypeStruct((rows, ns), F32),
                   jax.ShapeDtypeStruct((rows, ns), F32)],
        compiler_params=_params("arbitrary"),
        name="ssm_step",
    )(*args)


def _dil_attn_kernel(q_ref, kc_ref, kp_ref, vc_ref, vp_ref, o_ref, lse_ref, *, dil, slopes, chunks, pairs):
    steps, hd = ATT_STEPS, ATT_HEAD_DIM
    chunk = q_ref.shape[0]
    span = steps * dil
    step = pl.program_id(0)
    first_chunk = (step // pairs) % chunks == 0
    pair = step % pairs
    qi = lax.broadcasted_iota(jnp.int32, (steps, 2 * steps), 0)
    kj = lax.broadcasted_iota(jnp.int32, (steps, 2 * steps), 1)
    dist = qi + steps - kj
    band = (dist >= 0) & (dist <= steps)
    distf = (dist * dil).astype(F32)
    lane = lax.broadcasted_iota(jnp.int32, (steps, LANES), 1)
    heads = [lane < hd, lane >= hd]
    biases, first_biases = [], []
    for hh in range(2):
        slope = sum(jnp.where(pair == p, slopes[2 * p + hh], 0.0) for p in range(pairs))
        bias = jnp.where(band, -slope * distf, NEG_BIG)
        biases.append(bias)
        first_biases.append(jnp.where(first_chunk & (kj < steps), NEG_BIG, bias))

    def rows(ref, start):
        return ref[pl.ds(start, steps, stride=dil), :] if dil > 1 else ref[pl.ds(start, steps), :]

    for r in range(dil):
        for qb in range(chunk // span):
            start = r + qb * span
            q = rows(q_ref, start) * (hd ** -0.5)
            if qb == 0:
                k_prev, v_prev = rows(kp_ref, r), rows(vp_ref, r)
            else:
                k_prev, v_prev = rows(kc_ref, start - span), rows(vc_ref, start - span)
            kk = jnp.concatenate([k_prev, rows(kc_ref, start)], axis=0).astype(BF16)
            vv = jnp.concatenate([v_prev, rows(vc_ref, start)], axis=0).astype(BF16)
            out = jnp.zeros((steps, LANES), F32)
            lse = jnp.zeros((steps, LANES), F32)
            for hh in range(2):
                qh = jnp.where(heads[hh], q, 0.0).astype(BF16)
                s = lax.dot_general(qh, kk, (((1,), (1,)), ((), ())), preferred_element_type=F32)
                s = s + (first_biases[hh] if qb == 0 else biases[hh])
                m = jnp.max(s, axis=-1, keepdims=True)
                p = jnp.exp(s - m)
                den = jnp.sum(p, axis=-1, keepdims=True)
                oh = jnp.dot(p.astype(BF16), vv, preferred_element_type=F32) / den
                out = jnp.where(heads[hh], oh, out)
                lse = jnp.where(heads[hh], m + jnp.log(den), lse)
            if dil > 1:
                o_ref[pl.ds(start, steps, stride=dil), :] = out
                lse_ref[pl.ds(start, steps, stride=dil), :] = lse
            else:
                o_ref[pl.ds(start, steps), :] = out
                lse_ref[pl.ds(start, steps), :] = lse


def _dil_attn_job(q, k, v, n_seq, seq_len, group):
    win, dil = DIL_PAIRS[group]
    steps, gw = ATT_STEPS, ATT_GROUP_WIDTH
    span = steps * dil
    chunk = ATT_CHUNK
    assert win // dil == steps and chunk % span == 0 and seq_len % chunk == 0
    chunks = seq_len // chunk
    pairs = gw // LANES
    cur = pl.BlockSpec((chunk, LANES), lambda i: (i // pairs, pairs * group + i % pairs))
    prev = pl.BlockSpec((span, LANES), lambda i: (jnp.maximum((i // pairs) * (chunk // span) - 1, 0),
                                                  pairs * group + i % pairs))
    out = pl.BlockSpec((chunk, LANES), lambda i: (i // pairs, i % pairs))
    slopes = tuple(_alibi_slopes()[group * HEADS_PER_GROUP:(group + 1) * HEADS_PER_GROUP])
    rows = n_seq * seq_len
    return Job(
        kernel=functools.partial(_dil_attn_kernel, dil=dil, slopes=slopes, chunks=chunks, pairs=pairs),
        args=[q, k, k, v, v],
        in_specs=[cur, cur, prev, cur, prev],
        out_specs=[out, out],
        out_shapes=[jax.ShapeDtypeStruct((rows, gw), F32), jax.ShapeDtypeStruct((rows, gw), F32)],
        scratch=[],
        steps=n_seq * chunks * pairs,
    )


def _mem_attention(q_ref, mk_ref, mv_ref):
    hd = MEM_HEAD_DIM
    outs = []
    for h in range(MEM_HEADS):
        lanes = slice(h * hd, (h + 1) * hd)
        s = lax.dot_general(q_ref[:, lanes].astype(BF16), mk_ref[:, lanes].astype(BF16),
                            (((1,), (1,)), ((), ())), preferred_element_type=F32) * (hd ** -0.5)
        m = jnp.max(s, axis=-1, keepdims=True)
        p = jnp.exp(s - m)
        den = jnp.sum(p, axis=-1, keepdims=True)
        outs.append(jnp.dot(p.astype(BF16), mv_ref[:, lanes].astype(BF16), preferred_element_type=F32) / den)
    return jnp.concatenate(outs, axis=1)


def _cache_attn_kernel(q_ref, kn_ref, vn_ref, k1_ref, v1_ref, k2_ref, v2_ref, k3_ref, v3_ref, o_ref,
                       qt_scr, vnt_scr, *, slopes):
    hpg, hd = HEADS_PER_GROUP, ATT_HEAD_DIM
    j = pl.program_id(0) % hpg
    nb = q_ref.shape[0]
    scale = hd ** -0.5
    q, kn = q_ref[...], kn_ref[...]
    qt_scr[...] = q.T
    vnt_scr[...] = vn_ref[...].T
    lane_head = lax.broadcasted_iota(jnp.int32, q.shape, 1) // hd
    outs, lses = [], []
    for g, (k_ref, v_ref) in enumerate(((k1_ref, v1_ref), (k2_ref, v2_ref), (k3_ref, v3_ref))):
        dil = DIL_PAIRS[g][1]
        n_pos = k_ref.shape[-1]
        rows = pl.ds(pl.multiple_of((g * hpg + j) * hd, hd), hd)
        qg, vng = qt_scr[rows, :], vnt_scr[rows, :]
        slope = sum(jnp.where(j == h, slopes[g * hpg + h], 0.0) for h in range(hpg))
        back = n_pos - lax.broadcasted_iota(jnp.int32, (1, n_pos), 1)
        bias = jnp.where(back % dil == 0, -slope * back.astype(F32), NEG_BIG)
        row = lax.broadcasted_iota(jnp.int32, (nb, n_pos), 0)
        s = jnp.zeros((nb, n_pos), F32)
        for b in range(nb):
            s = jnp.where(row == b, jnp.sum(k_ref[b] * qg[:, b:b + 1], axis=0, keepdims=True), s)
        s = s * scale + bias
        s_new = jnp.sum(jnp.where(lane_head == g * hpg + j, q * kn, 0.0), axis=1, keepdims=True) * scale
        m = jnp.maximum(jnp.max(s, axis=1, keepdims=True), s_new)
        p = jnp.exp(s - m)
        p_new = jnp.exp(s_new - m)
        den = jnp.sum(p, axis=1, keepdims=True) + p_new
        cols = []
        for b in range(nb):
            acc = jnp.sum(v_ref[b] * p[b:b + 1, :], axis=1, keepdims=True) + p_new[b:b + 1, :] * vng[:, b:b + 1]
            cols.append(acc / den[b:b + 1, :])
        outs.append(cols)
        lses.append(m + jnp.log(den))
    top = functools.reduce(jnp.maximum, lses)
    ws = [jnp.exp(l - top) for l in lses]
    total = sum(ws)
    o_ref[...] = jnp.concatenate(
        [sum(w[b:b + 1, :] * cols[b] for w, cols in zip(ws, outs)) / total[b:b + 1, :] for b in range(nb)], axis=1)


def _cache_attn_job(q, k_new, v_new, caches_k, caches_v, nb):
    rows = q.shape[0]
    hpg, hd = HEADS_PER_GROUP, ATT_HEAD_DIM
    assert rows % nb == 0 and nb % SUBLANES == 0
    new = pl.BlockSpec((nb, ATT_WIDTH), lambda i: (i // hpg, 0))
    specs, args = [new, new, new], [q, k_new, v_new]
    for g, (win, dil) in enumerate(DIL_PAIRS):
        for c in (caches_k[g], caches_v[g]):
            assert c.shape == (rows, win, hpg, hd) and win % dil == 0, c.shape
            args.append(c.transpose(0, 2, 3, 1))
            specs.append(pl.BlockSpec((nb, None, hd, win), lambda i: (i // hpg, i % hpg, 0, 0)))
    return Job(
        kernel=functools.partial(_cache_attn_kernel, slopes=tuple(_alibi_slopes())),
        args=args,
        in_specs=specs,
        out_specs=[pl.BlockSpec((None, None, hd, nb), lambda i: (i % hpg, i // hpg, 0, 0))],
        out_shapes=[jax.ShapeDtypeStruct((hpg, rows // nb, hd, nb), F32)],
        scratch=[pltpu.VMEM((ATT_WIDTH, nb), F32) for _ in range(2)],
        steps=(rows // nb) * hpg,
    )


def _cache_attn_result(out):
    hpg, blocks, hd, nb = out.shape
    return out.transpose(1, 3, 0, 2).reshape(blocks * nb, hpg * hd)


def _cache_mem_attn_kernel(q_ref, k_ref, v_ref, o_ref):
    nb, rows, hd = k_ref.shape
    tiles = rows // SUBLANES

    def fold(x):
        return x, pltpu.roll(x, MEM_HEADS, 2)

    q = q_ref[...][:, None]
    k = k_ref[...].reshape(nb, tiles, SUBLANES, hd)
    v = v_ref[...].reshape(nb, tiles, SUBLANES, hd)
    s = jnp.sum(k * q, axis=-1, keepdims=True) * (MEM_HEAD_DIM ** -0.5)
    m = jnp.maximum(*fold(jnp.max(s, axis=1, keepdims=True)))
    p = jnp.exp(s - m)
    den = sum(fold(jnp.sum(p, axis=1, keepdims=True)))
    acc = sum(fold(jnp.sum(p * v, axis=1, keepdims=True)))
    o_ref[...] = (acc / den)[:, 0]


def _cache_mem_job(qm, mem_k, mem_v, nb):
    rows = qm.shape[0]
    assert rows % nb == 0 and SUBLANES == 2 * MEM_HEADS
    q4 = qm.reshape(rows, MEM_HEADS, MEM_HEAD_DIM)
    kv = pl.BlockSpec((nb, N_MEM * MEM_HEADS, MEM_HEAD_DIM), lambda i: (i, 0, 0))
    q = pl.BlockSpec((nb, SUBLANES, MEM_HEAD_DIM), lambda i: (i, 0, 0))
    return Job(
        kernel=_cache_mem_attn_kernel,
        args=[jnp.concatenate([q4, q4], axis=1), mem_k.reshape(rows, N_MEM * MEM_HEADS, MEM_HEAD_DIM),
              mem_v.reshape(rows, N_MEM * MEM_HEADS, MEM_HEAD_DIM)],
        in_specs=[q, kv, kv],
        out_specs=[q],
        out_shapes=[jax.ShapeDtypeStruct((rows, SUBLANES, MEM_HEAD_DIM), F32)],
        scratch=[],
        steps=rows // nb,
    )


def _cache_mem_result(out):
    return out[:, :MEM_HEADS].reshape(out.shape[0], MEM_WIDTH)


def _merge_kernel(x_ref, *refs, n_y, n_att, n_mem):
    y_refs, att_refs = refs[:n_y], refs[n_y:n_y + n_att]
    mem_refs = refs[n_y + n_att:n_y + n_att + n_mem]
    g1_ref, wgate_ref, wglu_ref, watt_ref, wmem_ref, wout_ref, o_ref = refs[n_y + n_att + n_mem:]
    d = D_MODEL
    x = x_ref[...]
    h = _rmsnorm(x, g1_ref[...]).astype(BF16)

    def gate(i):
        return jax.nn.sigmoid(jnp.dot(h, wgate_ref[:, i * d:(i + 1) * d], preferred_element_type=F32))

    y = jnp.concatenate([r[...] for r in y_refs], axis=1) if n_y > 1 else y_refs[0][...]
    z = jax.nn.gelu(y).astype(BF16)
    glu = jnp.dot(z, wglu_ref[...], preferred_element_type=F32)
    merged = gate(0) * (glu[:, 0:d] * jax.nn.sigmoid(glu[:, d:2 * d]))
    if n_att == 1:
        att = att_refs[0][...]
    else:
        lses = [r[...] for r in att_refs[1::2]]
        top = functools.reduce(jnp.maximum, lses)
        ws = [jnp.exp(l - top) for l in lses]
        att = sum(w * r[...] for w, r in zip(ws, att_refs[0::2])) / sum(ws)
    b_att = jnp.dot(att.astype(BF16), watt_ref[...], preferred_element_type=F32)
    merged = merged + gate(1) * b_att
    o_mem = mem_refs[0][...] if n_mem == 1 else _mem_attention(*mem_refs)
    b_mem = jnp.dot(o_mem.astype(BF16), wmem_ref[...], preferred_element_type=F32)
    merged = merged + gate(2) * b_mem
    o_ref[...] = x + jnp.dot(merged.astype(BF16), wout_ref[...], preferred_element_type=F32)


def _merge_job(x, ys, atts, mems, g1, w_in, w_glu, w_att_o, w_mem_o, w_out, tm):
    rows = x.shape[0]
    gate_cols = N_BRANCH * D_MODEL
    assert rows % tm == 0 and w_in.shape == (D_MODEL, IN_WIDTH)

    def tile(a):
        return pl.BlockSpec((tm, a.shape[1]), lambda i: (i, 0))

    acts = [x, *ys, *atts, mems[0]]
    act_specs = [tile(a) for a in acts]
    if len(mems) > 1:
        tiles_per_seq = rows // mems[1].shape[0] // tm
        assert tiles_per_seq * tm * mems[1].shape[0] == rows
        act_specs += [pl.BlockSpec((None, N_MEM, MEM_WIDTH), lambda i: (i // tiles_per_seq, 0, 0))] * 2
        acts += list(mems[1:])
    weights = [w_glu, w_att_o, w_mem_o, w_out]
    return Job(
        kernel=functools.partial(_merge_kernel, n_y=len(ys), n_att=len(atts), n_mem=len(mems)),
        args=acts + [g1.reshape(1, D_MODEL), w_in] + weights,
        in_specs=act_specs + [_full((1, D_MODEL)), _resident_cols(w_in, IN_WIDTH - gate_cols, gate_cols)]
        + [_full(w.shape) for w in weights],
        out_specs=[tile(x)],
        out_shapes=[jax.ShapeDtypeStruct(x.shape, F32)],
        scratch=[],
        steps=rows // tm,
    )


def _ffn_kernel(*refs, tm, tiles_per_seq, stepwise):
    if stepwise:
        x_ref, g2_ref, wup_ref, cw_ref, cb_ref, wdn_ref, gf_ref, prev_ref, y_ref, conv_ref = refs
    else:
        x_ref, g2_ref, wup_ref, cw_ref, cb_ref, wdn_ref, gf_ref, y_ref, conv_ref, a_scr = refs
        i = pl.program_id(0)
        first = i % tiles_per_seq == 0

        @pl.when(first)
        def _():
            a_scr[0:SUBLANES, :] = jnp.zeros((SUBLANES, D_FF), F32)

        @pl.when(jnp.logical_not(first))
        def _():
            a_scr[0:SUBLANES, :] = a_scr[tm:tm + SUBLANES, :]

    x = x_ref[...]
    h = _rmsnorm(x, g2_ref[...]).astype(BF16)
    a = jnp.dot(h, wup_ref[:, 0:D_FF], preferred_element_type=F32)
    if stepwise:
        a2, a1 = prev_ref[:, 0:D_FF], prev_ref[:, D_FF:2 * D_FF]
        conv_ref[:, 0:D_FF] = a1
        conv_ref[:, D_FF:2 * D_FF] = a
    else:
        a_scr[SUBLANES:SUBLANES + tm, :] = a
        a1 = a_scr[SUBLANES - 1:SUBLANES - 1 + tm, :]
        a2 = a_scr[SUBLANES - 2:SUBLANES - 2 + tm, :]
    c = a2 * cw_ref[0:1, :] + a1 * cw_ref[1:2, :] + a * cw_ref[2:3, :] + cb_ref[...]
    v = jnp.dot(h, wup_ref[:, D_FF:2 * D_FF], preferred_element_type=F32)
    y = jnp.dot((jax.nn.gelu(c) * v).astype(BF16), wdn_ref[...], preferred_element_type=F32)
    y_ref[...] = _rmsnorm(x + y, gf_ref[...])
    if not stepwise:
        @pl.when(i % tiles_per_seq == tiles_per_seq - 1)
        def _():
            conv_ref[...] = a_scr[SUBLANES + tm - (CONV_W - 1):SUBLANES + tm, :]


def _ffn_job(x, g2, w_up, conv_w, conv_b, w_down, gf, tm, n_seq=None, prev=None):
    rows, d = x.shape
    assert rows % tm == 0
    stepwise = prev is not None
    weights = [g2.reshape(1, d), w_up, conv_w, conv_b.reshape(1, D_FF), w_down, gf.reshape(1, d)]
    in_specs = [pl.BlockSpec((tm, d), lambda i: (i, 0))] + [_full(w.shape) for w in weights]
    args = [x] + weights
    if stepwise:
        tiles_per_seq = 1
        in_specs.append(pl.BlockSpec((tm, 2 * D_FF), lambda i: (i, 0)))
        args.append(prev)
        conv_spec = pl.BlockSpec((tm, 2 * D_FF), lambda i: (i, 0))
        conv_shape = jax.ShapeDtypeStruct((rows, 2 * D_FF), F32)
        scratch = []
    else:
        tiles_per_seq = rows // n_seq // tm
        assert tiles_per_seq * tm * n_seq == rows
        conv_spec = pl.BlockSpec((None, CONV_W - 1, D_FF), lambda i: (i // tiles_per_seq, 0, 0))
        conv_shape = jax.ShapeDtypeStruct((n_seq, CONV_W - 1, D_FF), F32)
        scratch = [pltpu.VMEM((tm + 2 * SUBLANES, D_FF), F32)]
    return Job(
        kernel=functools.partial(_ffn_kernel, tm=tm, tiles_per_seq=tiles_per_seq, stepwise=stepwise),
        args=args,
        in_specs=in_specs,
        out_specs=[pl.BlockSpec((tm, d), lambda i: (i, 0)), conv_spec],
        out_shapes=[jax.ShapeDtypeStruct((rows, d), F32), conv_shape],
        scratch=scratch,
        steps=rows // tm,
    )


def kernel(x_prompt, x_sample, state_ssm_re, state_ssm_im, cache_w1_k, cache_w1_v, cache_w2_k, cache_w2_v, cache_w3_k, cache_w3_v, cache_mem_k, cache_mem_v, state_ffn_conv, mem_prompt, norm1_g, w_in, ssm_a_re, ssm_a_im, ssm_log_dt, ssm_b_re, ssm_b_im, ssm_c_re, ssm_c_im, ssm_d, w_ssm_glu, w_att_o, mem_norm_g, w_mem_kv, w_mem_o, w_out, norm2_g, w_up, ffn_conv_w, ffn_conv_b, w_down, final_norm_g):
    n_seq, seq_len, d = x_prompt.shape
    n_dec, dec_len, _ = x_sample.shape
    depth = norm1_g.shape[0]
    assert d == D_MODEL and depth == 1 and dec_len == 1
    assert w_in.shape == (depth, D_MODEL, IN_WIDTH) and w_up.shape == (depth, D_MODEL, 2 * D_FF)
    assert mem_prompt.shape == (n_seq, N_MEM, D_MODEL)
    assert ssm_a_re.shape == (depth, SSM_GROUPS, SSM_STATE)
    assert seq_len % ROW_TILE == 0 and n_dec % SUBLANES == 0
    hpg, hd = HEADS_PER_GROUP, ATT_HEAD_DIM
    rows_p = n_seq * seq_len
    dec_tile = n_dec if n_dec <= ROW_TILE else ROW_TILE

    w_in_b = w_in[0].astype(BF16)

    m_mat, w_mat, v_mat, v0_mat, a1, apow = _ssm_prep(ssm_a_re[0], ssm_a_im[0], ssm_log_dt[0], ssm_b_re[0],
                                                      ssm_b_im[0], ssm_c_re[0], ssm_c_im[0])
    w0_mat = w_mat[:, (SSM_T - 1) * LANES:, :]

    xs = x_sample.reshape(n_dec, d)
    us, qs, ks, vs, qms = _run([_norm_proj_job(xs, norm1_g[0], w_in_b, PROJ_SPLITS, dec_tile)], "norm_proj")[0]
    cache_job = _cache_attn_job(qs, ks, vs, (cache_w1_k[0], cache_w2_k[0], cache_w3_k[0]),
                                (cache_w1_v[0], cache_w2_v[0], cache_w3_v[0]), CACHE_ROWS_PER_STEP)
    proj_steps = rows_p // ROW_TILE
    mem_rows = n_dec // proj_steps if n_dec % proj_steps == 0 else CACHE_ROWS_PER_STEP
    mem_job = _cache_mem_job(qms, cache_mem_k[0], cache_mem_v[0], mem_rows)

    xp = x_prompt.reshape(rows_p, d)
    gw = ATT_GROUP_WIDTH
    keeps = [min(win, seq_len) for win, _ in DIL_PAIRS]
    windows = [(src, g * gw, gw, keep) for g, keep in enumerate(keeps) for src in (2, 3)]
    proj_job = _norm_proj_job(xp, norm1_g[0], w_in_b, PROJ_SPLITS, ROW_TILE, n_seq, windows)
    cast_a = _cast_job([w_up[0], w_ssm_glu[0], w_out[0], w_mem_o[0], w_mem_kv[0]], proj_job.steps)
    (u, q, k, v, qm, *kv_win), ((mem_out,), (w_up_b, w_glu_b, w_out_b, w_memo_b, w_memkv_b)) = _run_with(
        proj_job, [mem_job, cast_a], "norm_proj_cache_mem")
    mem_s = _cache_mem_result(mem_out)

    dil_jobs = [_dil_attn_job(q, k, v, n_seq, seq_len, g) for g in range(len(DIL_PAIRS))]
    end_job = _ssm_end_state_job(u, n_seq, w_mat, dil_jobs[0].steps)
    (e,), (att0, (w_down_b, w_att_b)) = _run_with(
        end_job, [dil_jobs[0], _cast_job([w_down[0], w_att_o[0]], end_job.steps)], "ssm_end_state_dil")
    merge_w = (norm1_g[0], w_in_b, w_glu_b, w_att_b, w_memo_b, w_out_b)
    ffn_w = (norm2_g[0], w_up_b, ffn_conv_w[0], ffn_conv_b[0], w_down_b, final_norm_g)
    sp_re, sp_im, fin_re, fin_im = _ssm_scan(e, apow, n_seq)
    y_ssm, att1 = _run_pair(_ssm_output_job(u, n_seq, sp_re, sp_im, m_mat, v_mat, ssm_d[0], dil_jobs[1].steps),
                            dil_jobs[1], "ssm_output_dil")
    att2 = _run([dil_jobs[2]], "dil_attn")[0]
    atts = att0 + att1 + att2

    mk, mv = _run([_norm_proj_job(mem_prompt.reshape(n_seq * N_MEM, d), mem_norm_g[0], w_memkv_b,
                                  (MEM_WIDTH, MEM_WIDTH), ROW_TILE)], "mem_kv")[0]
    mems = [qm, mk.reshape(n_seq, N_MEM, MEM_WIDTH), mv.reshape(n_seq, N_MEM, MEM_WIDTH)]
    merge_tile = rows_p // cache_job.steps
    if not (rows_p % cache_job.steps == 0 and merge_tile % LANES == 0 and merge_tile <= ROW_TILE
            and seq_len % merge_tile == 0):
        merge_tile = ROW_TILE
    (x1,), (att_out,) = _run_pair(_merge_job(xp, y_ssm, atts, mems, *merge_w, merge_tile), cache_job,
                                  "merge_cache_attn")
    att_s = _cache_attn_result(att_out)
    y_p, conv_p = _run([_ffn_job(x1, *ffn_w, ROW_TILE, n_seq=n_seq)], "ffn")[0]

    def final_state(s):
        return s.reshape(1, n_seq, SSM_GROUPS, SSM_STATE)
    win_p = [t.reshape(n_seq, hpg, hd, t.shape[-1]).transpose(0, 3, 1, 2)[None] for t in kv_win]
    mem_kv = [mk.reshape(1, n_seq, N_MEM, MEM_HEADS, MEM_HEAD_DIM), mv.reshape(1, n_seq, N_MEM, MEM_HEADS, MEM_HEAD_DIM)]

    ys_ssm, sn_re, sn_im = _ssm_step(us, state_ssm_re[0], state_ssm_im[0], w0_mat, v0_mat, a1, ssm_d[0])
    n_g = len(DIL_PAIRS)
    ks4, vs4 = (t.reshape(n_dec, n_g, hpg, hd) for t in (ks, vs))
    xs1 = _run([_merge_job(xs, [ys_ssm], [att_s], [mem_s], *merge_w, dec_tile)], "merge")[0][0]
    y_s, conv_s = _run([_ffn_job(xs1, *ffn_w, dec_tile,
                                 prev=state_ffn_conv[0].reshape(n_dec, (CONV_W - 1) * D_FF))], "ffn_step")[0]

    win_s = []
    for g in range(len(DIL_PAIRS)):
        win_s += [ks4[None, :, g:g + 1], vs4[None, :, g:g + 1]]

    return (y_p.reshape(n_seq, seq_len, d), y_s.reshape(n_dec, 1, d),
            final_state(fin_re), final_state(fin_im), *win_p, *mem_kv, conv_p[None],
            sn_re.reshape(1, n_dec, SSM_GROUPS, SSM_STATE), sn_im.reshape(1, n_dec, SSM_GROUPS, SSM_STATE),
            *win_s, conv_s.reshape(1, n_dec, CONV_W - 1, D_FF))
```

```python
import collections
import functools

import jax
import jax.numpy as jnp
from jax import lax
from jax.experimental import pallas as pl
from jax.experimental.pallas import tpu as pltpu

F32 = jnp.float32
BF16 = jnp.bfloat16

D_MODEL = 1024
SSM_WIDTH = 512
SSM_GROUP = 16
SSM_GROUPS = 32
SSM_STATE = 64
ATT_HEAD_DIM = 64
HEADS_PER_GROUP = 4
DIL_PAIRS = ((128, 1), (512, 4), (2048, 16))
ATT_HEADS = len(DIL_PAIRS) * HEADS_PER_GROUP
ATT_WIDTH = ATT_HEADS * ATT_HEAD_DIM
ATT_GROUP_WIDTH = HEADS_PER_GROUP * ATT_HEAD_DIM
ATT_STEPS = 128
N_MEM = 256
MEM_HEADS = 4
MEM_HEAD_DIM = 128
MEM_WIDTH = MEM_HEADS * MEM_HEAD_DIM
N_BRANCH = 3
D_FF = 2816
CONV_W = 3
EPS = 1e-6
PROJ_SPLITS = (SSM_WIDTH, ATT_WIDTH, ATT_WIDTH, ATT_WIDTH, MEM_WIDTH)
IN_WIDTH = sum(PROJ_SPLITS) + N_BRANCH * D_MODEL

LANES = 128
SUBLANES = 8
VMEM_LIMIT_BYTES = 56 * 1024 * 1024

SSM_T = 8
SSM_LANE_BLOCKS = SSM_WIDTH // LANES
SSM_GROUPS_PER_BLOCK = LANES // SSM_GROUP
SSM_STATES_PER_BLOCK = SSM_GROUPS_PER_BLOCK * SSM_STATE
SSM_STATE_LANES = 2 * SSM_STATES_PER_BLOCK

ROW_TILE = 512
SSM_ROW_TILE = 256
CACHE_ROWS_PER_STEP = 8
ATT_CHUNK = 2048
NEG_BIG = -1e30


def _alibi_slopes():
    return [float(2.0 ** (-8.0 * h / ATT_HEADS)) for h in range(1, ATT_HEADS + 1)]


def _params(*sem):
    return pltpu.CompilerParams(dimension_semantics=sem, vmem_limit_bytes=VMEM_LIMIT_BYTES)


def _rmsnorm(x, g):
    ms = jnp.mean(x * x, axis=-1, keepdims=True)
    return x * lax.rsqrt(ms + EPS) * g


def _full(shape):
    nd = len(shape)
    return pl.BlockSpec(shape, lambda *_: (0,) * nd, pipeline_mode=pl.Buffered(1))


def _resident_cols(w, start, width):
    return pl.BlockSpec((pl.Element(w.shape[0]), pl.Element(width)), lambda *_: (0, start),
                        pipeline_mode=pl.Buffered(1))


Job = collections.namedtuple("Job", "kernel args in_specs out_specs out_shapes scratch steps")


def _jobs_kernel(*refs, kernels, layout):
    n_in, n_out = sum(l[0] for l in layout), sum(l[1] for l in layout)
    ins, outs, scratch = refs[:n_in], refs[n_in:n_in + n_out], refs[n_in + n_out:]
    i0 = o0 = s0 = 0
    for kernel, (ni, no, ns) in zip(kernels, layout):
        kernel(*ins[i0:i0 + ni], *outs[o0:o0 + no], *scratch[s0:s0 + ns])
        i0, o0, s0 = i0 + ni, o0 + no, s0 + ns


def _run(jobs, name):
    steps = jobs[0].steps
    assert all(j.steps == steps for j in jobs)
    layout = tuple((len(j.args), len(j.out_shapes), len(j.scratch)) for j in jobs)
    outs = pl.pallas_call(
        functools.partial(_jobs_kernel, kernels=tuple(j.kernel for j in jobs), layout=layout),
        grid=(steps,),
        in_specs=[s for j in jobs for s in j.in_specs],
        out_specs=[s for j in jobs for s in j.out_specs],
        out_shape=[s for j in jobs for s in j.out_shapes],
        scratch_shapes=[s for j in jobs for s in j.scratch],
        compiler_params=_params("arbitrary"),
        name=name,
    )(*[a for j in jobs for a in j.args])
    split, o0 = [], 0
    for _, no, _ in layout:
        split.append(list(outs[o0:o0 + no]))
        o0 += no
    return split


def _run_pair(main, rider, name):
    m, (r,) = _run_with(main, [rider], name)
    return m, r


def _run_with(main, riders, name):
    fused = [r for r in riders if r.steps == main.steps]
    outs = _run(fused + [main], name)
    by_rider = {id(r): o for r, o in zip(fused, outs)}
    return outs[-1], [by_rider[id(r)] if id(r) in by_rider else _run([r], name + "_rider")[0] for r in riders]


def _cast_kernel(*refs):
    n = len(refs) // 2
    for src, dst in zip(refs[:n], refs[n:]):
        dst[...] = src[...].astype(dst.dtype)


def _cast_job(weights, steps):
    specs = []
    for w in weights:
        rows = w.shape[0] // steps
        assert rows * steps == w.shape[0] and rows % (2 * SUBLANES) == 0, (w.shape, steps)
        specs.append(pl.BlockSpec((rows, w.shape[1]), lambda i: (i, 0)))
    return Job(kernel=_cast_kernel, args=list(weights), in_specs=specs, out_specs=specs,
               out_shapes=[jax.ShapeDtypeStruct(w.shape, BF16) for w in weights], scratch=[], steps=steps)


def _norm_proj_kernel(x_ref, g_ref, w_ref, *out_refs, splits, windows, tiles_per_seq):
    tm = x_ref.shape[0]
    h = _rmsnorm(x_ref[...], g_ref[...]).astype(BF16)
    off = 0
    for o_ref, width in zip(out_refs, splits):
        for c0 in range(0, width, 512):
            cw = min(512, width - c0)
            o_ref[:, c0:c0 + cw] = jnp.dot(h, w_ref[:, off + c0:off + c0 + cw], preferred_element_type=F32)
        off += width
    tile = pl.program_id(0) % tiles_per_seq
    for win_ref, (src, col0, cols, win) in zip(out_refs[len(splits):], windows):
        rows = min(win, tm)

        @pl.when(tile >= tiles_per_seq - max(win // tm, 1))
        def _(win_ref=win_ref, src=src, col0=col0, cols=cols, rows=rows):
            win_ref[...] = out_refs[src][tm - rows:tm, col0:col0 + cols].T


def _norm_proj_job(x, g, w_bf16, splits, tm, n_seq=1, windows=()):
    rows, d = x.shape
    assert rows % (tm * n_seq) == 0 and sum(splits) <= w_bf16.shape[1]
    tps = rows // n_seq // tm
    win_specs, win_shapes = [], []
    for _, _, cols, win in windows:
        assert (win % tm == 0 or tm % win == 0) and win <= tps * tm
        n_tiles = max(win // tm, 1)
        win_specs.append(pl.BlockSpec((None, cols, min(win, tm)), lambda i, n_tiles=n_tiles: (
            i // tps, 0, jnp.maximum(i % tps - (tps - n_tiles), 0))))
        win_shapes.append(jax.ShapeDtypeStruct((n_seq, cols, win), F32))
    return Job(
        kernel=functools.partial(_norm_proj_kernel, splits=splits, windows=tuple(windows), tiles_per_seq=tps),
        args=[x, g.reshape(1, d), w_bf16],
        in_specs=[pl.BlockSpec((tm, d), lambda i: (i, 0)), _full((1, d)), _resident_cols(w_bf16, 0, sum(splits))],
        out_specs=[pl.BlockSpec((tm, s), lambda i: (i, 0)) for s in splits] + win_specs,
        out_shapes=[jax.ShapeDtypeStruct((rows, s), F32) for s in splits] + win_shapes,
        scratch=[],
        steps=rows // tm,
    )


def _ssm_layout(a_re, a_im, log_dt, b_re, b_im, c_re, c_im):
    nb, gpb, p, c = SSM_LANE_BLOCKS, SSM_GROUPS_PER_BLOCK, SSM_STATE, SSM_GROUP
    rows = jnp.stack([a_re.reshape(nb, gpb * p), a_im.reshape(nb, gpb * p),
                      jnp.repeat(log_dt, p).reshape(nb, gpb * p)], axis=1)
    eye = jnp.eye(gpb, dtype=F32)

    def place_b(b):
        return jnp.einsum('bgpc,gh->bgchp', b.reshape(nb, gpb, p, c), eye).reshape(nb, gpb * c, gpb * p)

    def place_c(m):
        return jnp.einsum('bgcp,gh->bhpgc', m.reshape(nb, gpb, c, p), eye).reshape(nb, gpb * p, gpb * c)

    return rows, place_b(b_re), place_b(b_im), place_c(c_re), place_c(c_im)


def _ssm_prep_kernel(rows_ref, bre_ref, bim_ref, cre_ref, cim_ref,
                     m_ref, w_ref, v_ref, v0_ref, a1_ref, apow_ref, *, t_chunk):
    sp = SSM_STATES_PER_BLOCK
    a_re, a_im, dt = rows_ref[0:1, :], rows_ref[1:2, :], jnp.exp(rows_ref[2:3, :])

    def powers(k):
        mag = jnp.exp(a_re * dt * k)
        ang = a_im * dt * k
        return mag * jnp.cos(ang), mag * jnp.sin(ang)

    n_pow = 2 * SUBLANES
    assert t_chunk + 1 <= n_pow
    pw_re, pw_im = powers(lax.broadcasted_iota(jnp.int32, (n_pow, 1), 0).astype(F32))
    pwt_re, pwt_im = pw_re.T, pw_im.T
    ab_re, ab_im = pw_re[1:2, :], pw_im[1:2, :]
    den = a_re * a_re + a_im * a_im
    q_re = ((ab_re - 1.0) * a_re + ab_im * a_im) / den
    q_im = (ab_im * a_re - (ab_re - 1.0) * a_im) / den
    bre, bim = bre_ref[...], bim_ref[...]
    bb_re = q_re * bre - q_im * bim
    bb_im = q_re * bim + q_im * bre
    cre, cim = cre_ref[...], cim_ref[...]

    m_ref[...] = jnp.zeros(m_ref.shape, m_ref.dtype)
    for k in range(t_chunk):
        pk_re, pk_im = pw_re[k:k + 1, :], pw_im[k:k + 1, :]
        bk_re = bb_re * pk_re - bb_im * pk_im
        bk_im = bb_re * pk_im + bb_im * pk_re
        t = t_chunk - 1 - k
        w_ref[t * LANES:(t + 1) * LANES, 0:sp] = bk_re.astype(w_ref.dtype)
        w_ref[t * LANES:(t + 1) * LANES, sp:2 * sp] = bk_im.astype(w_ref.dtype)
        kk = (jnp.dot(bk_re, cre, precision=lax.Precision.HIGHEST, preferred_element_type=F32)
              - jnp.dot(bk_im, cim, precision=lax.Precision.HIGHEST, preferred_element_type=F32))
        kk = kk.astype(m_ref.dtype)
        for t0 in range(t_chunk - k):
            m_ref[t0 * LANES:(t0 + 1) * LANES, (t0 + k) * LANES:(t0 + k + 1) * LANES] = kk

    for t in range(t_chunk):
        pc_re, pc_im = pwt_re[:, t + 1:t + 2], pwt_im[:, t + 1:t + 2]
        v_ref[0:sp, t * LANES:(t + 1) * LANES] = (cre * pc_re - cim * pc_im).astype(v_ref.dtype)
        v_ref[sp:2 * sp, t * LANES:(t + 1) * LANES] = (-(cre * pc_im + cim * pc_re)).astype(v_ref.dtype)
    v0_ref[0:sp, :] = cre.astype(v0_ref.dtype)
    v0_ref[sp:2 * sp, :] = (-cim).astype(v0_ref.dtype)

    a1_ref[:, 0:sp] = ab_re
    a1_ref[:, sp:2 * sp] = ab_im
    steps = (lax.broadcasted_iota(jnp.int32, (SUBLANES, 1), 0) + 1) * t_chunk
    ap_re, ap_im = powers(steps.astype(F32))
    apow_ref[:, 0:sp] = ap_re
    apow_ref[:, sp:2 * sp] = ap_im


def _ssm_prep(a_re, a_im, log_dt, b_re, b_im, c_re, c_im):
    rows, pbre, pbim, pcre, pcim = _ssm_layout(a_re, a_im, log_dt, b_re, b_im, c_re, c_im)
    nb, sp, sl, tl = SSM_LANE_BLOCKS, SSM_STATES_PER_BLOCK, SSM_STATE_LANES, SSM_T * LANES

    def blk(shape):
        return pl.BlockSpec((None,) + shape, lambda b: (b, 0, 0))

    return pl.pallas_call(
        functools.partial(_ssm_prep_kernel, t_chunk=SSM_T),
        grid=(nb,),
        in_specs=[blk((3, sp)), blk((LANES, sp)), blk((LANES, sp)), blk((sp, LANES)), blk((sp, LANES))],
        out_specs=[blk((tl, tl)), blk((tl, sl)), blk((sl, tl)), blk((sl, LANES)), blk((1, sl)), blk((SUBLANES, sl))],
        out_shape=[jax.ShapeDtypeStruct((nb, tl, tl), BF16), jax.ShapeDtypeStruct((nb, tl, sl), BF16),
                   jax.ShapeDtypeStruct((nb, sl, tl), BF16), jax.ShapeDtypeStruct((nb, sl, LANES), BF16),
                   jax.ShapeDtypeStruct((nb, 1, sl), F32), jax.ShapeDtypeStruct((nb, SUBLANES, sl), F32)],
        compiler_params=_params("parallel"),
        name="ssm_prep",
    )(rows, pbre, pbim, pcre, pcim)


def _chunk_tokens(u_ref, t, t_chunk):
    return u_ref[pl.ds(t, u_ref.shape[0] // t_chunk, stride=t_chunk), :]


def _chunk_lanes(u_ref, t_chunk):
    return jnp.concatenate([_chunk_tokens(u_ref, t, t_chunk) for t in range(t_chunk)], axis=1)


def _ssm_end_state_kernel(*refs, t_chunk):
    nb, sl = SSM_LANE_BLOCKS, SSM_STATE_LANES
    u_refs, w_ref, e_ref = refs[:nb], refs[nb], refs[nb + 1]
    for b in range(nb):
        ub = _chunk_lanes(u_refs[b], t_chunk).astype(BF16)
        e_ref[:, b * sl:(b + 1) * sl] = jnp.dot(ub, w_ref[b], preferred_element_type=F32)


def _ssm_scan_kernel(ere_ref, eim_ref, pre_ref, pim_ref, spre_ref, spim_ref, fre_ref, fim_ref):
    n_tiles = ere_ref.shape[0] // SUBLANES
    width = ere_ref.shape[1]
    p_re, p_im = pre_ref[...], pim_ref[...]
    row = lax.broadcasted_iota(jnp.int32, (SUBLANES, width), 0)

    def shift_down(x, k):
        return jnp.where(row >= k, pltpu.roll(x, k, 0), 0.0)

    def body(i, carry):
        c_re, c_im = carry
        rows = pl.ds(pl.multiple_of(i * SUBLANES, SUBLANES), SUBLANES)
        x_re, x_im = ere_ref[rows, :], eim_ref[rows, :]
        for k in (1, 2, 4):
            a_re, a_im = p_re[k - 1:k, :], p_im[k - 1:k, :]
            s_re, s_im = shift_down(x_re, k), shift_down(x_im, k)
            x_re, x_im = x_re + a_re * s_re - a_im * s_im, x_im + a_re * s_im + a_im * s_re
        t_re = x_re + p_re * c_re - p_im * c_im
        t_im = x_im + p_re * c_im + p_im * c_re
        spre_ref[rows, :] = jnp.where(row >= 1, pltpu.roll(t_re, 1, 0), c_re)
        spim_ref[rows, :] = jnp.where(row >= 1, pltpu.roll(t_im, 1, 0), c_im)
        return t_re[SUBLANES - 1:SUBLANES, :], t_im[SUBLANES - 1:SUBLANES, :]

    zero = jnp.zeros((1, width), F32)
    f_re, f_im = lax.fori_loop(0, n_tiles, body, (zero, zero))
    fre_ref[...] = f_re
    fim_ref[...] = f_im


def _ssm_output_kernel(*refs, t_chunk):
    nb, sp = SSM_LANE_BLOCKS, SSM_STATES_PER_BLOCK
    u_refs, (spre_ref, spim_ref, m_ref, v_ref, d_ref), y_refs = refs[:nb], refs[nb:nb + 5], refs[nb + 5:]
    for b in range(nb):
        ub = _chunk_lanes(u_refs[b], t_chunk).astype(BF16)
        yb = jnp.dot(ub, m_ref[b], preferred_element_type=F32)
        states = slice(b * sp, (b + 1) * sp)
        sprev = jnp.concatenate([spre_ref[:, states], spim_ref[:, states]], axis=1).astype(BF16)
        yb = yb + jnp.dot(sprev, v_ref[b], preferred_element_type=F32)
        d = d_ref[:, b * LANES:(b + 1) * LANES]
        for t in range(t_chunk):
            y_t = yb[:, t * LANES:(t + 1) * LANES] + d * _chunk_tokens(u_refs[b], t, t_chunk)
            y_refs[b][pl.ds(t, yb.shape[0], stride=t_chunk), :] = y_t


def _ssm_tiling(rows, n_seq, steps_wanted):
    assert rows % (SSM_T * n_seq) == 0
    n_chunks = rows // SSM_T // n_seq
    assert n_chunks % SUBLANES == 0
    tr = min(SSM_ROW_TILE, n_chunks)
    if steps_wanted and (n_seq * n_chunks) % steps_wanted == 0:
        want = n_seq * n_chunks // steps_wanted
        if want % SUBLANES == 0 and n_chunks % want == 0 and want <= SSM_ROW_TILE:
            tr = want
    assert n_chunks % tr == 0
    return n_chunks, tr


def _ssm_end_state_job(u, n_seq, w_mat, steps_wanted=None):
    nb, sl = SSM_LANE_BLOCKS, SSM_STATE_LANES
    n_chunks, tr = _ssm_tiling(u.shape[0], n_seq, steps_wanted)
    tiles = n_chunks // tr
    return Job(
        kernel=functools.partial(_ssm_end_state_kernel, t_chunk=SSM_T),
        args=[u] * nb + [w_mat],
        in_specs=[pl.BlockSpec((tr * SSM_T, LANES), lambda i, b=b: (i, b)) for b in range(nb)] + [_full(w_mat.shape)],
        out_specs=[pl.BlockSpec((tr, nb * sl), lambda i: (i % tiles, i // tiles))],
        out_shapes=[jax.ShapeDtypeStruct((n_chunks, n_seq * nb * sl), F32)],
        scratch=[],
        steps=n_seq * tiles,
    )


def _ssm_scan(e, apow, n_seq):
    nb, sp = SSM_LANE_BLOCKS, SSM_STATES_PER_BLOCK
    n_chunks = e.shape[0]
    col = pl.BlockSpec((n_chunks, sp), lambda g: (0, g))
    fin = pl.BlockSpec((1, sp), lambda g: (0, g))
    return pl.pallas_call(
        _ssm_scan_kernel,
        grid=(n_seq * nb,),
        in_specs=[pl.BlockSpec((n_chunks, sp), lambda g: (0, 2 * g)),
                  pl.BlockSpec((n_chunks, sp), lambda g: (0, 2 * g + 1)),
                  pl.BlockSpec((None, SUBLANES, sp), lambda g: (g % nb, 0, 0)),
                  pl.BlockSpec((None, SUBLANES, sp), lambda g: (g % nb, 0, 1))],
        out_specs=[col, col, fin, fin],
        out_shape=[jax.ShapeDtypeStruct((n_chunks, n_seq * nb * sp), F32)] * 2
        + [jax.ShapeDtypeStruct((1, n_seq * nb * sp), F32)] * 2,
        compiler_params=_params("parallel"),
        name="ssm_scan",
    )(e, e, apow, apow)


def _ssm_output_job(u, n_seq, sp_re, sp_im, m_mat, v_mat, d_skip, steps_wanted=None):
    nb, sp = SSM_LANE_BLOCKS, SSM_STATES_PER_BLOCK
    rows = u.shape[0]
    n_chunks, tr = _ssm_tiling(rows, n_seq, steps_wanted)
    tiles = n_chunks // tr
    sp_spec = pl.BlockSpec((tr, nb * sp), lambda i: (i % tiles, i // tiles))
    return Job(
        kernel=functools.partial(_ssm_output_kernel, t_chunk=SSM_T),
        args=[u] * nb + [sp_re, sp_im, m_mat, v_mat, d_skip.reshape(1, SSM_WIDTH)],
        in_specs=[pl.BlockSpec((tr * SSM_T, LANES), lambda i, b=b: (i, b)) for b in range(nb)]
        + [sp_spec, sp_spec, _full(m_mat.shape), _full(v_mat.shape), _full((1, SSM_WIDTH))],
        out_specs=[pl.BlockSpec((tr * SSM_T, LANES), lambda i: (i, 0)) for _ in range(nb)],
        out_shapes=[jax.ShapeDtypeStruct((rows, LANES), F32) for _ in range(nb)],
        scratch=[],
        steps=n_seq * tiles,
    )


def _ssm_step_kernel(u_ref, sre_ref, sim_ref, w0_ref, v0_ref, a1_ref, d_ref, y_ref, nre_ref, nim_ref):
    sp = SSM_STATES_PER_BLOCK
    for b in range(SSM_LANE_BLOCKS):
        lanes = slice(b * LANES, (b + 1) * LANES)
        states = slice(b * sp, (b + 1) * sp)
        u = u_ref[:, lanes]
        e = jnp.dot(u.astype(BF16), w0_ref[b], preferred_element_type=F32)
        a_re, a_im = a1_ref[b, :, 0:sp], a1_ref[b, :, sp:2 * sp]
        s_re, s_im = sre_ref[:, states], sim_ref[:, states]
        n_re = a_re * s_re - a_im * s_im + e[:, 0:sp]
        n_im = a_re * s_im + a_im * s_re + e[:, sp:2 * sp]
        nre_ref[:, states] = n_re
        nim_ref[:, states] = n_im
        sn = jnp.concatenate([n_re, n_im], axis=1).astype(BF16)
        y_ref[:, lanes] = jnp.dot(sn, v0_ref[b], preferred_element_type=F32) + d_ref[:, lanes] * u


def _ssm_step(u, s_re, s_im, w0, v0, a1, d_skip):
    rows = u.shape[0]
    ns = SSM_GROUPS * SSM_STATE
    args = (u, s_re.reshape(rows, ns), s_im.reshape(rows, ns), w0, v0, a1, d_skip.reshape(1, SSM_WIDTH))
    return pl.pallas_call(
        _ssm_step_kernel,
        grid=(1,),
        in_specs=[_full(a.shape) for a in args],
        out_specs=[_full((rows, SSM_WIDTH)), _full((rows, ns)), _full((rows, ns))],
        out_shape=[jax.ShapeDtypeStruct((rows, SSM_WIDTH), F32), jax.ShapeDtypeStruct((rows, ns), F32),
                   jax.ShapeDtypeStruct((rows, ns), F32)],
        compiler_params=_params("arbitrary"),
        name="ssm_step",
    )(*args)


def _dil_attn_kernel(q_ref, kc_ref, kp_ref, vc_ref, vp_ref, o_ref, lse_ref, *, dil, slopes, chunks, pairs):
    steps, hd = ATT_STEPS, ATT_HEAD_DIM
    chunk = q_ref.shape[0]
    span = steps * dil
    step = pl.program_id(0)
    first_chunk = (step // pairs) % chunks == 0
    pair = step % pairs
    qi = lax.broadcasted_iota(jnp.int32, (steps, 2 * steps), 0)
    kj = lax.broadcasted_iota(jnp.int32, (steps, 2 * steps), 1)
    dist = qi + steps - kj
    band = (dist >= 0) & (dist <= steps)
    distf = (dist * dil).astype(F32)
    lane = lax.broadcasted_iota(jnp.int32, (steps, LANES), 1)
    heads = [lane < hd, lane >= hd]
    biases, first_biases = [], []
    for hh in range(2):
        slope = sum(jnp.where(pair == p, slopes[2 * p + hh], 0.0) for p in range(pairs))
        bias = jnp.where(band, -slope * distf, NEG_BIG)
        biases.append(bias)
        first_biases.append(jnp.where(first_chunk & (kj < steps), NEG_BIG, bias))

    def rows(ref, start):
        return ref[pl.ds(start, steps, stride=dil), :] if dil > 1 else ref[pl.ds(start, steps), :]

    for r in range(dil):
        for qb in range(chunk // span):
            start = r + qb * span
            q = rows(q_ref, start) * (hd ** -0.5)
            if qb == 0:
                k_prev, v_prev = rows(kp_ref, r), rows(vp_ref, r)
            else:
                k_prev, v_prev = rows(kc_ref, start - span), rows(vc_ref, start - span)
            kk = jnp.concatenate([k_prev, rows(kc_ref, start)], axis=0).astype(BF16)
            vv = jnp.concatenate([v_prev, rows(vc_ref, start)], axis=0).astype(BF16)
            out = jnp.zeros((steps, LANES), F32)
            lse = jnp.zeros((steps, LANES), F32)
            for hh in range(2):
                qh = jnp.where(heads[hh], q, 0.0).astype(BF16)
                s = lax.dot_general(qh, kk, (((1,), (1,)), ((), ())), preferred_element_type=F32)
                s = s + (first_biases[hh] if qb == 0 else biases[hh])
                m = jnp.max(s, axis=-1, keepdims=True)
                p = jnp.exp(s - m)
                den = jnp.sum(p, axis=-1, keepdims=True)
                oh = jnp.dot(p.astype(BF16), vv, preferred_element_type=F32) / den
                out = jnp.where(heads[hh], oh, out)
                lse = jnp.where(heads[hh], m + jnp.log(den), lse)
            if dil > 1:
                o_ref[pl.ds(start, steps, stride=dil), :] = out
                lse_ref[pl.ds(start, steps, stride=dil), :] = lse
            else:
                o_ref[pl.ds(start, steps), :] = out
                lse_ref[pl.ds(start, steps), :] = lse


def _dil_attn_job(q, k, v, n_seq, seq_len, group):
    win, dil = DIL_PAIRS[group]
    steps, gw = ATT_STEPS, ATT_GROUP_WIDTH
    span = steps * dil
    chunk = ATT_CHUNK
    assert win // dil == steps and chunk % span == 0 and seq_len % chunk == 0
    chunks = seq_len // chunk
    pairs = gw // LANES
    cur = pl.BlockSpec((chunk, LANES), lambda i: (i // pairs, pairs * group + i % pairs))
    prev = pl.BlockSpec((span, LANES), lambda i: (jnp.maximum((i // pairs) * (chunk // span) - 1, 0),
                                                  pairs * group + i % pairs))
    out = pl.BlockSpec((chunk, LANES), lambda i: (i // pairs, i % pairs))
    slopes = tuple(_alibi_slopes()[group * HEADS_PER_GROUP:(group + 1) * HEADS_PER_GROUP])
    rows = n_seq * seq_len
    return Job(
        kernel=functools.partial(_dil_attn_kernel, dil=dil, slopes=slopes, chunks=chunks, pairs=pairs),
        args=[q, k, k, v, v],
        in_specs=[cur, cur, prev, cur, prev],
        out_specs=[out, out],
        out_shapes=[jax.ShapeDtypeStruct((rows, gw), F32), jax.ShapeDtypeStruct((rows, gw), F32)],
        scratch=[],
        steps=n_seq * chunks * pairs,
    )


def _mem_attention(q_ref, mk_ref, mv_ref):
    hd = MEM_HEAD_DIM
    outs = []
    for h in range(MEM_HEADS):
        lanes = slice(h * hd, (h + 1) * hd)
        s = lax.dot_general(q_ref[:, lanes].astype(BF16), mk_ref[:, lanes].astype(BF16),
                            (((1,), (1,)), ((), ())), preferred_element_type=F32) * (hd ** -0.5)
        m = jnp.max(s, axis=-1, keepdims=True)
        p = jnp.exp(s - m)
        den = jnp.sum(p, axis=-1, keepdims=True)
        outs.append(jnp.dot(p.astype(BF16), mv_ref[:, lanes].astype(BF16), preferred_element_type=F32) / den)
    return jnp.concatenate(outs, axis=1)


def _cache_attn_kernel(q_ref, kn_ref, vn_ref, k1_ref, v1_ref, k2_ref, v2_ref, k3_ref, v3_ref, o_ref,
                       qt_scr, vnt_scr, *, slopes):
    hpg, hd = HEADS_PER_GROUP, ATT_HEAD_DIM
    j = pl.program_id(0) % hpg
    nb = q_ref.shape[0]
    scale = hd ** -0.5
    q, kn = q_ref[...], kn_ref[...]
    qt_scr[...] = q.T
    vnt_scr[...] = vn_ref[...].T
    lane_head = lax.broadcasted_iota(jnp.int32, q.shape, 1) // hd
    outs, lses = [], []
    for g, (k_ref, v_ref) in enumerate(((k1_ref, v1_ref), (k2_ref, v2_ref), (k3_ref, v3_ref))):
        dil = DIL_PAIRS[g][1]
        n_pos = k_ref.shape[-1]
        rows = pl.ds(pl.multiple_of((g * hpg + j) * hd, hd), hd)
        qg, vng = qt_scr[rows, :], vnt_scr[rows, :]
        slope = sum(jnp.where(j == h, slopes[g * hpg + h], 0.0) for h in range(hpg))
        back = n_pos - lax.broadcasted_iota(jnp.int32, (1, n_pos), 1)
        bias = jnp.where(back % dil == 0, -slope * back.astype(F32), NEG_BIG)
        row = lax.broadcasted_iota(jnp.int32, (nb, n_pos), 0)
        s = jnp.zeros((nb, n_pos), F32)
        for b in range(nb):
            s = jnp.where(row == b, jnp.sum(k_ref[b] * qg[:, b:b + 1], axis=0, keepdims=True), s)
        s = s * scale + bias
        s_new = jnp.sum(jnp.where(lane_head == g * hpg + j, q * kn, 0.0), axis=1, keepdims=True) * scale
        m = jnp.maximum(jnp.max(s, axis=1, keepdims=True), s_new)
        p = jnp.exp(s - m)
        p_new = jnp.exp(s_new - m)
        den = jnp.sum(p, axis=1, keepdims=True) + p_new
        cols = []
        for b in range(nb):
            acc = jnp.sum(v_ref[b] * p[b:b + 1, :], axis=1, keepdims=True) + p_new[b:b + 1, :] * vng[:, b:b + 1]
            cols.append(acc / den[b:b + 1, :])
        outs.append(cols)
        lses.append(m + jnp.log(den))
    top = functools.reduce(jnp.maximum, lses)
    ws = [jnp.exp(l - top) for l in lses]
    total = sum(ws)
    o_ref[...] = jnp.concatenate(
        [sum(w[b:b + 1, :] * cols[b] for w, cols in zip(ws, outs)) / total[b:b + 1, :] for b in range(nb)], axis=1)


def _cache_attn_job(q, k_new, v_new, caches_k, caches_v, nb):
    rows = q.shape[0]
    hpg, hd = HEADS_PER_GROUP, ATT_HEAD_DIM
    assert rows % nb == 0 and nb % SUBLANES == 0
    new = pl.BlockSpec((nb, ATT_WIDTH), lambda i: (i // hpg, 0))
    specs, args = [new, new, new], [q, k_new, v_new]
    for g, (win, dil) in enumerate(DIL_PAIRS):
        for c in (caches_k[g], caches_v[g]):
            assert c.shape == (rows, win, hpg, hd) and win % dil == 0, c.shape
            args.append(c.transpose(0, 2, 3, 1))
            specs.append(pl.BlockSpec((nb, None, hd, win), lambda i: (i // hpg, i % hpg, 0, 0)))
    return Job(
        kernel=functools.partial(_cache_attn_kernel, slopes=tuple(_alibi_slopes())),
        args=args,
        in_specs=specs,
        out_specs=[pl.BlockSpec((None, None, hd, nb), lambda i: (i % hpg, i // hpg, 0, 0))],
        out_shapes=[jax.ShapeDtypeStruct((hpg, rows // nb, hd, nb), F32)],
        scratch=[pltpu.VMEM((ATT_WIDTH, nb), F32) for _ in range(2)],
        steps=(rows // nb) * hpg,
    )


def _cache_attn_result(out):
    hpg, blocks, hd, nb = out.shape
    return out.transpose(1, 3, 0, 2).reshape(blocks * nb, hpg * hd)


def _cache_mem_attn_kernel(q_ref, k_ref, v_ref, o_ref):
    nb, rows, hd = k_ref.shape
    tiles = rows // SUBLANES

    def fold(x):
        return x, pltpu.roll(x, MEM_HEADS, 2)

    q = q_ref[...][:, None]
    k = k_ref[...].reshape(nb, tiles, SUBLANES, hd)
    v = v_ref[...].reshape(nb, tiles, SUBLANES, hd)
    s = jnp.sum(k * q, axis=-1, keepdims=True) * (MEM_HEAD_DIM ** -0.5)
    m = jnp.maximum(*fold(jnp.max(s, axis=1, keepdims=True)))
    p = jnp.exp(s - m)
    den = sum(fold(jnp.sum(p, axis=1, keepdims=True)))
    acc = sum(fold(jnp.sum(p * v, axis=1, keepdims=True)))
    o_ref[...] = (acc / den)[:, 0]


def _cache_mem_job(qm, mem_k, mem_v, nb):
    rows = qm.shape[0]
    assert rows % nb == 0 and SUBLANES == 2 * MEM_HEADS
    q4 = qm.reshape(rows, MEM_HEADS, MEM_HEAD_DIM)
    kv = pl.BlockSpec((nb, N_MEM * MEM_HEADS, MEM_HEAD_DIM), lambda i: (i, 0, 0))
    q = pl.BlockSpec((nb, SUBLANES, MEM_HEAD_DIM), lambda i: (i, 0, 0))
    return Job(
        kernel=_cache_mem_attn_kernel,
        args=[jnp.concatenate([q4, q4], axis=1), mem_k.reshape(rows, N_MEM * MEM_HEADS, MEM_HEAD_DIM),
              mem_v.reshape(rows, N_MEM * MEM_HEADS, MEM_HEAD_DIM)],
        in_specs=[q, kv, kv],
        out_specs=[q],
        out_shapes=[jax.ShapeDtypeStruct((rows, SUBLANES, MEM_HEAD_DIM), F32)],
        scratch=[],
        steps=rows // nb,
    )


def _cache_mem_result(out):
    return out[:, :MEM_HEADS].reshape(out.shape[0], MEM_WIDTH)


def _merge_kernel(x_ref, *refs, n_y, n_att, n_mem):
    y_refs, att_refs = refs[:n_y], refs[n_y:n_y + n_att]
    mem_refs = refs[n_y + n_att:n_y + n_att + n_mem]
    g1_ref, wgate_ref, wglu_ref, watt_ref, wmem_ref, wout_ref, o_ref = refs[n_y + n_att + n_mem:]
    d = D_MODEL
    x = x_ref[...]
    h = _rmsnorm(x, g1_ref[...]).astype(BF16)

    def gate(i):
        return jax.nn.sigmoid(jnp.dot(h, wgate_ref[:, i * d:(i + 1) * d], preferred_element_type=F32))

    y = jnp.concatenate([r[...] for r in y_refs], axis=1) if n_y > 1 else y_refs[0][...]
    z = jax.nn.gelu(y).astype(BF16)
    glu = jnp.dot(z, wglu_ref[...], preferred_element_type=F32)
    merged = gate(0) * (glu[:, 0:d] * jax.nn.sigmoid(glu[:, d:2 * d]))
    if n_att == 1:
        att = att_refs[0][...]
    else:
        lses = [r[...] for r in att_refs[1::2]]
        top = functools.reduce(jnp.maximum, lses)
        ws = [jnp.exp(l - top) for l in lses]
        att = sum(w * r[...] for w, r in zip(ws, att_refs[0::2])) / sum(ws)
    b_att = jnp.dot(att.astype(BF16), watt_ref[...], preferred_element_type=F32)
    merged = merged + gate(1) * b_att
    o_mem = mem_refs[0][...] if n_mem == 1 else _mem_attention(*mem_refs)
    b_mem = jnp.dot(o_mem.astype(BF16), wmem_ref[...], preferred_element_type=F32)
    merged = merged + gate(2) * b_mem
    o_ref[...] = x + jnp.dot(merged.astype(BF16), wout_ref[...], preferred_element_type=F32)


def _merge_job(x, ys, atts, mems, g1, w_in, w_glu, w_att_o, w_mem_o, w_out, tm):
    rows = x.shape[0]
    gate_cols = N_BRANCH * D_MODEL
    assert rows % tm == 0 and w_in.shape == (D_MODEL, IN_WIDTH)

    def tile(a):
        return pl.BlockSpec((tm, a.shape[1]), lambda i: (i, 0))

    acts = [x, *ys, *atts, mems[0]]
    act_specs = [tile(a) for a in acts]
    if len(mems) > 1:
        tiles_per_seq = rows // mems[1].shape[0] // tm
        assert tiles_per_seq * tm * mems[1].shape[0] == rows
        act_specs += [pl.BlockSpec((None, N_MEM, MEM_WIDTH), lambda i: (i // tiles_per_seq, 0, 0))] * 2
        acts += list(mems[1:])
    weights = [w_glu, w_att_o, w_mem_o, w_out]
    return Job(
        kernel=functools.partial(_merge_kernel, n_y=len(ys), n_att=len(atts), n_mem=len(mems)),
        args=acts + [g1.reshape(1, D_MODEL), w_in] + weights,
        in_specs=act_specs + [_full((1, D_MODEL)), _resident_cols(w_in, IN_WIDTH - gate_cols, gate_cols)]
        + [_full(w.shape) for w in weights],
        out_specs=[tile(x)],
        out_shapes=[jax.ShapeDtypeStruct(x.shape, F32)],
        scratch=[],
        steps=rows // tm,
    )


def _ffn_kernel(*refs, tm, tiles_per_seq, stepwise):
    if stepwise:
        x_ref, g2_ref, wup_ref, cw_ref, cb_ref, wdn_ref, gf_ref, prev_ref, y_ref, conv_ref = refs
    else:
        x_ref, g2_ref, wup_ref, cw_ref, cb_ref, wdn_ref, gf_ref, y_ref, conv_ref, a_scr = refs
        i = pl.program_id(0)
        first = i % tiles_per_seq == 0

        @pl.when(first)
        def _():
            a_scr[0:SUBLANES, :] = jnp.zeros((SUBLANES, D_FF), F32)

        @pl.when(jnp.logical_not(first))
        def _():
            a_scr[0:SUBLANES, :] = a_scr[tm:tm + SUBLANES, :]

    x = x_ref[...]
    h = _rmsnorm(x, g2_ref[...]).astype(BF16)
    a = jnp.dot(h, wup_ref[:, 0:D_FF], preferred_element_type=F32)
    if stepwise:
        a2, a1 = prev_ref[:, 0:D_FF], prev_ref[:, D_FF:2 * D_FF]
        conv_ref[:, 0:D_FF] = a1
        conv_ref[:, D_FF:2 * D_FF] = a
    else:
        a_scr[SUBLANES:SUBLANES + tm, :] = a
        a1 = a_scr[SUBLANES - 1:SUBLANES - 1 + tm, :]
        a2 = a_scr[SUBLANES - 2:SUBLANES - 2 + tm, :]
    c = a2 * cw_ref[0:1, :] + a1 * cw_ref[1:2, :] + a * cw_ref[2:3, :] + cb_ref[...]
    v = jnp.dot(h, wup_ref[:, D_FF:2 * D_FF], preferred_element_type=F32)
    y = jnp.dot((jax.nn.gelu(c) * v).astype(BF16), wdn_ref[...], preferred_element_type=F32)
    y_ref[...] = _rmsnorm(x + y, gf_ref[...])
    if not stepwise:
        @pl.when(i % tiles_per_seq == tiles_per_seq - 1)
        def _():
            conv_ref[...] = a_scr[SUBLANES + tm - (CONV_W - 1):SUBLANES + tm, :]


def _ffn_job(x, g2, w_up, conv_w, conv_b, w_down, gf, tm, n_seq=None, prev=None):
    rows, d = x.shape
    assert rows % tm == 0
    stepwise = prev is not None
    weights = [g2.reshape(1, d), w_up, conv_w, conv_b.reshape(1, D_FF), w_down, gf.reshape(1, d)]
    in_specs = [pl.BlockSpec((tm, d), lambda i: (i, 0))] + [_full(w.shape) for w in weights]
    args = [x] + weights
    if stepwise:
        tiles_per_seq = 1
        in_specs.append(pl.BlockSpec((tm, 2 * D_FF), lambda i: (i, 0)))
        args.append(prev)
        conv_spec = pl.BlockSpec((tm, 2 * D_FF), lambda i: (i, 0))
        conv_shape = jax.ShapeDtypeStruct((rows, 2 * D_FF), F32)
        scratch = []
    else:
        tiles_per_seq = rows // n_seq // tm
        assert tiles_per_seq * tm * n_seq == rows
        conv_spec = pl.BlockSpec((None, CONV_W - 1, D_FF), lambda i: (i // tiles_per_seq, 0, 0))
        conv_shape = jax.ShapeDtypeStruct((n_seq, CONV_W - 1, D_FF), F32)
        scratch = [pltpu.VMEM((tm + 2 * SUBLANES, D_FF), F32)]
    return Job(
        kernel=functools.partial(_ffn_kernel, tm=tm, tiles_per_seq=tiles_per_seq, stepwise=stepwise),
        args=args,
        in_specs=in_specs,
        out_specs=[pl.BlockSpec((tm, d), lambda i: (i, 0)), conv_spec],
        out_shapes=[jax.ShapeDtypeStruct((rows, d), F32), conv_shape],
        scratch=scratch,
        steps=rows // tm,
    )


def kernel(x_prompt, x_sample, state_ssm_re, state_ssm_im, cache_w1_k, cache_w1_v, cache_w2_k, cache_w2_v, cache_w3_k, cache_w3_v, cache_mem_k, cache_mem_v, state_ffn_conv, mem_prompt, norm1_g, w_in, ssm_a_re, ssm_a_im, ssm_log_dt, ssm_b_re, ssm_b_im, ssm_c_re, ssm_c_im, ssm_d, w_ssm_glu, w_att_o, mem_norm_g, w_mem_kv, w_mem_o, w_out, norm2_g, w_up, ffn_conv_w, ffn_conv_b, w_down, final_norm_g):
    n_seq, seq_len, d = x_prompt.shape
    n_dec, dec_len, _ = x_sample.shape
    depth = norm1_g.shape[0]
    assert d == D_MODEL and depth == 1 and dec_len == 1
    assert w_in.shape == (depth, D_MODEL, IN_WIDTH) and w_up.shape == (depth, D_MODEL, 2 * D_FF)
    assert mem_prompt.shape == (n_seq, N_MEM, D_MODEL)
    assert ssm_a_re.shape == (depth, SSM_GROUPS, SSM_STATE)
    assert seq_len % ROW_TILE == 0 and n_dec % SUBLANES == 0
    hpg, hd = HEADS_PER_GROUP, ATT_HEAD_DIM
    rows_p = n_seq * seq_len
    dec_tile = n_dec if n_dec <= ROW_TILE else ROW_TILE

    w_in_b = w_in[0].astype(BF16)

    m_mat, w_mat, v_mat, v0_mat, a1, apow = _ssm_prep(ssm_a_re[0], ssm_a_im[0], ssm_log_dt[0], ssm_b_re[0],
                                                      ssm_b_im[0], ssm_c_re[0], ssm_c_im[0])
    w0_mat = w_mat[:, (SSM_T - 1) * LANES:, :]

    xs = x_sample.reshape(n_dec, d)
    us, qs, ks, vs, qms = _run([_norm_proj_job(xs, norm1_g[0], w_in_b, PROJ_SPLITS, dec_tile)], "norm_proj")[0]
    cache_job = _cache_attn_job(qs, ks, vs, (cache_w1_k[0], cache_w2_k[0], cache_w3_k[0]),
                                (cache_w1_v[0], cache_w2_v[0], cache_w3_v[0]), CACHE_ROWS_PER_STEP)
    proj_steps = rows_p // ROW_TILE
    mem_rows = n_dec // proj_steps if n_dec % proj_steps == 0 else CACHE_ROWS_PER_STEP
    mem_job = _cache_mem_job(qms, cache_mem_k[0], cache_mem_v[0], mem_rows)

    xp = x_prompt.reshape(rows_p, d)
    gw = ATT_GROUP_WIDTH
    keeps = [min(win, seq_len) for win, _ in DIL_PAIRS]
    windows = [(src, g * gw, gw, keep) for g, keep in enumerate(keeps) for src in (2, 3)]
    proj_job = _norm_proj_job(xp, norm1_g[0], w_in_b, PROJ_SPLITS, ROW_TILE, n_seq, windows)
    (u, q, k, v, qm, *kv_win), (mem_out,) = _run_pair(proj_job, mem_job, "norm_proj_cache_mem")
    mem_s = _cache_mem_result(mem_out)

    dil_jobs = [_dil_attn_job(q, k, v, n_seq, seq_len, g) for g in range(len(DIL_PAIRS))]
    (e,), att0 = _run_pair(_ssm_end_state_job(u, n_seq, w_mat, dil_jobs[0].steps), dil_jobs[0], "ssm_end_state_dil")
    sp_re, sp_im, fin_re, fin_im = _ssm_scan(e, apow, n_seq)
    y_ssm, att1 = _run_pair(_ssm_output_job(u, n_seq, sp_re, sp_im, m_mat, v_mat, ssm_d[0], dil_jobs[1].steps),
                            dil_jobs[1], "ssm_output_dil")
    casts = _cast_job([w_up[0], w_down[0], w_ssm_glu[0], w_out[0], w_mem_o[0], w_mem_kv[0], w_att_o[0]],
                      dil_jobs[2].steps)
    att2, (w_up_b, w_down_b, w_glu_b, w_out_b, w_memo_b, w_memkv_b, w_att_b) = _run_pair(
        dil_jobs[2], casts, "dil_attn_casts")
    merge_w = (norm1_g[0], w_in_b, w_glu_b, w_att_b, w_memo_b, w_out_b)
    ffn_w = (norm2_g[0], w_up_b, ffn_conv_w[0], ffn_conv_b[0], w_down_b, final_norm_g)
    atts = att0 + att1 + att2

    mk, mv = _run([_norm_proj_job(mem_prompt.reshape(n_seq * N_MEM, d), mem_norm_g[0], w_memkv_b,
                                  (MEM_WIDTH, MEM_WIDTH), ROW_TILE)], "mem_kv")[0]
    mems = [qm, mk.reshape(n_seq, N_MEM, MEM_WIDTH), mv.reshape(n_seq, N_MEM, MEM_WIDTH)]
    merge_tile = rows_p // cache_job.steps
    if not (rows_p % cache_job.steps == 0 and merge_tile % LANES == 0 and merge_tile <= ROW_TILE
            and seq_len % merge_tile == 0):
        merge_tile = ROW_TILE
    (x1,), (att_out,) = _run_pair(_merge_job(xp, y_ssm, atts, mems, *merge_w, merge_tile), cache_job,
                                  "merge_cache_attn")
    att_s = _cache_attn_result(att_out)
    y_p, conv_p = _run([_ffn_job(x1, *ffn_w, ROW_TILE, n_seq=n_seq)], "ffn")[0]

    def final_state(s):
        return s.reshape(1, n_seq, SSM_GROUPS, SSM_STATE)
    win_p = [t.reshape(n_seq, hpg, hd, t.shape[-1]).transpose(0, 3, 1, 2)[None] for t in kv_win]
    mem_kv = [mk.reshape(1, n_seq, N_MEM, MEM_HEADS, MEM_HEAD_DIM), mv.reshape(1, n_seq, N_MEM, MEM_HEADS, MEM_HEAD_DIM)]

    ys_ssm, sn_re, sn_im = _ssm_step(us, state_ssm_re[0], state_ssm_im[0], w0_mat, v0_mat, a1, ssm_d[0])
    n_g = len(DIL_PAIRS)
    ks4, vs4 = (t.reshape(n_dec, n_g, hpg, hd) for t in (ks, vs))
    xs1 = _run([_merge_job(xs, [ys_ssm], [att_s], [mem_s], *merge_w, dec_tile)], "merge")[0][0]
    y_s, conv_s = _run([_ffn_job(xs1, *ffn_w, dec_tile,
                                 prev=state_ffn_conv[0].reshape(n_dec, (CONV_W - 1) * D_FF))], "ffn_step")[0]

    win_s = []
    for g in range(len(DIL_PAIRS)):
        win_s += [ks4[None, :, g:g + 1], vs4[None, :, g:g + 1]]

    return (y_p.reshape(n_seq, seq_len, d), y_s.reshape(n_dec, 1, d),
            final_state(fin_re), final_state(fin_im), *win_p, *mem_kv, conv_p[None],
            sn_re.reshape(1, n_dec, SSM_GROUPS, SSM_STATE), sn_im.reshape(1, n_dec, SSM_GROUPS, SSM_STATE),
            *win_s, conv_s.reshape(1, n_dec, CONV_W - 1, D_FF))
```

```python
import collections
import functools

import jax
import jax.numpy as jnp
from jax import lax
from jax.experimental import pallas as pl
from jax.experimental.pallas import tpu as pltpu

F32 = jnp.float32
BF16 = jnp.bfloat16

D_MODEL = 1024
SSM_WIDTH = 512
SSM_GROUP = 16
SSM_GROUPS = 32
SSM_STATE = 64
ATT_HEAD_DIM = 64
HEADS_PER_GROUP = 4
DIL_PAIRS = ((128, 1), (512, 4), (2048, 16))
ATT_HEADS = len(DIL_PAIRS) * HEADS_PER_GROUP
ATT_WIDTH = ATT_HEADS * ATT_HEAD_DIM
ATT_GROUP_WIDTH = HEADS_PER_GROUP * ATT_HEAD_DIM
ATT_STEPS = 128
N_MEM = 256
MEM_HEADS = 4
MEM_HEAD_DIM = 128
MEM_WIDTH = MEM_HEADS * MEM_HEAD_DIM
N_BRANCH = 3
D_FF = 2816
CONV_W = 3
EPS = 1e-6
PROJ_SPLITS = (SSM_WIDTH, ATT_WIDTH, ATT_WIDTH, ATT_WIDTH, MEM_WIDTH)
IN_WIDTH = sum(PROJ_SPLITS) + N_BRANCH * D_MODEL

LANES = 128
SUBLANES = 8
VMEM_LIMIT_BYTES = 56 * 1024 * 1024

SSM_T = 8
SSM_LANE_BLOCKS = SSM_WIDTH // LANES
SSM_GROUPS_PER_BLOCK = LANES // SSM_GROUP
SSM_STATES_PER_BLOCK = SSM_GROUPS_PER_BLOCK * SSM_STATE
SSM_STATE_LANES = 2 * SSM_STATES_PER_BLOCK

ROW_TILE = 512
PROJ_COL_CHUNK = 512
SSM_ROW_TILE = 256
CACHE_ROWS_PER_STEP = 8
ATT_CHUNK = 2048
NEG_BIG = -1e30


def _alibi_slopes():
    return [float(2.0 ** (-8.0 * h / ATT_HEADS)) for h in range(1, ATT_HEADS + 1)]


def _params(*sem):
    return pltpu.CompilerParams(dimension_semantics=sem, vmem_limit_bytes=VMEM_LIMIT_BYTES)


def _rmsnorm(x, g):
    ms = jnp.mean(x * x, axis=-1, keepdims=True)
    return x * lax.rsqrt(ms + EPS) * g


def _full(shape):
    nd = len(shape)
    return pl.BlockSpec(shape, lambda *_: (0,) * nd, pipeline_mode=pl.Buffered(1))


def _resident_cols(w, start, width):
    return pl.BlockSpec((pl.Element(w.shape[0]), pl.Element(width)), lambda *_: (0, start),
                        pipeline_mode=pl.Buffered(1))


Job = collections.namedtuple("Job", "kernel args in_specs out_specs out_shapes scratch steps")


def _jobs_kernel(*refs, kernels, layout):
    n_in, n_out = sum(l[0] for l in layout), sum(l[1] for l in layout)
    ins, outs, scratch = refs[:n_in], refs[n_in:n_in + n_out], refs[n_in + n_out:]
    i0 = o0 = s0 = 0
    for kernel, (ni, no, ns) in zip(kernels, layout):
        kernel(*ins[i0:i0 + ni], *outs[o0:o0 + no], *scratch[s0:s0 + ns])
        i0, o0, s0 = i0 + ni, o0 + no, s0 + ns


def _run(jobs, name):
    steps = jobs[0].steps
    assert all(j.steps == steps for j in jobs)
    layout = tuple((len(j.args), len(j.out_shapes), len(j.scratch)) for j in jobs)
    outs = pl.pallas_call(
        functools.partial(_jobs_kernel, kernels=tuple(j.kernel for j in jobs), layout=layout),
        grid=(steps,),
        in_specs=[s for j in jobs for s in j.in_specs],
        out_specs=[s for j in jobs for s in j.out_specs],
        out_shape=[s for j in jobs for s in j.out_shapes],
        scratch_shapes=[s for j in jobs for s in j.scratch],
        compiler_params=_params("arbitrary"),
        name=name,
    )(*[a for j in jobs for a in j.args])
    split, o0 = [], 0
    for _, no, _ in layout:
        split.append(list(outs[o0:o0 + no]))
        o0 += no
    return split


def _run_pair(main, rider, name):
    m, (r,) = _run_with(main, [rider], name)
    return m, r


def _run_with(main, riders, name):
    fused = [r for r in riders if r.steps == main.steps]
    outs = _run(fused + [main], name)
    by_rider = {id(r): o for r, o in zip(fused, outs)}
    return outs[-1], [by_rider[id(r)] if id(r) in by_rider else _run([r], name + "_rider")[0] for r in riders]


def _cast_kernel(*refs):
    n = len(refs) // 2
    for src, dst in zip(refs[:n], refs[n:]):
        dst[...] = src[...].astype(dst.dtype)


def _cast_job(weights, steps):
    specs = []
    for w in weights:
        rows = w.shape[0] // steps
        assert rows * steps == w.shape[0] and rows % (2 * SUBLANES) == 0, (w.shape, steps)
        specs.append(pl.BlockSpec((rows, w.shape[1]), lambda i: (i, 0)))
    return Job(kernel=_cast_kernel, args=list(weights), in_specs=specs, out_specs=specs,
               out_shapes=[jax.ShapeDtypeStruct(w.shape, BF16) for w in weights], scratch=[], steps=steps)


def _norm_proj_kernel(x_ref, g_ref, w_ref, *out_refs, splits, windows, tiles_per_seq):
    tm = x_ref.shape[0]
    h = _rmsnorm(x_ref[...], g_ref[...]).astype(BF16)
    off = 0
    for o_ref, width in zip(out_refs, splits):
        for c0 in range(0, width, PROJ_COL_CHUNK):
            cw = min(PROJ_COL_CHUNK, width - c0)
            o_ref[:, c0:c0 + cw] = jnp.dot(h, w_ref[:, off + c0:off + c0 + cw], preferred_element_type=F32)
        off += width
    tile = pl.program_id(0) % tiles_per_seq
    for win_ref, (src, col0, cols, win) in zip(out_refs[len(splits):], windows):
        rows = min(win, tm)

        @pl.when(tile >= tiles_per_seq - max(win // tm, 1))
        def _(win_ref=win_ref, src=src, col0=col0, cols=cols, rows=rows):
            win_ref[...] = out_refs[src][tm - rows:tm, col0:col0 + cols].T


def _norm_proj_job(x, g, w_bf16, splits, tm, n_seq=1, windows=()):
    rows, d = x.shape
    assert rows % (tm * n_seq) == 0 and sum(splits) <= w_bf16.shape[1]
    tps = rows // n_seq // tm
    win_specs, win_shapes = [], []
    for _, _, cols, win in windows:
        assert (win % tm == 0 or tm % win == 0) and win <= tps * tm
        n_tiles = max(win // tm, 1)
        win_specs.append(pl.BlockSpec((None, cols, min(win, tm)), lambda i, n_tiles=n_tiles: (
            i // tps, 0, jnp.maximum(i % tps - (tps - n_tiles), 0))))
        win_shapes.append(jax.ShapeDtypeStruct((n_seq, cols, win), F32))
    return Job(
        kernel=functools.partial(_norm_proj_kernel, splits=splits, windows=tuple(windows), tiles_per_seq=tps),
        args=[x, g.reshape(1, d), w_bf16],
        in_specs=[pl.BlockSpec((tm, d), lambda i: (i, 0)), _full((1, d)), _resident_cols(w_bf16, 0, sum(splits))],
        out_specs=[pl.BlockSpec((tm, s), lambda i: (i, 0)) for s in splits] + win_specs,
        out_shapes=[jax.ShapeDtypeStruct((rows, s), F32) for s in splits] + win_shapes,
        scratch=[],
        steps=rows // tm,
    )


def _ssm_layout(a_re, a_im, log_dt, b_re, b_im, c_re, c_im):
    nb, gpb, p, c = SSM_LANE_BLOCKS, SSM_GROUPS_PER_BLOCK, SSM_STATE, SSM_GROUP
    rows = jnp.stack([a_re.reshape(nb, gpb * p), a_im.reshape(nb, gpb * p),
                      jnp.repeat(log_dt, p).reshape(nb, gpb * p)], axis=1)
    eye = jnp.eye(gpb, dtype=F32)

    def place_b(b):
        return jnp.einsum('bgpc,gh->bgchp', b.reshape(nb, gpb, p, c), eye).reshape(nb, gpb * c, gpb * p)

    def place_c(m):
        return jnp.einsum('bgcp,gh->bhpgc', m.reshape(nb, gpb, c, p), eye).reshape(nb, gpb * p, gpb * c)

    return rows, place_b(b_re), place_b(b_im), place_c(c_re), place_c(c_im)


def _ssm_prep_kernel(rows_ref, bre_ref, bim_ref, cre_ref, cim_ref,
                     m_ref, w_ref, v_ref, v0_ref, a1_ref, apow_ref, *, t_chunk):
    sp = SSM_STATES_PER_BLOCK
    a_re, a_im, dt = rows_ref[0:1, :], rows_ref[1:2, :], jnp.exp(rows_ref[2:3, :])

    def powers(k):
        mag = jnp.exp(a_re * dt * k)
        ang = a_im * dt * k
        return mag * jnp.cos(ang), mag * jnp.sin(ang)

    n_pow = 2 * SUBLANES
    assert t_chunk + 1 <= n_pow
    pw_re, pw_im = powers(lax.broadcasted_iota(jnp.int32, (n_pow, 1), 0).astype(F32))
    pwt_re, pwt_im = pw_re.T, pw_im.T
    ab_re, ab_im = pw_re[1:2, :], pw_im[1:2, :]
    den = a_re * a_re + a_im * a_im
    q_re = ((ab_re - 1.0) * a_re + ab_im * a_im) / den
    q_im = (ab_im * a_re - (ab_re - 1.0) * a_im) / den
    bre, bim = bre_ref[...], bim_ref[...]
    bb_re = q_re * bre - q_im * bim
    bb_im = q_re * bim + q_im * bre
    cre, cim = cre_ref[...], cim_ref[...]

    m_ref[...] = jnp.zeros(m_ref.shape, m_ref.dtype)
    for k in range(t_chunk):
        pk_re, pk_im = pw_re[k:k + 1, :], pw_im[k:k + 1, :]
        bk_re = bb_re * pk_re - bb_im * pk_im
        bk_im = bb_re * pk_im + bb_im * pk_re
        t = t_chunk - 1 - k
        w_ref[t * LANES:(t + 1) * LANES, 0:sp] = bk_re.astype(w_ref.dtype)
        w_ref[t * LANES:(t + 1) * LANES, sp:2 * sp] = bk_im.astype(w_ref.dtype)
        kk = (jnp.dot(bk_re, cre, precision=lax.Precision.HIGHEST, preferred_element_type=F32)
              - jnp.dot(bk_im, cim, precision=lax.Precision.HIGHEST, preferred_element_type=F32))
        kk = kk.astype(m_ref.dtype)
        for t0 in range(t_chunk - k):
            m_ref[t0 * LANES:(t0 + 1) * LANES, (t0 + k) * LANES:(t0 + k + 1) * LANES] = kk

    for t in range(t_chunk):
        pc_re, pc_im = pwt_re[:, t + 1:t + 2], pwt_im[:, t + 1:t + 2]
        v_ref[0:sp, t * LANES:(t + 1) * LANES] = (cre * pc_re - cim * pc_im).astype(v_ref.dtype)
        v_ref[sp:2 * sp, t * LANES:(t + 1) * LANES] = (-(cre * pc_im + cim * pc_re)).astype(v_ref.dtype)
    v0_ref[0:sp, :] = cre.astype(v0_ref.dtype)
    v0_ref[sp:2 * sp, :] = (-cim).astype(v0_ref.dtype)

    a1_ref[:, 0:sp] = ab_re
    a1_ref[:, sp:2 * sp] = ab_im
    steps = (lax.broadcasted_iota(jnp.int32, (SUBLANES, 1), 0) + 1) * t_chunk
    ap_re, ap_im = powers(steps.astype(F32))
    apow_ref[:, 0:sp] = ap_re
    apow_ref[:, sp:2 * sp] = ap_im


def _ssm_prep_job(a_re, a_im, log_dt, b_re, b_im, c_re, c_im):
    nb, sp, sl, tl = SSM_LANE_BLOCKS, SSM_STATES_PER_BLOCK, SSM_STATE_LANES, SSM_T * LANES

    def blk(shape):
        return pl.BlockSpec((None,) + shape, lambda b: (b, 0, 0))

    return Job(
        kernel=functools.partial(_ssm_prep_kernel, t_chunk=SSM_T),
        args=list(_ssm_layout(a_re, a_im, log_dt, b_re, b_im, c_re, c_im)),
        in_specs=[blk((3, sp)), blk((LANES, sp)), blk((LANES, sp)), blk((sp, LANES)), blk((sp, LANES))],
        out_specs=[blk((tl, tl)), blk((tl, sl)), blk((sl, tl)), blk((sl, LANES)), blk((1, sl)), blk((SUBLANES, sl))],
        out_shapes=[jax.ShapeDtypeStruct((nb, tl, tl), BF16), jax.ShapeDtypeStruct((nb, tl, sl), BF16),
                    jax.ShapeDtypeStruct((nb, sl, tl), BF16), jax.ShapeDtypeStruct((nb, sl, LANES), BF16),
                    jax.ShapeDtypeStruct((nb, 1, sl), F32), jax.ShapeDtypeStruct((nb, SUBLANES, sl), F32)],
        scratch=[],
        steps=nb,
    )


def _chunk_tokens(u_ref, t, t_chunk):
    return u_ref[pl.ds(t, u_ref.shape[0] // t_chunk, stride=t_chunk), :]


def _chunk_lanes(u_ref, t_chunk):
    return jnp.concatenate([_chunk_tokens(u_ref, t, t_chunk) for t in range(t_chunk)], axis=1)


def _ssm_end_state_kernel(*refs, t_chunk):
    nb, sl = SSM_LANE_BLOCKS, SSM_STATE_LANES
    u_refs, w_ref, e_ref = refs[:nb], refs[nb], refs[nb + 1]
    for b in range(nb):
        ub = _chunk_lanes(u_refs[b], t_chunk).astype(BF16)
        e_ref[:, b * sl:(b + 1) * sl] = jnp.dot(ub, w_ref[b], preferred_element_type=F32)


def _ssm_scan_kernel(ere_ref, eim_ref, pre_ref, pim_ref, spre_ref, spim_ref, fre_ref, fim_ref):
    n_tiles = ere_ref.shape[0] // SUBLANES
    width = ere_ref.shape[1]
    p_re, p_im = pre_ref[...], pim_ref[...]
    row = lax.broadcasted_iota(jnp.int32, (SUBLANES, width), 0)

    def shift_down(x, k):
        return jnp.where(row >= k, pltpu.roll(x, k, 0), 0.0)

    def body(i, carry):
        c_re, c_im = carry
        rows = pl.ds(pl.multiple_of(i * SUBLANES, SUBLANES), SUBLANES)
        x_re, x_im = ere_ref[rows, :], eim_ref[rows, :]
        for k in (1, 2, 4):
            a_re, a_im = p_re[k - 1:k, :], p_im[k - 1:k, :]
            s_re, s_im = shift_down(x_re, k), shift_down(x_im, k)
            x_re, x_im = x_re + a_re * s_re - a_im * s_im, x_im + a_re * s_im + a_im * s_re
        t_re = x_re + p_re * c_re - p_im * c_im
        t_im = x_im + p_re * c_im + p_im * c_re
        spre_ref[rows, :] = jnp.where(row >= 1, pltpu.roll(t_re, 1, 0), c_re)
        spim_ref[rows, :] = jnp.where(row >= 1, pltpu.roll(t_im, 1, 0), c_im)
        return t_re[SUBLANES - 1:SUBLANES, :], t_im[SUBLANES - 1:SUBLANES, :]

    zero = jnp.zeros((1, width), F32)
    f_re, f_im = lax.fori_loop(0, n_tiles, body, (zero, zero))
    fre_ref[...] = f_re
    fim_ref[...] = f_im


def _ssm_output_kernel(*refs, t_chunk):
    nb, sp = SSM_LANE_BLOCKS, SSM_STATES_PER_BLOCK
    u_refs, (spre_ref, spim_ref, m_ref, v_ref, d_ref), y_refs = refs[:nb], refs[nb:nb + 5], refs[nb + 5:]
    for b in range(nb):
        ub = _chunk_lanes(u_refs[b], t_chunk).astype(BF16)
        yb = jnp.dot(ub, m_ref[b], preferred_element_type=F32)
        states = slice(b * sp, (b + 1) * sp)
        sprev = jnp.concatenate([spre_ref[:, states], spim_ref[:, states]], axis=1).astype(BF16)
        yb = yb + jnp.dot(sprev, v_ref[b], preferred_element_type=F32)
        d = d_ref[:, b * LANES:(b + 1) * LANES]
        for t in range(t_chunk):
            y_t = yb[:, t * LANES:(t + 1) * LANES] + d * _chunk_tokens(u_refs[b], t, t_chunk)
            y_refs[b][pl.ds(t, yb.shape[0], stride=t_chunk), :] = y_t


def _ssm_tiling(rows, n_seq, steps_wanted):
    assert rows % (SSM_T * n_seq) == 0
    n_chunks = rows // SSM_T // n_seq
    assert n_chunks % SUBLANES == 0
    tr = min(SSM_ROW_TILE, n_chunks)
    if steps_wanted and (n_seq * n_chunks) % steps_wanted == 0:
        want = n_seq * n_chunks // steps_wanted
        if want % SUBLANES == 0 and n_chunks % want == 0 and want <= SSM_ROW_TILE:
            tr = want
    assert n_chunks % tr == 0
    return n_chunks, tr


def _ssm_end_state_job(u, n_seq, w_mat, steps_wanted=None):
    nb, sl = SSM_LANE_BLOCKS, SSM_STATE_LANES
    n_chunks, tr = _ssm_tiling(u.shape[0], n_seq, steps_wanted)
    tiles = n_chunks // tr
    return Job(
        kernel=functools.partial(_ssm_end_state_kernel, t_chunk=SSM_T),
        args=[u] * nb + [w_mat],
        in_specs=[pl.BlockSpec((tr * SSM_T, LANES), lambda i, b=b: (i, b)) for b in range(nb)] + [_full(w_mat.shape)],
        out_specs=[pl.BlockSpec((tr, nb * sl), lambda i: (i % tiles, i // tiles))],
        out_shapes=[jax.ShapeDtypeStruct((n_chunks, n_seq * nb * sl), F32)],
        scratch=[],
        steps=n_seq * tiles,
    )


def _ssm_scan(e, apow, n_seq):
    nb, sp = SSM_LANE_BLOCKS, SSM_STATES_PER_BLOCK
    n_chunks = e.shape[0]
    col = pl.BlockSpec((n_chunks, sp), lambda g: (0, g))
    fin = pl.BlockSpec((1, sp), lambda g: (0, g))
    return pl.pallas_call(
        _ssm_scan_kernel,
        grid=(n_seq * nb,),
        in_specs=[pl.BlockSpec((n_chunks, sp), lambda g: (0, 2 * g)),
                  pl.BlockSpec((n_chunks, sp), lambda g: (0, 2 * g + 1)),
                  pl.BlockSpec((None, SUBLANES, sp), lambda g: (g % nb, 0, 0)),
                  pl.BlockSpec((None, SUBLANES, sp), lambda g: (g % nb, 0, 1))],
        out_specs=[col, col, fin, fin],
        out_shape=[jax.ShapeDtypeStruct((n_chunks, n_seq * nb * sp), F32)] * 2
        + [jax.ShapeDtypeStruct((1, n_seq * nb * sp), F32)] * 2,
        compiler_params=_params("parallel"),
        name="ssm_scan",
    )(e, e, apow, apow)


def _ssm_output_job(u, n_seq, sp_re, sp_im, m_mat, v_mat, d_skip, steps_wanted=None):
    nb, sp = SSM_LANE_BLOCKS, SSM_STATES_PER_BLOCK
    rows = u.shape[0]
    n_chunks, tr = _ssm_tiling(rows, n_seq, steps_wanted)
    tiles = n_chunks // tr
    sp_spec = pl.BlockSpec((tr, nb * sp), lambda i: (i % tiles, i // tiles))
    return Job(
        kernel=functools.partial(_ssm_output_kernel, t_chunk=SSM_T),
        args=[u] * nb + [sp_re, sp_im, m_mat, v_mat, d_skip.reshape(1, SSM_WIDTH)],
        in_specs=[pl.BlockSpec((tr * SSM_T, LANES), lambda i, b=b: (i, b)) for b in range(nb)]
        + [sp_spec, sp_spec, _full(m_mat.shape), _full(v_mat.shape), _full((1, SSM_WIDTH))],
        out_specs=[pl.BlockSpec((tr * SSM_T, LANES), lambda i: (i, 0)) for _ in range(nb)],
        out_shapes=[jax.ShapeDtypeStruct((rows, LANES), F32) for _ in range(nb)],
        scratch=[],
        steps=n_seq * tiles,
    )


def _ssm_step_kernel(u_ref, sre_ref, sim_ref, w0_ref, v0_ref, a1_ref, d_ref, y_ref, nre_ref, nim_ref):
    sp = SSM_STATES_PER_BLOCK
    for b in range(SSM_LANE_BLOCKS):
        lanes = slice(b * LANES, (b + 1) * LANES)
        states = slice(b * sp, (b + 1) * sp)
        u = u_ref[:, lanes]
        e = jnp.dot(u.astype(BF16), w0_ref[b], preferred_element_type=F32)
        a_re, a_im = a1_ref[b, :, 0:sp], a1_ref[b, :, sp:2 * sp]
        s_re, s_im = sre_ref[states, :].T, sim_ref[states, :].T
        n_re = a_re * s_re - a_im * s_im + e[:, 0:sp]
        n_im = a_re * s_im + a_im * s_re + e[:, sp:2 * sp]
        nre_ref[states, :] = n_re.T
        nim_ref[states, :] = n_im.T
        sn = jnp.concatenate([n_re, n_im], axis=1).astype(BF16)
        y_ref[:, lanes] = jnp.dot(sn, v0_ref[b], preferred_element_type=F32) + d_ref[:, lanes] * u


def _ssm_step(u, s_re, s_im, w0, v0, a1, d_skip):
    rows = u.shape[0]
    ns = SSM_GROUPS * SSM_STATE
    assert rows % LANES == 0
    args = (u, s_re.reshape(rows, ns).T, s_im.reshape(rows, ns).T, w0, v0, a1, d_skip.reshape(1, SSM_WIDTH))
    y, n_re, n_im = pl.pallas_call(
        _ssm_step_kernel,
        grid=(1,),
        in_specs=[_full(a.shape) for a in args],
        out_specs=[_full((rows, SSM_WIDTH)), _full((ns, rows)), _full((ns, rows))],
        out_shape=[jax.ShapeDtypeStruct((rows, SSM_WIDTH), F32), jax.ShapeDtypeStruct((ns, rows), F32),
                   jax.ShapeDtypeStruct((ns, rows), F32)],
        compiler_params=_params("arbitrary"),
        name="ssm_step",
    )(*args)
    return y, n_re.T.reshape(s_re.shape), n_im.T.reshape(s_im.shape)


def _dil_attn_kernel(q_ref, kc_ref, kp_ref, vc_ref, vp_ref, o_ref, lse_ref, *, dil, slopes, chunks, pairs):
    steps, hd = ATT_STEPS, ATT_HEAD_DIM
    chunk = q_ref.shape[0]
    span = steps * dil
    step = pl.program_id(0)
    first_chunk = (step // pairs) % chunks == 0
    pair = step % pairs
    qi = lax.broadcasted_iota(jnp.int32, (steps, 2 * steps), 0)
    kj = lax.broadcasted_iota(jnp.int32, (steps, 2 * steps), 1)
    dist = qi + steps - kj
    band = (dist >= 0) & (dist <= steps)
    distf = (dist * dil).astype(F32)
    lane = lax.broadcasted_iota(jnp.int32, (steps, LANES), 1)
    heads = [lane < hd, lane >= hd]
    biases, first_biases = [], []
    for hh in range(2):
        slope = sum(jnp.where(pair == p, slopes[2 * p + hh], 0.0) for p in range(pairs))
        bias = jnp.where(band, -slope * distf, NEG_BIG)
        biases.append(bias)
        first_biases.append(jnp.where(first_chunk & (kj < steps), NEG_BIG, bias))

    def rows(ref, start):
        return ref[pl.ds(start, steps, stride=dil), :] if dil > 1 else ref[pl.ds(start, steps), :]

    for r in range(dil):
        for qb in range(chunk // span):
            start = r + qb * span
            q = rows(q_ref, start) * (hd ** -0.5)
            if qb == 0:
                k_prev, v_prev = rows(kp_ref, r), rows(vp_ref, r)
            else:
                k_prev, v_prev = rows(kc_ref, start - span), rows(vc_ref, start - span)
            kk = jnp.concatenate([k_prev, rows(kc_ref, start)], axis=0).astype(BF16)
            vv = jnp.concatenate([v_prev, rows(vc_ref, start)], axis=0).astype(BF16)
            out = jnp.zeros((steps, LANES), F32)
            lse = jnp.zeros((steps, LANES), F32)
            for hh in range(2):
                qh = jnp.where(heads[hh], q, 0.0).astype(BF16)
                s = lax.dot_general(qh, kk, (((1,), (1,)), ((), ())), preferred_element_type=F32)
                s = s + (first_biases[hh] if qb == 0 else biases[hh])
                m = jnp.max(s, axis=-1, keepdims=True)
                p = jnp.exp(s - m)
                den = jnp.sum(p, axis=-1, keepdims=True)
                oh = jnp.dot(p.astype(BF16), vv, preferred_element_type=F32) / den
                out = jnp.where(heads[hh], oh, out)
                lse = jnp.where(heads[hh], m + jnp.log(den), lse)
            if dil > 1:
                o_ref[pl.ds(start, steps, stride=dil), :] = out
                lse_ref[pl.ds(start, steps, stride=dil), :] = lse
            else:
                o_ref[pl.ds(start, steps), :] = out
                lse_ref[pl.ds(start, steps), :] = lse


def _dil_attn_job(q, k, v, n_seq, seq_len, group):
    win, dil = DIL_PAIRS[group]
    steps, gw = ATT_STEPS, ATT_GROUP_WIDTH
    span = steps * dil
    chunk = ATT_CHUNK
    assert win // dil == steps and chunk % span == 0 and seq_len % chunk == 0
    chunks = seq_len // chunk
    pairs = gw // LANES
    cur = pl.BlockSpec((chunk, LANES), lambda i: (i // pairs, pairs * group + i % pairs))
    prev = pl.BlockSpec((span, LANES), lambda i: (jnp.maximum((i // pairs) * (chunk // span) - 1, 0),
                                                  pairs * group + i % pairs))
    out = pl.BlockSpec((chunk, LANES), lambda i: (i // pairs, i % pairs))
    slopes = tuple(_alibi_slopes()[group * HEADS_PER_GROUP:(group + 1) * HEADS_PER_GROUP])
    rows = n_seq * seq_len
    return Job(
        kernel=functools.partial(_dil_attn_kernel, dil=dil, slopes=slopes, chunks=chunks, pairs=pairs),
        args=[q, k, k, v, v],
        in_specs=[cur, cur, prev, cur, prev],
        out_specs=[out, out],
        out_shapes=[jax.ShapeDtypeStruct((rows, gw), F32), jax.ShapeDtypeStruct((rows, gw), F32)],
        scratch=[],
        steps=n_seq * chunks * pairs,
    )


def _mem_attention(q_ref, mk_ref, mv_ref):
    hd = MEM_HEAD_DIM
    outs = []
    for h in range(MEM_HEADS):
        lanes = slice(h * hd, (h + 1) * hd)
        s = lax.dot_general(q_ref[:, lanes].astype(BF16), mk_ref[:, lanes].astype(BF16),
                            (((1,), (1,)), ((), ())), preferred_element_type=F32) * (hd ** -0.5)
        m = jnp.max(s, axis=-1, keepdims=True)
        p = jnp.exp(s - m)
        den = jnp.sum(p, axis=-1, keepdims=True)
        outs.append(jnp.dot(p.astype(BF16), mv_ref[:, lanes].astype(BF16), preferred_element_type=F32) / den)
    return jnp.concatenate(outs, axis=1)


def _cache_attn_kernel(q_ref, kn_ref, vn_ref, k1_ref, v1_ref, k2_ref, v2_ref, k3_ref, v3_ref, o_ref,
                       qt_scr, vnt_scr, *, slopes):
    hpg, hd = HEADS_PER_GROUP, ATT_HEAD_DIM
    j = pl.program_id(0) % hpg
    nb = q_ref.shape[0]
    scale = hd ** -0.5
    q, kn = q_ref[...], kn_ref[...]
    qt_scr[...] = q.T
    vnt_scr[...] = vn_ref[...].T
    lane_head = lax.broadcasted_iota(jnp.int32, q.shape, 1) // hd
    outs, lses = [], []
    for g, (k_ref, v_ref) in enumerate(((k1_ref, v1_ref), (k2_ref, v2_ref), (k3_ref, v3_ref))):
        dil = DIL_PAIRS[g][1]
        n_pos = k_ref.shape[-1]
        rows = pl.ds(pl.multiple_of((g * hpg + j) * hd, hd), hd)
        qg, vng = qt_scr[rows, :], vnt_scr[rows, :]
        slope = sum(jnp.where(j == h, slopes[g * hpg + h], 0.0) for h in range(hpg))
        back = n_pos - lax.broadcasted_iota(jnp.int32, (1, n_pos), 1)
        bias = jnp.where(back % dil == 0, -slope * back.astype(F32), NEG_BIG)
        row = lax.broadcasted_iota(jnp.int32, (nb, n_pos), 0)
        s = jnp.zeros((nb, n_pos), F32)
        for b in range(nb):
            s = jnp.where(row == b, jnp.sum(k_ref[b] * qg[:, b:b + 1], axis=0, keepdims=True), s)
        s = s * scale + bias
        s_new = jnp.sum(jnp.where(lane_head == g * hpg + j, q * kn, 0.0), axis=1, keepdims=True) * scale
        m = jnp.maximum(jnp.max(s, axis=1, keepdims=True), s_new)
        p = jnp.exp(s - m)
        p_new = jnp.exp(s_new - m)
        den = jnp.sum(p, axis=1, keepdims=True) + p_new
        cols = []
        for b in range(nb):
            acc = jnp.sum(v_ref[b] * p[b:b + 1, :], axis=1, keepdims=True) + p_new[b:b + 1, :] * vng[:, b:b + 1]
            cols.append(acc / den[b:b + 1, :])
        outs.append(cols)
        lses.append(m + jnp.log(den))
    top = functools.reduce(jnp.maximum, lses)
    ws = [jnp.exp(l - top) for l in lses]
    total = sum(ws)
    o_ref[...] = jnp.concatenate(
        [sum(w[b:b + 1, :] * cols[b] for w, cols in zip(ws, outs)) / total[b:b + 1, :] for b in range(nb)], axis=1)


def _cache_attn_job(q, k_new, v_new, caches_k, caches_v, nb):
    rows = q.shape[0]
    hpg, hd = HEADS_PER_GROUP, ATT_HEAD_DIM
    assert rows % nb == 0 and nb % SUBLANES == 0
    new = pl.BlockSpec((nb, ATT_WIDTH), lambda i: (i // hpg, 0))
    specs, args = [new, new, new], [q, k_new, v_new]
    for g, (win, dil) in enumerate(DIL_PAIRS):
        for c in (caches_k[g], caches_v[g]):
            assert c.shape == (rows, win, hpg, hd) and win % dil == 0, c.shape
            args.append(c.transpose(0, 2, 3, 1))
            specs.append(pl.BlockSpec((nb, None, hd, win), lambda i: (i // hpg, i % hpg, 0, 0)))
    return Job(
        kernel=functools.partial(_cache_attn_kernel, slopes=tuple(_alibi_slopes())),
        args=args,
        in_specs=specs,
        out_specs=[pl.BlockSpec((None, None, hd, nb), lambda i: (i % hpg, i // hpg, 0, 0))],
        out_shapes=[jax.ShapeDtypeStruct((hpg, rows // nb, hd, nb), F32)],
        scratch=[pltpu.VMEM((ATT_WIDTH, nb), F32) for _ in range(2)],
        steps=(rows // nb) * hpg,
    )


def _cache_attn_result(out):
    hpg, blocks, hd, nb = out.shape
    return out.transpose(1, 3, 0, 2).reshape(blocks * nb, hpg * hd)


def _cache_mem_attn_kernel(q_ref, k_ref, v_ref, o_ref):
    nb, rows, hd = k_ref.shape
    tiles = rows // SUBLANES

    def fold(x):
        return x, pltpu.roll(x, MEM_HEADS, 2)

    q = q_ref[...][:, None]
    k = k_ref[...].reshape(nb, tiles, SUBLANES, hd)
    v = v_ref[...].reshape(nb, tiles, SUBLANES, hd)
    s = jnp.sum(k * q, axis=-1, keepdims=True) * (MEM_HEAD_DIM ** -0.5)
    m = jnp.maximum(*fold(jnp.max(s, axis=1, keepdims=True)))
    p = jnp.exp(s - m)
    den = sum(fold(jnp.sum(p, axis=1, keepdims=True)))
    acc = sum(fold(jnp.sum(p * v, axis=1, keepdims=True)))
    o_ref[...] = (acc / den)[:, 0]


def _cache_mem_job(qm, mem_k, mem_v, nb):
    rows = qm.shape[0]
    assert rows % nb == 0 and SUBLANES == 2 * MEM_HEADS
    q4 = qm.reshape(rows, MEM_HEADS, MEM_HEAD_DIM)
    kv = pl.BlockSpec((nb, N_MEM * MEM_HEADS, MEM_HEAD_DIM), lambda i: (i, 0, 0))
    q = pl.BlockSpec((nb, SUBLANES, MEM_HEAD_DIM), lambda i: (i, 0, 0))
    return Job(
        kernel=_cache_mem_attn_kernel,
        args=[jnp.concatenate([q4, q4], axis=1), mem_k.reshape(rows, N_MEM * MEM_HEADS, MEM_HEAD_DIM),
              mem_v.reshape(rows, N_MEM * MEM_HEADS, MEM_HEAD_DIM)],
        in_specs=[q, kv, kv],
        out_specs=[q],
        out_shapes=[jax.ShapeDtypeStruct((rows, SUBLANES, MEM_HEAD_DIM), F32)],
        scratch=[],
        steps=rows // nb,
    )


def _cache_mem_result(out):
    return out[:, :MEM_HEADS].reshape(out.shape[0], MEM_WIDTH)


def _merge_kernel(x_ref, *refs, n_y, n_att, n_mem):
    y_refs, att_refs = refs[:n_y], refs[n_y:n_y + n_att]
    mem_refs = refs[n_y + n_att:n_y + n_att + n_mem]
    g1_ref, wgate_ref, wglu_ref, watt_ref, wmem_ref, wout_ref, o_ref = refs[n_y + n_att + n_mem:]
    d = D_MODEL
    x = x_ref[...]
    h = _rmsnorm(x, g1_ref[...]).astype(BF16)

    def gate(i):
        return jax.nn.sigmoid(jnp.dot(h, wgate_ref[:, i * d:(i + 1) * d], preferred_element_type=F32))

    y = jnp.concatenate([r[...] for r in y_refs], axis=1) if n_y > 1 else y_refs[0][...]
    z = jax.nn.gelu(y).astype(BF16)
    glu = jnp.dot(z, wglu_ref[...], preferred_element_type=F32)
    merged = gate(0) * (glu[:, 0:d] * jax.nn.sigmoid(glu[:, d:2 * d]))
    if n_att == 1:
        att = att_refs[0][...]
    else:
        lses = [r[...] for r in att_refs[1::2]]
        top = functools.reduce(jnp.maximum, lses)
        ws = [jnp.exp(l - top) for l in lses]
        att = sum(w * r[...] for w, r in zip(ws, att_refs[0::2])) / sum(ws)
    b_att = jnp.dot(att.astype(BF16), watt_ref[...], preferred_element_type=F32)
    merged = merged + gate(1) * b_att
    o_mem = mem_refs[0][...] if n_mem == 1 else _mem_attention(*mem_refs)
    b_mem = jnp.dot(o_mem.astype(BF16), wmem_ref[...], preferred_element_type=F32)
    merged = merged + gate(2) * b_mem
    o_ref[...] = x + jnp.dot(merged.astype(BF16), wout_ref[...], preferred_element_type=F32)


def _merge_job(x, ys, atts, mems, g1, w_in, w_glu, w_att_o, w_mem_o, w_out, tm):
    rows = x.shape[0]
    gate_cols = N_BRANCH * D_MODEL
    assert rows % tm == 0 and w_in.shape == (D_MODEL, IN_WIDTH)

    def tile(a):
        return pl.BlockSpec((tm, a.shape[1]), lambda i: (i, 0))

    acts = [x, *ys, *atts, mems[0]]
    act_specs = [tile(a) for a in acts]
    if len(mems) > 1:
        tiles_per_seq = rows // mems[1].shape[0] // tm
        assert tiles_per_seq * tm * mems[1].shape[0] == rows
        act_specs += [pl.BlockSpec((None, N_MEM, MEM_WIDTH), lambda i: (i // tiles_per_seq, 0, 0))] * 2
        acts += list(mems[1:])
    weights = [w_glu, w_att_o, w_mem_o, w_out]
    return Job(
        kernel=functools.partial(_merge_kernel, n_y=len(ys), n_att=len(atts), n_mem=len(mems)),
        args=acts + [g1.reshape(1, D_MODEL), w_in] + weights,
        in_specs=act_specs + [_full((1, D_MODEL)), _resident_cols(w_in, IN_WIDTH - gate_cols, gate_cols)]
        + [_full(w.shape) for w in weights],
        out_specs=[tile(x)],
        out_shapes=[jax.ShapeDtypeStruct(x.shape, F32)],
        scratch=[],
        steps=rows // tm,
    )


def _ffn_kernel(*refs, tm, tiles_per_seq, stepwise):
    if stepwise:
        x_ref, g2_ref, wup_ref, cw_ref, cb_ref, wdn_ref, gf_ref, prev_ref, y_ref, conv_ref = refs
    else:
        x_ref, g2_ref, wup_ref, cw_ref, cb_ref, wdn_ref, gf_ref, y_ref, conv_ref, a_scr = refs
        i = pl.program_id(0)
        first = i % tiles_per_seq == 0

        @pl.when(first)
        def _():
            a_scr[0:SUBLANES, :] = jnp.zeros((SUBLANES, D_FF), F32)

        @pl.when(jnp.logical_not(first))
        def _():
            a_scr[0:SUBLANES, :] = a_scr[tm:tm + SUBLANES, :]

    x = x_ref[...]
    h = _rmsnorm(x, g2_ref[...]).astype(BF16)
    a = jnp.dot(h, wup_ref[:, 0:D_FF], preferred_element_type=F32)
    if stepwise:
        a2, a1 = prev_ref[:, 0:D_FF], prev_ref[:, D_FF:2 * D_FF]
        conv_ref[:, 0:D_FF] = a1
        conv_ref[:, D_FF:2 * D_FF] = a
    else:
        a_scr[SUBLANES:SUBLANES + tm, :] = a
        a1 = a_scr[SUBLANES - 1:SUBLANES - 1 + tm, :]
        a2 = a_scr[SUBLANES - 2:SUBLANES - 2 + tm, :]
    c = a2 * cw_ref[0:1, :] + a1 * cw_ref[1:2, :] + a * cw_ref[2:3, :] + cb_ref[...]
    v = jnp.dot(h, wup_ref[:, D_FF:2 * D_FF], preferred_element_type=F32)
    y = jnp.dot((jax.nn.gelu(c) * v).astype(BF16), wdn_ref[...], preferred_element_type=F32)
    y_ref[...] = _rmsnorm(x + y, gf_ref[...])
    if not stepwise:
        @pl.when(i % tiles_per_seq == tiles_per_seq - 1)
        def _():
            conv_ref[...] = a_scr[SUBLANES + tm - (CONV_W - 1):SUBLANES + tm, :]


def _ffn_job(x, g2, w_up, conv_w, conv_b, w_down, gf, tm, n_seq=None, prev=None):
    rows, d = x.shape
    assert rows % tm == 0
    stepwise = prev is not None
    weights = [g2.reshape(1, d), w_up, conv_w, conv_b.reshape(1, D_FF), w_down, gf.reshape(1, d)]
    in_specs = [pl.BlockSpec((tm, d), lambda i: (i, 0))] + [_full(w.shape) for w in weights]
    args = [x] + weights
    if stepwise:
        tiles_per_seq = 1
        in_specs.append(pl.BlockSpec((tm, 2 * D_FF), lambda i: (i, 0)))
        args.append(prev)
        conv_spec = pl.BlockSpec((tm, 2 * D_FF), lambda i: (i, 0))
        conv_shape = jax.ShapeDtypeStruct((rows, 2 * D_FF), F32)
        scratch = []
    else:
        tiles_per_seq = rows // n_seq // tm
        assert tiles_per_seq * tm * n_seq == rows
        conv_spec = pl.BlockSpec((None, CONV_W - 1, D_FF), lambda i: (i // tiles_per_seq, 0, 0))
        conv_shape = jax.ShapeDtypeStruct((n_seq, CONV_W - 1, D_FF), F32)
        scratch = [pltpu.VMEM((tm + SUBLANES, D_FF), F32)]
    return Job(
        kernel=functools.partial(_ffn_kernel, tm=tm, tiles_per_seq=tiles_per_seq, stepwise=stepwise),
        args=args,
        in_specs=in_specs,
        out_specs=[pl.BlockSpec((tm, d), lambda i: (i, 0)), conv_spec],
        out_shapes=[jax.ShapeDtypeStruct((rows, d), F32), conv_shape],
        scratch=scratch,
        steps=rows // tm,
    )


def kernel(x_prompt, x_sample, state_ssm_re, state_ssm_im, cache_w1_k, cache_w1_v, cache_w2_k, cache_w2_v, cache_w3_k, cache_w3_v, cache_mem_k, cache_mem_v, state_ffn_conv, mem_prompt, norm1_g, w_in, ssm_a_re, ssm_a_im, ssm_log_dt, ssm_b_re, ssm_b_im, ssm_c_re, ssm_c_im, ssm_d, w_ssm_glu, w_att_o, mem_norm_g, w_mem_kv, w_mem_o, w_out, norm2_g, w_up, ffn_conv_w, ffn_conv_b, w_down, final_norm_g):
    n_seq, seq_len, d = x_prompt.shape
    n_dec, dec_len, _ = x_sample.shape
    depth = norm1_g.shape[0]
    assert d == D_MODEL and depth == 1 and dec_len == 1
    assert w_in.shape == (depth, D_MODEL, IN_WIDTH) and w_up.shape == (depth, D_MODEL, 2 * D_FF)
    assert mem_prompt.shape == (n_seq, N_MEM, D_MODEL)
    assert ssm_a_re.shape == (depth, SSM_GROUPS, SSM_STATE)
    assert seq_len % ROW_TILE == 0 and n_dec % SUBLANES == 0
    hpg, hd = HEADS_PER_GROUP, ATT_HEAD_DIM
    rows_p = n_seq * seq_len
    dec_tile = n_dec if n_dec <= ROW_TILE else ROW_TILE

    prep_job = _ssm_prep_job(ssm_a_re[0], ssm_a_im[0], ssm_log_dt[0], ssm_b_re[0], ssm_b_im[0], ssm_c_re[0],
                             ssm_c_im[0])
    (m_mat, w_mat, v_mat, v0_mat, a1, apow), (w_in_b,) = _run_pair(
        prep_job, _cast_job([w_in[0]], prep_job.steps), "ssm_prep_cast")
    w0_mat = w_mat[:, (SSM_T - 1) * LANES:, :]

    xs = x_sample.reshape(n_dec, d)
    us, qs, ks, vs, qms = _run([_norm_proj_job(xs, norm1_g[0], w_in_b, PROJ_SPLITS, dec_tile)], "norm_proj")[0]
    cache_job = _cache_attn_job(qs, ks, vs, (cache_w1_k[0], cache_w2_k[0], cache_w3_k[0]),
                                (cache_w1_v[0], cache_w2_v[0], cache_w3_v[0]), CACHE_ROWS_PER_STEP)
    proj_steps = rows_p // ROW_TILE
    mem_rows = n_dec // proj_steps if n_dec % proj_steps == 0 else CACHE_ROWS_PER_STEP
    mem_job = _cache_mem_job(qms, cache_mem_k[0], cache_mem_v[0], mem_rows)

    xp = x_prompt.reshape(rows_p, d)
    gw = ATT_GROUP_WIDTH
    keeps = [min(win, seq_len) for win, _ in DIL_PAIRS]
    windows = [(src, g * gw, gw, keep) for g, keep in enumerate(keeps) for src in (2, 3)]
    proj_job = _norm_proj_job(xp, norm1_g[0], w_in_b, PROJ_SPLITS, ROW_TILE, n_seq, windows)
    (u, q, k, v, qm, *kv_win), (mem_out,) = _run_pair(proj_job, mem_job, "norm_proj_cache_mem")
    mem_s = _cache_mem_result(mem_out)

    dil_jobs = [_dil_attn_job(q, k, v, n_seq, seq_len, g) for g in range(len(DIL_PAIRS))]
    (e,), att0 = _run_pair(_ssm_end_state_job(u, n_seq, w_mat, dil_jobs[0].steps), dil_jobs[0], "ssm_end_state_dil")
    sp_re, sp_im, fin_re, fin_im = _ssm_scan(e, apow, n_seq)
    y_ssm, att1 = _run_pair(_ssm_output_job(u, n_seq, sp_re, sp_im, m_mat, v_mat, ssm_d[0], dil_jobs[1].steps),
                            dil_jobs[1], "ssm_output_dil")
    casts = _cast_job([w_up[0], w_down[0], w_ssm_glu[0], w_out[0], w_mem_o[0], w_mem_kv[0], w_att_o[0]],
                      dil_jobs[2].steps)
    att2, (w_up_b, w_down_b, w_glu_b, w_out_b, w_memo_b, w_memkv_b, w_att_b) = _run_pair(
        dil_jobs[2], casts, "dil_attn_casts")
    merge_w = (norm1_g[0], w_in_b, w_glu_b, w_att_b, w_memo_b, w_out_b)
    ffn_w = (norm2_g[0], w_up_b, ffn_conv_w[0], ffn_conv_b[0], w_down_b, final_norm_g)
    atts = att0 + att1 + att2

    mk, mv = _run([_norm_proj_job(mem_prompt.reshape(n_seq * N_MEM, d), mem_norm_g[0], w_memkv_b,
                                  (MEM_WIDTH, MEM_WIDTH), ROW_TILE)], "mem_kv")[0]
    mems = [qm, mk.reshape(n_seq, N_MEM, MEM_WIDTH), mv.reshape(n_seq, N_MEM, MEM_WIDTH)]
    merge_tile = rows_p // cache_job.steps
    if not (rows_p % cache_job.steps == 0 and merge_tile % LANES == 0 and merge_tile <= ROW_TILE
            and seq_len % merge_tile == 0):
        merge_tile = ROW_TILE
    (x1,), (att_out,) = _run_pair(_merge_job(xp, y_ssm, atts, mems, *merge_w, merge_tile), cache_job,
                                  "merge_cache_attn")
    att_s = _cache_attn_result(att_out)
    y_p, conv_p = _run([_ffn_job(x1, *ffn_w, ROW_TILE, n_seq=n_seq)], "ffn")[0]

    def final_state(s):
        return s.reshape(1, n_seq, SSM_GROUPS, SSM_STATE)
    win_p = [t.reshape(n_seq, hpg, hd, t.shape[-1]).transpose(0, 3, 1, 2)[None] for t in kv_win]
    mem_kv = [mk.reshape(1, n_seq, N_MEM, MEM_HEADS, MEM_HEAD_DIM), mv.reshape(1, n_seq, N_MEM, MEM_HEADS, MEM_HEAD_DIM)]

    ys_ssm, sn_re, sn_im = _ssm_step(us, state_ssm_re[0], state_ssm_im[0], w0_mat, v0_mat, a1, ssm_d[0])
    n_g = len(DIL_PAIRS)
    ks4, vs4 = (t.reshape(n_dec, n_g, hpg, hd) for t in (ks, vs))
    xs1 = _run([_merge_job(xs, [ys_ssm], [att_s], [mem_s], *merge_w, dec_tile)], "merge")[0][0]
    y_s, conv_s = _run([_ffn_job(xs1, *ffn_w, dec_tile,
                                 prev=state_ffn_conv[0].reshape(n_dec, (CONV_W - 1) * D_FF))], "ffn_step")[0]

    win_s = []
    for g in range(len(DIL_PAIRS)):
        win_s += [ks4[None, :, g:g + 1], vs4[None, :, g:g + 1]]

    return (y_p.reshape(n_seq, seq_len, d), y_s.reshape(n_dec, 1, d),
            final_state(fin_re), final_state(fin_im), *win_p, *mem_kv, conv_p[None],
            sn_re[None], sn_im[None],
            *win_s, conv_s.reshape(1, n_dec, CONV_W - 1, D_FF))
```

```python
import collections
import functools

import jax
import jax.numpy as jnp
from jax import lax
from jax.experimental import pallas as pl
from jax.experimental.pallas import tpu as pltpu

F32 = jnp.float32
BF16 = jnp.bfloat16

D_MODEL = 1024
SSM_WIDTH = 512
SSM_GROUP = 16
SSM_GROUPS = 32
SSM_STATE = 64
ATT_HEAD_DIM = 64
HEADS_PER_GROUP = 4
DIL_PAIRS = ((128, 1), (512, 4), (2048, 16))
ATT_HEADS = len(DIL_PAIRS) * HEADS_PER_GROUP
ATT_WIDTH = ATT_HEADS * ATT_HEAD_DIM
ATT_GROUP_WIDTH = HEADS_PER_GROUP * ATT_HEAD_DIM
ATT_STEPS = 128
N_MEM = 256
MEM_HEADS = 4
MEM_HEAD_DIM = 128
MEM_WIDTH = MEM_HEADS * MEM_HEAD_DIM
N_BRANCH = 3
D_FF = 2816
CONV_W = 3
EPS = 1e-6
PROJ_SPLITS = (SSM_WIDTH, ATT_WIDTH, ATT_WIDTH, ATT_WIDTH, MEM_WIDTH)
IN_WIDTH = sum(PROJ_SPLITS) + N_BRANCH * D_MODEL

LANES = 128
SUBLANES = 8
VMEM_LIMIT_BYTES = 56 * 1024 * 1024

SSM_T = 8
SSM_LANE_BLOCKS = SSM_WIDTH // LANES
SSM_GROUPS_PER_BLOCK = LANES // SSM_GROUP
SSM_STATES_PER_BLOCK = SSM_GROUPS_PER_BLOCK * SSM_STATE
SSM_STATE_LANES = 2 * SSM_STATES_PER_BLOCK

ROW_TILE = 512
PROJ_COL_CHUNK = 512
SSM_ROW_TILE = 256
CACHE_ROWS_PER_STEP = 8
ATT_CHUNK = 2048
NEG_BIG = -1e30


def _alibi_slopes():
    return [float(2.0 ** (-8.0 * h / ATT_HEADS)) for h in range(1, ATT_HEADS + 1)]


def _params(*sem):
    return pltpu.CompilerParams(dimension_semantics=sem, vmem_limit_bytes=VMEM_LIMIT_BYTES)


def _rmsnorm(x, g):
    ms = jnp.mean(x * x, axis=-1, keepdims=True)
    return x * lax.rsqrt(ms + EPS) * g


def _full(shape):
    nd = len(shape)
    return pl.BlockSpec(shape, lambda *_: (0,) * nd, pipeline_mode=pl.Buffered(1))


def _resident_cols(w, start, width):
    return pl.BlockSpec((pl.Element(w.shape[0]), pl.Element(width)), lambda *_: (0, start),
                        pipeline_mode=pl.Buffered(1))


Job = collections.namedtuple("Job", "kernel args in_specs out_specs out_shapes scratch steps")


def _jobs_kernel(*refs, kernels, layout):
    n_in, n_out = sum(l[0] for l in layout), sum(l[1] for l in layout)
    ins, outs, scratch = refs[:n_in], refs[n_in:n_in + n_out], refs[n_in + n_out:]
    i0 = o0 = s0 = 0
    for kernel, (ni, no, ns) in zip(kernels, layout):
        kernel(*ins[i0:i0 + ni], *outs[o0:o0 + no], *scratch[s0:s0 + ns])
        i0, o0, s0 = i0 + ni, o0 + no, s0 + ns


def _run(jobs, name):
    steps = jobs[0].steps
    assert all(j.steps == steps for j in jobs)
    layout = tuple((len(j.args), len(j.out_shapes), len(j.scratch)) for j in jobs)
    outs = pl.pallas_call(
        functools.partial(_jobs_kernel, kernels=tuple(j.kernel for j in jobs), layout=layout),
        grid=(steps,),
        in_specs=[s for j in jobs for s in j.in_specs],
        out_specs=[s for j in jobs for s in j.out_specs],
        out_shape=[s for j in jobs for s in j.out_shapes],
        scratch_shapes=[s for j in jobs for s in j.scratch],
        compiler_params=_params("arbitrary"),
        name=name,
    )(*[a for j in jobs for a in j.args])
    split, o0 = [], 0
    for _, no, _ in layout:
        split.append(list(outs[o0:o0 + no]))
        o0 += no
    return split


def _run_pair(main, rider, name):
    m, (r,) = _run_with(main, [rider], name)
    return m, r


def _run_with(main, riders, name):
    fused = [r for r in riders if r.steps == main.steps]
    outs = _run(fused + [main], name)
    by_rider = {id(r): o for r, o in zip(fused, outs)}
    return outs[-1], [by_rider[id(r)] if id(r) in by_rider else _run([r], name + "_rider")[0] for r in riders]


def _cast_kernel(*refs):
    n = len(refs) // 2
    for src, dst in zip(refs[:n], refs[n:]):
        dst[...] = src[...].astype(dst.dtype)


def _cast_job(weights, steps):
    specs = []
    for w in weights:
        rows = w.shape[0] // steps
        assert rows * steps == w.shape[0] and rows % (2 * SUBLANES) == 0, (w.shape, steps)
        specs.append(pl.BlockSpec((rows, w.shape[1]), lambda i: (i, 0)))
    return Job(kernel=_cast_kernel, args=list(weights), in_specs=specs, out_specs=specs,
               out_shapes=[jax.ShapeDtypeStruct(w.shape, BF16) for w in weights], scratch=[], steps=steps)


def _norm_proj_kernel(x_ref, g_ref, w_ref, *out_refs, splits, windows, tiles_per_seq):
    tm = x_ref.shape[0]
    h = _rmsnorm(x_ref[...], g_ref[...]).astype(BF16)
    off = 0
    for o_ref, width in zip(out_refs, splits):
        for c0 in range(0, width, PROJ_COL_CHUNK):
            cw = min(PROJ_COL_CHUNK, width - c0)
            o_ref[:, c0:c0 + cw] = jnp.dot(h, w_ref[:, off + c0:off + c0 + cw], preferred_element_type=F32)
        off += width
    tile = pl.program_id(0) % tiles_per_seq
    for win_ref, (src, col0, cols, win) in zip(out_refs[len(splits):], windows):
        rows = min(win, tm)

        @pl.when(tile >= tiles_per_seq - max(win // tm, 1))
        def _(win_ref=win_ref, src=src, col0=col0, cols=cols, rows=rows):
            win_ref[...] = out_refs[src][tm - rows:tm, col0:col0 + cols].T


def _norm_proj_job(x, g, w_bf16, splits, tm, n_seq=1, windows=()):
    rows, d = x.shape
    assert rows % (tm * n_seq) == 0 and sum(splits) <= w_bf16.shape[1]
    tps = rows // n_seq // tm
    win_specs, win_shapes = [], []
    for _, _, cols, win in windows:
        assert (win % tm == 0 or tm % win == 0) and win <= tps * tm
        n_tiles = max(win // tm, 1)
        win_specs.append(pl.BlockSpec((None, cols, min(win, tm)), lambda i, n_tiles=n_tiles: (
            i // tps, 0, jnp.maximum(i % tps - (tps - n_tiles), 0))))
        win_shapes.append(jax.ShapeDtypeStruct((n_seq, cols, win), F32))
    return Job(
        kernel=functools.partial(_norm_proj_kernel, splits=splits, windows=tuple(windows), tiles_per_seq=tps),
        args=[x, g.reshape(1, d), w_bf16],
        in_specs=[pl.BlockSpec((tm, d), lambda i: (i, 0)), _full((1, d)), _resident_cols(w_bf16, 0, sum(splits))],
        out_specs=[pl.BlockSpec((tm, s), lambda i: (i, 0)) for s in splits] + win_specs,
        out_shapes=[jax.ShapeDtypeStruct((rows, s), F32) for s in splits] + win_shapes,
        scratch=[],
        steps=rows // tm,
    )


def _ssm_layout(a_re, a_im, log_dt, b_re, b_im, c_re, c_im):
    nb, gpb, p, c = SSM_LANE_BLOCKS, SSM_GROUPS_PER_BLOCK, SSM_STATE, SSM_GROUP
    rows = jnp.stack([a_re.reshape(nb, gpb * p), a_im.reshape(nb, gpb * p),
                      jnp.repeat(log_dt, p).reshape(nb, gpb * p)], axis=1)
    eye = jnp.eye(gpb, dtype=F32)

    def place_b(b):
        return jnp.einsum('bgpc,gh->bgchp', b.reshape(nb, gpb, p, c), eye).reshape(nb, gpb * c, gpb * p)

    def place_c(m):
        return jnp.einsum('bgcp,gh->bhpgc', m.reshape(nb, gpb, c, p), eye).reshape(nb, gpb * p, gpb * c)

    return rows, place_b(b_re), place_b(b_im), place_c(c_re), place_c(c_im)


def _ssm_prep_kernel(rows_ref, bre_ref, bim_ref, cre_ref, cim_ref,
                     m_ref, w_ref, v_ref, v0_ref, a1_ref, apow_ref, *, t_chunk):
    sp = SSM_STATES_PER_BLOCK
    a_re, a_im, dt = rows_ref[0:1, :], rows_ref[1:2, :], jnp.exp(rows_ref[2:3, :])

    def powers(k):
        mag = jnp.exp(a_re * dt * k)
        ang = a_im * dt * k
        return mag * jnp.cos(ang), mag * jnp.sin(ang)

    n_pow = 2 * SUBLANES
    assert t_chunk + 1 <= n_pow
    pw_re, pw_im = powers(lax.broadcasted_iota(jnp.int32, (n_pow, 1), 0).astype(F32))
    pwt_re, pwt_im = pw_re.T, pw_im.T
    ab_re, ab_im = pw_re[1:2, :], pw_im[1:2, :]
    den = a_re * a_re + a_im * a_im
    q_re = ((ab_re - 1.0) * a_re + ab_im * a_im) / den
    q_im = (ab_im * a_re - (ab_re - 1.0) * a_im) / den
    bre, bim = bre_ref[...], bim_ref[...]
    bb_re = q_re * bre - q_im * bim
    bb_im = q_re * bim + q_im * bre
    cre, cim = cre_ref[...], cim_ref[...]

    m_ref[...] = jnp.zeros(m_ref.shape, m_ref.dtype)
    for k in range(t_chunk):
        pk_re, pk_im = pw_re[k:k + 1, :], pw_im[k:k + 1, :]
        bk_re = bb_re * pk_re - bb_im * pk_im
        bk_im = bb_re * pk_im + bb_im * pk_re
        t = t_chunk - 1 - k
        w_ref[t * LANES:(t + 1) * LANES, 0:sp] = bk_re.astype(w_ref.dtype)
        w_ref[t * LANES:(t + 1) * LANES, sp:2 * sp] = bk_im.astype(w_ref.dtype)
        kk = (jnp.dot(bk_re, cre, precision=lax.Precision.HIGHEST, preferred_element_type=F32)
              - jnp.dot(bk_im, cim, precision=lax.Precision.HIGHEST, preferred_element_type=F32))
        kk = kk.astype(m_ref.dtype)
        for t0 in range(t_chunk - k):
            m_ref[t0 * LANES:(t0 + 1) * LANES, (t0 + k) * LANES:(t0 + k + 1) * LANES] = kk

    for t in range(t_chunk):
        pc_re, pc_im = pwt_re[:, t + 1:t + 2], pwt_im[:, t + 1:t + 2]
        v_ref[0:sp, t * LANES:(t + 1) * LANES] = (cre * pc_re - cim * pc_im).astype(v_ref.dtype)
        v_ref[sp:2 * sp, t * LANES:(t + 1) * LANES] = (-(cre * pc_im + cim * pc_re)).astype(v_ref.dtype)
    v0_ref[0:sp, :] = cre.astype(v0_ref.dtype)
    v0_ref[sp:2 * sp, :] = (-cim).astype(v0_ref.dtype)

    a1_ref[:, 0:sp] = ab_re
    a1_ref[:, sp:2 * sp] = ab_im
    steps = (lax.broadcasted_iota(jnp.int32, (SUBLANES, 1), 0) + 1) * t_chunk
    ap_re, ap_im = powers(steps.astype(F32))
    apow_ref[:, 0:sp] = ap_re
    apow_ref[:, sp:2 * sp] = ap_im


def _ssm_prep_job(a_re, a_im, log_dt, b_re, b_im, c_re, c_im):
    nb, sp, sl, tl = SSM_LANE_BLOCKS, SSM_STATES_PER_BLOCK, SSM_STATE_LANES, SSM_T * LANES

    def blk(shape):
        return pl.BlockSpec((None,) + shape, lambda b: (b, 0, 0))

    return Job(
        kernel=functools.partial(_ssm_prep_kernel, t_chunk=SSM_T),
        args=list(_ssm_layout(a_re, a_im, log_dt, b_re, b_im, c_re, c_im)),
        in_specs=[blk((3, sp)), blk((LANES, sp)), blk((LANES, sp)), blk((sp, LANES)), blk((sp, LANES))],
        out_specs=[blk((tl, tl)), blk((tl, sl)), blk((sl, tl)), blk((sl, LANES)), blk((1, sl)), blk((SUBLANES, sl))],
        out_shapes=[jax.ShapeDtypeStruct((nb, tl, tl), BF16), jax.ShapeDtypeStruct((nb, tl, sl), BF16),
                    jax.ShapeDtypeStruct((nb, sl, tl), BF16), jax.ShapeDtypeStruct((nb, sl, LANES), BF16),
                    jax.ShapeDtypeStruct((nb, 1, sl), F32), jax.ShapeDtypeStruct((nb, SUBLANES, sl), F32)],
        scratch=[],
        steps=nb,
    )


def _chunk_tokens(u_ref, t, t_chunk):
    return u_ref[pl.ds(t, u_ref.shape[0] // t_chunk, stride=t_chunk), :]


def _chunk_lanes(u_ref, t_chunk):
    return jnp.concatenate([_chunk_tokens(u_ref, t, t_chunk) for t in range(t_chunk)], axis=1)


def _ssm_end_state_kernel(*refs, t_chunk):
    nb, sl = SSM_LANE_BLOCKS, SSM_STATE_LANES
    u_refs, w_ref, e_ref = refs[:nb], refs[nb], refs[nb + 1]
    for b in range(nb):
        ub = _chunk_lanes(u_refs[b], t_chunk).astype(BF16)
        e_ref[:, b * sl:(b + 1) * sl] = jnp.dot(ub, w_ref[b], preferred_element_type=F32)


def _ssm_scan_kernel(ere_ref, eim_ref, pre_ref, pim_ref, spre_ref, spim_ref, fre_ref, fim_ref):
    n_tiles = ere_ref.shape[0] // SUBLANES
    width = ere_ref.shape[1]
    p_re, p_im = pre_ref[...], pim_ref[...]
    row = lax.broadcasted_iota(jnp.int32, (SUBLANES, width), 0)

    def shift_down(x, k):
        return jnp.where(row >= k, pltpu.roll(x, k, 0), 0.0)

    def body(i, carry):
        c_re, c_im = carry
        rows = pl.ds(pl.multiple_of(i * SUBLANES, SUBLANES), SUBLANES)
        x_re, x_im = ere_ref[rows, :], eim_ref[rows, :]
        for k in (1, 2, 4):
            a_re, a_im = p_re[k - 1:k, :], p_im[k - 1:k, :]
            s_re, s_im = shift_down(x_re, k), shift_down(x_im, k)
            x_re, x_im = x_re + a_re * s_re - a_im * s_im, x_im + a_re * s_im + a_im * s_re
        t_re = x_re + p_re * c_re - p_im * c_im
        t_im = x_im + p_re * c_im + p_im * c_re
        spre_ref[rows, :] = jnp.where(row >= 1, pltpu.roll(t_re, 1, 0), c_re)
        spim_ref[rows, :] = jnp.where(row >= 1, pltpu.roll(t_im, 1, 0), c_im)
        return t_re[SUBLANES - 1:SUBLANES, :], t_im[SUBLANES - 1:SUBLANES, :]

    zero = jnp.zeros((1, width), F32)
    f_re, f_im = lax.fori_loop(0, n_tiles, body, (zero, zero))
    fre_ref[...] = f_re
    fim_ref[...] = f_im


def _ssm_output_kernel(*refs, t_chunk):
    nb, sp = SSM_LANE_BLOCKS, SSM_STATES_PER_BLOCK
    u_refs, (spre_ref, spim_ref, m_ref, v_ref, d_ref), y_refs = refs[:nb], refs[nb:nb + 5], refs[nb + 5:]
    for b in range(nb):
        ub = _chunk_lanes(u_refs[b], t_chunk).astype(BF16)
        yb = jnp.dot(ub, m_ref[b], preferred_element_type=F32)
        states = slice(b * sp, (b + 1) * sp)
        sprev = jnp.concatenate([spre_ref[:, states], spim_ref[:, states]], axis=1).astype(BF16)
        yb = yb + jnp.dot(sprev, v_ref[b], preferred_element_type=F32)
        d = d_ref[:, b * LANES:(b + 1) * LANES]
        for t in range(t_chunk):
            y_t = yb[:, t * LANES:(t + 1) * LANES] + d * _chunk_tokens(u_refs[b], t, t_chunk)
            y_refs[b][pl.ds(t, yb.shape[0], stride=t_chunk), :] = y_t


def _ssm_tiling(rows, n_seq, steps_wanted):
    assert rows % (SSM_T * n_seq) == 0
    n_chunks = rows // SSM_T // n_seq
    assert n_chunks % SUBLANES == 0
    tr = min(SSM_ROW_TILE, n_chunks)
    if steps_wanted and (n_seq * n_chunks) % steps_wanted == 0:
        want = n_seq * n_chunks // steps_wanted
        if want % SUBLANES == 0 and n_chunks % want == 0 and want <= SSM_ROW_TILE:
            tr = want
    assert n_chunks % tr == 0
    return n_chunks, tr


def _ssm_end_state_job(u, n_seq, w_mat, steps_wanted=None):
    nb, sl = SSM_LANE_BLOCKS, SSM_STATE_LANES
    n_chunks, tr = _ssm_tiling(u.shape[0], n_seq, steps_wanted)
    tiles = n_chunks // tr
    return Job(
        kernel=functools.partial(_ssm_end_state_kernel, t_chunk=SSM_T),
        args=[u] * nb + [w_mat],
        in_specs=[pl.BlockSpec((tr * SSM_T, LANES), lambda i, b=b: (i, b)) for b in range(nb)] + [_full(w_mat.shape)],
        out_specs=[pl.BlockSpec((tr, nb * sl), lambda i: (i % tiles, i // tiles))],
        out_shapes=[jax.ShapeDtypeStruct((n_chunks, n_seq * nb * sl), F32)],
        scratch=[],
        steps=n_seq * tiles,
    )


def _ssm_scan(e, apow, n_seq):
    nb, sp = SSM_LANE_BLOCKS, SSM_STATES_PER_BLOCK
    n_chunks = e.shape[0]
    col = pl.BlockSpec((n_chunks, sp), lambda g: (0, g))
    fin = pl.BlockSpec((1, sp), lambda g: (0, g))
    return pl.pallas_call(
        _ssm_scan_kernel,
        grid=(n_seq * nb,),
        in_specs=[pl.BlockSpec((n_chunks, sp), lambda g: (0, 2 * g)),
                  pl.BlockSpec((n_chunks, sp), lambda g: (0, 2 * g + 1)),
                  pl.BlockSpec((None, SUBLANES, sp), lambda g: (g % nb, 0, 0)),
                  pl.BlockSpec((None, SUBLANES, sp), lambda g: (g % nb, 0, 1))],
        out_specs=[col, col, fin, fin],
        out_shape=[jax.ShapeDtypeStruct((n_chunks, n_seq * nb * sp), F32)] * 2
        + [jax.ShapeDtypeStruct((1, n_seq * nb * sp), F32)] * 2,
        compiler_params=_params("parallel"),
        name="ssm_scan",
    )(e, e, apow, apow)


def _ssm_output_job(u, n_seq, sp_re, sp_im, m_mat, v_mat, d_skip, steps_wanted=None):
    nb, sp = SSM_LANE_BLOCKS, SSM_STATES_PER_BLOCK
    rows = u.shape[0]
    n_chunks, tr = _ssm_tiling(rows, n_seq, steps_wanted)
    tiles = n_chunks // tr
    sp_spec = pl.BlockSpec((tr, nb * sp), lambda i: (i % tiles, i // tiles))
    return Job(
        kernel=functools.partial(_ssm_output_kernel, t_chunk=SSM_T),
        args=[u] * nb + [sp_re, sp_im, m_mat, v_mat, d_skip.reshape(1, SSM_WIDTH)],
        in_specs=[pl.BlockSpec((tr * SSM_T, LANES), lambda i, b=b: (i, b)) for b in range(nb)]
        + [sp_spec, sp_spec, _full(m_mat.shape), _full(v_mat.shape), _full((1, SSM_WIDTH))],
        out_specs=[pl.BlockSpec((tr * SSM_T, LANES), lambda i: (i, 0)) for _ in range(nb)],
        out_shapes=[jax.ShapeDtypeStruct((rows, LANES), F32) for _ in range(nb)],
        scratch=[],
        steps=n_seq * tiles,
    )


def _ssm_step_kernel(u_ref, sre_ref, sim_ref, w0_ref, v0_ref, a1_ref, d_ref, y_ref, nre_ref, nim_ref):
    sp = SSM_STATES_PER_BLOCK
    for b in range(SSM_LANE_BLOCKS):
        lanes = slice(b * LANES, (b + 1) * LANES)
        states = slice(b * sp, (b + 1) * sp)
        u = u_ref[:, lanes]
        e = jnp.dot(u.astype(BF16), w0_ref[b], preferred_element_type=F32)
        a_re, a_im = a1_ref[b, :, 0:sp], a1_ref[b, :, sp:2 * sp]
        s_re, s_im = sre_ref[states, :].T, sim_ref[states, :].T
        n_re = a_re * s_re - a_im * s_im + e[:, 0:sp]
        n_im = a_re * s_im + a_im * s_re + e[:, sp:2 * sp]
        nre_ref[states, :] = n_re.T
        nim_ref[states, :] = n_im.T
        sn = jnp.concatenate([n_re, n_im], axis=1).astype(BF16)
        y_ref[:, lanes] = jnp.dot(sn, v0_ref[b], preferred_element_type=F32) + d_ref[:, lanes] * u


def _ssm_step(u, s_re, s_im, w0, v0, a1, d_skip):
    rows = u.shape[0]
    ns = SSM_GROUPS * SSM_STATE
    assert rows % LANES == 0
    args = (u, s_re.reshape(rows, ns).T, s_im.reshape(rows, ns).T, w0, v0, a1, d_skip.reshape(1, SSM_WIDTH))
    y, n_re, n_im = pl.pallas_call(
        _ssm_step_kernel,
        grid=(1,),
        in_specs=[_full(a.shape) for a in args],
        out_specs=[_full((rows, SSM_WIDTH)), _full((ns, rows)), _full((ns, rows))],
        out_shape=[jax.ShapeDtypeStruct((rows, SSM_WIDTH), F32), jax.ShapeDtypeStruct((ns, rows), F32),
                   jax.ShapeDtypeStruct((ns, rows), F32)],
        compiler_params=_params("arbitrary"),
        name="ssm_step",
    )(*args)
    return y, n_re.T.reshape(s_re.shape), n_im.T.reshape(s_im.shape)


def _dil_attn_kernel(q_ref, kc_ref, kp_ref, vc_ref, vp_ref, o_ref, lse_ref, *, dil, slopes, chunks, pairs):
    steps, hd = ATT_STEPS, ATT_HEAD_DIM
    chunk = q_ref.shape[0]
    span = steps * dil
    step = pl.program_id(0)
    first_chunk = (step // pairs) % chunks == 0
    pair = step % pairs
    qi = lax.broadcasted_iota(jnp.int32, (steps, 2 * steps), 0)
    kj = lax.broadcasted_iota(jnp.int32, (steps, 2 * steps), 1)
    dist = qi + steps - kj
    band = (dist >= 0) & (dist <= steps)
    distf = (dist * dil).astype(F32)
    lane = lax.broadcasted_iota(jnp.int32, (steps, LANES), 1)
    heads = [lane < hd, lane >= hd]
    biases, first_biases = [], []
    for hh in range(2):
        slope = sum(jnp.where(pair == p, slopes[2 * p + hh], 0.0) for p in range(pairs))
        bias = jnp.where(band, -slope * distf, NEG_BIG)
        biases.append(bias)
        first_biases.append(jnp.where(first_chunk & (kj < steps), NEG_BIG, bias))

    def rows(ref, start):
        return ref[pl.ds(start, steps, stride=dil), :] if dil > 1 else ref[pl.ds(start, steps), :]

    for r in range(dil):
        for qb in range(chunk // span):
            start = r + qb * span
            q = rows(q_ref, start) * (hd ** -0.5)
            if qb == 0:
                k_prev, v_prev = rows(kp_ref, r), rows(vp_ref, r)
            else:
                k_prev, v_prev = rows(kc_ref, start - span), rows(vc_ref, start - span)
            kk = jnp.concatenate([k_prev, rows(kc_ref, start)], axis=0).astype(BF16)
            vv = jnp.concatenate([v_prev, rows(vc_ref, start)], axis=0).astype(BF16)
            out = jnp.zeros((steps, LANES), F32)
            lse = jnp.zeros((steps, LANES), F32)
            for hh in range(2):
                qh = jnp.where(heads[hh], q, 0.0).astype(BF16)
                s = lax.dot_general(qh, kk, (((1,), (1,)), ((), ())), preferred_element_type=F32)
                s = s + (first_biases[hh] if qb == 0 else biases[hh])
                m = jnp.max(s, axis=-1, keepdims=True)
                p = jnp.exp(s - m)
                den = jnp.sum(p, axis=-1, keepdims=True)
                oh = jnp.dot(p.astype(BF16), vv, preferred_element_type=F32) / den
                out = jnp.where(heads[hh], oh, out)
                lse = jnp.where(heads[hh], m + jnp.log(den), lse)
            if dil > 1:
                o_ref[pl.ds(start, steps, stride=dil), :] = out
                lse_ref[pl.ds(start, steps, stride=dil), :] = lse
            else:
                o_ref[pl.ds(start, steps), :] = out
                lse_ref[pl.ds(start, steps), :] = lse


def _dil_attn_job(q, k, v, n_seq, seq_len, group):
    win, dil = DIL_PAIRS[group]
    steps, gw = ATT_STEPS, ATT_GROUP_WIDTH
    span = steps * dil
    chunk = ATT_CHUNK
    assert win // dil == steps and chunk % span == 0 and seq_len % chunk == 0
    chunks = seq_len // chunk
    pairs = gw // LANES
    cur = pl.BlockSpec((chunk, LANES), lambda i: (i // pairs, pairs * group + i % pairs))
    prev = pl.BlockSpec((span, LANES), lambda i: (jnp.maximum((i // pairs) * (chunk // span) - 1, 0),
                                                  pairs * group + i % pairs))
    out = pl.BlockSpec((chunk, LANES), lambda i: (i // pairs, i % pairs))
    slopes = tuple(_alibi_slopes()[group * HEADS_PER_GROUP:(group + 1) * HEADS_PER_GROUP])
    rows = n_seq * seq_len
    return Job(
        kernel=functools.partial(_dil_attn_kernel, dil=dil, slopes=slopes, chunks=chunks, pairs=pairs),
        args=[q, k, k, v, v],
        in_specs=[cur, cur, prev, cur, prev],
        out_specs=[out, out],
        out_shapes=[jax.ShapeDtypeStruct((rows, gw), F32), jax.ShapeDtypeStruct((rows, gw), F32)],
        scratch=[],
        steps=n_seq * chunks * pairs,
    )


def _mem_attention(q_ref, mk_ref, mv_ref):
    hd = MEM_HEAD_DIM
    outs = []
    for h in range(MEM_HEADS):
        lanes = slice(h * hd, (h + 1) * hd)
        s = lax.dot_general(q_ref[:, lanes].astype(BF16), mk_ref[:, lanes].astype(BF16),
                            (((1,), (1,)), ((), ())), preferred_element_type=F32) * (hd ** -0.5)
        m = jnp.max(s, axis=-1, keepdims=True)
        p = jnp.exp(s - m)
        den = jnp.sum(p, axis=-1, keepdims=True)
        outs.append(jnp.dot(p.astype(BF16), mv_ref[:, lanes].astype(BF16), preferred_element_type=F32) / den)
    return jnp.concatenate(outs, axis=1)


def _cache_attn_kernel(q_ref, kn_ref, vn_ref, k1_ref, v1_ref, k2_ref, v2_ref, k3_ref, v3_ref, o_ref,
                       qt_scr, vnt_scr, *, slopes):
    hpg, hd = HEADS_PER_GROUP, ATT_HEAD_DIM
    j = pl.program_id(0) % hpg
    nb = q_ref.shape[0]
    scale = hd ** -0.5
    q, kn = q_ref[...], kn_ref[...]
    qt_scr[...] = q.T
    vnt_scr[...] = vn_ref[...].T
    lane_head = lax.broadcasted_iota(jnp.int32, q.shape, 1) // hd
    outs, lses = [], []
    for g, (k_ref, v_ref) in enumerate(((k1_ref, v1_ref), (k2_ref, v2_ref), (k3_ref, v3_ref))):
        dil = DIL_PAIRS[g][1]
        n_pos = k_ref.shape[-1]
        rows = pl.ds(pl.multiple_of((g * hpg + j) * hd, hd), hd)
        qg, vng = qt_scr[rows, :], vnt_scr[rows, :]
        slope = sum(jnp.where(j == h, slopes[g * hpg + h], 0.0) for h in range(hpg))
        back = n_pos - lax.broadcasted_iota(jnp.int32, (1, n_pos), 1)
        bias = jnp.where(back % dil == 0, -slope * back.astype(F32), NEG_BIG)
        row = lax.broadcasted_iota(jnp.int32, (nb, n_pos), 0)
        s = jnp.zeros((nb, n_pos), F32)
        for b in range(nb):
            s = jnp.where(row == b, jnp.sum(k_ref[b] * qg[:, b:b + 1], axis=0, keepdims=True), s)
        s = s * scale + bias
        s_new = jnp.sum(jnp.where(lane_head == g * hpg + j, q * kn, 0.0), axis=1, keepdims=True) * scale
        m = jnp.maximum(jnp.max(s, axis=1, keepdims=True), s_new)
        p = jnp.exp(s - m)
        p_new = jnp.exp(s_new - m)
        den = jnp.sum(p, axis=1, keepdims=True) + p_new
        cols = []
        for b in range(nb):
            acc = jnp.sum(v_ref[b] * p[b:b + 1, :], axis=1, keepdims=True) + p_new[b:b + 1, :] * vng[:, b:b + 1]
            cols.append(acc / den[b:b + 1, :])
        outs.append(cols)
        lses.append(m + jnp.log(den))
    top = functools.reduce(jnp.maximum, lses)
    ws = [jnp.exp(l - top) for l in lses]
    total = sum(ws)
    o_ref[...] = jnp.concatenate(
        [sum(w[b:b + 1, :] * cols[b] for w, cols in zip(ws, outs)) / total[b:b + 1, :] for b in range(nb)], axis=1)


def _cache_attn_job(q, k_new, v_new, caches_k, caches_v, nb):
    rows = q.shape[0]
    hpg, hd = HEADS_PER_GROUP, ATT_HEAD_DIM
    assert rows % nb == 0 and nb % SUBLANES == 0
    new = pl.BlockSpec((nb, ATT_WIDTH), lambda i: (i // hpg, 0))
    specs, args = [new, new, new], [q, k_new, v_new]
    for g, (win, dil) in enumerate(DIL_PAIRS):
        for c in (caches_k[g], caches_v[g]):
            assert c.shape == (rows, win, hpg, hd) and win % dil == 0, c.shape
            args.append(c.transpose(0, 2, 3, 1))
            specs.append(pl.BlockSpec((nb, None, hd, win), lambda i: (i // hpg, i % hpg, 0, 0)))
    return Job(
        kernel=functools.partial(_cache_attn_kernel, slopes=tuple(_alibi_slopes())),
        args=args,
        in_specs=specs,
        out_specs=[pl.BlockSpec((None, None, hd, nb), lambda i: (i % hpg, i // hpg, 0, 0))],
        out_shapes=[jax.ShapeDtypeStruct((hpg, rows // nb, hd, nb), F32)],
        scratch=[pltpu.VMEM((ATT_WIDTH, nb), F32) for _ in range(2)],
        steps=(rows // nb) * hpg,
    )


def _cache_attn_result(out):
    hpg, blocks, hd, nb = out.shape
    return out.transpose(1, 3, 0, 2).reshape(blocks * nb, hpg * hd)


def _cache_mem_attn_kernel(q_ref, k_ref, v_ref, o_ref):
    nb, rows, hd = k_ref.shape
    tiles = rows // SUBLANES

    def fold(x):
        return x, pltpu.roll(x, MEM_HEADS, 2)

    q = q_ref[...][:, None]
    k = k_ref[...].reshape(nb, tiles, SUBLANES, hd)
    v = v_ref[...].reshape(nb, tiles, SUBLANES, hd)
    s = jnp.sum(k * q, axis=-1, keepdims=True) * (MEM_HEAD_DIM ** -0.5)
    m = jnp.maximum(*fold(jnp.max(s, axis=1, keepdims=True)))
    p = jnp.exp(s - m)
    den = sum(fold(jnp.sum(p, axis=1, keepdims=True)))
    acc = sum(fold(jnp.sum(p * v, axis=1, keepdims=True)))
    o_ref[...] = (acc / den)[:, 0]


def _cache_mem_job(qm, mem_k, mem_v, nb):
    rows = qm.shape[0]
    assert rows % nb == 0 and SUBLANES == 2 * MEM_HEADS
    q4 = qm.reshape(rows, MEM_HEADS, MEM_HEAD_DIM)
    kv = pl.BlockSpec((nb, N_MEM * MEM_HEADS, MEM_HEAD_DIM), lambda i: (i, 0, 0))
    q = pl.BlockSpec((nb, SUBLANES, MEM_HEAD_DIM), lambda i: (i, 0, 0))
    return Job(
        kernel=_cache_mem_attn_kernel,
        args=[jnp.concatenate([q4, q4], axis=1), mem_k.reshape(rows, N_MEM * MEM_HEADS, MEM_HEAD_DIM),
              mem_v.reshape(rows, N_MEM * MEM_HEADS, MEM_HEAD_DIM)],
        in_specs=[q, kv, kv],
        out_specs=[q],
        out_shapes=[jax.ShapeDtypeStruct((rows, SUBLANES, MEM_HEAD_DIM), F32)],
        scratch=[],
        steps=rows // nb,
    )


def _cache_mem_result(out):
    return out[:, :MEM_HEADS].reshape(out.shape[0], MEM_WIDTH)


def _merge_kernel(x_ref, *refs, n_y, n_att, n_mem):
    y_refs, att_refs = refs[:n_y], refs[n_y:n_y + n_att]
    mem_refs = refs[n_y + n_att:n_y + n_att + n_mem]
    g1_ref, wgate_ref, wglu_ref, watt_ref, wmem_ref, wout_ref, o_ref = refs[n_y + n_att + n_mem:]
    d = D_MODEL
    x = x_ref[...]
    h = _rmsnorm(x, g1_ref[...]).astype(BF16)

    def gate(i):
        return jax.nn.sigmoid(jnp.dot(h, wgate_ref[:, i * d:(i + 1) * d], preferred_element_type=F32))

    y = jnp.concatenate([r[...] for r in y_refs], axis=1) if n_y > 1 else y_refs[0][...]
    z = jax.nn.gelu(y).astype(BF16)
    glu = jnp.dot(z, wglu_ref[...], preferred_element_type=F32)
    merged = gate(0) * (glu[:, 0:d] * jax.nn.sigmoid(glu[:, d:2 * d]))
    if n_att == 1:
        att = att_refs[0][...]
    else:
        lses = [r[...] for r in att_refs[1::2]]
        top = functools.reduce(jnp.maximum, lses)
        ws = [jnp.exp(l - top) for l in lses]
        att = sum(w * r[...] for w, r in zip(ws, att_refs[0::2])) / sum(ws)
    b_att = jnp.dot(att.astype(BF16), watt_ref[...], preferred_element_type=F32)
    merged = merged + gate(1) * b_att
    o_mem = mem_refs[0][...] if n_mem == 1 else _mem_attention(*mem_refs)
    b_mem = jnp.dot(o_mem.astype(BF16), wmem_ref[...], preferred_element_type=F32)
    merged = merged + gate(2) * b_mem
    o_ref[...] = x + jnp.dot(merged.astype(BF16), wout_ref[...], preferred_element_type=F32)


def _merge_job(x, ys, atts, mems, g1, w_in, w_glu, w_att_o, w_mem_o, w_out, tm):
    rows = x.shape[0]
    gate_cols = N_BRANCH * D_MODEL
    assert rows % tm == 0 and w_in.shape == (D_MODEL, IN_WIDTH)

    def tile(a):
        return pl.BlockSpec((tm, a.shape[1]), lambda i: (i, 0))

    acts = [x, *ys, *atts, mems[0]]
    act_specs = [tile(a) for a in acts]
    if len(mems) > 1:
        tiles_per_seq = rows // mems[1].shape[0] // tm
        assert tiles_per_seq * tm * mems[1].shape[0] == rows
        act_specs += [pl.BlockSpec((None, N_MEM, MEM_WIDTH), lambda i: (i // tiles_per_seq, 0, 0))] * 2
        acts += list(mems[1:])
    weights = [w_glu, w_att_o, w_mem_o, w_out]
    return Job(
        kernel=functools.partial(_merge_kernel, n_y=len(ys), n_att=len(atts), n_mem=len(mems)),
        args=acts + [g1.reshape(1, D_MODEL), w_in] + weights,
        in_specs=act_specs + [_full((1, D_MODEL)), _resident_cols(w_in, IN_WIDTH - gate_cols, gate_cols)]
        + [_full(w.shape) for w in weights],
        out_specs=[tile(x)],
        out_shapes=[jax.ShapeDtypeStruct(x.shape, F32)],
        scratch=[],
        steps=rows // tm,
    )


def _ffn_kernel(*refs, tm, tiles_per_seq, stepwise):
    if stepwise:
        x_ref, g2_ref, wup_ref, cw_ref, cb_ref, wdn_ref, gf_ref, prev_ref, y_ref, conv_ref = refs
    else:
        x_ref, g2_ref, wup_ref, cw_ref, cb_ref, wdn_ref, gf_ref, y_ref, conv_ref, a_scr = refs
        i = pl.program_id(0)
        first = i % tiles_per_seq == 0

        @pl.when(first)
        def _():
            a_scr[0:SUBLANES, :] = jnp.zeros((SUBLANES, D_FF), F32)

        @pl.when(jnp.logical_not(first))
        def _():
            a_scr[0:SUBLANES, :] = a_scr[tm:tm + SUBLANES, :]

    x = x_ref[...]
    h = _rmsnorm(x, g2_ref[...]).astype(BF16)
    a = jnp.dot(h, wup_ref[:, 0:D_FF], preferred_element_type=F32)
    if stepwise:
        a2, a1 = prev_ref[:, 0:D_FF], prev_ref[:, D_FF:2 * D_FF]
        conv_ref[:, 0:D_FF] = a1
        conv_ref[:, D_FF:2 * D_FF] = a
    else:
        a_scr[SUBLANES:SUBLANES + tm, :] = a
        a1 = a_scr[SUBLANES - 1:SUBLANES - 1 + tm, :]
        a2 = a_scr[SUBLANES - 2:SUBLANES - 2 + tm, :]
    c = a2 * cw_ref[0:1, :] + a1 * cw_ref[1:2, :] + a * cw_ref[2:3, :] + cb_ref[...]
    v = jnp.dot(h, wup_ref[:, D_FF:2 * D_FF], preferred_element_type=F32)
    y = jnp.dot((jax.nn.gelu(c) * v).astype(BF16), wdn_ref[...], preferred_element_type=F32)
    y_ref[...] = _rmsnorm(x + y, gf_ref[...])
    if not stepwise:
        conv_ref[...] = a_scr[SUBLANES + tm - (CONV_W - 1):SUBLANES + tm, :]


def _ffn_job(x, g2, w_up, conv_w, conv_b, w_down, gf, tm, n_seq=None, prev=None):
    rows, d = x.shape
    assert rows % tm == 0
    stepwise = prev is not None
    weights = [g2.reshape(1, d), w_up, conv_w, conv_b.reshape(1, D_FF), w_down, gf.reshape(1, d)]
    in_specs = [pl.BlockSpec((tm, d), lambda i: (i, 0))] + [_full(w.shape) for w in weights]
    args = [x] + weights
    if stepwise:
        tiles_per_seq = 1
        in_specs.append(pl.BlockSpec((tm, 2 * D_FF), lambda i: (i, 0)))
        args.append(prev)
        conv_spec = pl.BlockSpec((tm, 2 * D_FF), lambda i: (i, 0))
        conv_shape = jax.ShapeDtypeStruct((rows, 2 * D_FF), F32)
        scratch = []
    else:
        tiles_per_seq = rows // n_seq // tm
        assert tiles_per_seq * tm * n_seq == rows
        conv_spec = pl.BlockSpec((None, CONV_W - 1, D_FF), lambda i: (i // tiles_per_seq, 0, 0))
        conv_shape = jax.ShapeDtypeStruct((n_seq, CONV_W - 1, D_FF), F32)
        scratch = [pltpu.VMEM((tm + SUBLANES, D_FF), F32)]
    return Job(
        kernel=functools.partial(_ffn_kernel, tm=tm, tiles_per_seq=tiles_per_seq, stepwise=stepwise),
        args=args,
        in_specs=in_specs,
        out_specs=[pl.BlockSpec((tm, d), lambda i: (i, 0)), conv_spec],
        out_shapes=[jax.ShapeDtypeStruct((rows, d), F32), conv_shape],
        scratch=scratch,
        steps=rows // tm,
    )


def kernel(x_prompt, x_sample, state_ssm_re, state_ssm_im, cache_w1_k, cache_w1_v, cache_w2_k, cache_w2_v, cache_w3_k, cache_w3_v, cache_mem_k, cache_mem_v, state_ffn_conv, mem_prompt, norm1_g, w_in, ssm_a_re, ssm_a_im, ssm_log_dt, ssm_b_re, ssm_b_im, ssm_c_re, ssm_c_im, ssm_d, w_ssm_glu, w_att_o, mem_norm_g, w_mem_kv, w_mem_o, w_out, norm2_g, w_up, ffn_conv_w, ffn_conv_b, w_down, final_norm_g):
    n_seq, seq_len, d = x_prompt.shape
    n_dec, dec_len, _ = x_sample.shape
    depth = norm1_g.shape[0]
    assert d == D_MODEL and depth == 1 and dec_len == 1
    assert w_in.shape == (depth, D_MODEL, IN_WIDTH) and w_up.shape == (depth, D_MODEL, 2 * D_FF)
    assert mem_prompt.shape == (n_seq, N_MEM, D_MODEL)
    assert ssm_a_re.shape == (depth, SSM_GROUPS, SSM_STATE)
    assert seq_len % ROW_TILE == 0 and n_dec % SUBLANES == 0
    hpg, hd = HEADS_PER_GROUP, ATT_HEAD_DIM
    rows_p = n_seq * seq_len
    dec_tile = n_dec if n_dec <= ROW_TILE else ROW_TILE

    prep_job = _ssm_prep_job(ssm_a_re[0], ssm_a_im[0], ssm_log_dt[0], ssm_b_re[0], ssm_b_im[0], ssm_c_re[0],
                             ssm_c_im[0])
    (m_mat, w_mat, v_mat, v0_mat, a1, apow), (w_in_b,) = _run_pair(
        prep_job, _cast_job([w_in[0]], prep_job.steps), "ssm_prep_cast")
    w0_mat = w_mat[:, (SSM_T - 1) * LANES:, :]

    xs = x_sample.reshape(n_dec, d)
    us, qs, ks, vs, qms = _run([_norm_proj_job(xs, norm1_g[0], w_in_b, PROJ_SPLITS, dec_tile)], "norm_proj")[0]
    cache_job = _cache_attn_job(qs, ks, vs, (cache_w1_k[0], cache_w2_k[0], cache_w3_k[0]),
                                (cache_w1_v[0], cache_w2_v[0], cache_w3_v[0]), CACHE_ROWS_PER_STEP)
    proj_steps = rows_p // ROW_TILE
    mem_rows = n_dec // proj_steps if n_dec % proj_steps == 0 else CACHE_ROWS_PER_STEP
    mem_job = _cache_mem_job(qms, cache_mem_k[0], cache_mem_v[0], mem_rows)

    xp = x_prompt.reshape(rows_p, d)
    gw = ATT_GROUP_WIDTH
    keeps = [min(win, seq_len) for win, _ in DIL_PAIRS]
    windows = [(src, g * gw, gw, keep) for g, keep in enumerate(keeps) for src in (2, 3)]
    proj_job = _norm_proj_job(xp, norm1_g[0], w_in_b, PROJ_SPLITS, ROW_TILE, n_seq, windows)
    u, q, k, v, qm, *kv_win = _run([proj_job], "norm_proj")[0]

    dil_jobs = [_dil_attn_job(q, k, v, n_seq, seq_len, g) for g in range(len(DIL_PAIRS))]
    (e,), att0 = _run_pair(_ssm_end_state_job(u, n_seq, w_mat, dil_jobs[0].steps), dil_jobs[0], "ssm_end_state_dil")
    sp_re, sp_im, fin_re, fin_im = _ssm_scan(e, apow, n_seq)
    y_ssm, att1 = _run_pair(_ssm_output_job(u, n_seq, sp_re, sp_im, m_mat, v_mat, ssm_d[0], dil_jobs[1].steps),
                            dil_jobs[1], "ssm_output_dil")
    casts = _cast_job([w_up[0], w_down[0], w_ssm_glu[0], w_out[0], w_mem_o[0], w_mem_kv[0], w_att_o[0]],
                      dil_jobs[2].steps)
    att2, (w_up_b, w_down_b, w_glu_b, w_out_b, w_memo_b, w_memkv_b, w_att_b) = _run_pair(
        dil_jobs[2], casts, "dil_attn_casts")
    merge_w = (norm1_g[0], w_in_b, w_glu_b, w_att_b, w_memo_b, w_out_b)
    ffn_w = (norm2_g[0], w_up_b, ffn_conv_w[0], ffn_conv_b[0], w_down_b, final_norm_g)
    atts = att0 + att1 + att2

    mk, mv = _run([_norm_proj_job(mem_prompt.reshape(n_seq * N_MEM, d), mem_norm_g[0], w_memkv_b,
                                  (MEM_WIDTH, MEM_WIDTH), ROW_TILE)], "mem_kv")[0]
    mems = [qm, mk.reshape(n_seq, N_MEM, MEM_WIDTH), mv.reshape(n_seq, N_MEM, MEM_WIDTH)]
    merge_tile = rows_p // cache_job.steps
    if not (rows_p % cache_job.steps == 0 and merge_tile % LANES == 0 and merge_tile <= ROW_TILE
            and seq_len % merge_tile == 0):
        merge_tile = ROW_TILE
    (x1,), (att_out,) = _run_pair(_merge_job(xp, y_ssm, atts, mems, *merge_w, merge_tile), cache_job,
                                  "merge_cache_attn")
    att_s = _cache_attn_result(att_out)
    ffn_job = _ffn_job(x1, *ffn_w, ROW_TILE, n_seq=n_seq)
    if mem_job.steps == ffn_job.steps:
        (y_p, conv_p), (mem_out,) = _run([ffn_job, mem_job], "ffn_cache_mem")
    else:
        (y_p, conv_p), (mem_out,) = _run([ffn_job], "ffn")[0], _run([mem_job], "cache_mem_attn")[0]
    mem_s = _cache_mem_result(mem_out)

    def final_state(s):
        return s.reshape(1, n_seq, SSM_GROUPS, SSM_STATE)
    win_p = [t.reshape(n_seq, hpg, hd, t.shape[-1]).transpose(0, 3, 1, 2)[None] for t in kv_win]
    mem_kv = [mk.reshape(1, n_seq, N_MEM, MEM_HEADS, MEM_HEAD_DIM), mv.reshape(1, n_seq, N_MEM, MEM_HEADS, MEM_HEAD_DIM)]

    ys_ssm, sn_re, sn_im = _ssm_step(us, state_ssm_re[0], state_ssm_im[0], w0_mat, v0_mat, a1, ssm_d[0])
    n_g = len(DIL_PAIRS)
    ks4, vs4 = (t.reshape(n_dec, n_g, hpg, hd) for t in (ks, vs))
    xs1 = _run([_merge_job(xs, [ys_ssm], [att_s], [mem_s], *merge_w, dec_tile)], "merge")[0][0]
    y_s, conv_s = _run([_ffn_job(xs1, *ffn_w, dec_tile,
                                 prev=state_ffn_conv[0].reshape(n_dec, (CONV_W - 1) * D_FF))], "ffn_step")[0]

    win_s = []
    for g in range(len(DIL_PAIRS)):
        win_s += [ks4[None, :, g:g + 1], vs4[None, :, g:g + 1]]

    return (y_p.reshape(n_seq, seq_len, d), y_s.reshape(n_dec, 1, d),
            final_state(fin_re), final_state(fin_im), *win_p, *mem_kv, conv_p[None],
            sn_re[None], sn_im[None],
            *win_s, conv_s.reshape(1, n_dec, CONV_W - 1, D_FF))
```

```python
import collections
import functools

import jax
import jax.numpy as jnp
from jax import lax
from jax.experimental import pallas as pl
from jax.experimental.pallas import tpu as pltpu

F32 = jnp.float32
BF16 = jnp.bfloat16

D_MODEL = 1024
SSM_WIDTH = 512
SSM_GROUP = 16
SSM_GROUPS = 32
SSM_STATE = 64
ATT_HEAD_DIM = 64
HEADS_PER_GROUP = 4
DIL_PAIRS = ((128, 1), (512, 4), (2048, 16))
ATT_HEADS = len(DIL_PAIRS) * HEADS_PER_GROUP
ATT_WIDTH = ATT_HEADS * ATT_HEAD_DIM
ATT_GROUP_WIDTH = HEADS_PER_GROUP * ATT_HEAD_DIM
ATT_STEPS = 128
N_MEM = 256
MEM_HEADS = 4
MEM_HEAD_DIM = 128
MEM_WIDTH = MEM_HEADS * MEM_HEAD_DIM
N_BRANCH = 3
D_FF = 2816
CONV_W = 3
EPS = 1e-6
PROJ_SPLITS = (SSM_WIDTH, ATT_WIDTH, ATT_WIDTH, ATT_WIDTH, MEM_WIDTH)
IN_WIDTH = sum(PROJ_SPLITS) + N_BRANCH * D_MODEL

LANES = 128
SUBLANES = 8
VMEM_LIMIT_BYTES = 56 * 1024 * 1024

SSM_T = 8
SSM_LANE_BLOCKS = SSM_WIDTH // LANES
SSM_GROUPS_PER_BLOCK = LANES // SSM_GROUP
SSM_STATES_PER_BLOCK = SSM_GROUPS_PER_BLOCK * SSM_STATE
SSM_STATE_LANES = 2 * SSM_STATES_PER_BLOCK

ROW_TILE = 512
PROJ_COL_CHUNK = 512
SSM_ROW_TILE = 256
CACHE_ROWS_PER_STEP = 8
ATT_CHUNK = 2048
NEG_BIG = -1e30


def _alibi_slopes():
    return [float(2.0 ** (-8.0 * h / ATT_HEADS)) for h in range(1, ATT_HEADS + 1)]


def _params(*sem):
    return pltpu.CompilerParams(dimension_semantics=sem, vmem_limit_bytes=VMEM_LIMIT_BYTES)


def _rmsnorm(x, g):
    ms = jnp.mean(x * x, axis=-1, keepdims=True)
    return x * lax.rsqrt(ms + EPS) * g


def _full(shape):
    nd = len(shape)
    return pl.BlockSpec(shape, lambda *_: (0,) * nd, pipeline_mode=pl.Buffered(1))


def _resident_cols(w, start, width):
    return pl.BlockSpec((pl.Element(w.shape[0]), pl.Element(width)), lambda *_: (0, start),
                        pipeline_mode=pl.Buffered(1))


Job = collections.namedtuple("Job", "kernel args in_specs out_specs out_shapes scratch steps")


def _jobs_kernel(*refs, kernels, layout):
    n_in, n_out = sum(l[0] for l in layout), sum(l[1] for l in layout)
    ins, outs, scratch = refs[:n_in], refs[n_in:n_in + n_out], refs[n_in + n_out:]
    i0 = o0 = s0 = 0
    for kernel, (ni, no, ns) in zip(kernels, layout):
        kernel(*ins[i0:i0 + ni], *outs[o0:o0 + no], *scratch[s0:s0 + ns])
        i0, o0, s0 = i0 + ni, o0 + no, s0 + ns


def _run(jobs, name):
    steps = jobs[0].steps
    assert all(j.steps == steps for j in jobs)
    layout = tuple((len(j.args), len(j.out_shapes), len(j.scratch)) for j in jobs)
    outs = pl.pallas_call(
        functools.partial(_jobs_kernel, kernels=tuple(j.kernel for j in jobs), layout=layout),
        grid=(steps,),
        in_specs=[s for j in jobs for s in j.in_specs],
        out_specs=[s for j in jobs for s in j.out_specs],
        out_shape=[s for j in jobs for s in j.out_shapes],
        scratch_shapes=[s for j in jobs for s in j.scratch],
        compiler_params=_params("arbitrary"),
        name=name,
    )(*[a for j in jobs for a in j.args])
    split, o0 = [], 0
    for _, no, _ in layout:
        split.append(list(outs[o0:o0 + no]))
        o0 += no
    return split


def _run_pair(main, rider, name):
    m, (r,) = _run_with(main, [rider], name)
    return m, r


def _run_with(main, riders, name):
    fused = [r for r in riders if r.steps == main.steps]
    outs = _run(fused + [main], name)
    by_rider = {id(r): o for r, o in zip(fused, outs)}
    return outs[-1], [by_rider[id(r)] if id(r) in by_rider else _run([r], name + "_rider")[0] for r in riders]


def _cast_kernel(*refs):
    n = len(refs) // 2
    for src, dst in zip(refs[:n], refs[n:]):
        dst[...] = src[...].astype(dst.dtype)


def _cast_job(weights, steps):
    specs = []
    for w in weights:
        rows = w.shape[0] // steps
        assert rows * steps == w.shape[0] and rows % (2 * SUBLANES) == 0, (w.shape, steps)
        specs.append(pl.BlockSpec((rows, w.shape[1]), lambda i: (i, 0)))
    return Job(kernel=_cast_kernel, args=list(weights), in_specs=specs, out_specs=specs,
               out_shapes=[jax.ShapeDtypeStruct(w.shape, BF16) for w in weights], scratch=[], steps=steps)


def _norm_proj_kernel(x_ref, g_ref, w_ref, *out_refs, splits, windows, tiles_per_seq):
    tm = x_ref.shape[0]
    h = _rmsnorm(x_ref[...], g_ref[...]).astype(BF16)
    off = 0
    for o_ref, width in zip(out_refs, splits):
        for c0 in range(0, width, PROJ_COL_CHUNK):
            cw = min(PROJ_COL_CHUNK, width - c0)
            o_ref[:, c0:c0 + cw] = jnp.dot(h, w_ref[:, off + c0:off + c0 + cw], preferred_element_type=F32)
        off += width
    tile = pl.program_id(0) % tiles_per_seq
    for win_ref, (src, col0, cols, win) in zip(out_refs[len(splits):], windows):
        rows = min(win, tm)

        @pl.when(tile >= tiles_per_seq - max(win // tm, 1))
        def _(win_ref=win_ref, src=src, col0=col0, cols=cols, rows=rows):
            win_ref[...] = out_refs[src][tm - rows:tm, col0:col0 + cols].T


def _norm_proj_job(x, g, w_bf16, splits, tm, n_seq=1, windows=()):
    rows, d = x.shape
    assert rows % (tm * n_seq) == 0 and sum(splits) <= w_bf16.shape[1]
    tps = rows // n_seq // tm
    win_specs, win_shapes = [], []
    for _, _, cols, win in windows:
        assert (win % tm == 0 or tm % win == 0) and win <= tps * tm
        n_tiles = max(win // tm, 1)
        win_specs.append(pl.BlockSpec((None, cols, min(win, tm)), lambda i, n_tiles=n_tiles: (
            i // tps, 0, jnp.maximum(i % tps - (tps - n_tiles), 0))))
        win_shapes.append(jax.ShapeDtypeStruct((n_seq, cols, win), F32))
    return Job(
        kernel=functools.partial(_norm_proj_kernel, splits=splits, windows=tuple(windows), tiles_per_seq=tps),
        args=[x, g.reshape(1, d), w_bf16],
        in_specs=[pl.BlockSpec((tm, d), lambda i: (i, 0)), _full((1, d)), _resident_cols(w_bf16, 0, sum(splits))],
        out_specs=[pl.BlockSpec((tm, s), lambda i: (i, 0)) for s in splits] + win_specs,
        out_shapes=[jax.ShapeDtypeStruct((rows, s), F32) for s in splits] + win_shapes,
        scratch=[],
        steps=rows // tm,
    )


def _ssm_layout(a_re, a_im, log_dt, b_re, b_im, c_re, c_im):
    nb, gpb, p, c = SSM_LANE_BLOCKS, SSM_GROUPS_PER_BLOCK, SSM_STATE, SSM_GROUP
    rows = jnp.stack([a_re.reshape(nb, gpb * p), a_im.reshape(nb, gpb * p),
                      jnp.repeat(log_dt, p).reshape(nb, gpb * p)], axis=1)
    eye = jnp.eye(gpb, dtype=F32)

    def place_b(b):
        return jnp.einsum('bgpc,gh->bgchp', b.reshape(nb, gpb, p, c), eye).reshape(nb, gpb * c, gpb * p)

    def place_c(m):
        return jnp.einsum('bgcp,gh->bhpgc', m.reshape(nb, gpb, c, p), eye).reshape(nb, gpb * p, gpb * c)

    return rows, place_b(b_re), place_b(b_im), place_c(c_re), place_c(c_im)


def _ssm_prep_kernel(rows_ref, bre_ref, bim_ref, cre_ref, cim_ref,
                     m_ref, w_ref, v_ref, v0_ref, a1_ref, apow_ref, *, t_chunk):
    sp = SSM_STATES_PER_BLOCK
    a_re, a_im, dt = rows_ref[0:1, :], rows_ref[1:2, :], jnp.exp(rows_ref[2:3, :])

    def powers(k):
        mag = jnp.exp(a_re * dt * k)
        ang = a_im * dt * k
        return mag * jnp.cos(ang), mag * jnp.sin(ang)

    n_pow = 2 * SUBLANES
    assert t_chunk + 1 <= n_pow
    pw_re, pw_im = powers(lax.broadcasted_iota(jnp.int32, (n_pow, 1), 0).astype(F32))
    pwt_re, pwt_im = pw_re.T, pw_im.T
    ab_re, ab_im = pw_re[1:2, :], pw_im[1:2, :]
    den = a_re * a_re + a_im * a_im
    q_re = ((ab_re - 1.0) * a_re + ab_im * a_im) / den
    q_im = (ab_im * a_re - (ab_re - 1.0) * a_im) / den
    bre, bim = bre_ref[...], bim_ref[...]
    bb_re = q_re * bre - q_im * bim
    bb_im = q_re * bim + q_im * bre
    cre, cim = cre_ref[...], cim_ref[...]

    m_ref[...] = jnp.zeros(m_ref.shape, m_ref.dtype)
    for k in range(t_chunk):
        pk_re, pk_im = pw_re[k:k + 1, :], pw_im[k:k + 1, :]
        bk_re = bb_re * pk_re - bb_im * pk_im
        bk_im = bb_re * pk_im + bb_im * pk_re
        t = t_chunk - 1 - k
        w_ref[t * LANES:(t + 1) * LANES, 0:sp] = bk_re.astype(w_ref.dtype)
        w_ref[t * LANES:(t + 1) * LANES, sp:2 * sp] = bk_im.astype(w_ref.dtype)
        kk = (jnp.dot(bk_re, cre, precision=lax.Precision.HIGHEST, preferred_element_type=F32)
              - jnp.dot(bk_im, cim, precision=lax.Precision.HIGHEST, preferred_element_type=F32))
        kk = kk.astype(m_ref.dtype)
        for t0 in range(t_chunk - k):
            m_ref[t0 * LANES:(t0 + 1) * LANES, (t0 + k) * LANES:(t0 + k + 1) * LANES] = kk

    for t in range(t_chunk):
        pc_re, pc_im = pwt_re[:, t + 1:t + 2], pwt_im[:, t + 1:t + 2]
        v_ref[0:sp, t * LANES:(t + 1) * LANES] = (cre * pc_re - cim * pc_im).astype(v_ref.dtype)
        v_ref[sp:2 * sp, t * LANES:(t + 1) * LANES] = (-(cre * pc_im + cim * pc_re)).astype(v_ref.dtype)
    v0_ref[0:sp, :] = cre.astype(v0_ref.dtype)
    v0_ref[sp:2 * sp, :] = (-cim).astype(v0_ref.dtype)

    a1_ref[:, 0:sp] = ab_re
    a1_ref[:, sp:2 * sp] = ab_im
    steps = (lax.broadcasted_iota(jnp.int32, (SUBLANES, 1), 0) + 1) * t_chunk
    ap_re, ap_im = powers(steps.astype(F32))
    apow_ref[:, 0:sp] = ap_re
    apow_ref[:, sp:2 * sp] = ap_im


def _ssm_prep_job(a_re, a_im, log_dt, b_re, b_im, c_re, c_im):
    nb, sp, sl, tl = SSM_LANE_BLOCKS, SSM_STATES_PER_BLOCK, SSM_STATE_LANES, SSM_T * LANES

    def blk(shape):
        return pl.BlockSpec((None,) + shape, lambda b: (b, 0, 0))

    return Job(
        kernel=functools.partial(_ssm_prep_kernel, t_chunk=SSM_T),
        args=list(_ssm_layout(a_re, a_im, log_dt, b_re, b_im, c_re, c_im)),
        in_specs=[blk((3, sp)), blk((LANES, sp)), blk((LANES, sp)), blk((sp, LANES)), blk((sp, LANES))],
        out_specs=[blk((tl, tl)), blk((tl, sl)), blk((sl, tl)), blk((sl, LANES)), blk((1, sl)), blk((SUBLANES, sl))],
        out_shapes=[jax.ShapeDtypeStruct((nb, tl, tl), BF16), jax.ShapeDtypeStruct((nb, tl, sl), BF16),
                    jax.ShapeDtypeStruct((nb, sl, tl), BF16), jax.ShapeDtypeStruct((nb, sl, LANES), BF16),
                    jax.ShapeDtypeStruct((nb, 1, sl), F32), jax.ShapeDtypeStruct((nb, SUBLANES, sl), F32)],
        scratch=[],
        steps=nb,
    )


def _chunk_tokens(u_ref, t, t_chunk):
    return u_ref[pl.ds(t, u_ref.shape[0] // t_chunk, stride=t_chunk), :]


def _chunk_lanes(u_ref, t_chunk):
    return jnp.concatenate([_chunk_tokens(u_ref, t, t_chunk) for t in range(t_chunk)], axis=1)


def _ssm_end_state_kernel(*refs, t_chunk):
    nb, sl = SSM_LANE_BLOCKS, SSM_STATE_LANES
    u_refs, w_ref, e_ref = refs[:nb], refs[nb], refs[nb + 1]
    for b in range(nb):
        ub = _chunk_lanes(u_refs[b], t_chunk).astype(BF16)
        e_ref[:, b * sl:(b + 1) * sl] = jnp.dot(ub, w_ref[b], preferred_element_type=F32)


def _ssm_scan_kernel(ere_ref, eim_ref, pre_ref, pim_ref, spre_ref, spim_ref, fre_ref, fim_ref):
    n_tiles = ere_ref.shape[0] // SUBLANES
    width = ere_ref.shape[1]
    p_re, p_im = pre_ref[...], pim_ref[...]
    row = lax.broadcasted_iota(jnp.int32, (SUBLANES, width), 0)

    def shift_down(x, k):
        return jnp.where(row >= k, pltpu.roll(x, k, 0), 0.0)

    def body(i, carry):
        c_re, c_im = carry
        rows = pl.ds(pl.multiple_of(i * SUBLANES, SUBLANES), SUBLANES)
        x_re, x_im = ere_ref[rows, :], eim_ref[rows, :]
        for k in (1, 2, 4):
            a_re, a_im = p_re[k - 1:k, :], p_im[k - 1:k, :]
            s_re, s_im = shift_down(x_re, k), shift_down(x_im, k)
            x_re, x_im = x_re + a_re * s_re - a_im * s_im, x_im + a_re * s_im + a_im * s_re
        t_re = x_re + p_re * c_re - p_im * c_im
        t_im = x_im + p_re * c_im + p_im * c_re
        spre_ref[rows, :] = jnp.where(row >= 1, pltpu.roll(t_re, 1, 0), c_re)
        spim_ref[rows, :] = jnp.where(row >= 1, pltpu.roll(t_im, 1, 0), c_im)
        return t_re[SUBLANES - 1:SUBLANES, :], t_im[SUBLANES - 1:SUBLANES, :]

    zero = jnp.zeros((1, width), F32)
    f_re, f_im = lax.fori_loop(0, n_tiles, body, (zero, zero))
    fre_ref[...] = f_re
    fim_ref[...] = f_im


def _ssm_output_kernel(*refs, t_chunk):
    nb, sp = SSM_LANE_BLOCKS, SSM_STATES_PER_BLOCK
    u_refs, (spre_ref, spim_ref, m_ref, v_ref, d_ref), y_refs = refs[:nb], refs[nb:nb + 5], refs[nb + 5:]
    for b in range(nb):
        ub = _chunk_lanes(u_refs[b], t_chunk).astype(BF16)
        yb = jnp.dot(ub, m_ref[b], preferred_element_type=F32)
        states = slice(b * sp, (b + 1) * sp)
        sprev = jnp.concatenate([spre_ref[:, states], spim_ref[:, states]], axis=1).astype(BF16)
        yb = yb + jnp.dot(sprev, v_ref[b], preferred_element_type=F32)
        d = d_ref[:, b * LANES:(b + 1) * LANES]
        for t in range(t_chunk):
            y_t = yb[:, t * LANES:(t + 1) * LANES] + d * _chunk_tokens(u_refs[b], t, t_chunk)
            y_refs[b][pl.ds(t, yb.shape[0], stride=t_chunk), :] = y_t


def _ssm_tiling(rows, n_seq, steps_wanted):
    assert rows % (SSM_T * n_seq) == 0
    n_chunks = rows // SSM_T // n_seq
    assert n_chunks % SUBLANES == 0
    tr = min(SSM_ROW_TILE, n_chunks)
    if steps_wanted and (n_seq * n_chunks) % steps_wanted == 0:
        want = n_seq * n_chunks // steps_wanted
        if want % SUBLANES == 0 and n_chunks % want == 0 and want <= SSM_ROW_TILE:
            tr = want
    assert n_chunks % tr == 0
    return n_chunks, tr


def _ssm_end_state_job(u, n_seq, w_mat, steps_wanted=None):
    nb, sl = SSM_LANE_BLOCKS, SSM_STATE_LANES
    n_chunks, tr = _ssm_tiling(u.shape[0], n_seq, steps_wanted)
    tiles = n_chunks // tr
    return Job(
        kernel=functools.partial(_ssm_end_state_kernel, t_chunk=SSM_T),
        args=[u] * nb + [w_mat],
        in_specs=[pl.BlockSpec((tr * SSM_T, LANES), lambda i, b=b: (i, b)) for b in range(nb)] + [_full(w_mat.shape)],
        out_specs=[pl.BlockSpec((tr, nb * sl), lambda i: (i % tiles, i // tiles))],
        out_shapes=[jax.ShapeDtypeStruct((n_chunks, n_seq * nb * sl), F32)],
        scratch=[],
        steps=n_seq * tiles,
    )


def _ssm_scan(e, apow, n_seq):
    nb, sp = SSM_LANE_BLOCKS, SSM_STATES_PER_BLOCK
    n_chunks = e.shape[0]
    col = pl.BlockSpec((n_chunks, sp), lambda g: (0, g))
    fin = pl.BlockSpec((1, sp), lambda g: (0, g))
    return pl.pallas_call(
        _ssm_scan_kernel,
        grid=(n_seq * nb,),
        in_specs=[pl.BlockSpec((n_chunks, sp), lambda g: (0, 2 * g)),
                  pl.BlockSpec((n_chunks, sp), lambda g: (0, 2 * g + 1)),
                  pl.BlockSpec((None, SUBLANES, sp), lambda g: (g % nb, 0, 0)),
                  pl.BlockSpec((None, SUBLANES, sp), lambda g: (g % nb, 0, 1))],
        out_specs=[col, col, fin, fin],
        out_shape=[jax.ShapeDtypeStruct((n_chunks, n_seq * nb * sp), F32)] * 2
        + [jax.ShapeDtypeStruct((1, n_seq * nb * sp), F32)] * 2,
        compiler_params=_params("parallel"),
        name="ssm_scan",
    )(e, e, apow, apow)


def _ssm_output_job(u, n_seq, sp_re, sp_im, m_mat, v_mat, d_skip, steps_wanted=None):
    nb, sp = SSM_LANE_BLOCKS, SSM_STATES_PER_BLOCK
    rows = u.shape[0]
    n_chunks, tr = _ssm_tiling(rows, n_seq, steps_wanted)
    tiles = n_chunks // tr
    sp_spec = pl.BlockSpec((tr, nb * sp), lambda i: (i % tiles, i // tiles))
    return Job(
        kernel=functools.partial(_ssm_output_kernel, t_chunk=SSM_T),
        args=[u] * nb + [sp_re, sp_im, m_mat, v_mat, d_skip.reshape(1, SSM_WIDTH)],
        in_specs=[pl.BlockSpec((tr * SSM_T, LANES), lambda i, b=b: (i, b)) for b in range(nb)]
        + [sp_spec, sp_spec, _full(m_mat.shape), _full(v_mat.shape), _full((1, SSM_WIDTH))],
        out_specs=[pl.BlockSpec((tr * SSM_T, LANES), lambda i: (i, 0)) for _ in range(nb)],
        out_shapes=[jax.ShapeDtypeStruct((rows, LANES), F32) for _ in range(nb)],
        scratch=[],
        steps=n_seq * tiles,
    )


def _ssm_step_kernel(u_ref, sre_ref, sim_ref, w0_ref, v0_ref, a1_ref, d_ref, y_ref, nre_ref, nim_ref):
    sp = SSM_STATES_PER_BLOCK
    for b in range(SSM_LANE_BLOCKS):
        lanes = slice(b * LANES, (b + 1) * LANES)
        states = slice(b * sp, (b + 1) * sp)
        u = u_ref[:, lanes]
        e = jnp.dot(u.astype(BF16), w0_ref[b], preferred_element_type=F32)
        a_re, a_im = a1_ref[b, :, 0:sp], a1_ref[b, :, sp:2 * sp]
        s_re, s_im = sre_ref[states, :].T, sim_ref[states, :].T
        n_re = a_re * s_re - a_im * s_im + e[:, 0:sp]
        n_im = a_re * s_im + a_im * s_re + e[:, sp:2 * sp]
        nre_ref[states, :] = n_re.T
        nim_ref[states, :] = n_im.T
        sn = jnp.concatenate([n_re, n_im], axis=1).astype(BF16)
        y_ref[:, lanes] = jnp.dot(sn, v0_ref[b], preferred_element_type=F32) + d_ref[:, lanes] * u


def _ssm_step(u, s_re, s_im, w0, v0, a1, d_skip):
    rows = u.shape[0]
    ns = SSM_GROUPS * SSM_STATE
    assert rows % LANES == 0
    args = (u, s_re.reshape(rows, ns).T, s_im.reshape(rows, ns).T, w0, v0, a1, d_skip.reshape(1, SSM_WIDTH))
    y, n_re, n_im = pl.pallas_call(
        _ssm_step_kernel,
        grid=(1,),
        in_specs=[_full(a.shape) for a in args],
        out_specs=[_full((rows, SSM_WIDTH)), _full((ns, rows)), _full((ns, rows))],
        out_shape=[jax.ShapeDtypeStruct((rows, SSM_WIDTH), F32), jax.ShapeDtypeStruct((ns, rows), F32),
                   jax.ShapeDtypeStruct((ns, rows), F32)],
        compiler_params=_params("arbitrary"),
        name="ssm_step",
    )(*args)
    return y, n_re.T.reshape(s_re.shape), n_im.T.reshape(s_im.shape)


def _dil_attn_kernel(q_ref, kc_ref, kp_ref, vc_ref, vp_ref, o_ref, lse_ref, *, dil, slopes, chunks, pairs):
    steps, hd = ATT_STEPS, ATT_HEAD_DIM
    chunk = q_ref.shape[0]
    span = steps * dil
    step = pl.program_id(0)
    first_chunk = (step // pairs) % chunks == 0
    pair = step % pairs
    qi = lax.broadcasted_iota(jnp.int32, (steps, 2 * steps), 0)
    kj = lax.broadcasted_iota(jnp.int32, (steps, 2 * steps), 1)
    dist = qi + steps - kj
    band = (dist >= 0) & (dist <= steps)
    distf = (dist * dil).astype(F32)
    lane = lax.broadcasted_iota(jnp.int32, (steps, LANES), 1)
    heads = [lane < hd, lane >= hd]
    biases, first_biases = [], []
    for hh in range(2):
        slope = sum(jnp.where(pair == p, slopes[2 * p + hh], 0.0) for p in range(pairs))
        bias = jnp.where(band, -slope * distf, NEG_BIG)
        biases.append(bias)
        first_biases.append(jnp.where(first_chunk & (kj < steps), NEG_BIG, bias))

    def rows(ref, start):
        return ref[pl.ds(start, steps, stride=dil), :] if dil > 1 else ref[pl.ds(start, steps), :]

    for r in range(dil):
        for qb in range(chunk // span):
            start = r + qb * span
            q = rows(q_ref, start) * (hd ** -0.5)
            if qb == 0:
                k_prev, v_prev = rows(kp_ref, r), rows(vp_ref, r)
            else:
                k_prev, v_prev = rows(kc_ref, start - span), rows(vc_ref, start - span)
            kk = jnp.concatenate([k_prev, rows(kc_ref, start)], axis=0).astype(BF16)
            vv = jnp.concatenate([v_prev, rows(vc_ref, start)], axis=0).astype(BF16)
            out = jnp.zeros((steps, LANES), F32)
            lse = jnp.zeros((steps, LANES), F32)
            for hh in range(2):
                qh = jnp.where(heads[hh], q, 0.0).astype(BF16)
                s = lax.dot_general(qh, kk, (((1,), (1,)), ((), ())), preferred_element_type=F32)
                s = s + (first_biases[hh] if qb == 0 else biases[hh])
                m = jnp.max(s, axis=-1, keepdims=True)
                p = jnp.exp(s - m)
                den = jnp.sum(p, axis=-1, keepdims=True)
                oh = jnp.dot(p.astype(BF16), vv, preferred_element_type=F32) / den
                out = jnp.where(heads[hh], oh, out)
                lse = jnp.where(heads[hh], m + jnp.log(den), lse)
            if dil > 1:
                o_ref[pl.ds(start, steps, stride=dil), :] = out
                lse_ref[pl.ds(start, steps, stride=dil), :] = lse
            else:
                o_ref[pl.ds(start, steps), :] = out
                lse_ref[pl.ds(start, steps), :] = lse


def _dil_attn_job(q, k, v, n_seq, seq_len, group):
    win, dil = DIL_PAIRS[group]
    steps, gw = ATT_STEPS, ATT_GROUP_WIDTH
    span = steps * dil
    chunk = ATT_CHUNK
    assert win // dil == steps and chunk % span == 0 and seq_len % chunk == 0
    chunks = seq_len // chunk
    pairs = gw // LANES
    cur = pl.BlockSpec((chunk, LANES), lambda i: (i // pairs, pairs * group + i % pairs))
    prev = pl.BlockSpec((span, LANES), lambda i: (jnp.maximum((i // pairs) * (chunk // span) - 1, 0),
                                                  pairs * group + i % pairs))
    out = pl.BlockSpec((chunk, LANES), lambda i: (i // pairs, i % pairs))
    slopes = tuple(_alibi_slopes()[group * HEADS_PER_GROUP:(group + 1) * HEADS_PER_GROUP])
    rows = n_seq * seq_len
    return Job(
        kernel=functools.partial(_dil_attn_kernel, dil=dil, slopes=slopes, chunks=chunks, pairs=pairs),
        args=[q, k, k, v, v],
        in_specs=[cur, cur, prev, cur, prev],
        out_specs=[out, out],
        out_shapes=[jax.ShapeDtypeStruct((rows, gw), F32), jax.ShapeDtypeStruct((rows, gw), F32)],
        scratch=[],
        steps=n_seq * chunks * pairs,
    )


def _mem_attention(q_ref, mk_ref, mv_ref):
    hd = MEM_HEAD_DIM
    outs = []
    for h in range(MEM_HEADS):
        lanes = slice(h * hd, (h + 1) * hd)
        s = lax.dot_general(q_ref[:, lanes].astype(BF16), mk_ref[:, lanes].astype(BF16),
                            (((1,), (1,)), ((), ())), preferred_element_type=F32) * (hd ** -0.5)
        m = jnp.max(s, axis=-1, keepdims=True)
        p = jnp.exp(s - m)
        den = jnp.sum(p, axis=-1, keepdims=True)
        outs.append(jnp.dot(p.astype(BF16), mv_ref[:, lanes].astype(BF16), preferred_element_type=F32) / den)
    return jnp.concatenate(outs, axis=1)


def _cache_attn_kernel(q_ref, kn_ref, vn_ref, k1_ref, v1_ref, k2_ref, v2_ref, k3_ref, v3_ref, o_ref,
                       qt_scr, vnt_scr, *, slopes):
    hpg, hd = HEADS_PER_GROUP, ATT_HEAD_DIM
    j = pl.program_id(0) % hpg
    nb = q_ref.shape[0]
    scale = hd ** -0.5
    q, kn = q_ref[...], kn_ref[...]
    qt_scr[...] = q.T
    vnt_scr[...] = vn_ref[...].T
    lane_head = lax.broadcasted_iota(jnp.int32, q.shape, 1) // hd
    outs, lses = [], []
    for g, (k_ref, v_ref) in enumerate(((k1_ref, v1_ref), (k2_ref, v2_ref), (k3_ref, v3_ref))):
        dil = DIL_PAIRS[g][1]
        n_pos = k_ref.shape[-1]
        rows = pl.ds(pl.multiple_of((g * hpg + j) * hd, hd), hd)
        qg, vng = qt_scr[rows, :], vnt_scr[rows, :]
        slope = sum(jnp.where(j == h, slopes[g * hpg + h], 0.0) for h in range(hpg))
        back = n_pos - lax.broadcasted_iota(jnp.int32, (1, n_pos), 1)
        bias = jnp.where(back % dil == 0, -slope * back.astype(F32), NEG_BIG)
        row = lax.broadcasted_iota(jnp.int32, (nb, n_pos), 0)
        s = jnp.zeros((nb, n_pos), F32)
        for b in range(nb):
            s = jnp.where(row == b, jnp.sum(k_ref[b] * qg[:, b:b + 1], axis=0, keepdims=True), s)
        s = s * scale + bias
        s_new = jnp.sum(jnp.where(lane_head == g * hpg + j, q * kn, 0.0), axis=1, keepdims=True) * scale
        m = jnp.maximum(jnp.max(s, axis=1, keepdims=True), s_new)
        p = jnp.exp(s - m)
        p_new = jnp.exp(s_new - m)
        den = jnp.sum(p, axis=1, keepdims=True) + p_new
        cols = []
        for b in range(nb):
            acc = jnp.sum(v_ref[b] * p[b:b + 1, :], axis=1, keepdims=True) + p_new[b:b + 1, :] * vng[:, b:b + 1]
            cols.append(acc / den[b:b + 1, :])
        outs.append(cols)
        lses.append(m + jnp.log(den))
    top = functools.reduce(jnp.maximum, lses)
    ws = [jnp.exp(l - top) for l in lses]
    total = sum(ws)
    o_ref[...] = jnp.concatenate(
        [sum(w[b:b + 1, :] * cols[b] for w, cols in zip(ws, outs)) / total[b:b + 1, :] for b in range(nb)], axis=1)


def _cache_attn_job(q, k_new, v_new, caches_k, caches_v, nb):
    rows = q.shape[0]
    hpg, hd = HEADS_PER_GROUP, ATT_HEAD_DIM
    assert rows % nb == 0 and nb % SUBLANES == 0
    new = pl.BlockSpec((nb, ATT_WIDTH), lambda i: (i // hpg, 0))
    specs, args = [new, new, new], [q, k_new, v_new]
    for g, (win, dil) in enumerate(DIL_PAIRS):
        for c in (caches_k[g], caches_v[g]):
            assert c.shape == (rows, win, hpg, hd) and win % dil == 0, c.shape
            args.append(c.transpose(0, 2, 3, 1))
            specs.append(pl.BlockSpec((nb, None, hd, win), lambda i: (i // hpg, i % hpg, 0, 0)))
    return Job(
        kernel=functools.partial(_cache_attn_kernel, slopes=tuple(_alibi_slopes())),
        args=args,
        in_specs=specs,
        out_specs=[pl.BlockSpec((None, None, hd, nb), lambda i: (i % hpg, i // hpg, 0, 0))],
        out_shapes=[jax.ShapeDtypeStruct((hpg, rows // nb, hd, nb), F32)],
        scratch=[pltpu.VMEM((ATT_WIDTH, nb), F32) for _ in range(2)],
        steps=(rows // nb) * hpg,
    )


def _cache_attn_result(out):
    hpg, blocks, hd, nb = out.shape
    return out.transpose(1, 3, 0, 2).reshape(blocks * nb, hpg * hd)


def _cache_mem_attn_kernel(q_ref, k_ref, v_ref, o_ref):
    nb, rows, hd = k_ref.shape
    tiles = rows // SUBLANES

    def fold(x):
        return x, pltpu.roll(x, MEM_HEADS, 2)

    q = q_ref[...][:, None]
    k = k_ref[...].reshape(nb, tiles, SUBLANES, hd)
    v = v_ref[...].reshape(nb, tiles, SUBLANES, hd)
    s = jnp.sum(k * q, axis=-1, keepdims=True) * (MEM_HEAD_DIM ** -0.5)
    m = jnp.maximum(*fold(jnp.max(s, axis=1, keepdims=True)))
    p = jnp.exp(s - m)
    den = sum(fold(jnp.sum(p, axis=1, keepdims=True)))
    acc = sum(fold(jnp.sum(p * v, axis=1, keepdims=True)))
    o_ref[...] = (acc / den)[:, 0]


def _cache_mem_job(qm, mem_k, mem_v, nb):
    rows = qm.shape[0]
    assert rows % nb == 0 and SUBLANES == 2 * MEM_HEADS
    q4 = qm.reshape(rows, MEM_HEADS, MEM_HEAD_DIM)
    kv = pl.BlockSpec((nb, N_MEM * MEM_HEADS, MEM_HEAD_DIM), lambda i: (i, 0, 0))
    q = pl.BlockSpec((nb, SUBLANES, MEM_HEAD_DIM), lambda i: (i, 0, 0))
    return Job(
        kernel=_cache_mem_attn_kernel,
        args=[jnp.concatenate([q4, q4], axis=1), mem_k.reshape(rows, N_MEM * MEM_HEADS, MEM_HEAD_DIM),
              mem_v.reshape(rows, N_MEM * MEM_HEADS, MEM_HEAD_DIM)],
        in_specs=[q, kv, kv],
        out_specs=[q],
        out_shapes=[jax.ShapeDtypeStruct((rows, SUBLANES, MEM_HEAD_DIM), F32)],
        scratch=[],
        steps=rows // nb,
    )


def _cache_mem_result(out):
    return out[:, :MEM_HEADS].reshape(out.shape[0], MEM_WIDTH)


def _merge_kernel(x_ref, *refs, n_y, n_att, n_mem):
    y_refs, att_refs = refs[:n_y], refs[n_y:n_y + n_att]
    mem_refs = refs[n_y + n_att:n_y + n_att + n_mem]
    g1_ref, wgate_ref, wglu_ref, watt_ref, wmem_ref, wout_ref, o_ref = refs[n_y + n_att + n_mem:]
    d = D_MODEL
    x = x_ref[...]
    h = _rmsnorm(x, g1_ref[...]).astype(BF16)

    def gate(i):
        return jax.nn.sigmoid(jnp.dot(h, wgate_ref[:, i * d:(i + 1) * d], preferred_element_type=F32))

    y = jnp.concatenate([r[...] for r in y_refs], axis=1) if n_y > 1 else y_refs[0][...]
    z = jax.nn.gelu(y).astype(BF16)
    glu = jnp.dot(z, wglu_ref[...], preferred_element_type=F32)
    merged = gate(0) * (glu[:, 0:d] * jax.nn.sigmoid(glu[:, d:2 * d]))
    if n_att == 1:
        att = att_refs[0][...]
    else:
        lses = [r[...] for r in att_refs[1::2]]
        top = functools.reduce(jnp.maximum, lses)
        ws = [jnp.exp(l - top) for l in lses]
        att = sum(w * r[...] for w, r in zip(ws, att_refs[0::2])) / sum(ws)
    b_att = jnp.dot(att.astype(BF16), watt_ref[...], preferred_element_type=F32)
    merged = merged + gate(1) * b_att
    o_mem = mem_refs[0][...] if n_mem == 1 else _mem_attention(*mem_refs)
    b_mem = jnp.dot(o_mem.astype(BF16), wmem_ref[...], preferred_element_type=F32)
    merged = merged + gate(2) * b_mem
    o_ref[...] = x + jnp.dot(merged.astype(BF16), wout_ref[...], preferred_element_type=F32)


def _merge_job(x, ys, atts, mems, g1, w_in, w_glu, w_att_o, w_mem_o, w_out, tm):
    rows = x.shape[0]
    gate_cols = N_BRANCH * D_MODEL
    assert rows % tm == 0 and w_in.shape == (D_MODEL, IN_WIDTH)

    def tile(a):
        return pl.BlockSpec((tm, a.shape[1]), lambda i: (i, 0))

    acts = [x, *ys, *atts, mems[0]]
    act_specs = [tile(a) for a in acts]
    if len(mems) > 1:
        tiles_per_seq = rows // mems[1].shape[0] // tm
        assert tiles_per_seq * tm * mems[1].shape[0] == rows
        act_specs += [pl.BlockSpec((None, N_MEM, MEM_WIDTH), lambda i: (i // tiles_per_seq, 0, 0))] * 2
        acts += list(mems[1:])
    weights = [w_glu, w_att_o, w_mem_o, w_out]
    return Job(
        kernel=functools.partial(_merge_kernel, n_y=len(ys), n_att=len(atts), n_mem=len(mems)),
        args=acts + [g1.reshape(1, D_MODEL), w_in] + weights,
        in_specs=act_specs + [_full((1, D_MODEL)), _resident_cols(w_in, IN_WIDTH - gate_cols, gate_cols)]
        + [_full(w.shape) for w in weights],
        out_specs=[tile(x)],
        out_shapes=[jax.ShapeDtypeStruct(x.shape, F32)],
        scratch=[],
        steps=rows // tm,
    )


def _ffn_kernel(*refs, tm, tiles_per_seq, stepwise):
    if stepwise:
        x_ref, g2_ref, wup_ref, cw_ref, cb_ref, wdn_ref, gf_ref, prev_ref, y_ref, conv_ref = refs
    else:
        x_ref, g2_ref, wup_ref, cw_ref, cb_ref, wdn_ref, gf_ref, y_ref, conv_ref, a_scr = refs
        i = pl.program_id(0)
        first = i % tiles_per_seq == 0

        @pl.when(first)
        def _():
            a_scr[0:SUBLANES, :] = jnp.zeros((SUBLANES, D_FF), F32)

        @pl.when(jnp.logical_not(first))
        def _():
            a_scr[0:SUBLANES, :] = a_scr[tm:tm + SUBLANES, :]

    x = x_ref[...]
    h = _rmsnorm(x, g2_ref[...]).astype(BF16)
    a = jnp.dot(h, wup_ref[:, 0:D_FF], preferred_element_type=F32)
    if stepwise:
        a2, a1 = prev_ref[:, 0:D_FF], prev_ref[:, D_FF:2 * D_FF]
        conv_ref[:, 0:D_FF] = a1
        conv_ref[:, D_FF:2 * D_FF] = a
    else:
        a_scr[SUBLANES:SUBLANES + tm, :] = a
        a1 = a_scr[SUBLANES - 1:SUBLANES - 1 + tm, :]
        a2 = a_scr[SUBLANES - 2:SUBLANES - 2 + tm, :]
    c = a2 * cw_ref[0:1, :] + a1 * cw_ref[1:2, :] + a * cw_ref[2:3, :] + cb_ref[...]
    v = jnp.dot(h, wup_ref[:, D_FF:2 * D_FF], preferred_element_type=F32)
    y = jnp.dot((jax.nn.gelu(c) * v).astype(BF16), wdn_ref[...], preferred_element_type=F32)
    y_ref[...] = _rmsnorm(x + y, gf_ref[...])
    if not stepwise:
        conv_ref[...] = a_scr[SUBLANES + tm - (CONV_W - 1):SUBLANES + tm, :]


def _ffn_job(x, g2, w_up, conv_w, conv_b, w_down, gf, tm, n_seq=None, prev=None):
    rows, d = x.shape
    assert rows % tm == 0
    stepwise = prev is not None
    weights = [g2.reshape(1, d), w_up, conv_w, conv_b.reshape(1, D_FF), w_down, gf.reshape(1, d)]
    in_specs = [pl.BlockSpec((tm, d), lambda i: (i, 0))] + [_full(w.shape) for w in weights]
    args = [x] + weights
    if stepwise:
        tiles_per_seq = 1
        in_specs.append(pl.BlockSpec((tm, 2 * D_FF), lambda i: (i, 0)))
        args.append(prev)
        conv_spec = pl.BlockSpec((tm, 2 * D_FF), lambda i: (i, 0))
        conv_shape = jax.ShapeDtypeStruct((rows, 2 * D_FF), F32)
        scratch = []
    else:
        tiles_per_seq = rows // n_seq // tm
        assert tiles_per_seq * tm * n_seq == rows
        conv_spec = pl.BlockSpec((None, CONV_W - 1, D_FF), lambda i: (i // tiles_per_seq, 0, 0))
        conv_shape = jax.ShapeDtypeStruct((n_seq, CONV_W - 1, D_FF), F32)
        scratch = [pltpu.VMEM((tm + SUBLANES, D_FF), F32)]
    return Job(
        kernel=functools.partial(_ffn_kernel, tm=tm, tiles_per_seq=tiles_per_seq, stepwise=stepwise),
        args=args,
        in_specs=in_specs,
        out_specs=[pl.BlockSpec((tm, d), lambda i: (i, 0)), conv_spec],
        out_shapes=[jax.ShapeDtypeStruct((rows, d), F32), conv_shape],
        scratch=scratch,
        steps=rows // tm,
    )


def kernel(x_prompt, x_sample, state_ssm_re, state_ssm_im, cache_w1_k, cache_w1_v, cache_w2_k, cache_w2_v, cache_w3_k, cache_w3_v, cache_mem_k, cache_mem_v, state_ffn_conv, mem_prompt, norm1_g, w_in, ssm_a_re, ssm_a_im, ssm_log_dt, ssm_b_re, ssm_b_im, ssm_c_re, ssm_c_im, ssm_d, w_ssm_glu, w_att_o, mem_norm_g, w_mem_kv, w_mem_o, w_out, norm2_g, w_up, ffn_conv_w, ffn_conv_b, w_down, final_norm_g):
    n_seq, seq_len, d = x_prompt.shape
    n_dec, dec_len, _ = x_sample.shape
    depth = norm1_g.shape[0]
    assert d == D_MODEL and depth == 1 and dec_len == 1
    assert w_in.shape == (depth, D_MODEL, IN_WIDTH) and w_up.shape == (depth, D_MODEL, 2 * D_FF)
    assert mem_prompt.shape == (n_seq, N_MEM, D_MODEL)
    assert ssm_a_re.shape == (depth, SSM_GROUPS, SSM_STATE)
    assert seq_len % ROW_TILE == 0 and n_dec % SUBLANES == 0
    hpg, hd = HEADS_PER_GROUP, ATT_HEAD_DIM
    rows_p = n_seq * seq_len
    dec_tile = n_dec if n_dec <= ROW_TILE else ROW_TILE

    prep_job = _ssm_prep_job(ssm_a_re[0], ssm_a_im[0], ssm_log_dt[0], ssm_b_re[0], ssm_b_im[0], ssm_c_re[0],
                             ssm_c_im[0])
    (m_mat, w_mat, v_mat, v0_mat, a1, apow), (w_in_b,) = _run_pair(
        prep_job, _cast_job([w_in[0]], prep_job.steps), "ssm_prep_cast")
    w0_mat = w_mat[:, (SSM_T - 1) * LANES:, :]

    xs = x_sample.reshape(n_dec, d)
    us, qs, ks, vs, qms = _run([_norm_proj_job(xs, norm1_g[0], w_in_b, PROJ_SPLITS, dec_tile)], "norm_proj")[0]
    cache_job = _cache_attn_job(qs, ks, vs, (cache_w1_k[0], cache_w2_k[0], cache_w3_k[0]),
                                (cache_w1_v[0], cache_w2_v[0], cache_w3_v[0]), CACHE_ROWS_PER_STEP)
    proj_steps = rows_p // ROW_TILE
    mem_rows = n_dec // proj_steps if n_dec % proj_steps == 0 else CACHE_ROWS_PER_STEP
    mem_job = _cache_mem_job(qms, cache_mem_k[0], cache_mem_v[0], mem_rows)

    xp = x_prompt.reshape(rows_p, d)
    gw = ATT_GROUP_WIDTH
    keeps = [min(win, seq_len) for win, _ in DIL_PAIRS]
    windows = [(src, g * gw, gw, keep) for g, keep in enumerate(keeps) for src in (2, 3)]
    proj_tile = 2 * ROW_TILE if seq_len % (2 * ROW_TILE) == 0 else ROW_TILE
    proj_job = _norm_proj_job(xp, norm1_g[0], w_in_b, PROJ_SPLITS, proj_tile, n_seq, windows)
    u, q, k, v, qm, *kv_win = _run([proj_job], "norm_proj")[0]

    dil_jobs = [_dil_attn_job(q, k, v, n_seq, seq_len, g) for g in range(len(DIL_PAIRS))]
    (e,), att0 = _run_pair(_ssm_end_state_job(u, n_seq, w_mat, dil_jobs[0].steps), dil_jobs[0], "ssm_end_state_dil")
    sp_re, sp_im, fin_re, fin_im = _ssm_scan(e, apow, n_seq)
    y_ssm, att1 = _run_pair(_ssm_output_job(u, n_seq, sp_re, sp_im, m_mat, v_mat, ssm_d[0], dil_jobs[1].steps),
                            dil_jobs[1], "ssm_output_dil")
    casts = _cast_job([w_up[0], w_down[0], w_ssm_glu[0], w_out[0], w_mem_o[0], w_mem_kv[0], w_att_o[0]],
                      dil_jobs[2].steps)
    att2, (w_up_b, w_down_b, w_glu_b, w_out_b, w_memo_b, w_memkv_b, w_att_b) = _run_pair(
        dil_jobs[2], casts, "dil_attn_casts")
    merge_w = (norm1_g[0], w_in_b, w_glu_b, w_att_b, w_memo_b, w_out_b)
    ffn_w = (norm2_g[0], w_up_b, ffn_conv_w[0], ffn_conv_b[0], w_down_b, final_norm_g)
    atts = att0 + att1 + att2

    mk, mv = _run([_norm_proj_job(mem_prompt.reshape(n_seq * N_MEM, d), mem_norm_g[0], w_memkv_b,
                                  (MEM_WIDTH, MEM_WIDTH), ROW_TILE)], "mem_kv")[0]
    mems = [qm, mk.reshape(n_seq, N_MEM, MEM_WIDTH), mv.reshape(n_seq, N_MEM, MEM_WIDTH)]
    merge_tile = rows_p // cache_job.steps
    if not (rows_p % cache_job.steps == 0 and merge_tile % LANES == 0 and merge_tile <= ROW_TILE
            and seq_len % merge_tile == 0):
        merge_tile = ROW_TILE
    (x1,), (att_out,) = _run_pair(_merge_job(xp, y_ssm, atts, mems, *merge_w, merge_tile), cache_job,
                                  "merge_cache_attn")
    att_s = _cache_attn_result(att_out)
    ffn_job = _ffn_job(x1, *ffn_w, ROW_TILE, n_seq=n_seq)
    if mem_job.steps == ffn_job.steps:
        (y_p, conv_p), (mem_out,) = _run([ffn_job, mem_job], "ffn_cache_mem")
    else:
        (y_p, conv_p), (mem_out,) = _run([ffn_job], "ffn")[0], _run([mem_job], "cache_mem_attn")[0]
    mem_s = _cache_mem_result(mem_out)

    def final_state(s):
        return s.reshape(1, n_seq, SSM_GROUPS, SSM_STATE)
    win_p = [t.reshape(n_seq, hpg, hd, t.shape[-1]).transpose(0, 3, 1, 2)[None] for t in kv_win]
    mem_kv = [mk.reshape(1, n_seq, N_MEM, MEM_HEADS, MEM_HEAD_DIM), mv.reshape(1, n_seq, N_MEM, MEM_HEADS, MEM_HEAD_DIM)]

    ys_ssm, sn_re, sn_im = _ssm_step(us, state_ssm_re[0], state_ssm_im[0], w0_mat, v0_mat, a1, ssm_d[0])
    n_g = len(DIL_PAIRS)
    ks4, vs4 = (t.reshape(n_dec, n_g, hpg, hd) for t in (ks, vs))
    xs1 = _run([_merge_job(xs, [ys_ssm], [att_s], [mem_s], *merge_w, dec_tile)], "merge")[0][0]
    y_s, conv_s = _run([_ffn_job(xs1, *ffn_w, dec_tile,
                                 prev=state_ffn_conv[0].reshape(n_dec, (CONV_W - 1) * D_FF))], "ffn_step")[0]

    win_s = []
    for g in range(len(DIL_PAIRS)):
        win_s += [ks4[None, :, g:g + 1], vs4[None, :, g:g + 1]]

    return (y_p.reshape(n_seq, seq_len, d), y_s.reshape(n_dec, 1, d),
            final_state(fin_re), final_state(fin_im), *win_p, *mem_kv, conv_p[None],
            sn_re[None], sn_im[None],
            *win_s, conv_s.reshape(1, n_dec, CONV_W - 1, D_FF))
```

```python
import collections
import functools

import jax
import jax.numpy as jnp
from jax import lax
from jax.experimental import pallas as pl
from jax.experimental.pallas import tpu as pltpu

F32 = jnp.float32
BF16 = jnp.bfloat16

D_MODEL = 1024
SSM_WIDTH = 512
SSM_GROUP = 16
SSM_GROUPS = 32
SSM_STATE = 64
ATT_HEAD_DIM = 64
HEADS_PER_GROUP = 4
DIL_PAIRS = ((128, 1), (512, 4), (2048, 16))
ATT_HEADS = len(DIL_PAIRS) * HEADS_PER_GROUP
ATT_WIDTH = ATT_HEADS * ATT_HEAD_DIM
ATT_GROUP_WIDTH = HEADS_PER_GROUP * ATT_HEAD_DIM
ATT_STEPS = 128
N_MEM = 256
MEM_HEADS = 4
MEM_HEAD_DIM = 128
MEM_WIDTH = MEM_HEADS * MEM_HEAD_DIM
N_BRANCH = 3
D_FF = 2816
CONV_W = 3
EPS = 1e-6
PROJ_SPLITS = (SSM_WIDTH, ATT_WIDTH, ATT_WIDTH, ATT_WIDTH, MEM_WIDTH)
IN_WIDTH = sum(PROJ_SPLITS) + N_BRANCH * D_MODEL

LANES = 128
SUBLANES = 8
VMEM_LIMIT_BYTES = 56 * 1024 * 1024

SSM_T = 8
SSM_LANE_BLOCKS = SSM_WIDTH // LANES
SSM_GROUPS_PER_BLOCK = LANES // SSM_GROUP
SSM_STATES_PER_BLOCK = SSM_GROUPS_PER_BLOCK * SSM_STATE
SSM_STATE_LANES = 2 * SSM_STATES_PER_BLOCK

ROW_TILE = 512
PROJ_COL_CHUNK = 512
SSM_ROW_TILE = 256
CACHE_ROWS_PER_STEP = 8
ATT_CHUNK = 2048
NEG_BIG = -1e30


def _alibi_slopes():
    return [float(2.0 ** (-8.0 * h / ATT_HEADS)) for h in range(1, ATT_HEADS + 1)]


def _params(*sem):
    return pltpu.CompilerParams(dimension_semantics=sem, vmem_limit_bytes=VMEM_LIMIT_BYTES)


def _rmsnorm(x, g):
    ms = jnp.mean(x * x, axis=-1, keepdims=True)
    return x * lax.rsqrt(ms + EPS) * g


def _full(shape):
    nd = len(shape)
    return pl.BlockSpec(shape, lambda *_: (0,) * nd, pipeline_mode=pl.Buffered(1))


def _resident_cols(w, start, width):
    return pl.BlockSpec((pl.Element(w.shape[0]), pl.Element(width)), lambda *_: (0, start),
                        pipeline_mode=pl.Buffered(1))


Job = collections.namedtuple("Job", "kernel args in_specs out_specs out_shapes scratch steps")


def _jobs_kernel(*refs, kernels, layout):
    n_in, n_out = sum(l[0] for l in layout), sum(l[1] for l in layout)
    ins, outs, scratch = refs[:n_in], refs[n_in:n_in + n_out], refs[n_in + n_out:]
    i0 = o0 = s0 = 0
    for kernel, (ni, no, ns) in zip(kernels, layout):
        kernel(*ins[i0:i0 + ni], *outs[o0:o0 + no], *scratch[s0:s0 + ns])
        i0, o0, s0 = i0 + ni, o0 + no, s0 + ns


def _run(jobs, name):
    steps = jobs[0].steps
    assert all(j.steps == steps for j in jobs)
    layout = tuple((len(j.args), len(j.out_shapes), len(j.scratch)) for j in jobs)
    outs = pl.pallas_call(
        functools.partial(_jobs_kernel, kernels=tuple(j.kernel for j in jobs), layout=layout),
        grid=(steps,),
        in_specs=[s for j in jobs for s in j.in_specs],
        out_specs=[s for j in jobs for s in j.out_specs],
        out_shape=[s for j in jobs for s in j.out_shapes],
        scratch_shapes=[s for j in jobs for s in j.scratch],
        compiler_params=_params("arbitrary"),
        name=name,
    )(*[a for j in jobs for a in j.args])
    split, o0 = [], 0
    for _, no, _ in layout:
        split.append(list(outs[o0:o0 + no]))
        o0 += no
    return split


def _run_pair(main, rider, name):
    m, (r,) = _run_with(main, [rider], name)
    return m, r


def _run_with(main, riders, name):
    fused = [r for r in riders if r.steps == main.steps]
    outs = _run(fused + [main], name)
    by_rider = {id(r): o for r, o in zip(fused, outs)}
    return outs[-1], [by_rider[id(r)] if id(r) in by_rider else _run([r], name + "_rider")[0] for r in riders]


def _cast_kernel(*refs):
    n = len(refs) // 2
    for src, dst in zip(refs[:n], refs[n:]):
        dst[...] = src[...].astype(dst.dtype)


def _cast_job(weights, steps):
    specs = []
    for w in weights:
        rows = w.shape[0] // steps
        assert rows * steps == w.shape[0] and rows % (2 * SUBLANES) == 0, (w.shape, steps)
        specs.append(pl.BlockSpec((rows, w.shape[1]), lambda i: (i, 0)))
    return Job(kernel=_cast_kernel, args=list(weights), in_specs=specs, out_specs=specs,
               out_shapes=[jax.ShapeDtypeStruct(w.shape, BF16) for w in weights], scratch=[], steps=steps)


def _norm_proj_kernel(x_ref, g_ref, w_ref, *out_refs, splits, windows, tiles_per_seq):
    tm = x_ref.shape[0]
    h = _rmsnorm(x_ref[...], g_ref[...]).astype(BF16)
    off = 0
    for o_ref, width in zip(out_refs, splits):
        for c0 in range(0, width, PROJ_COL_CHUNK):
            cw = min(PROJ_COL_CHUNK, width - c0)
            o_ref[:, c0:c0 + cw] = jnp.dot(h, w_ref[:, off + c0:off + c0 + cw], preferred_element_type=F32)
        off += width
    tile = pl.program_id(0) % tiles_per_seq
    for win_ref, (src, col0, cols, win) in zip(out_refs[len(splits):], windows):
        rows = min(win, tm)

        @pl.when(tile >= tiles_per_seq - max(win // tm, 1))
        def _(win_ref=win_ref, src=src, col0=col0, cols=cols, rows=rows):
            win_ref[...] = out_refs[src][tm - rows:tm, col0:col0 + cols].T


def _norm_proj_job(x, g, w_bf16, splits, tm, n_seq=1, windows=()):
    rows, d = x.shape
    assert rows % (tm * n_seq) == 0 and sum(splits) <= w_bf16.shape[1]
    tps = rows // n_seq // tm
    win_specs, win_shapes = [], []
    for _, _, cols, win in windows:
        assert (win % tm == 0 or tm % win == 0) and win <= tps * tm
        n_tiles = max(win // tm, 1)
        win_specs.append(pl.BlockSpec((None, cols, min(win, tm)), lambda i, n_tiles=n_tiles: (
            i // tps, 0, jnp.maximum(i % tps - (tps - n_tiles), 0))))
        win_shapes.append(jax.ShapeDtypeStruct((n_seq, cols, win), F32))
    return Job(
        kernel=functools.partial(_norm_proj_kernel, splits=splits, windows=tuple(windows), tiles_per_seq=tps),
        args=[x, g.reshape(1, d), w_bf16],
        in_specs=[pl.BlockSpec((tm, d), lambda i: (i, 0)), _full((1, d)), _resident_cols(w_bf16, 0, sum(splits))],
        out_specs=[pl.BlockSpec((tm, s), lambda i: (i, 0)) for s in splits] + win_specs,
        out_shapes=[jax.ShapeDtypeStruct((rows, s), F32) for s in splits] + win_shapes,
        scratch=[],
        steps=rows // tm,
    )


def _ssm_layout(a_re, a_im, log_dt, b_re, b_im, c_re, c_im):
    nb, gpb, p, c = SSM_LANE_BLOCKS, SSM_GROUPS_PER_BLOCK, SSM_STATE, SSM_GROUP
    rows = jnp.stack([a_re.reshape(nb, gpb * p), a_im.reshape(nb, gpb * p),
                      jnp.repeat(log_dt, p).reshape(nb, gpb * p)], axis=1)
    eye = jnp.eye(gpb, dtype=F32)

    def place_b(b):
        return jnp.einsum('bgpc,gh->bgchp', b.reshape(nb, gpb, p, c), eye).reshape(nb, gpb * c, gpb * p)

    def place_c(m):
        return jnp.einsum('bgcp,gh->bhpgc', m.reshape(nb, gpb, c, p), eye).reshape(nb, gpb * p, gpb * c)

    return rows, place_b(b_re), place_b(b_im), place_c(c_re), place_c(c_im)


def _split_bf16(x):
    hi = x.astype(BF16)
    return hi, (x - hi.astype(F32)).astype(BF16)


def _dot_3pass(a, b_split):
    (a_hi, a_lo), (b_hi, b_lo) = _split_bf16(a), b_split
    dot = functools.partial(jnp.dot, preferred_element_type=F32)
    return dot(a_hi, b_hi) + (dot(a_hi, b_lo) + dot(a_lo, b_hi))


def _ssm_prep_kernel(rows_ref, bre_ref, bim_ref, cre_ref, cim_ref,
                     m_ref, w_ref, v_ref, v0_ref, a1_ref, apow_ref, *, t_chunk):
    sp = SSM_STATES_PER_BLOCK
    a_re, a_im, dt = rows_ref[0:1, :], rows_ref[1:2, :], jnp.exp(rows_ref[2:3, :])

    def powers(k):
        mag = jnp.exp(a_re * dt * k)
        ang = a_im * dt * k
        return mag * jnp.cos(ang), mag * jnp.sin(ang)

    n_pow = 2 * SUBLANES
    assert t_chunk + 1 <= n_pow
    pw_re, pw_im = powers(lax.broadcasted_iota(jnp.int32, (n_pow, 1), 0).astype(F32))
    pwt_re, pwt_im = pw_re.T, pw_im.T
    ab_re, ab_im = pw_re[1:2, :], pw_im[1:2, :]
    den = a_re * a_re + a_im * a_im
    q_re = ((ab_re - 1.0) * a_re + ab_im * a_im) / den
    q_im = (ab_im * a_re - (ab_re - 1.0) * a_im) / den
    bre, bim = bre_ref[...], bim_ref[...]
    bb_re = q_re * bre - q_im * bim
    bb_im = q_re * bim + q_im * bre
    cre, cim = cre_ref[...], cim_ref[...]
    cre_split, cim_split = _split_bf16(cre), _split_bf16(cim)

    m_ref[...] = jnp.zeros(m_ref.shape, m_ref.dtype)
    for k in range(t_chunk):
        pk_re, pk_im = pw_re[k:k + 1, :], pw_im[k:k + 1, :]
        bk_re = bb_re * pk_re - bb_im * pk_im
        bk_im = bb_re * pk_im + bb_im * pk_re
        t = t_chunk - 1 - k
        w_ref[t * LANES:(t + 1) * LANES, 0:sp] = bk_re.astype(w_ref.dtype)
        w_ref[t * LANES:(t + 1) * LANES, sp:2 * sp] = bk_im.astype(w_ref.dtype)
        kk = _dot_3pass(bk_re, cre_split) - _dot_3pass(bk_im, cim_split)
        kk = kk.astype(m_ref.dtype)
        for t0 in range(t_chunk - k):
            m_ref[t0 * LANES:(t0 + 1) * LANES, (t0 + k) * LANES:(t0 + k + 1) * LANES] = kk

    for t in range(t_chunk):
        pc_re, pc_im = pwt_re[:, t + 1:t + 2], pwt_im[:, t + 1:t + 2]
        v_ref[0:sp, t * LANES:(t + 1) * LANES] = (cre * pc_re - cim * pc_im).astype(v_ref.dtype)
        v_ref[sp:2 * sp, t * LANES:(t + 1) * LANES] = (-(cre * pc_im + cim * pc_re)).astype(v_ref.dtype)
    v0_ref[0:sp, :] = cre.astype(v0_ref.dtype)
    v0_ref[sp:2 * sp, :] = (-cim).astype(v0_ref.dtype)

    a1_ref[:, 0:sp] = ab_re
    a1_ref[:, sp:2 * sp] = ab_im
    steps = (lax.broadcasted_iota(jnp.int32, (SUBLANES, 1), 0) + 1) * t_chunk
    ap_re, ap_im = powers(steps.astype(F32))
    apow_ref[:, 0:sp] = ap_re
    apow_ref[:, sp:2 * sp] = ap_im


def _ssm_prep_job(a_re, a_im, log_dt, b_re, b_im, c_re, c_im):
    nb, sp, sl, tl = SSM_LANE_BLOCKS, SSM_STATES_PER_BLOCK, SSM_STATE_LANES, SSM_T * LANES

    def blk(shape):
        return pl.BlockSpec((None,) + shape, lambda b: (b, 0, 0))

    return Job(
        kernel=functools.partial(_ssm_prep_kernel, t_chunk=SSM_T),
        args=list(_ssm_layout(a_re, a_im, log_dt, b_re, b_im, c_re, c_im)),
        in_specs=[blk((3, sp)), blk((LANES, sp)), blk((LANES, sp)), blk((sp, LANES)), blk((sp, LANES))],
        out_specs=[blk((tl, tl)), blk((tl, sl)), blk((sl, tl)), blk((sl, LANES)), blk((1, sl)), blk((SUBLANES, sl))],
        out_shapes=[jax.ShapeDtypeStruct((nb, tl, tl), BF16), jax.ShapeDtypeStruct((nb, tl, sl), BF16),
                    jax.ShapeDtypeStruct((nb, sl, tl), BF16), jax.ShapeDtypeStruct((nb, sl, LANES), BF16),
                    jax.ShapeDtypeStruct((nb, 1, sl), F32), jax.ShapeDtypeStruct((nb, SUBLANES, sl), F32)],
        scratch=[],
        steps=nb,
    )


def _chunk_tokens(u_ref, t, t_chunk):
    return u_ref[pl.ds(t, u_ref.shape[0] // t_chunk, stride=t_chunk), :]


def _chunk_lanes(u_ref, t_chunk):
    return jnp.concatenate([_chunk_tokens(u_ref, t, t_chunk) for t in range(t_chunk)], axis=1)


def _ssm_end_state_kernel(*refs, t_chunk):
    nb, sl = SSM_LANE_BLOCKS, SSM_STATE_LANES
    u_refs, w_ref, e_ref = refs[:nb], refs[nb], refs[nb + 1]
    for b in range(nb):
        ub = _chunk_lanes(u_refs[b], t_chunk).astype(BF16)
        e_ref[:, b * sl:(b + 1) * sl] = jnp.dot(ub, w_ref[b], preferred_element_type=F32)


def _ssm_scan_kernel(ere_ref, eim_ref, pre_ref, pim_ref, spre_ref, spim_ref, fre_ref, fim_ref):
    n_tiles = ere_ref.shape[0] // SUBLANES
    width = ere_ref.shape[1]
    p_re, p_im = pre_ref[...], pim_ref[...]
    row = lax.broadcasted_iota(jnp.int32, (SUBLANES, width), 0)

    def shift_down(x, k):
        return jnp.where(row >= k, pltpu.roll(x, k, 0), 0.0)

    def body(i, carry):
        c_re, c_im = carry
        rows = pl.ds(pl.multiple_of(i * SUBLANES, SUBLANES), SUBLANES)
        x_re, x_im = ere_ref[rows, :], eim_ref[rows, :]
        for k in (1, 2, 4):
            a_re, a_im = p_re[k - 1:k, :], p_im[k - 1:k, :]
            s_re, s_im = shift_down(x_re, k), shift_down(x_im, k)
            x_re, x_im = x_re + a_re * s_re - a_im * s_im, x_im + a_re * s_im + a_im * s_re
        t_re = x_re + p_re * c_re - p_im * c_im
        t_im = x_im + p_re * c_im + p_im * c_re
        spre_ref[rows, :] = jnp.where(row >= 1, pltpu.roll(t_re, 1, 0), c_re)
        spim_ref[rows, :] = jnp.where(row >= 1, pltpu.roll(t_im, 1, 0), c_im)
        return t_re[SUBLANES - 1:SUBLANES, :], t_im[SUBLANES - 1:SUBLANES, :]

    zero = jnp.zeros((1, width), F32)
    f_re, f_im = lax.fori_loop(0, n_tiles, body, (zero, zero))
    fre_ref[...] = f_re
    fim_ref[...] = f_im


def _ssm_output_kernel(*refs, t_chunk):
    nb, sp = SSM_LANE_BLOCKS, SSM_STATES_PER_BLOCK
    u_refs, (spre_ref, spim_ref, m_ref, v_ref, d_ref), y_refs = refs[:nb], refs[nb:nb + 5], refs[nb + 5:]
    for b in range(nb):
        ub = _chunk_lanes(u_refs[b], t_chunk).astype(BF16)
        yb = jnp.dot(ub, m_ref[b], preferred_element_type=F32)
        states = slice(b * sp, (b + 1) * sp)
        sprev = jnp.concatenate([spre_ref[:, states], spim_ref[:, states]], axis=1).astype(BF16)
        yb = yb + jnp.dot(sprev, v_ref[b], preferred_element_type=F32)
        d = d_ref[:, b * LANES:(b + 1) * LANES]
        for t in range(t_chunk):
            y_t = yb[:, t * LANES:(t + 1) * LANES] + d * _chunk_tokens(u_refs[b], t, t_chunk)
            y_refs[b][pl.ds(t, yb.shape[0], stride=t_chunk), :] = y_t


def _ssm_tiling(rows, n_seq, steps_wanted):
    assert rows % (SSM_T * n_seq) == 0
    n_chunks = rows // SSM_T // n_seq
    assert n_chunks % SUBLANES == 0
    tr = min(SSM_ROW_TILE, n_chunks)
    if steps_wanted and (n_seq * n_chunks) % steps_wanted == 0:
        want = n_seq * n_chunks // steps_wanted
        if want % SUBLANES == 0 and n_chunks % want == 0 and want <= SSM_ROW_TILE:
            tr = want
    assert n_chunks % tr == 0
    return n_chunks, tr


def _ssm_end_state_job(u, n_seq, w_mat, steps_wanted=None):
    nb, sl = SSM_LANE_BLOCKS, SSM_STATE_LANES
    n_chunks, tr = _ssm_tiling(u.shape[0], n_seq, steps_wanted)
    tiles = n_chunks // tr
    return Job(
        kernel=functools.partial(_ssm_end_state_kernel, t_chunk=SSM_T),
        args=[u] * nb + [w_mat],
        in_specs=[pl.BlockSpec((tr * SSM_T, LANES), lambda i, b=b: (i, b)) for b in range(nb)] + [_full(w_mat.shape)],
        out_specs=[pl.BlockSpec((tr, nb * sl), lambda i: (i % tiles, i // tiles))],
        out_shapes=[jax.ShapeDtypeStruct((n_chunks, n_seq * nb * sl), F32)],
        scratch=[],
        steps=n_seq * tiles,
    )


def _ssm_scan(e, apow, n_seq):
    nb, sp = SSM_LANE_BLOCKS, SSM_STATES_PER_BLOCK
    n_chunks = e.shape[0]
    col = pl.BlockSpec((n_chunks, sp), lambda g: (0, g))
    fin = pl.BlockSpec((1, sp), lambda g: (0, g))
    return pl.pallas_call(
        _ssm_scan_kernel,
        grid=(n_seq * nb,),
        in_specs=[pl.BlockSpec((n_chunks, sp), lambda g: (0, 2 * g)),
                  pl.BlockSpec((n_chunks, sp), lambda g: (0, 2 * g + 1)),
                  pl.BlockSpec((None, SUBLANES, sp), lambda g: (g % nb, 0, 0)),
                  pl.BlockSpec((None, SUBLANES, sp), lambda g: (g % nb, 0, 1))],
        out_specs=[col, col, fin, fin],
        out_shape=[jax.ShapeDtypeStruct((n_chunks, n_seq * nb * sp), F32)] * 2
        + [jax.ShapeDtypeStruct((1, n_seq * nb * sp), F32)] * 2,
        compiler_params=_params("parallel"),
        name="ssm_scan",
    )(e, e, apow, apow)


def _ssm_output_job(u, n_seq, sp_re, sp_im, m_mat, v_mat, d_skip, steps_wanted=None):
    nb, sp = SSM_LANE_BLOCKS, SSM_STATES_PER_BLOCK
    rows = u.shape[0]
    n_chunks, tr = _ssm_tiling(rows, n_seq, steps_wanted)
    tiles = n_chunks // tr
    sp_spec = pl.BlockSpec((tr, nb * sp), lambda i: (i % tiles, i // tiles))
    return Job(
        kernel=functools.partial(_ssm_output_kernel, t_chunk=SSM_T),
        args=[u] * nb + [sp_re, sp_im, m_mat, v_mat, d_skip.reshape(1, SSM_WIDTH)],
        in_specs=[pl.BlockSpec((tr * SSM_T, LANES), lambda i, b=b: (i, b)) for b in range(nb)]
        + [sp_spec, sp_spec, _full(m_mat.shape), _full(v_mat.shape), _full((1, SSM_WIDTH))],
        out_specs=[pl.BlockSpec((tr * SSM_T, LANES), lambda i: (i, 0)) for _ in range(nb)],
        out_shapes=[jax.ShapeDtypeStruct((rows, LANES), F32) for _ in range(nb)],
        scratch=[],
        steps=n_seq * tiles,
    )


def _ssm_step_kernel(u_ref, sre_ref, sim_ref, w0_ref, v0_ref, a1_ref, d_ref, y_ref, nre_ref, nim_ref):
    sp = SSM_STATES_PER_BLOCK
    for b in range(SSM_LANE_BLOCKS):
        lanes = slice(b * LANES, (b + 1) * LANES)
        states = slice(b * sp, (b + 1) * sp)
        u = u_ref[:, lanes]
        e = jnp.dot(u.astype(BF16), w0_ref[b], preferred_element_type=F32)
        a_re, a_im = a1_ref[b, :, 0:sp], a1_ref[b, :, sp:2 * sp]
        s_re, s_im = sre_ref[states, :].T, sim_ref[states, :].T
        n_re = a_re * s_re - a_im * s_im + e[:, 0:sp]
        n_im = a_re * s_im + a_im * s_re + e[:, sp:2 * sp]
        nre_ref[states, :] = n_re.T
        nim_ref[states, :] = n_im.T
        sn = jnp.concatenate([n_re, n_im], axis=1).astype(BF16)
        y_ref[:, lanes] = jnp.dot(sn, v0_ref[b], preferred_element_type=F32) + d_ref[:, lanes] * u


def _ssm_step(u, s_re, s_im, w0, v0, a1, d_skip):
    rows = u.shape[0]
    ns = SSM_GROUPS * SSM_STATE
    assert rows % LANES == 0
    args = (u, s_re.reshape(rows, ns).T, s_im.reshape(rows, ns).T, w0, v0, a1, d_skip.reshape(1, SSM_WIDTH))
    y, n_re, n_im = pl.pallas_call(
        _ssm_step_kernel,
        grid=(1,),
        in_specs=[_full(a.shape) for a in args],
        out_specs=[_full((rows, SSM_WIDTH)), _full((ns, rows)), _full((ns, rows))],
        out_shape=[jax.ShapeDtypeStruct((rows, SSM_WIDTH), F32), jax.ShapeDtypeStruct((ns, rows), F32),
                   jax.ShapeDtypeStruct((ns, rows), F32)],
        compiler_params=_params("arbitrary"),
        name="ssm_step",
    )(*args)
    return y, n_re.T.reshape(s_re.shape), n_im.T.reshape(s_im.shape)


def _dil_attn_kernel(q_ref, kc_ref, kp_ref, vc_ref, vp_ref, o_ref, lse_ref, *, dil, slopes, chunks, pairs):
    steps, hd = ATT_STEPS, ATT_HEAD_DIM
    chunk = q_ref.shape[0]
    span = steps * dil
    step = pl.program_id(0)
    first_chunk = (step // pairs) % chunks == 0
    pair = step % pairs
    qi = lax.broadcasted_iota(jnp.int32, (steps, 2 * steps), 0)
    kj = lax.broadcasted_iota(jnp.int32, (steps, 2 * steps), 1)
    dist = qi + steps - kj
    band = (dist >= 0) & (dist <= steps)
    distf = (dist * dil).astype(F32)
    lane = lax.broadcasted_iota(jnp.int32, (steps, LANES), 1)
    heads = [lane < hd, lane >= hd]
    biases, first_biases = [], []
    for hh in range(2):
        slope = sum(jnp.where(pair == p, slopes[2 * p + hh], 0.0) for p in range(pairs))
        bias = jnp.where(band, -slope * distf, NEG_BIG)
        biases.append(bias)
        first_biases.append(jnp.where(first_chunk & (kj < steps), NEG_BIG, bias))

    def rows(ref, start):
        return ref[pl.ds(start, steps, stride=dil), :] if dil > 1 else ref[pl.ds(start, steps), :]

    for r in range(dil):
        for qb in range(chunk // span):
            start = r + qb * span
            q = rows(q_ref, start) * (hd ** -0.5)
            if qb == 0:
                k_prev, v_prev = rows(kp_ref, r), rows(vp_ref, r)
            else:
                k_prev, v_prev = rows(kc_ref, start - span), rows(vc_ref, start - span)
            kk = jnp.concatenate([k_prev, rows(kc_ref, start)], axis=0).astype(BF16)
            vv = jnp.concatenate([v_prev, rows(vc_ref, start)], axis=0).astype(BF16)
            out = jnp.zeros((steps, LANES), F32)
            lse = jnp.zeros((steps, LANES), F32)
            for hh in range(2):
                qh = jnp.where(heads[hh], q, 0.0).astype(BF16)
                s = lax.dot_general(qh, kk, (((1,), (1,)), ((), ())), preferred_element_type=F32)
                s = s + (first_biases[hh] if qb == 0 else biases[hh])
                m = jnp.max(s, axis=-1, keepdims=True)
                p = jnp.exp(s - m)
                den = jnp.sum(p, axis=-1, keepdims=True)
                oh = jnp.dot(p.astype(BF16), vv, preferred_element_type=F32) / den
                out = jnp.where(heads[hh], oh, out)
                lse = jnp.where(heads[hh], m + jnp.log(den), lse)
            if dil > 1:
                o_ref[pl.ds(start, steps, stride=dil), :] = out
                lse_ref[pl.ds(start, steps, stride=dil), :] = lse
            else:
                o_ref[pl.ds(start, steps), :] = out
                lse_ref[pl.ds(start, steps), :] = lse


def _dil_attn_job(q, k, v, n_seq, seq_len, group):
    win, dil = DIL_PAIRS[group]
    steps, gw = ATT_STEPS, ATT_GROUP_WIDTH
    span = steps * dil
    chunk = ATT_CHUNK
    assert win // dil == steps and chunk % span == 0 and seq_len % chunk == 0
    chunks = seq_len // chunk
    pairs = gw // LANES
    cur = pl.BlockSpec((chunk, LANES), lambda i: (i // pairs, pairs * group + i % pairs))
    prev = pl.BlockSpec((span, LANES), lambda i: (jnp.maximum((i // pairs) * (chunk // span) - 1, 0),
                                                  pairs * group + i % pairs))
    out = pl.BlockSpec((chunk, LANES), lambda i: (i // pairs, i % pairs))
    slopes = tuple(_alibi_slopes()[group * HEADS_PER_GROUP:(group + 1) * HEADS_PER_GROUP])
    rows = n_seq * seq_len
    return Job(
        kernel=functools.partial(_dil_attn_kernel, dil=dil, slopes=slopes, chunks=chunks, pairs=pairs),
        args=[q, k, k, v, v],
        in_specs=[cur, cur, prev, cur, prev],
        out_specs=[out, out],
        out_shapes=[jax.ShapeDtypeStruct((rows, gw), F32), jax.ShapeDtypeStruct((rows, gw), F32)],
        scratch=[],
        steps=n_seq * chunks * pairs,
    )


def _mem_attention(q_ref, mk_ref, mv_ref):
    hd = MEM_HEAD_DIM
    outs = []
    for h in range(MEM_HEADS):
        lanes = slice(h * hd, (h + 1) * hd)
        s = lax.dot_general(q_ref[:, lanes].astype(BF16), mk_ref[:, lanes].astype(BF16),
                            (((1,), (1,)), ((), ())), preferred_element_type=F32) * (hd ** -0.5)
        m = jnp.max(s, axis=-1, keepdims=True)
        p = jnp.exp(s - m)
        den = jnp.sum(p, axis=-1, keepdims=True)
        outs.append(jnp.dot(p.astype(BF16), mv_ref[:, lanes].astype(BF16), preferred_element_type=F32) / den)
    return jnp.concatenate(outs, axis=1)


def _cache_attn_kernel(q_ref, kn_ref, vn_ref, k1_ref, v1_ref, k2_ref, v2_ref, k3_ref, v3_ref, o_ref,
                       qt_scr, vnt_scr, *, slopes):
    hpg, hd = HEADS_PER_GROUP, ATT_HEAD_DIM
    j = pl.program_id(0) % hpg
    nb = q_ref.shape[0]
    scale = hd ** -0.5
    q, kn = q_ref[...], kn_ref[...]
    qt_scr[...] = q.T
    vnt_scr[...] = vn_ref[...].T
    lane_head = lax.broadcasted_iota(jnp.int32, q.shape, 1) // hd
    outs, lses = [], []
    for g, (k_ref, v_ref) in enumerate(((k1_ref, v1_ref), (k2_ref, v2_ref), (k3_ref, v3_ref))):
        dil = DIL_PAIRS[g][1]
        n_pos = k_ref.shape[-1]
        rows = pl.ds(pl.multiple_of((g * hpg + j) * hd, hd), hd)
        qg, vng = qt_scr[rows, :], vnt_scr[rows, :]
        slope = sum(jnp.where(j == h, slopes[g * hpg + h], 0.0) for h in range(hpg))
        back = n_pos - lax.broadcasted_iota(jnp.int32, (1, n_pos), 1)
        bias = jnp.where(back % dil == 0, -slope * back.astype(F32), NEG_BIG)
        row = lax.broadcasted_iota(jnp.int32, (nb, n_pos), 0)
        s = jnp.zeros((nb, n_pos), F32)
        for b in range(nb):
            s = jnp.where(row == b, jnp.sum(k_ref[b] * qg[:, b:b + 1], axis=0, keepdims=True), s)
        s = s * scale + bias
        s_new = jnp.sum(jnp.where(lane_head == g * hpg + j, q * kn, 0.0), axis=1, keepdims=True) * scale
        m = jnp.maximum(jnp.max(s, axis=1, keepdims=True), s_new)
        p = jnp.exp(s - m)
        p_new = jnp.exp(s_new - m)
        den = jnp.sum(p, axis=1, keepdims=True) + p_new
        cols = []
        for b in range(nb):
            acc = jnp.sum(v_ref[b] * p[b:b + 1, :], axis=1, keepdims=True) + p_new[b:b + 1, :] * vng[:, b:b + 1]
            cols.append(acc / den[b:b + 1, :])
        outs.append(cols)
        lses.append(m + jnp.log(den))
    top = functools.reduce(jnp.maximum, lses)
    ws = [jnp.exp(l - top) for l in lses]
    total = sum(ws)
    o_ref[...] = jnp.concatenate(
        [sum(w[b:b + 1, :] * cols[b] for w, cols in zip(ws, outs)) / total[b:b + 1, :] for b in range(nb)], axis=1)


def _cache_attn_job(q, k_new, v_new, caches_k, caches_v, nb):
    rows = q.shape[0]
    hpg, hd = HEADS_PER_GROUP, ATT_HEAD_DIM
    assert rows % nb == 0 and nb % SUBLANES == 0
    new = pl.BlockSpec((nb, ATT_WIDTH), lambda i: (i // hpg, 0))
    specs, args = [new, new, new], [q, k_new, v_new]
    for g, (win, dil) in enumerate(DIL_PAIRS):
        for c in (caches_k[g], caches_v[g]):
            assert c.shape == (rows, win, hpg, hd) and win % dil == 0, c.shape
            args.append(c.transpose(0, 2, 3, 1))
            specs.append(pl.BlockSpec((nb, None, hd, win), lambda i: (i // hpg, i % hpg, 0, 0)))
    return Job(
        kernel=functools.partial(_cache_attn_kernel, slopes=tuple(_alibi_slopes())),
        args=args,
        in_specs=specs,
        out_specs=[pl.BlockSpec((None, None, hd, nb), lambda i: (i % hpg, i // hpg, 0, 0))],
        out_shapes=[jax.ShapeDtypeStruct((hpg, rows // nb, hd, nb), F32)],
        scratch=[pltpu.VMEM((ATT_WIDTH, nb), F32) for _ in range(2)],
        steps=(rows // nb) * hpg,
    )


def _cache_attn_result(out):
    hpg, blocks, hd, nb = out.shape
    return out.transpose(1, 3, 0, 2).reshape(blocks * nb, hpg * hd)


def _cache_mem_attn_kernel(q_ref, k_ref, v_ref, o_ref):
    nb, rows, hd = k_ref.shape
    tiles = rows // SUBLANES

    def fold(x):
        return x, pltpu.roll(x, MEM_HEADS, 2)

    q = q_ref[...][:, None]
    k = k_ref[...].reshape(nb, tiles, SUBLANES, hd)
    v = v_ref[...].reshape(nb, tiles, SUBLANES, hd)
    s = jnp.sum(k * q, axis=-1, keepdims=True) * (MEM_HEAD_DIM ** -0.5)
    m = jnp.maximum(*fold(jnp.max(s, axis=1, keepdims=True)))
    p = jnp.exp(s - m)
    den = sum(fold(jnp.sum(p, axis=1, keepdims=True)))
    acc = sum(fold(jnp.sum(p * v, axis=1, keepdims=True)))
    o_ref[...] = (acc / den)[:, 0]


def _cache_mem_job(qm, mem_k, mem_v, nb):
    rows = qm.shape[0]
    assert rows % nb == 0 and SUBLANES == 2 * MEM_HEADS
    q4 = qm.reshape(rows, MEM_HEADS, MEM_HEAD_DIM)
    kv = pl.BlockSpec((nb, N_MEM * MEM_HEADS, MEM_HEAD_DIM), lambda i: (i, 0, 0))
    q = pl.BlockSpec((nb, SUBLANES, MEM_HEAD_DIM), lambda i: (i, 0, 0))
    return Job(
        kernel=_cache_mem_attn_kernel,
        args=[jnp.concatenate([q4, q4], axis=1), mem_k.reshape(rows, N_MEM * MEM_HEADS, MEM_HEAD_DIM),
              mem_v.reshape(rows, N_MEM * MEM_HEADS, MEM_HEAD_DIM)],
        in_specs=[q, kv, kv],
        out_specs=[q],
        out_shapes=[jax.ShapeDtypeStruct((rows, SUBLANES, MEM_HEAD_DIM), F32)],
        scratch=[],
        steps=rows // nb,
    )


def _cache_mem_result(out):
    return out[:, :MEM_HEADS].reshape(out.shape[0], MEM_WIDTH)


def _merge_kernel(x_ref, *refs, n_y, n_att, n_mem):
    y_refs, att_refs = refs[:n_y], refs[n_y:n_y + n_att]
    mem_refs = refs[n_y + n_att:n_y + n_att + n_mem]
    g1_ref, wgate_ref, wglu_ref, watt_ref, wmem_ref, wout_ref, o_ref = refs[n_y + n_att + n_mem:]
    d = D_MODEL
    x = x_ref[...]
    h = _rmsnorm(x, g1_ref[...]).astype(BF16)

    def gate(i):
        return jax.nn.sigmoid(jnp.dot(h, wgate_ref[:, i * d:(i + 1) * d], preferred_element_type=F32))

    y = jnp.concatenate([r[...] for r in y_refs], axis=1) if n_y > 1 else y_refs[0][...]
    z = jax.nn.gelu(y).astype(BF16)
    glu = jnp.dot(z, wglu_ref[...], preferred_element_type=F32)
    merged = gate(0) * (glu[:, 0:d] * jax.nn.sigmoid(glu[:, d:2 * d]))
    if n_att == 1:
        att = att_refs[0][...]
    else:
        lses = [r[...] for r in att_refs[1::2]]
        top = functools.reduce(jnp.maximum, lses)
        ws = [jnp.exp(l - top) for l in lses]
        att = sum(w * r[...] for w, r in zip(ws, att_refs[0::2])) / sum(ws)
    b_att = jnp.dot(att.astype(BF16), watt_ref[...], preferred_element_type=F32)
    merged = merged + gate(1) * b_att
    o_mem = mem_refs[0][...] if n_mem == 1 else _mem_attention(*mem_refs)
    b_mem = jnp.dot(o_mem.astype(BF16), wmem_ref[...], preferred_element_type=F32)
    merged = merged + gate(2) * b_mem
    o_ref[...] = x + jnp.dot(merged.astype(BF16), wout_ref[...], preferred_element_type=F32)


def _merge_job(x, ys, atts, mems, g1, w_in, w_glu, w_att_o, w_mem_o, w_out, tm):
    rows = x.shape[0]
    gate_cols = N_BRANCH * D_MODEL
    assert rows % tm == 0 and w_in.shape == (D_MODEL, IN_WIDTH)

    def tile(a):
        return pl.BlockSpec((tm, a.shape[1]), lambda i: (i, 0))

    acts = [x, *ys, *atts, mems[0]]
    act_specs = [tile(a) for a in acts]
    if len(mems) > 1:
        tiles_per_seq = rows // mems[1].shape[0] // tm
        assert tiles_per_seq * tm * mems[1].shape[0] == rows
        act_specs += [pl.BlockSpec((None, N_MEM, MEM_WIDTH), lambda i: (i // tiles_per_seq, 0, 0))] * 2
        acts += list(mems[1:])
    weights = [w_glu, w_att_o, w_mem_o, w_out]
    return Job(
        kernel=functools.partial(_merge_kernel, n_y=len(ys), n_att=len(atts), n_mem=len(mems)),
        args=acts + [g1.reshape(1, D_MODEL), w_in] + weights,
        in_specs=act_specs + [_full((1, D_MODEL)), _resident_cols(w_in, IN_WIDTH - gate_cols, gate_cols)]
        + [_full(w.shape) for w in weights],
        out_specs=[tile(x)],
        out_shapes=[jax.ShapeDtypeStruct(x.shape, F32)],
        scratch=[],
        steps=rows // tm,
    )


def _ffn_kernel(*refs, tm, tiles_per_seq, stepwise):
    if stepwise:
        x_ref, g2_ref, wup_ref, cw_ref, cb_ref, wdn_ref, gf_ref, prev_ref, y_ref, conv_ref = refs
    else:
        x_ref, g2_ref, wup_ref, cw_ref, cb_ref, wdn_ref, gf_ref, y_ref, conv_ref, a_scr = refs
        i = pl.program_id(0)
        first = i % tiles_per_seq == 0

        @pl.when(first)
        def _():
            a_scr[0:SUBLANES, :] = jnp.zeros((SUBLANES, D_FF), F32)

        @pl.when(jnp.logical_not(first))
        def _():
            a_scr[0:SUBLANES, :] = a_scr[tm:tm + SUBLANES, :]

    x = x_ref[...]
    h = _rmsnorm(x, g2_ref[...]).astype(BF16)
    a = jnp.dot(h, wup_ref[:, 0:D_FF], preferred_element_type=F32)
    if stepwise:
        a2, a1 = prev_ref[:, 0:D_FF], prev_ref[:, D_FF:2 * D_FF]
        conv_ref[:, 0:D_FF] = a1
        conv_ref[:, D_FF:2 * D_FF] = a
    else:
        a_scr[SUBLANES:SUBLANES + tm, :] = a
        a1 = a_scr[SUBLANES - 1:SUBLANES - 1 + tm, :]
        a2 = a_scr[SUBLANES - 2:SUBLANES - 2 + tm, :]
    c = a2 * cw_ref[0:1, :] + a1 * cw_ref[1:2, :] + a * cw_ref[2:3, :] + cb_ref[...]
    v = jnp.dot(h, wup_ref[:, D_FF:2 * D_FF], preferred_element_type=F32)
    y = jnp.dot((jax.nn.gelu(c) * v).astype(BF16), wdn_ref[...], preferred_element_type=F32)
    y_ref[...] = _rmsnorm(x + y, gf_ref[...])
    if not stepwise:
        conv_ref[...] = a_scr[SUBLANES + tm - (CONV_W - 1):SUBLANES + tm, :]


def _ffn_job(x, g2, w_up, conv_w, conv_b, w_down, gf, tm, n_seq=None, prev=None):
    rows, d = x.shape
    assert rows % tm == 0
    stepwise = prev is not None
    weights = [g2.reshape(1, d), w_up, conv_w, conv_b.reshape(1, D_FF), w_down, gf.reshape(1, d)]
    in_specs = [pl.BlockSpec((tm, d), lambda i: (i, 0))] + [_full(w.shape) for w in weights]
    args = [x] + weights
    if stepwise:
        tiles_per_seq = 1
        in_specs.append(pl.BlockSpec((tm, 2 * D_FF), lambda i: (i, 0)))
        args.append(prev)
        conv_spec = pl.BlockSpec((tm, 2 * D_FF), lambda i: (i, 0))
        conv_shape = jax.ShapeDtypeStruct((rows, 2 * D_FF), F32)
        scratch = []
    else:
        tiles_per_seq = rows // n_seq // tm
        assert tiles_per_seq * tm * n_seq == rows
        conv_spec = pl.BlockSpec((None, CONV_W - 1, D_FF), lambda i: (i // tiles_per_seq, 0, 0))
        conv_shape = jax.ShapeDtypeStruct((n_seq, CONV_W - 1, D_FF), F32)
        scratch = [pltpu.VMEM((tm + SUBLANES, D_FF), F32)]
    return Job(
        kernel=functools.partial(_ffn_kernel, tm=tm, tiles_per_seq=tiles_per_seq, stepwise=stepwise),
        args=args,
        in_specs=in_specs,
        out_specs=[pl.BlockSpec((tm, d), lambda i: (i, 0)), conv_spec],
        out_shapes=[jax.ShapeDtypeStruct((rows, d), F32), conv_shape],
        scratch=scratch,
        steps=rows // tm,
    )


def kernel(x_prompt, x_sample, state_ssm_re, state_ssm_im, cache_w1_k, cache_w1_v, cache_w2_k, cache_w2_v, cache_w3_k, cache_w3_v, cache_mem_k, cache_mem_v, state_ffn_conv, mem_prompt, norm1_g, w_in, ssm_a_re, ssm_a_im, ssm_log_dt, ssm_b_re, ssm_b_im, ssm_c_re, ssm_c_im, ssm_d, w_ssm_glu, w_att_o, mem_norm_g, w_mem_kv, w_mem_o, w_out, norm2_g, w_up, ffn_conv_w, ffn_conv_b, w_down, final_norm_g):
    n_seq, seq_len, d = x_prompt.shape
    n_dec, dec_len, _ = x_sample.shape
    depth = norm1_g.shape[0]
    assert d == D_MODEL and depth == 1 and dec_len == 1
    assert w_in.shape == (depth, D_MODEL, IN_WIDTH) and w_up.shape == (depth, D_MODEL, 2 * D_FF)
    assert mem_prompt.shape == (n_seq, N_MEM, D_MODEL)
    assert ssm_a_re.shape == (depth, SSM_GROUPS, SSM_STATE)
    assert seq_len % ROW_TILE == 0 and n_dec % SUBLANES == 0
    hpg, hd = HEADS_PER_GROUP, ATT_HEAD_DIM
    rows_p = n_seq * seq_len
    dec_tile = n_dec if n_dec <= ROW_TILE else ROW_TILE

    prep_job = _ssm_prep_job(ssm_a_re[0], ssm_a_im[0], ssm_log_dt[0], ssm_b_re[0], ssm_b_im[0], ssm_c_re[0],
                             ssm_c_im[0])
    (m_mat, w_mat, v_mat, v0_mat, a1, apow), (w_in_b,) = _run_pair(
        prep_job, _cast_job([w_in[0]], prep_job.steps), "ssm_prep_cast")
    w0_mat = w_mat[:, (SSM_T - 1) * LANES:, :]

    xs = x_sample.reshape(n_dec, d)
    us, qs, ks, vs, qms = _run([_norm_proj_job(xs, norm1_g[0], w_in_b, PROJ_SPLITS, dec_tile)], "norm_proj")[0]
    cache_job = _cache_attn_job(qs, ks, vs, (cache_w1_k[0], cache_w2_k[0], cache_w3_k[0]),
                                (cache_w1_v[0], cache_w2_v[0], cache_w3_v[0]), CACHE_ROWS_PER_STEP)
    ffn_steps = rows_p // ROW_TILE
    mem_rows = n_dec // ffn_steps if n_dec % ffn_steps == 0 else CACHE_ROWS_PER_STEP
    mem_job = _cache_mem_job(qms, cache_mem_k[0], cache_mem_v[0], mem_rows)

    xp = x_prompt.reshape(rows_p, d)
    gw = ATT_GROUP_WIDTH
    keeps = [min(win, seq_len) for win, _ in DIL_PAIRS]
    windows = [(src, g * gw, gw, keep) for g, keep in enumerate(keeps) for src in (2, 3)]
    proj_tile = 2 * ROW_TILE if seq_len % (2 * ROW_TILE) == 0 else ROW_TILE
    proj_job = _norm_proj_job(xp, norm1_g[0], w_in_b, PROJ_SPLITS, proj_tile, n_seq, windows)
    u, q, k, v, qm, *kv_win = _run([proj_job], "norm_proj")[0]

    dil_jobs = [_dil_attn_job(q, k, v, n_seq, seq_len, g) for g in range(len(DIL_PAIRS))]
    (e,), att0 = _run_pair(_ssm_end_state_job(u, n_seq, w_mat, dil_jobs[0].steps), dil_jobs[0], "ssm_end_state_dil")
    sp_re, sp_im, fin_re, fin_im = _ssm_scan(e, apow, n_seq)
    y_ssm, att1 = _run_pair(_ssm_output_job(u, n_seq, sp_re, sp_im, m_mat, v_mat, ssm_d[0], dil_jobs[1].steps),
                            dil_jobs[1], "ssm_output_dil")
    casts = _cast_job([w_up[0], w_down[0], w_ssm_glu[0], w_out[0], w_mem_o[0], w_mem_kv[0], w_att_o[0]],
                      dil_jobs[2].steps)
    att2, (w_up_b, w_down_b, w_glu_b, w_out_b, w_memo_b, w_memkv_b, w_att_b) = _run_pair(
        dil_jobs[2], casts, "dil_attn_casts")
    merge_w = (norm1_g[0], w_in_b, w_glu_b, w_att_b, w_memo_b, w_out_b)
    ffn_w = (norm2_g[0], w_up_b, ffn_conv_w[0], ffn_conv_b[0], w_down_b, final_norm_g)
    atts = att0 + att1 + att2

    mk, mv = _run([_norm_proj_job(mem_prompt.reshape(n_seq * N_MEM, d), mem_norm_g[0], w_memkv_b,
                                  (MEM_WIDTH, MEM_WIDTH), ROW_TILE)], "mem_kv")[0]
    mems = [qm, mk.reshape(n_seq, N_MEM, MEM_WIDTH), mv.reshape(n_seq, N_MEM, MEM_WIDTH)]
    merge_tile = rows_p // cache_job.steps
    if not (rows_p % cache_job.steps == 0 and merge_tile % LANES == 0 and merge_tile <= ROW_TILE
            and seq_len % merge_tile == 0):
        merge_tile = ROW_TILE
    (x1,), (att_out,) = _run_pair(_merge_job(xp, y_ssm, atts, mems, *merge_w, merge_tile), cache_job,
                                  "merge_cache_attn")
    att_s = _cache_attn_result(att_out)
    ffn_job = _ffn_job(x1, *ffn_w, ROW_TILE, n_seq=n_seq)
    if mem_job.steps == ffn_job.steps:
        (y_p, conv_p), (mem_out,) = _run([ffn_job, mem_job], "ffn_cache_mem")
    else:
        (y_p, conv_p), (mem_out,) = _run([ffn_job], "ffn")[0], _run([mem_job], "cache_mem_attn")[0]
    mem_s = _cache_mem_result(mem_out)

    def final_state(s):
        return s.reshape(1, n_seq, SSM_GROUPS, SSM_STATE)
    win_p = [t.reshape(n_seq, hpg, hd, t.shape[-1]).transpose(0, 3, 1, 2)[None] for t in kv_win]
    mem_kv = [mk.reshape(1, n_seq, N_MEM, MEM_HEADS, MEM_HEAD_DIM), mv.reshape(1, n_seq, N_MEM, MEM_HEADS, MEM_HEAD_DIM)]

    ys_ssm, sn_re, sn_im = _ssm_step(us, state_ssm_re[0], state_ssm_im[0], w0_mat, v0_mat, a1, ssm_d[0])
    n_g = len(DIL_PAIRS)
    ks4, vs4 = (t.reshape(n_dec, n_g, hpg, hd) for t in (ks, vs))
    xs1 = _run([_merge_job(xs, [ys_ssm], [att_s], [mem_s], *merge_w, dec_tile)], "merge")[0][0]
    y_s, conv_s = _run([_ffn_job(xs1, *ffn_w, dec_tile,
                                 prev=state_ffn_conv[0].reshape(n_dec, (CONV_W - 1) * D_FF))], "ffn_step")[0]

    win_s = []
    for g in range(len(DIL_PAIRS)):
        win_s += [ks4[None, :, g:g + 1], vs4[None, :, g:g + 1]]

    return (y_p.reshape(n_seq, seq_len, d), y_s.reshape(n_dec, 1, d),
            final_state(fin_re), final_state(fin_im), *win_p, *mem_kv, conv_p[None],
            sn_re[None], sn_im[None],
            *win_s, conv_s.reshape(1, n_dec, CONV_W - 1, D_FF))
```

```python
import collections
import functools

import jax
import jax.numpy as jnp
from jax import lax
from jax.experimental import pallas as pl
from jax.experimental.pallas import tpu as pltpu

F32 = jnp.float32
BF16 = jnp.bfloat16

D_MODEL = 1024
SSM_WIDTH = 512
SSM_GROUP = 16
SSM_GROUPS = 32
SSM_STATE = 64
ATT_HEAD_DIM = 64
HEADS_PER_GROUP = 4
DIL_PAIRS = ((128, 1), (512, 4), (2048, 16))
ATT_HEADS = len(DIL_PAIRS) * HEADS_PER_GROUP
ATT_WIDTH = ATT_HEADS * ATT_HEAD_DIM
ATT_GROUP_WIDTH = HEADS_PER_GROUP * ATT_HEAD_DIM
ATT_STEPS = 128
N_MEM = 256
MEM_HEADS = 4
MEM_HEAD_DIM = 128
MEM_WIDTH = MEM_HEADS * MEM_HEAD_DIM
N_BRANCH = 3
D_FF = 2816
CONV_W = 3
EPS = 1e-6
PROJ_SPLITS = (SSM_WIDTH, ATT_WIDTH, ATT_WIDTH, ATT_WIDTH, MEM_WIDTH)
IN_WIDTH = sum(PROJ_SPLITS) + N_BRANCH * D_MODEL

LANES = 128
SUBLANES = 8
VMEM_LIMIT_BYTES = 56 * 1024 * 1024

SSM_T = 8
SSM_LANE_BLOCKS = SSM_WIDTH // LANES
SSM_GROUPS_PER_BLOCK = LANES // SSM_GROUP
SSM_STATES_PER_BLOCK = SSM_GROUPS_PER_BLOCK * SSM_STATE
SSM_STATE_LANES = 2 * SSM_STATES_PER_BLOCK

ROW_TILE = 512
PROJ_COL_CHUNK = 512
SSM_ROW_TILE = 256
CACHE_ROWS_PER_STEP = 8
ATT_CHUNK = 2048
NEG_BIG = -1e30


def _alibi_slopes():
    return [float(2.0 ** (-8.0 * h / ATT_HEADS)) for h in range(1, ATT_HEADS + 1)]


def _params(*sem):
    return pltpu.CompilerParams(dimension_semantics=sem, vmem_limit_bytes=VMEM_LIMIT_BYTES)


def _rmsnorm(x, g):
    ms = jnp.mean(x * x, axis=-1, keepdims=True)
    return x * lax.rsqrt(ms + EPS) * g


def _full(shape):
    nd = len(shape)
    return pl.BlockSpec(shape, lambda *_: (0,) * nd, pipeline_mode=pl.Buffered(1))


def _resident_cols(w, start, width):
    return pl.BlockSpec((pl.Element(w.shape[0]), pl.Element(width)), lambda *_: (0, start),
                        pipeline_mode=pl.Buffered(1))


Job = collections.namedtuple("Job", "kernel args in_specs out_specs out_shapes scratch steps")


def _jobs_kernel(*refs, kernels, layout):
    n_in, n_out = sum(l[0] for l in layout), sum(l[1] for l in layout)
    ins, outs, scratch = refs[:n_in], refs[n_in:n_in + n_out], refs[n_in + n_out:]
    i0 = o0 = s0 = 0
    for kernel, (ni, no, ns) in zip(kernels, layout):
        kernel(*ins[i0:i0 + ni], *outs[o0:o0 + no], *scratch[s0:s0 + ns])
        i0, o0, s0 = i0 + ni, o0 + no, s0 + ns


def _run(jobs, name):
    steps = jobs[0].steps
    assert all(j.steps == steps for j in jobs)
    layout = tuple((len(j.args), len(j.out_shapes), len(j.scratch)) for j in jobs)
    outs = pl.pallas_call(
        functools.partial(_jobs_kernel, kernels=tuple(j.kernel for j in jobs), layout=layout),
        grid=(steps,),
        in_specs=[s for j in jobs for s in j.in_specs],
        out_specs=[s for j in jobs for s in j.out_specs],
        out_shape=[s for j in jobs for s in j.out_shapes],
        scratch_shapes=[s for j in jobs for s in j.scratch],
        compiler_params=_params("arbitrary"),
        name=name,
    )(*[a for j in jobs for a in j.args])
    split, o0 = [], 0
    for _, no, _ in layout:
        split.append(list(outs[o0:o0 + no]))
        o0 += no
    return split


def _run_pair(main, rider, name):
    m, (r,) = _run_with(main, [rider], name)
    return m, r


def _run_with(main, riders, name):
    fused = [r for r in riders if r.steps == main.steps]
    outs = _run(fused + [main], name)
    by_rider = {id(r): o for r, o in zip(fused, outs)}
    return outs[-1], [by_rider[id(r)] if id(r) in by_rider else _run([r], name + "_rider")[0] for r in riders]


def _chain_kernel(*refs, first, second, n_first_in, n_second_in):
    link = refs[-1]
    first(*refs[:n_first_in], link)
    second(link, *refs[n_first_in:n_first_in + n_second_in], *refs[n_first_in + n_second_in:-1])


def _chain_jobs(first, second):
    assert first.steps == second.steps and len(first.out_shapes) == 1 and not first.scratch and not second.scratch
    block = first.out_specs[0].block_shape
    assert tuple(block) == tuple(second.in_specs[0].block_shape)
    return Job(
        kernel=functools.partial(_chain_kernel, first=first.kernel, second=second.kernel,
                                 n_first_in=len(first.args), n_second_in=len(second.args) - 1),
        args=list(first.args) + list(second.args[1:]),
        in_specs=list(first.in_specs) + list(second.in_specs[1:]),
        out_specs=second.out_specs,
        out_shapes=second.out_shapes,
        scratch=[pltpu.VMEM(tuple(block), first.out_shapes[0].dtype)],
        steps=first.steps,
    )


def _cast_kernel(*refs):
    n = len(refs) // 2
    for src, dst in zip(refs[:n], refs[n:]):
        dst[...] = src[...].astype(dst.dtype)


def _cast_job(weights, steps):
    specs = []
    for w in weights:
        rows = w.shape[0] // steps
        assert rows * steps == w.shape[0] and rows % (2 * SUBLANES) == 0, (w.shape, steps)
        specs.append(pl.BlockSpec((rows, w.shape[1]), lambda i: (i, 0)))
    return Job(kernel=_cast_kernel, args=list(weights), in_specs=specs, out_specs=specs,
               out_shapes=[jax.ShapeDtypeStruct(w.shape, BF16) for w in weights], scratch=[], steps=steps)


def _norm_proj_kernel(x_ref, g_ref, w_ref, *out_refs, splits, windows, tiles_per_seq):
    tm = x_ref.shape[0]
    h = _rmsnorm(x_ref[...], g_ref[...]).astype(BF16)
    off = 0
    for o_ref, width in zip(out_refs, splits):
        for c0 in range(0, width, PROJ_COL_CHUNK):
            cw = min(PROJ_COL_CHUNK, width - c0)
            o_ref[:, c0:c0 + cw] = jnp.dot(h, w_ref[:, off + c0:off + c0 + cw], preferred_element_type=F32)
        off += width
    tile = pl.program_id(0) % tiles_per_seq
    for win_ref, (src, col0, cols, win) in zip(out_refs[len(splits):], windows):
        rows = min(win, tm)

        @pl.when(tile >= tiles_per_seq - max(win // tm, 1))
        def _(win_ref=win_ref, src=src, col0=col0, cols=cols, rows=rows):
            win_ref[...] = out_refs[src][tm - rows:tm, col0:col0 + cols].T


def _norm_proj_job(x, g, w_bf16, splits, tm, n_seq=1, windows=()):
    rows, d = x.shape
    assert rows % (tm * n_seq) == 0 and sum(splits) <= w_bf16.shape[1]
    tps = rows // n_seq // tm
    win_specs, win_shapes = [], []
    for _, _, cols, win in windows:
        assert (win % tm == 0 or tm % win == 0) and win <= tps * tm
        n_tiles = max(win // tm, 1)
        win_specs.append(pl.BlockSpec((None, cols, min(win, tm)), lambda i, n_tiles=n_tiles: (
            i // tps, 0, jnp.maximum(i % tps - (tps - n_tiles), 0))))
        win_shapes.append(jax.ShapeDtypeStruct((n_seq, cols, win), F32))
    return Job(
        kernel=functools.partial(_norm_proj_kernel, splits=splits, windows=tuple(windows), tiles_per_seq=tps),
        args=[x, g.reshape(1, d), w_bf16],
        in_specs=[pl.BlockSpec((tm, d), lambda i: (i, 0)), _full((1, d)), _resident_cols(w_bf16, 0, sum(splits))],
        out_specs=[pl.BlockSpec((tm, s), lambda i: (i, 0)) for s in splits] + win_specs,
        out_shapes=[jax.ShapeDtypeStruct((rows, s), F32) for s in splits] + win_shapes,
        scratch=[],
        steps=rows // tm,
    )


def _ssm_layout(a_re, a_im, log_dt, b_re, b_im, c_re, c_im):
    nb, gpb, p, c = SSM_LANE_BLOCKS, SSM_GROUPS_PER_BLOCK, SSM_STATE, SSM_GROUP
    rows = jnp.stack([a_re.reshape(nb, gpb * p), a_im.reshape(nb, gpb * p),
                      jnp.repeat(log_dt, p).reshape(nb, gpb * p)], axis=1)
    eye = jnp.eye(gpb, dtype=F32)

    def place_b(b):
        return jnp.einsum('bgpc,gh->bgchp', b.reshape(nb, gpb, p, c), eye).reshape(nb, gpb * c, gpb * p)

    def place_c(m):
        return jnp.einsum('bgcp,gh->bhpgc', m.reshape(nb, gpb, c, p), eye).reshape(nb, gpb * p, gpb * c)

    return rows, place_b(b_re), place_b(b_im), place_c(c_re), place_c(c_im)


def _split_bf16(x):
    hi = x.astype(BF16)
    return hi, (x - hi.astype(F32)).astype(BF16)


def _dot_3pass(a, b_split):
    (a_hi, a_lo), (b_hi, b_lo) = _split_bf16(a), b_split
    dot = functools.partial(jnp.dot, preferred_element_type=F32)
    return dot(a_hi, b_hi) + (dot(a_hi, b_lo) + dot(a_lo, b_hi))


def _ssm_prep_kernel(rows_ref, bre_ref, bim_ref, cre_ref, cim_ref,
                     m_ref, w_ref, v_ref, v0_ref, a1_ref, apow_ref, *, t_chunk):
    sp = SSM_STATES_PER_BLOCK
    a_re, a_im, dt = rows_ref[0:1, :], rows_ref[1:2, :], jnp.exp(rows_ref[2:3, :])

    def powers(k):
        mag = jnp.exp(a_re * dt * k)
        ang = a_im * dt * k
        return mag * jnp.cos(ang), mag * jnp.sin(ang)

    n_pow = 2 * SUBLANES
    assert t_chunk + 1 <= n_pow
    pw_re, pw_im = powers(lax.broadcasted_iota(jnp.int32, (n_pow, 1), 0).astype(F32))
    pwt_re, pwt_im = pw_re.T, pw_im.T
    ab_re, ab_im = pw_re[1:2, :], pw_im[1:2, :]
    den = a_re * a_re + a_im * a_im
    q_re = ((ab_re - 1.0) * a_re + ab_im * a_im) / den
    q_im = (ab_im * a_re - (ab_re - 1.0) * a_im) / den
    bre, bim = bre_ref[...], bim_ref[...]
    bb_re = q_re * bre - q_im * bim
    bb_im = q_re * bim + q_im * bre
    cre, cim = cre_ref[...], cim_ref[...]
    cre_split, cim_split = _split_bf16(cre), _split_bf16(cim)

    m_ref[...] = jnp.zeros(m_ref.shape, m_ref.dtype)
    for k in range(t_chunk):
        pk_re, pk_im = pw_re[k:k + 1, :], pw_im[k:k + 1, :]
        bk_re = bb_re * pk_re - bb_im * pk_im
        bk_im = bb_re * pk_im + bb_im * pk_re
        t = t_chunk - 1 - k
        w_ref[t * LANES:(t + 1) * LANES, 0:sp] = bk_re.astype(w_ref.dtype)
        w_ref[t * LANES:(t + 1) * LANES, sp:2 * sp] = bk_im.astype(w_ref.dtype)
        kk = _dot_3pass(bk_re, cre_split) - _dot_3pass(bk_im, cim_split)
        kk = kk.astype(m_ref.dtype)
        for t0 in range(t_chunk - k):
            m_ref[t0 * LANES:(t0 + 1) * LANES, (t0 + k) * LANES:(t0 + k + 1) * LANES] = kk

    for t in range(t_chunk):
        pc_re, pc_im = pwt_re[:, t + 1:t + 2], pwt_im[:, t + 1:t + 2]
        v_ref[0:sp, t * LANES:(t + 1) * LANES] = (cre * pc_re - cim * pc_im).astype(v_ref.dtype)
        v_ref[sp:2 * sp, t * LANES:(t + 1) * LANES] = (-(cre * pc_im + cim * pc_re)).astype(v_ref.dtype)
    v0_ref[0:sp, :] = cre.astype(v0_ref.dtype)
    v0_ref[sp:2 * sp, :] = (-cim).astype(v0_ref.dtype)

    a1_ref[:, 0:sp] = ab_re
    a1_ref[:, sp:2 * sp] = ab_im
    steps = (lax.broadcasted_iota(jnp.int32, (SUBLANES, 1), 0) + 1) * t_chunk
    ap_re, ap_im = powers(steps.astype(F32))
    apow_ref[:, 0:sp] = ap_re
    apow_ref[:, sp:2 * sp] = ap_im


def _ssm_prep_job(a_re, a_im, log_dt, b_re, b_im, c_re, c_im):
    nb, sp, sl, tl = SSM_LANE_BLOCKS, SSM_STATES_PER_BLOCK, SSM_STATE_LANES, SSM_T * LANES

    def blk(shape):
        return pl.BlockSpec((None,) + shape, lambda b: (b, 0, 0))

    return Job(
        kernel=functools.partial(_ssm_prep_kernel, t_chunk=SSM_T),
        args=list(_ssm_layout(a_re, a_im, log_dt, b_re, b_im, c_re, c_im)),
        in_specs=[blk((3, sp)), blk((LANES, sp)), blk((LANES, sp)), blk((sp, LANES)), blk((sp, LANES))],
        out_specs=[blk((tl, tl)), blk((tl, sl)), blk((sl, tl)), blk((sl, LANES)), blk((1, sl)), blk((SUBLANES, sl))],
        out_shapes=[jax.ShapeDtypeStruct((nb, tl, tl), BF16), jax.ShapeDtypeStruct((nb, tl, sl), BF16),
                    jax.ShapeDtypeStruct((nb, sl, tl), BF16), jax.ShapeDtypeStruct((nb, sl, LANES), BF16),
                    jax.ShapeDtypeStruct((nb, 1, sl), F32), jax.ShapeDtypeStruct((nb, SUBLANES, sl), F32)],
        scratch=[],
        steps=nb,
    )


def _chunk_tokens(u_ref, t, t_chunk):
    return u_ref[pl.ds(t, u_ref.shape[0] // t_chunk, stride=t_chunk), :]


def _chunk_lanes(u_ref, t_chunk):
    return jnp.concatenate([_chunk_tokens(u_ref, t, t_chunk) for t in range(t_chunk)], axis=1)


def _ssm_end_state_kernel(*refs, t_chunk):
    nb, sl = SSM_LANE_BLOCKS, SSM_STATE_LANES
    u_refs, w_ref, e_ref = refs[:nb], refs[nb], refs[nb + 1]
    for b in range(nb):
        ub = _chunk_lanes(u_refs[b], t_chunk).astype(BF16)
        e_ref[:, b * sl:(b + 1) * sl] = jnp.dot(ub, w_ref[b], preferred_element_type=F32)


def _ssm_scan_kernel(ere_ref, eim_ref, pre_ref, pim_ref, spre_ref, spim_ref, fre_ref, fim_ref):
    n_tiles = ere_ref.shape[0] // SUBLANES
    width = ere_ref.shape[1]
    p_re, p_im = pre_ref[...], pim_ref[...]
    row = lax.broadcasted_iota(jnp.int32, (SUBLANES, width), 0)

    def shift_down(x, k):
        return jnp.where(row >= k, pltpu.roll(x, k, 0), 0.0)

    def body(i, carry):
        c_re, c_im = carry
        rows = pl.ds(pl.multiple_of(i * SUBLANES, SUBLANES), SUBLANES)
        x_re, x_im = ere_ref[rows, :], eim_ref[rows, :]
        for k in (1, 2, 4):
            a_re, a_im = p_re[k - 1:k, :], p_im[k - 1:k, :]
            s_re, s_im = shift_down(x_re, k), shift_down(x_im, k)
            x_re, x_im = x_re + a_re * s_re - a_im * s_im, x_im + a_re * s_im + a_im * s_re
        t_re = x_re + p_re * c_re - p_im * c_im
        t_im = x_im + p_re * c_im + p_im * c_re
        spre_ref[rows, :] = jnp.where(row >= 1, pltpu.roll(t_re, 1, 0), c_re)
        spim_ref[rows, :] = jnp.where(row >= 1, pltpu.roll(t_im, 1, 0), c_im)
        return t_re[SUBLANES - 1:SUBLANES, :], t_im[SUBLANES - 1:SUBLANES, :]

    zero = jnp.zeros((1, width), F32)
    f_re, f_im = lax.fori_loop(0, n_tiles, body, (zero, zero))
    fre_ref[...] = f_re
    fim_ref[...] = f_im


def _ssm_output_kernel(*refs, t_chunk):
    nb, sp = SSM_LANE_BLOCKS, SSM_STATES_PER_BLOCK
    u_refs, (spre_ref, spim_ref, m_ref, v_ref, d_ref), y_refs = refs[:nb], refs[nb:nb + 5], refs[nb + 5:]
    for b in range(nb):
        ub = _chunk_lanes(u_refs[b], t_chunk).astype(BF16)
        yb = jnp.dot(ub, m_ref[b], preferred_element_type=F32)
        states = slice(b * sp, (b + 1) * sp)
        sprev = jnp.concatenate([spre_ref[:, states], spim_ref[:, states]], axis=1).astype(BF16)
        yb = yb + jnp.dot(sprev, v_ref[b], preferred_element_type=F32)
        d = d_ref[:, b * LANES:(b + 1) * LANES]
        for t in range(t_chunk):
            y_t = yb[:, t * LANES:(t + 1) * LANES] + d * _chunk_tokens(u_refs[b], t, t_chunk)
            y_refs[b][pl.ds(t, yb.shape[0], stride=t_chunk), :] = y_t


def _ssm_tiling(rows, n_seq, steps_wanted):
    assert rows % (SSM_T * n_seq) == 0
    n_chunks = rows // SSM_T // n_seq
    assert n_chunks % SUBLANES == 0
    tr = min(SSM_ROW_TILE, n_chunks)
    if steps_wanted and (n_seq * n_chunks) % steps_wanted == 0:
        want = n_seq * n_chunks // steps_wanted
        if want % SUBLANES == 0 and n_chunks % want == 0 and want <= SSM_ROW_TILE:
            tr = want
    assert n_chunks % tr == 0
    return n_chunks, tr


def _ssm_end_state_job(u, n_seq, w_mat, steps_wanted=None):
    nb, sl = SSM_LANE_BLOCKS, SSM_STATE_LANES
    n_chunks, tr = _ssm_tiling(u.shape[0], n_seq, steps_wanted)
    tiles = n_chunks // tr
    return Job(
        kernel=functools.partial(_ssm_end_state_kernel, t_chunk=SSM_T),
        args=[u] * nb + [w_mat],
        in_specs=[pl.BlockSpec((tr * SSM_T, LANES), lambda i, b=b: (i, b)) for b in range(nb)] + [_full(w_mat.shape)],
        out_specs=[pl.BlockSpec((tr, nb * sl), lambda i: (i % tiles, i // tiles))],
        out_shapes=[jax.ShapeDtypeStruct((n_chunks, n_seq * nb * sl), F32)],
        scratch=[],
        steps=n_seq * tiles,
    )


def _ssm_scan(e, apow, n_seq):
    nb, sp = SSM_LANE_BLOCKS, SSM_STATES_PER_BLOCK
    n_chunks = e.shape[0]
    col = pl.BlockSpec((n_chunks, sp), lambda g: (0, g))
    fin = pl.BlockSpec((1, sp), lambda g: (0, g))
    return pl.pallas_call(
        _ssm_scan_kernel,
        grid=(n_seq * nb,),
        in_specs=[pl.BlockSpec((n_chunks, sp), lambda g: (0, 2 * g)),
                  pl.BlockSpec((n_chunks, sp), lambda g: (0, 2 * g + 1)),
                  pl.BlockSpec((None, SUBLANES, sp), lambda g: (g % nb, 0, 0)),
                  pl.BlockSpec((None, SUBLANES, sp), lambda g: (g % nb, 0, 1))],
        out_specs=[col, col, fin, fin],
        out_shape=[jax.ShapeDtypeStruct((n_chunks, n_seq * nb * sp), F32)] * 2
        + [jax.ShapeDtypeStruct((1, n_seq * nb * sp), F32)] * 2,
        compiler_params=_params("parallel"),
        name="ssm_scan",
    )(e, e, apow, apow)


def _ssm_output_job(u, n_seq, sp_re, sp_im, m_mat, v_mat, d_skip, steps_wanted=None):
    nb, sp = SSM_LANE_BLOCKS, SSM_STATES_PER_BLOCK
    rows = u.shape[0]
    n_chunks, tr = _ssm_tiling(rows, n_seq, steps_wanted)
    tiles = n_chunks // tr
    sp_spec = pl.BlockSpec((tr, nb * sp), lambda i: (i % tiles, i // tiles))
    return Job(
        kernel=functools.partial(_ssm_output_kernel, t_chunk=SSM_T),
        args=[u] * nb + [sp_re, sp_im, m_mat, v_mat, d_skip.reshape(1, SSM_WIDTH)],
        in_specs=[pl.BlockSpec((tr * SSM_T, LANES), lambda i, b=b: (i, b)) for b in range(nb)]
        + [sp_spec, sp_spec, _full(m_mat.shape), _full(v_mat.shape), _full((1, SSM_WIDTH))],
        out_specs=[pl.BlockSpec((tr * SSM_T, LANES), lambda i: (i, 0)) for _ in range(nb)],
        out_shapes=[jax.ShapeDtypeStruct((rows, LANES), F32) for _ in range(nb)],
        scratch=[],
        steps=n_seq * tiles,
    )


def _ssm_step_kernel(u_ref, sre_ref, sim_ref, w0_ref, v0_ref, a1_ref, d_ref, y_ref, nre_ref, nim_ref):
    sp = SSM_STATES_PER_BLOCK
    for b in range(SSM_LANE_BLOCKS):
        lanes = slice(b * LANES, (b + 1) * LANES)
        states = slice(b * sp, (b + 1) * sp)
        u = u_ref[:, lanes]
        e = jnp.dot(u.astype(BF16), w0_ref[b], preferred_element_type=F32)
        a_re, a_im = a1_ref[b, :, 0:sp], a1_ref[b, :, sp:2 * sp]
        s_re, s_im = sre_ref[states, :].T, sim_ref[states, :].T
        n_re = a_re * s_re - a_im * s_im + e[:, 0:sp]
        n_im = a_re * s_im + a_im * s_re + e[:, sp:2 * sp]
        nre_ref[states, :] = n_re.T
        nim_ref[states, :] = n_im.T
        sn = jnp.concatenate([n_re, n_im], axis=1).astype(BF16)
        y_ref[:, lanes] = jnp.dot(sn, v0_ref[b], preferred_element_type=F32) + d_ref[:, lanes] * u


def _ssm_step(u, s_re, s_im, w0, v0, a1, d_skip):
    rows = u.shape[0]
    ns = SSM_GROUPS * SSM_STATE
    assert rows % LANES == 0
    args = (u, s_re.reshape(rows, ns).T, s_im.reshape(rows, ns).T, w0, v0, a1, d_skip.reshape(1, SSM_WIDTH))
    y, n_re, n_im = pl.pallas_call(
        _ssm_step_kernel,
        grid=(1,),
        in_specs=[_full(a.shape) for a in args],
        out_specs=[_full((rows, SSM_WIDTH)), _full((ns, rows)), _full((ns, rows))],
        out_shape=[jax.ShapeDtypeStruct((rows, SSM_WIDTH), F32), jax.ShapeDtypeStruct((ns, rows), F32),
                   jax.ShapeDtypeStruct((ns, rows), F32)],
        compiler_params=_params("arbitrary"),
        name="ssm_step",
    )(*args)
    return y, n_re.T.reshape(s_re.shape), n_im.T.reshape(s_im.shape)


def _dil_attn_kernel(q_ref, kc_ref, kp_ref, vc_ref, vp_ref, o_ref, lse_ref, *, dil, slopes, chunks, pairs):
    steps, hd = ATT_STEPS, ATT_HEAD_DIM
    chunk = q_ref.shape[0]
    span = steps * dil
    step = pl.program_id(0)
    first_chunk = (step // pairs) % chunks == 0
    pair = step % pairs
    qi = lax.broadcasted_iota(jnp.int32, (steps, 2 * steps), 0)
    kj = lax.broadcasted_iota(jnp.int32, (steps, 2 * steps), 1)
    dist = qi + steps - kj
    band = (dist >= 0) & (dist <= steps)
    distf = (dist * dil).astype(F32)
    lane = lax.broadcasted_iota(jnp.int32, (steps, LANES), 1)
    heads = [lane < hd, lane >= hd]
    biases, first_biases = [], []
    for hh in range(2):
        slope = sum(jnp.where(pair == p, slopes[2 * p + hh], 0.0) for p in range(pairs))
        bias = jnp.where(band, -slope * distf, NEG_BIG)
        biases.append(bias)
        first_biases.append(jnp.where(first_chunk & (kj < steps), NEG_BIG, bias))

    def rows(ref, start):
        return ref[pl.ds(start, steps, stride=dil), :] if dil > 1 else ref[pl.ds(start, steps), :]

    for r in range(dil):
        for qb in range(chunk // span):
            start = r + qb * span
            q = rows(q_ref, start) * (hd ** -0.5)
            if qb == 0:
                k_prev, v_prev = rows(kp_ref, r), rows(vp_ref, r)
            else:
                k_prev, v_prev = rows(kc_ref, start - span), rows(vc_ref, start - span)
            kk = jnp.concatenate([k_prev, rows(kc_ref, start)], axis=0).astype(BF16)
            vv = jnp.concatenate([v_prev, rows(vc_ref, start)], axis=0).astype(BF16)
            out = jnp.zeros((steps, LANES), F32)
            lse = jnp.zeros((steps, LANES), F32)
            for hh in range(2):
                qh = jnp.where(heads[hh], q, 0.0).astype(BF16)
                s = lax.dot_general(qh, kk, (((1,), (1,)), ((), ())), preferred_element_type=F32)
                s = s + (first_biases[hh] if qb == 0 else biases[hh])
                m = jnp.max(s, axis=-1, keepdims=True)
                p = jnp.exp(s - m)
                den = jnp.sum(p, axis=-1, keepdims=True)
                oh = jnp.dot(p.astype(BF16), vv, preferred_element_type=F32) / den
                out = jnp.where(heads[hh], oh, out)
                lse = jnp.where(heads[hh], m + jnp.log(den), lse)
            if dil > 1:
                o_ref[pl.ds(start, steps, stride=dil), :] = out
                lse_ref[pl.ds(start, steps, stride=dil), :] = lse
            else:
                o_ref[pl.ds(start, steps), :] = out
                lse_ref[pl.ds(start, steps), :] = lse


def _dil_attn_job(q, k, v, n_seq, seq_len, group):
    win, dil = DIL_PAIRS[group]
    steps, gw = ATT_STEPS, ATT_GROUP_WIDTH
    span = steps * dil
    chunk = ATT_CHUNK
    assert win // dil == steps and chunk % span == 0 and seq_len % chunk == 0
    chunks = seq_len // chunk
    pairs = gw // LANES
    cur = pl.BlockSpec((chunk, LANES), lambda i: (i // pairs, pairs * group + i % pairs))
    prev = pl.BlockSpec((span, LANES), lambda i: (jnp.maximum((i // pairs) * (chunk // span) - 1, 0),
                                                  pairs * group + i % pairs))
    out = pl.BlockSpec((chunk, LANES), lambda i: (i // pairs, i % pairs))
    slopes = tuple(_alibi_slopes()[group * HEADS_PER_GROUP:(group + 1) * HEADS_PER_GROUP])
    rows = n_seq * seq_len
    return Job(
        kernel=functools.partial(_dil_attn_kernel, dil=dil, slopes=slopes, chunks=chunks, pairs=pairs),
        args=[q, k, k, v, v],
        in_specs=[cur, cur, prev, cur, prev],
        out_specs=[out, out],
        out_shapes=[jax.ShapeDtypeStruct((rows, gw), F32), jax.ShapeDtypeStruct((rows, gw), F32)],
        scratch=[],
        steps=n_seq * chunks * pairs,
    )


def _mem_attention(q_ref, mk_ref, mv_ref):
    hd = MEM_HEAD_DIM
    outs = []
    for h in range(MEM_HEADS):
        lanes = slice(h * hd, (h + 1) * hd)
        s = lax.dot_general(q_ref[:, lanes].astype(BF16), mk_ref[:, lanes].astype(BF16),
                            (((1,), (1,)), ((), ())), preferred_element_type=F32) * (hd ** -0.5)
        m = jnp.max(s, axis=-1, keepdims=True)
        p = jnp.exp(s - m)
        den = jnp.sum(p, axis=-1, keepdims=True)
        outs.append(jnp.dot(p.astype(BF16), mv_ref[:, lanes].astype(BF16), preferred_element_type=F32) / den)
    return jnp.concatenate(outs, axis=1)


def _cache_attn_kernel(q_ref, kn_ref, vn_ref, k1_ref, v1_ref, k2_ref, v2_ref, k3_ref, v3_ref, o_ref,
                       qt_scr, vnt_scr, *, slopes):
    hpg, hd = HEADS_PER_GROUP, ATT_HEAD_DIM
    j = pl.program_id(0) % hpg
    nb = q_ref.shape[0]
    scale = hd ** -0.5
    q, kn = q_ref[...], kn_ref[...]
    qt_scr[...] = q.T
    vnt_scr[...] = vn_ref[...].T
    lane_head = lax.broadcasted_iota(jnp.int32, q.shape, 1) // hd
    outs, lses = [], []
    for g, (k_ref, v_ref) in enumerate(((k1_ref, v1_ref), (k2_ref, v2_ref), (k3_ref, v3_ref))):
        dil = DIL_PAIRS[g][1]
        n_pos = k_ref.shape[-1]
        rows = pl.ds(pl.multiple_of((g * hpg + j) * hd, hd), hd)
        qg, vng = qt_scr[rows, :], vnt_scr[rows, :]
        slope = sum(jnp.where(j == h, slopes[g * hpg + h], 0.0) for h in range(hpg))
        back = n_pos - lax.broadcasted_iota(jnp.int32, (1, n_pos), 1)
        bias = jnp.where(back % dil == 0, -slope * back.astype(F32), NEG_BIG)
        row = lax.broadcasted_iota(jnp.int32, (nb, n_pos), 0)
        s = jnp.zeros((nb, n_pos), F32)
        for b in range(nb):
            s = jnp.where(row == b, jnp.sum(k_ref[b] * qg[:, b:b + 1], axis=0, keepdims=True), s)
        s = s * scale + bias
        s_new = jnp.sum(jnp.where(lane_head == g * hpg + j, q * kn, 0.0), axis=1, keepdims=True) * scale
        m = jnp.maximum(jnp.max(s, axis=1, keepdims=True), s_new)
        p = jnp.exp(s - m)
        p_new = jnp.exp(s_new - m)
        den = jnp.sum(p, axis=1, keepdims=True) + p_new
        cols = []
        for b in range(nb):
            acc = jnp.sum(v_ref[b] * p[b:b + 1, :], axis=1, keepdims=True) + p_new[b:b + 1, :] * vng[:, b:b + 1]
            cols.append(acc / den[b:b + 1, :])
        outs.append(cols)
        lses.append(m + jnp.log(den))
    top = functools.reduce(jnp.maximum, lses)
    ws = [jnp.exp(l - top) for l in lses]
    total = sum(ws)
    o_ref[...] = jnp.concatenate(
        [sum(w[b:b + 1, :] * cols[b] for w, cols in zip(ws, outs)) / total[b:b + 1, :] for b in range(nb)], axis=1)


def _cache_attn_job(q, k_new, v_new, caches_k, caches_v, nb):
    rows = q.shape[0]
    hpg, hd = HEADS_PER_GROUP, ATT_HEAD_DIM
    assert rows % nb == 0 and nb % SUBLANES == 0
    new = pl.BlockSpec((nb, ATT_WIDTH), lambda i: (i // hpg, 0))
    specs, args = [new, new, new], [q, k_new, v_new]
    for g, (win, dil) in enumerate(DIL_PAIRS):
        for c in (caches_k[g], caches_v[g]):
            assert c.shape == (rows, win, hpg, hd) and win % dil == 0, c.shape
            args.append(c.transpose(0, 2, 3, 1))
            specs.append(pl.BlockSpec((nb, None, hd, win), lambda i: (i // hpg, i % hpg, 0, 0)))
    return Job(
        kernel=functools.partial(_cache_attn_kernel, slopes=tuple(_alibi_slopes())),
        args=args,
        in_specs=specs,
        out_specs=[pl.BlockSpec((None, None, hd, nb), lambda i: (i % hpg, i // hpg, 0, 0))],
        out_shapes=[jax.ShapeDtypeStruct((hpg, rows // nb, hd, nb), F32)],
        scratch=[pltpu.VMEM((ATT_WIDTH, nb), F32) for _ in range(2)],
        steps=(rows // nb) * hpg,
    )


def _cache_attn_result(out):
    hpg, blocks, hd, nb = out.shape
    return out.transpose(1, 3, 0, 2).reshape(blocks * nb, hpg * hd)


def _cache_mem_attn_kernel(q_ref, k_ref, v_ref, o_ref):
    nb, rows, hd = k_ref.shape
    tiles = rows // SUBLANES

    def fold(x):
        return x, pltpu.roll(x, MEM_HEADS, 2)

    q = q_ref[...][:, None]
    k = k_ref[...].reshape(nb, tiles, SUBLANES, hd)
    v = v_ref[...].reshape(nb, tiles, SUBLANES, hd)
    s = jnp.sum(k * q, axis=-1, keepdims=True) * (MEM_HEAD_DIM ** -0.5)
    m = jnp.maximum(*fold(jnp.max(s, axis=1, keepdims=True)))
    p = jnp.exp(s - m)
    den = sum(fold(jnp.sum(p, axis=1, keepdims=True)))
    acc = sum(fold(jnp.sum(p * v, axis=1, keepdims=True)))
    o_ref[...] = (acc / den)[:, 0]


def _cache_mem_job(qm, mem_k, mem_v, nb):
    rows = qm.shape[0]
    assert rows % nb == 0 and SUBLANES == 2 * MEM_HEADS
    q4 = qm.reshape(rows, MEM_HEADS, MEM_HEAD_DIM)
    kv = pl.BlockSpec((nb, N_MEM * MEM_HEADS, MEM_HEAD_DIM), lambda i: (i, 0, 0))
    q = pl.BlockSpec((nb, SUBLANES, MEM_HEAD_DIM), lambda i: (i, 0, 0))
    return Job(
        kernel=_cache_mem_attn_kernel,
        args=[jnp.concatenate([q4, q4], axis=1), mem_k.reshape(rows, N_MEM * MEM_HEADS, MEM_HEAD_DIM),
              mem_v.reshape(rows, N_MEM * MEM_HEADS, MEM_HEAD_DIM)],
        in_specs=[q, kv, kv],
        out_specs=[q],
        out_shapes=[jax.ShapeDtypeStruct((rows, SUBLANES, MEM_HEAD_DIM), F32)],
        scratch=[],
        steps=rows // nb,
    )


def _cache_mem_result(out):
    return out[:, :MEM_HEADS].reshape(out.shape[0], MEM_WIDTH)


def _merge_kernel(x_ref, *refs, n_y, n_att, n_mem):
    y_refs, att_refs = refs[:n_y], refs[n_y:n_y + n_att]
    mem_refs = refs[n_y + n_att:n_y + n_att + n_mem]
    g1_ref, wgate_ref, wglu_ref, watt_ref, wmem_ref, wout_ref, o_ref = refs[n_y + n_att + n_mem:]
    d = D_MODEL
    x = x_ref[...]
    h = _rmsnorm(x, g1_ref[...]).astype(BF16)

    def gate(i):
        return jax.nn.sigmoid(jnp.dot(h, wgate_ref[:, i * d:(i + 1) * d], preferred_element_type=F32))

    y = jnp.concatenate([r[...] for r in y_refs], axis=1) if n_y > 1 else y_refs[0][...]
    z = jax.nn.gelu(y).astype(BF16)
    glu = jnp.dot(z, wglu_ref[...], preferred_element_type=F32)
    merged = gate(0) * (glu[:, 0:d] * jax.nn.sigmoid(glu[:, d:2 * d]))
    if n_att == 1:
        att = att_refs[0][...]
    else:
        lses = [r[...] for r in att_refs[1::2]]
        top = functools.reduce(jnp.maximum, lses)
        ws = [jnp.exp(l - top) for l in lses]
        att = sum(w * r[...] for w, r in zip(ws, att_refs[0::2])) / sum(ws)
    b_att = jnp.dot(att.astype(BF16), watt_ref[...], preferred_element_type=F32)
    merged = merged + gate(1) * b_att
    o_mem = mem_refs[0][...] if n_mem == 1 else _mem_attention(*mem_refs)
    b_mem = jnp.dot(o_mem.astype(BF16), wmem_ref[...], preferred_element_type=F32)
    merged = merged + gate(2) * b_mem
    o_ref[...] = x + jnp.dot(merged.astype(BF16), wout_ref[...], preferred_element_type=F32)


def _merge_job(x, ys, atts, mems, g1, w_in, w_glu, w_att_o, w_mem_o, w_out, tm):
    rows = x.shape[0]
    gate_cols = N_BRANCH * D_MODEL
    assert rows % tm == 0 and w_in.shape == (D_MODEL, IN_WIDTH)

    def tile(a):
        return pl.BlockSpec((tm, a.shape[1]), lambda i: (i, 0))

    acts = [x, *ys, *atts, mems[0]]
    act_specs = [tile(a) for a in acts]
    if len(mems) > 1:
        tiles_per_seq = rows // mems[1].shape[0] // tm
        assert tiles_per_seq * tm * mems[1].shape[0] == rows
        act_specs += [pl.BlockSpec((None, N_MEM, MEM_WIDTH), lambda i: (i // tiles_per_seq, 0, 0))] * 2
        acts += list(mems[1:])
    weights = [w_glu, w_att_o, w_mem_o, w_out]
    return Job(
        kernel=functools.partial(_merge_kernel, n_y=len(ys), n_att=len(atts), n_mem=len(mems)),
        args=acts + [g1.reshape(1, D_MODEL), w_in] + weights,
        in_specs=act_specs + [_full((1, D_MODEL)), _resident_cols(w_in, IN_WIDTH - gate_cols, gate_cols)]
        + [_full(w.shape) for w in weights],
        out_specs=[tile(x)],
        out_shapes=[jax.ShapeDtypeStruct(x.shape, F32)],
        scratch=[],
        steps=rows // tm,
    )


def _ffn_kernel(*refs, tm, tiles_per_seq, stepwise):
    if stepwise:
        x_ref, g2_ref, wup_ref, cw_ref, cb_ref, wdn_ref, gf_ref, prev_ref, y_ref, conv_ref = refs
    else:
        x_ref, g2_ref, wup_ref, cw_ref, cb_ref, wdn_ref, gf_ref, y_ref, conv_ref, a_scr = refs
        i = pl.program_id(0)
        first = i % tiles_per_seq == 0

        @pl.when(first)
        def _():
            a_scr[0:SUBLANES, :] = jnp.zeros((SUBLANES, D_FF), F32)

        @pl.when(jnp.logical_not(first))
        def _():
            a_scr[0:SUBLANES, :] = a_scr[tm:tm + SUBLANES, :]

    x = x_ref[...]
    h = _rmsnorm(x, g2_ref[...]).astype(BF16)
    a = jnp.dot(h, wup_ref[:, 0:D_FF], preferred_element_type=F32)
    if stepwise:
        a2, a1 = prev_ref[:, 0:D_FF], prev_ref[:, D_FF:2 * D_FF]
        conv_ref[:, 0:D_FF] = a1
        conv_ref[:, D_FF:2 * D_FF] = a
    else:
        a_scr[SUBLANES:SUBLANES + tm, :] = a
        a1 = a_scr[SUBLANES - 1:SUBLANES - 1 + tm, :]
        a2 = a_scr[SUBLANES - 2:SUBLANES - 2 + tm, :]
    c = a2 * cw_ref[0:1, :] + a1 * cw_ref[1:2, :] + a * cw_ref[2:3, :] + cb_ref[...]
    v = jnp.dot(h, wup_ref[:, D_FF:2 * D_FF], preferred_element_type=F32)
    y = jnp.dot((jax.nn.gelu(c) * v).astype(BF16), wdn_ref[...], preferred_element_type=F32)
    y_ref[...] = _rmsnorm(x + y, gf_ref[...])
    if not stepwise:
        conv_ref[...] = a_scr[SUBLANES + tm - (CONV_W - 1):SUBLANES + tm, :]


def _ffn_job(x, g2, w_up, conv_w, conv_b, w_down, gf, tm, n_seq=None, prev=None):
    rows, d = x.shape
    assert rows % tm == 0
    stepwise = prev is not None
    weights = [g2.reshape(1, d), w_up, conv_w, conv_b.reshape(1, D_FF), w_down, gf.reshape(1, d)]
    in_specs = [pl.BlockSpec((tm, d), lambda i: (i, 0))] + [_full(w.shape) for w in weights]
    args = [x] + weights
    if stepwise:
        tiles_per_seq = 1
        in_specs.append(pl.BlockSpec((tm, 2 * D_FF), lambda i: (i, 0)))
        args.append(prev)
        conv_spec = pl.BlockSpec((tm, 2 * D_FF), lambda i: (i, 0))
        conv_shape = jax.ShapeDtypeStruct((rows, 2 * D_FF), F32)
        scratch = []
    else:
        tiles_per_seq = rows // n_seq // tm
        assert tiles_per_seq * tm * n_seq == rows
        conv_spec = pl.BlockSpec((None, CONV_W - 1, D_FF), lambda i: (i // tiles_per_seq, 0, 0))
        conv_shape = jax.ShapeDtypeStruct((n_seq, CONV_W - 1, D_FF), F32)
        scratch = [pltpu.VMEM((tm + SUBLANES, D_FF), F32)]
    return Job(
        kernel=functools.partial(_ffn_kernel, tm=tm, tiles_per_seq=tiles_per_seq, stepwise=stepwise),
        args=args,
        in_specs=in_specs,
        out_specs=[pl.BlockSpec((tm, d), lambda i: (i, 0)), conv_spec],
        out_shapes=[jax.ShapeDtypeStruct((rows, d), F32), conv_shape],
        scratch=scratch,
        steps=rows // tm,
    )


def kernel(x_prompt, x_sample, state_ssm_re, state_ssm_im, cache_w1_k, cache_w1_v, cache_w2_k, cache_w2_v, cache_w3_k, cache_w3_v, cache_mem_k, cache_mem_v, state_ffn_conv, mem_prompt, norm1_g, w_in, ssm_a_re, ssm_a_im, ssm_log_dt, ssm_b_re, ssm_b_im, ssm_c_re, ssm_c_im, ssm_d, w_ssm_glu, w_att_o, mem_norm_g, w_mem_kv, w_mem_o, w_out, norm2_g, w_up, ffn_conv_w, ffn_conv_b, w_down, final_norm_g):
    n_seq, seq_len, d = x_prompt.shape
    n_dec, dec_len, _ = x_sample.shape
    depth = norm1_g.shape[0]
    assert d == D_MODEL and depth == 1 and dec_len == 1
    assert w_in.shape == (depth, D_MODEL, IN_WIDTH) and w_up.shape == (depth, D_MODEL, 2 * D_FF)
    assert mem_prompt.shape == (n_seq, N_MEM, D_MODEL)
    assert ssm_a_re.shape == (depth, SSM_GROUPS, SSM_STATE)
    assert seq_len % ROW_TILE == 0 and n_dec % SUBLANES == 0
    hpg, hd = HEADS_PER_GROUP, ATT_HEAD_DIM
    rows_p = n_seq * seq_len
    dec_tile = n_dec if n_dec <= ROW_TILE else ROW_TILE

    prep_job = _ssm_prep_job(ssm_a_re[0], ssm_a_im[0], ssm_log_dt[0], ssm_b_re[0], ssm_b_im[0], ssm_c_re[0],
                             ssm_c_im[0])
    (m_mat, w_mat, v_mat, v0_mat, a1, apow), (w_in_b,) = _run_pair(
        prep_job, _cast_job([w_in[0]], prep_job.steps), "ssm_prep_cast")
    w0_mat = w_mat[:, (SSM_T - 1) * LANES:, :]

    xs = x_sample.reshape(n_dec, d)
    us, qs, ks, vs, qms = _run([_norm_proj_job(xs, norm1_g[0], w_in_b, PROJ_SPLITS, dec_tile)], "norm_proj")[0]
    cache_job = _cache_attn_job(qs, ks, vs, (cache_w1_k[0], cache_w2_k[0], cache_w3_k[0]),
                                (cache_w1_v[0], cache_w2_v[0], cache_w3_v[0]), CACHE_ROWS_PER_STEP)
    ffn_steps = rows_p // ROW_TILE
    mem_rows = n_dec // ffn_steps if n_dec % ffn_steps == 0 else CACHE_ROWS_PER_STEP
    mem_job = _cache_mem_job(qms, cache_mem_k[0], cache_mem_v[0], mem_rows)

    xp = x_prompt.reshape(rows_p, d)
    gw = ATT_GROUP_WIDTH
    keeps = [min(win, seq_len) for win, _ in DIL_PAIRS]
    windows = [(src, g * gw, gw, keep) for g, keep in enumerate(keeps) for src in (2, 3)]
    proj_tile = 2 * ROW_TILE if seq_len % (2 * ROW_TILE) == 0 else ROW_TILE
    proj_job = _norm_proj_job(xp, norm1_g[0], w_in_b, PROJ_SPLITS, proj_tile, n_seq, windows)
    u, q, k, v, qm, *kv_win = _run([proj_job], "norm_proj")[0]

    dil_jobs = [_dil_attn_job(q, k, v, n_seq, seq_len, g) for g in range(len(DIL_PAIRS))]
    (e,), att0 = _run_pair(_ssm_end_state_job(u, n_seq, w_mat, dil_jobs[0].steps), dil_jobs[0], "ssm_end_state_dil")
    sp_re, sp_im, fin_re, fin_im = _ssm_scan(e, apow, n_seq)
    y_ssm, att1 = _run_pair(_ssm_output_job(u, n_seq, sp_re, sp_im, m_mat, v_mat, ssm_d[0], dil_jobs[1].steps),
                            dil_jobs[1], "ssm_output_dil")
    casts = _cast_job([w_up[0], w_down[0], w_ssm_glu[0], w_out[0], w_mem_o[0], w_mem_kv[0], w_att_o[0]],
                      dil_jobs[2].steps)
    att2, (w_up_b, w_down_b, w_glu_b, w_out_b, w_memo_b, w_memkv_b, w_att_b) = _run_pair(
        dil_jobs[2], casts, "dil_attn_casts")
    merge_w = (norm1_g[0], w_in_b, w_glu_b, w_att_b, w_memo_b, w_out_b)
    ffn_w = (norm2_g[0], w_up_b, ffn_conv_w[0], ffn_conv_b[0], w_down_b, final_norm_g)
    atts = att0 + att1 + att2

    mk, mv = _run([_norm_proj_job(mem_prompt.reshape(n_seq * N_MEM, d), mem_norm_g[0], w_memkv_b,
                                  (MEM_WIDTH, MEM_WIDTH), ROW_TILE)], "mem_kv")[0]
    mems = [qm, mk.reshape(n_seq, N_MEM, MEM_WIDTH), mv.reshape(n_seq, N_MEM, MEM_WIDTH)]
    merge_tile = rows_p // cache_job.steps
    if not (rows_p % cache_job.steps == 0 and merge_tile % LANES == 0 and merge_tile <= ROW_TILE
            and seq_len % merge_tile == 0):
        merge_tile = ROW_TILE
    (x1,), (att_out,) = _run_pair(_merge_job(xp, y_ssm, atts, mems, *merge_w, merge_tile), cache_job,
                                  "merge_cache_attn")
    att_s = _cache_attn_result(att_out)
    ffn_job = _ffn_job(x1, *ffn_w, ROW_TILE, n_seq=n_seq)
    if mem_job.steps == ffn_job.steps:
        (y_p, conv_p), (mem_out,) = _run([ffn_job, mem_job], "ffn_cache_mem")
    else:
        (y_p, conv_p), (mem_out,) = _run([ffn_job], "ffn")[0], _run([mem_job], "cache_mem_attn")[0]
    mem_s = _cache_mem_result(mem_out)

    def final_state(s):
        return s.reshape(1, n_seq, SSM_GROUPS, SSM_STATE)
    win_p = [t.reshape(n_seq, hpg, hd, t.shape[-1]).transpose(0, 3, 1, 2)[None] for t in kv_win]
    mem_kv = [mk.reshape(1, n_seq, N_MEM, MEM_HEADS, MEM_HEAD_DIM), mv.reshape(1, n_seq, N_MEM, MEM_HEADS, MEM_HEAD_DIM)]

    ys_ssm, sn_re, sn_im = _ssm_step(us, state_ssm_re[0], state_ssm_im[0], w0_mat, v0_mat, a1, ssm_d[0])
    n_g = len(DIL_PAIRS)
    ks4, vs4 = (t.reshape(n_dec, n_g, hpg, hd) for t in (ks, vs))
    y_s, conv_s = _run([_chain_jobs(
        _merge_job(xs, [ys_ssm], [att_s], [mem_s], *merge_w, dec_tile),
        _ffn_job(xs, *ffn_w, dec_tile, prev=state_ffn_conv[0].reshape(n_dec, (CONV_W - 1) * D_FF)))],
        "merge_ffn_step")[0]

    win_s = []
    for g in range(len(DIL_PAIRS)):
        win_s += [ks4[None, :, g:g + 1], vs4[None, :, g:g + 1]]

    return (y_p.reshape(n_seq, seq_len, d), y_s.reshape(n_dec, 1, d),
            final_state(fin_re), final_state(fin_im), *win_p, *mem_kv, conv_p[None],
            sn_re[None], sn_im[None],
            *win_s, conv_s.reshape(1, n_dec, CONV_W - 1, D_FF))
```
